```python
import math
import jax, jax.numpy as jnp
from jax import lax
import numpy as np

D_MODEL = 4096
BATCH = 8
SEQ = 4096
DEPTH = 2

N_A_LAYERS = DEPTH // 2
N_B_LAYERS = DEPTH - N_A_LAYERS
CONV_WIDTH = D_MODEL
CONV_KERNEL = 31
MLA_HEADS = 64
Q_LORA_RANK = 1024
KV_LORA_RANK = 512
QK_NOPE_DIM = 128
QK_ROPE_DIM = 64
V_HEAD_DIM = 128
MLA_WIDTH = MLA_HEADS * V_HEAD_DIM
ROPE_BASE = 10000.0
Q_BLOCK = 128
LN_EPS = 1e-5
RMS_EPS = 1e-6
DEEPNORM_ALPHA = (2.0 * DEPTH) ** 0.25
DEEPNORM_BETA = (8.0 * DEPTH) ** -0.25

kernel_name = "yoco_conformer_mla_deepnorm_adaln"


def _layernorm(x, g=None, b=None):
    xf = x.astype(jnp.float32)
    mu = jnp.mean(xf, axis=-1, keepdims=True)
    var = jnp.mean(jnp.square(xf - mu), axis=-1, keepdims=True)
    y = (xf - mu) * lax.rsqrt(var + LN_EPS)
    if g is not None:
        y = y * g.astype(jnp.float32) + b.astype(jnp.float32)
    return y.astype(x.dtype)


def _rmsnorm(x, g):
    xf = x.astype(jnp.float32)
    y = xf * lax.rsqrt(jnp.mean(jnp.square(xf), axis=-1, keepdims=True) + RMS_EPS)
    return (y * g.astype(jnp.float32)).astype(x.dtype)


def _rope(x, pos):
    half = QK_ROPE_DIM // 2
    inv_freq = ROPE_BASE ** (-jnp.arange(half, dtype=jnp.float32) / half)
    ang = pos.astype(jnp.float32)[:, None] * inv_freq[None, :]
    cos = jnp.cos(ang)[None, :, None, :]
    sin = jnp.sin(ang)[None, :, None, :]
    xf = x.astype(jnp.float32)
    x1, x2 = xf[..., :half], xf[..., half:]
    return jnp.concatenate([x1 * cos - x2 * sin, x1 * sin + x2 * cos], axis=-1).astype(x.dtype)


def _modulate(x, c, w_ada, b_ada):
    mod = jax.nn.silu(c) @ w_ada + b_ada
    shift, scale, gate = jnp.split(mod, 3, axis=-1)
    h = _layernorm(x) * (1.0 + scale[:, None, :]) + shift[:, None, :]
    return h, (1.0 + gate)[:, None, :]


def _conformer_conv(h, w_in, w_dw, b_dw, g_cn, b_cn, w_out):
    u = h @ w_in
    a, g, z = jnp.split(u, 3, axis=-1)
    v = a * jax.nn.sigmoid(g)
    v = lax.conv_general_dilated(
        v, w_dw[:, None, :], window_strides=(1,),
        padding=[(CONV_KERNEL - 1, 0)],
        dimension_numbers=("NWC", "WIO", "NWC"),
        feature_group_count=CONV_WIDTH) + b_dw
    v = jax.nn.silu(_layernorm(v, g_cn, b_cn))
    return (v * jax.nn.silu(z)) @ w_out


def _shared_kv(xs, w_kva, g_kv, w_kvb, pos):
    B, S, _ = xs.shape
    kva = xs @ w_kva
    c_kv = _rmsnorm(kva[..., :KV_LORA_RANK], g_kv)
    k_rope = _rope(kva[..., KV_LORA_RANK:][:, :, None, :], pos)[:, :, 0]
    kv = (c_kv @ w_kvb).reshape(B, S, MLA_HEADS, QK_NOPE_DIM + V_HEAD_DIM)
    return kv[..., :QK_NOPE_DIM], k_rope, kv[..., QK_NOPE_DIM:]


def _mla(h, k_nope, k_rope, v, w_in, g_q, w_qb, w_out, pos):
    B, S, _ = h.shape
    u = h @ w_in
    c_q, z = u[..., :Q_LORA_RANK], u[..., Q_LORA_RANK:]
    q = (_rmsnorm(c_q, g_q) @ w_qb).reshape(B, S, MLA_HEADS, QK_NOPE_DIM + QK_ROPE_DIM)
    q_nope = q[..., :QK_NOPE_DIM]
    q_rope = _rope(q[..., QK_NOPE_DIM:], pos)
    scale = (QK_NOPE_DIM + QK_ROPE_DIM) ** -0.5
    outs = []
    for start in range(0, S, Q_BLOCK):
        end = min(start + Q_BLOCK, S)
        s = (jnp.einsum("bqhd,bkhd->bhqk", q_nope[:, start:end], k_nope[:, :end])
             + jnp.einsum("bqhd,bkd->bhqk", q_rope[:, start:end], k_rope[:, :end]))
        s = s.astype(jnp.float32) * scale
        mask = (start + jnp.arange(end - start))[:, None] >= jnp.arange(end)[None, :]
        s = jnp.where(mask[None, None], s, -jnp.inf)
        p = jax.nn.softmax(s, axis=-1).astype(v.dtype)
        outs.append(jnp.einsum("bhqk,bkhd->bqhd", p, v[:, :end]))
    o = jnp.concatenate(outs, axis=1).reshape(B, S, MLA_WIDTH)
    return (o * jax.nn.silu(z)) @ w_out


def _fwd_setup_inputs(seed: int = 0) -> dict:
    key = jax.random.key(seed)
    ks = jax.random.split(key, 24)
    f32 = jnp.float32
    D = D_MODEL
    nrm = lambda k, shape, s: jax.random.normal(k, shape, f32) * s
    return {
        "x": nrm(ks[0], (BATCH, SEQ, D), 1.0),
        "c": nrm(ks[1], (BATCH, D), 1.0),
        "w_ada": nrm(ks[2], (DEPTH, D, 3 * D), D ** -0.5),
        "b_ada": nrm(ks[3], (DEPTH, 3 * D), 0.02),
        "ln_g": 1.0 + nrm(ks[4], (DEPTH, D), 0.02),
        "ln_b": nrm(ks[5], (DEPTH, D), 0.02),
        "a_w_in": nrm(ks[6], (N_A_LAYERS, D, 3 * CONV_WIDTH), D ** -0.5),
        "a_w_dw": nrm(ks[7], (N_A_LAYERS, CONV_KERNEL, CONV_WIDTH), CONV_KERNEL ** -0.5),
        "a_b_dw": nrm(ks[8], (N_A_LAYERS, CONV_WIDTH), 0.02),
        "a_norm_g": 1.0 + nrm(ks[9], (N_A_LAYERS, CONV_WIDTH), 0.02),
        "a_norm_b": nrm(ks[10], (N_A_LAYERS, CONV_WIDTH), 0.02),
        "a_w_out": nrm(ks[11], (N_A_LAYERS, CONV_WIDTH, D), DEEPNORM_BETA * CONV_WIDTH ** -0.5),
        "b_w_in": nrm(ks[12], (N_B_LAYERS, D, Q_LORA_RANK + MLA_WIDTH), D ** -0.5),
        "b_q_norm_g": 1.0 + nrm(ks[13], (N_B_LAYERS, Q_LORA_RANK), 0.02),
        "b_w_qb": nrm(ks[14], (N_B_LAYERS, Q_LORA_RANK, MLA_HEADS * (QK_NOPE_DIM + QK_ROPE_DIM)), Q_LORA_RANK ** -0.5),
        "b_w_out": nrm(ks[15], (N_B_LAYERS, MLA_WIDTH, D), DEEPNORM_BETA * MLA_WIDTH ** -0.5),
        "kv_w_a": nrm(ks[16], (D, KV_LORA_RANK + QK_ROPE_DIM), D ** -0.5),
        "kv_norm_g": 1.0 + nrm(ks[17], (KV_LORA_RANK,), 0.02),
        "kv_w_b": nrm(ks[18], (KV_LORA_RANK, MLA_HEADS * (QK_NOPE_DIM + V_HEAD_DIM)), KV_LORA_RANK ** -0.5),
    }


def _fwd_reference(x, c, w_ada, b_ada, ln_g, ln_b, a_w_in, a_w_dw, a_b_dw, a_norm_g, a_norm_b, a_w_out,
              b_w_in, b_q_norm_g, b_w_qb, b_w_out, kv_w_a, kv_norm_g, kv_w_b):
    S = x.shape[1]
    pos = jnp.arange(S, dtype=jnp.int32)
    k_nope = k_rope = v = None
    for layer in range(DEPTH):
        if layer == N_A_LAYERS:
            k_nope, k_rope, v = _shared_kv(x, kv_w_a, kv_norm_g, kv_w_b, pos)
        h, gate = _modulate(x, c, w_ada[layer], b_ada[layer])
        if layer < N_A_LAYERS:
            i = layer
            out = _conformer_conv(h, a_w_in[i], a_w_dw[i], a_b_dw[i], a_norm_g[i], a_norm_b[i], a_w_out[i])
        else:
            j = layer - N_A_LAYERS
            out = _mla(h, k_nope, k_rope, v, b_w_in[j], b_q_norm_g[j], b_w_qb[j], b_w_out[j], pos)
        x = _layernorm(DEEPNORM_ALPHA * x + gate * out, ln_g[layer], ln_b[layer])
    return x


import jax as _jax
import jax.numpy as _jnp

TWIN_FORMAT = 'train_step'
FWD_PARAMS = ['x', 'c', 'w_ada', 'b_ada', 'ln_g', 'ln_b', 'a_w_in', 'a_w_dw', 'a_b_dw', 'a_norm_g', 'a_norm_b', 'a_w_out', 'b_w_in', 'b_q_norm_g', 'b_w_qb', 'b_w_out', 'kv_w_a', 'kv_norm_g', 'kv_w_b']
TWIN_WEIGHTS = ['w_ada', 'b_ada', 'ln_g', 'ln_b', 'a_w_in', 'a_w_dw', 'a_b_dw', 'a_norm_g', 'a_norm_b', 'a_w_out', 'b_w_in', 'b_q_norm_g', 'b_w_qb', 'b_w_out', 'kv_w_a', 'kv_norm_g', 'kv_w_b']
TWIN_DIFF_INPUT = 'x'
TWIN_INPUTS = ['x', 'c', 'w_ada', 'b_ada', 'ln_g', 'ln_b', 'a_w_in', 'a_w_dw', 'a_b_dw', 'a_norm_g', 'a_norm_b', 'a_w_out', 'b_w_in', 'b_q_norm_g', 'b_w_qb', 'b_w_out', 'kv_w_a', 'kv_norm_g', 'kv_w_b', 'loss_target', 'm_w_ada', 'm_b_ada', 'm_ln_g', 'm_ln_b', 'm_a_w_in', 'm_a_w_dw', 'm_a_b_dw', 'm_a_norm_g', 'm_a_norm_b', 'm_a_w_out', 'm_b_w_in', 'm_b_q_norm_g', 'm_b_w_qb', 'm_b_w_out', 'm_kv_w_a', 'm_kv_norm_g', 'm_kv_w_b', 'v_w_ada', 'v_b_ada', 'v_ln_g', 'v_ln_b', 'v_a_w_in', 'v_a_w_dw', 'v_a_b_dw', 'v_a_norm_g', 'v_a_norm_b', 'v_a_w_out', 'v_b_w_in', 'v_b_q_norm_g', 'v_b_w_qb', 'v_b_w_out', 'v_kv_w_a', 'v_kv_norm_g', 'v_kv_w_b']
TWIN_OUTPUTS = ['loss', 'grad_x', 'grad_w_ada', 'grad_b_ada', 'grad_ln_g', 'grad_ln_b', 'grad_a_w_in', 'grad_a_w_dw', 'grad_a_b_dw', 'grad_a_norm_g', 'grad_a_norm_b', 'grad_a_w_out', 'grad_b_w_in', 'grad_b_q_norm_g', 'grad_b_w_qb', 'grad_b_w_out', 'grad_kv_w_a', 'grad_kv_norm_g', 'grad_kv_w_b', 'delta_w_ada', 'delta_b_ada', 'delta_ln_g', 'delta_ln_b', 'delta_a_w_in', 'delta_a_w_dw', 'delta_a_b_dw', 'delta_a_norm_g', 'delta_a_norm_b', 'delta_a_w_out', 'delta_b_w_in', 'delta_b_q_norm_g', 'delta_b_w_qb', 'delta_b_w_out', 'delta_kv_w_a', 'delta_kv_norm_g', 'delta_kv_w_b', 'new_m_w_ada', 'new_m_b_ada', 'new_m_ln_g', 'new_m_ln_b', 'new_m_a_w_in', 'new_m_a_w_dw', 'new_m_a_b_dw', 'new_m_a_norm_g', 'new_m_a_norm_b', 'new_m_a_w_out', 'new_m_b_w_in', 'new_m_b_q_norm_g', 'new_m_b_w_qb', 'new_m_b_w_out', 'new_m_kv_w_a', 'new_m_kv_norm_g', 'new_m_kv_w_b', 'new_v_w_ada', 'new_v_b_ada', 'new_v_ln_g', 'new_v_ln_b', 'new_v_a_w_in', 'new_v_a_w_dw', 'new_v_a_b_dw', 'new_v_a_norm_g', 'new_v_a_norm_b', 'new_v_a_w_out', 'new_v_b_w_in', 'new_v_b_q_norm_g', 'new_v_b_w_qb', 'new_v_b_w_out', 'new_v_kv_w_a', 'new_v_kv_norm_g', 'new_v_kv_w_b']
TWIN_LEAF_KINDS = {'loss': 'loss', 'grad_x': 'grad_x', 'grad_w_ada': 'grad_w', 'grad_b_ada': 'grad_w', 'grad_ln_g': 'grad_w', 'grad_ln_b': 'grad_w', 'grad_a_w_in': 'grad_w', 'grad_a_w_dw': 'grad_w', 'grad_a_b_dw': 'grad_w', 'grad_a_norm_g': 'grad_w', 'grad_a_norm_b': 'grad_w', 'grad_a_w_out': 'grad_w', 'grad_b_w_in': 'grad_w', 'grad_b_q_norm_g': 'grad_w', 'grad_b_w_qb': 'grad_w', 'grad_b_w_out': 'grad_w', 'grad_kv_w_a': 'grad_w', 'grad_kv_norm_g': 'grad_w', 'grad_kv_w_b': 'grad_w', 'delta_w_ada': 'delta_w', 'delta_b_ada': 'delta_w', 'delta_ln_g': 'delta_w', 'delta_ln_b': 'delta_w', 'delta_a_w_in': 'delta_w', 'delta_a_w_dw': 'delta_w', 'delta_a_b_dw': 'delta_w', 'delta_a_norm_g': 'delta_w', 'delta_a_norm_b': 'delta_w', 'delta_a_w_out': 'delta_w', 'delta_b_w_in': 'delta_w', 'delta_b_q_norm_g': 'delta_w', 'delta_b_w_qb': 'delta_w', 'delta_b_w_out': 'delta_w', 'delta_kv_w_a': 'delta_w', 'delta_kv_norm_g': 'delta_w', 'delta_kv_w_b': 'delta_w', 'new_m_w_ada': 'new_m', 'new_m_b_ada': 'new_m', 'new_m_ln_g': 'new_m', 'new_m_ln_b': 'new_m', 'new_m_a_w_in': 'new_m', 'new_m_a_w_dw': 'new_m', 'new_m_a_b_dw': 'new_m', 'new_m_a_norm_g': 'new_m', 'new_m_a_norm_b': 'new_m', 'new_m_a_w_out': 'new_m', 'new_m_b_w_in': 'new_m', 'new_m_b_q_norm_g': 'new_m', 'new_m_b_w_qb': 'new_m', 'new_m_b_w_out': 'new_m', 'new_m_kv_w_a': 'new_m', 'new_m_kv_norm_g': 'new_m', 'new_m_kv_w_b': 'new_m', 'new_v_w_ada': 'new_v', 'new_v_b_ada': 'new_v', 'new_v_ln_g': 'new_v', 'new_v_ln_b': 'new_v', 'new_v_a_w_in': 'new_v', 'new_v_a_w_dw': 'new_v', 'new_v_a_b_dw': 'new_v', 'new_v_a_norm_g': 'new_v', 'new_v_a_norm_b': 'new_v', 'new_v_a_w_out': 'new_v', 'new_v_b_w_in': 'new_v', 'new_v_b_q_norm_g': 'new_v', 'new_v_b_w_qb': 'new_v', 'new_v_b_w_out': 'new_v', 'new_v_kv_w_a': 'new_v', 'new_v_kv_norm_g': 'new_v', 'new_v_kv_w_b': 'new_v'}


def _forward(args):
    return _fwd_reference(*[args[k] for k in FWD_PARAMS])


def _output_shape():
    out = _jax.eval_shape(lambda: _forward(_fwd_setup_inputs(0)))
    return out.shape, out.dtype

N_MICROBATCH = 1
ADAM_LR = 0.001
ADAM_B1 = 0.9
ADAM_B2 = 0.999
ADAM_EPS = 1e-08
ADAM_WD = 0.01
ADAM_STEP = 10
PER_EXAMPLE_BATCH_AXIS = {'x': 0, 'c': 0, 'loss_target': 0}
SHARED_INPUTS = []
_WEIGHT_DTYPES = {'w_ada': _jnp.float32, 'b_ada': _jnp.float32, 'ln_g': _jnp.float32, 'ln_b': _jnp.float32, 'a_w_in': _jnp.float32, 'a_w_dw': _jnp.float32, 'a_b_dw': _jnp.float32, 'a_norm_g': _jnp.float32, 'a_norm_b': _jnp.float32, 'a_w_out': _jnp.float32, 'b_w_in': _jnp.float32, 'b_q_norm_g': _jnp.float32, 'b_w_qb': _jnp.float32, 'b_w_out': _jnp.float32, 'kv_w_a': _jnp.float32, 'kv_norm_g': _jnp.float32, 'kv_w_b': _jnp.float32}
MOMENT_SCALE = {'w_ada': 4.383580e-03, 'b_ada': 7.676342e-03, 'ln_g': 5.656516e+00, 'ln_b': 2.235185e-01, 'a_w_in': 8.100263e-03, 'a_w_dw': 9.526266e-03, 'a_b_dw': 1.492400e-02, 'a_norm_g': 1.212109e-02, 'a_norm_b': 1.081340e-02, 'a_w_out': 1.901369e-02, 'b_w_in': 2.851411e-03, 'b_q_norm_g': 5.336029e-03, 'b_w_qb': 1.506003e-03, 'b_w_out': 6.576661e-03, 'kv_w_a': 1.104195e-02, 'kv_norm_g': 1.144053e-02, 'kv_w_b': 1.968445e-03}


def _to_microbatches(a, axis):
    t = _jnp.moveaxis(a, axis, 0)
    t = t.reshape((N_MICROBATCH, t.shape[0] // N_MICROBATCH) + t.shape[1:])
    return _jnp.moveaxis(t, 1, axis + 1)


def setup_inputs(seed: int = 0) -> dict:
    inp = _fwd_setup_inputs(seed)
    key = _jax.random.fold_in(_jax.random.key(seed), 7919)
    shape, _ = _output_shape()
    out = dict(inp)
    out["loss_target"] = _jax.random.normal(_jax.random.fold_in(key, 0), shape, _jnp.float32)
    for i, name in enumerate(TWIN_WEIGHTS):
        w = inp[name].astype(_jnp.float32)
        if MOMENT_SCALE is None:
            s = _jnp.sqrt(_jnp.mean(_jnp.square(w)) + 1e-30)
        else:
            s = MOMENT_SCALE[name]
        km, kv = _jax.random.split(_jax.random.fold_in(key, i + 1))
        out[name] = w
        out["m_" + name] = s * _jax.random.normal(km, w.shape, _jnp.float32)
        out["v_" + name] = (s * s) * _jax.random.uniform(kv, w.shape, _jnp.float32, 0.5, 1.5)
    if N_MICROBATCH > 1:
        for name, axis in PER_EXAMPLE_BATCH_AXIS.items():
            out[name] = _to_microbatches(out[name], axis)
    return {'x': out['x'], 'c': out['c'], 'w_ada': out['w_ada'], 'b_ada': out['b_ada'], 'ln_g': out['ln_g'], 'ln_b': out['ln_b'], 'a_w_in': out['a_w_in'], 'a_w_dw': out['a_w_dw'], 'a_b_dw': out['a_b_dw'], 'a_norm_g': out['a_norm_g'], 'a_norm_b': out['a_norm_b'], 'a_w_out': out['a_w_out'], 'b_w_in': out['b_w_in'], 'b_q_norm_g': out['b_q_norm_g'], 'b_w_qb': out['b_w_qb'], 'b_w_out': out['b_w_out'], 'kv_w_a': out['kv_w_a'], 'kv_norm_g': out['kv_norm_g'], 'kv_w_b': out['kv_w_b'], 'loss_target': out['loss_target'], 'm_w_ada': out['m_w_ada'], 'm_b_ada': out['m_b_ada'], 'm_ln_g': out['m_ln_g'], 'm_ln_b': out['m_ln_b'], 'm_a_w_in': out['m_a_w_in'], 'm_a_w_dw': out['m_a_w_dw'], 'm_a_b_dw': out['m_a_b_dw'], 'm_a_norm_g': out['m_a_norm_g'], 'm_a_norm_b': out['m_a_norm_b'], 'm_a_w_out': out['m_a_w_out'], 'm_b_w_in': out['m_b_w_in'], 'm_b_q_norm_g': out['m_b_q_norm_g'], 'm_b_w_qb': out['m_b_w_qb'], 'm_b_w_out': out['m_b_w_out'], 'm_kv_w_a': out['m_kv_w_a'], 'm_kv_norm_g': out['m_kv_norm_g'], 'm_kv_w_b': out['m_kv_w_b'], 'v_w_ada': out['v_w_ada'], 'v_b_ada': out['v_b_ada'], 'v_ln_g': out['v_ln_g'], 'v_ln_b': out['v_ln_b'], 'v_a_w_in': out['v_a_w_in'], 'v_a_w_dw': out['v_a_w_dw'], 'v_a_b_dw': out['v_a_b_dw'], 'v_a_norm_g': out['v_a_norm_g'], 'v_a_norm_b': out['v_a_norm_b'], 'v_a_w_out': out['v_a_w_out'], 'v_b_w_in': out['v_b_w_in'], 'v_b_q_norm_g': out['v_b_q_norm_g'], 'v_b_w_qb': out['v_b_w_qb'], 'v_b_w_out': out['v_b_w_out'], 'v_kv_w_a': out['v_kv_w_a'], 'v_kv_norm_g': out['v_kv_norm_g'], 'v_kv_w_b': out['v_kv_w_b']}


def _loss(weights, diff, rest, loss_target):
    with _jax.named_scope("forward"):
        args = {**rest, TWIN_DIFF_INPUT: diff, **{k: w.astype(_WEIGHT_DTYPES[k]) for k, w in weights.items()}}
        y = _forward(args)
    with _jax.named_scope("loss_head"):
        err = _jnp.square(y.astype(_jnp.float32) - loss_target)
        return 0.5 * _jnp.sum(_jnp.mean(err, axis=-1)) if err.ndim else 0.5 * err


def _adamw(w, g, m, v):
    m = ADAM_B1 * m + (1.0 - ADAM_B1) * g
    v = ADAM_B2 * v + (1.0 - ADAM_B2) * _jnp.square(g)
    m_hat = m / (1.0 - ADAM_B1 ** ADAM_STEP)
    v_hat = v / (1.0 - ADAM_B2 ** ADAM_STEP)
    delta = -ADAM_LR * (m_hat / (_jnp.sqrt(v_hat) + ADAM_EPS) + ADAM_WD * w)
    return delta, m, v


def reference(x, c, w_ada, b_ada, ln_g, ln_b, a_w_in, a_w_dw, a_b_dw, a_norm_g, a_norm_b, a_w_out, b_w_in, b_q_norm_g, b_w_qb, b_w_out, kv_w_a, kv_norm_g, kv_w_b, loss_target, m_w_ada, m_b_ada, m_ln_g, m_ln_b, m_a_w_in, m_a_w_dw, m_a_b_dw, m_a_norm_g, m_a_norm_b, m_a_w_out, m_b_w_in, m_b_q_norm_g, m_b_w_qb, m_b_w_out, m_kv_w_a, m_kv_norm_g, m_kv_w_b, v_w_ada, v_b_ada, v_ln_g, v_ln_b, v_a_w_in, v_a_w_dw, v_a_b_dw, v_a_norm_g, v_a_norm_b, v_a_w_out, v_b_w_in, v_b_q_norm_g, v_b_w_qb, v_b_w_out, v_kv_w_a, v_kv_norm_g, v_kv_w_b):
    given = dict(x=x, c=c, w_ada=w_ada, b_ada=b_ada, ln_g=ln_g, ln_b=ln_b, a_w_in=a_w_in, a_w_dw=a_w_dw, a_b_dw=a_b_dw, a_norm_g=a_norm_g, a_norm_b=a_norm_b, a_w_out=a_w_out, b_w_in=b_w_in, b_q_norm_g=b_q_norm_g, b_w_qb=b_w_qb, b_w_out=b_w_out, kv_w_a=kv_w_a, kv_norm_g=kv_norm_g, kv_w_b=kv_w_b, loss_target=loss_target, m_w_ada=m_w_ada, m_b_ada=m_b_ada, m_ln_g=m_ln_g, m_ln_b=m_ln_b, m_a_w_in=m_a_w_in, m_a_w_dw=m_a_w_dw, m_a_b_dw=m_a_b_dw, m_a_norm_g=m_a_norm_g, m_a_norm_b=m_a_norm_b, m_a_w_out=m_a_w_out, m_b_w_in=m_b_w_in, m_b_q_norm_g=m_b_q_norm_g, m_b_w_qb=m_b_w_qb, m_b_w_out=m_b_w_out, m_kv_w_a=m_kv_w_a, m_kv_norm_g=m_kv_norm_g, m_kv_w_b=m_kv_w_b, v_w_ada=v_w_ada, v_b_ada=v_b_ada, v_ln_g=v_ln_g, v_ln_b=v_ln_b, v_a_w_in=v_a_w_in, v_a_w_dw=v_a_w_dw, v_a_b_dw=v_a_b_dw, v_a_norm_g=v_a_norm_g, v_a_norm_b=v_a_norm_b, v_a_w_out=v_a_w_out, v_b_w_in=v_b_w_in, v_b_q_norm_g=v_b_q_norm_g, v_b_w_qb=v_b_w_qb, v_b_w_out=v_b_w_out, v_kv_w_a=v_kv_w_a, v_kv_norm_g=v_kv_norm_g, v_kv_w_b=v_kv_w_b)
    weights = {n: given[n] for n in TWIN_WEIGHTS}
    shared = {n: given[n] for n in SHARED_INPUTS}
    per_example = {n: given[n] for n in ['x', 'c']}
    grad_fn = _jax.value_and_grad(_loss, argnums=(0, 1))

    def one_microbatch(ex, loss_target):
        ex = dict(ex)
        diff = ex.pop(TWIN_DIFF_INPUT)
        return grad_fn(weights, diff, {**shared, **ex}, loss_target)

    if N_MICROBATCH == 1:
        loss, (grad_w, grad_x) = one_microbatch(per_example, given["loss_target"])
    else:
        def body(carry, xs):
            loss_sum, grad_sum = carry
            l_k, (gw_k, gx_k) = one_microbatch(xs[0], xs[1])
            with _jax.named_scope("update"):
                return (loss_sum + l_k, _jax.tree.map(_jnp.add, grad_sum, gw_k)), gx_k

        init = (_jnp.zeros((), _jnp.float32), _jax.tree.map(_jnp.zeros_like, weights))
        (loss, grad_w), grad_x = _jax.lax.scan(body, init, (per_example, given["loss_target"]))
    with _jax.named_scope("update"):
        delta_w, new_m, new_v = {}, {}, {}
        for n in TWIN_WEIGHTS:
            delta_w[n], new_m[n], new_v[n] = _adamw(weights[n], grad_w[n], given["m_" + n], given["v_" + n])
    return (loss, grad_x, *[grad_w[n] for n in TWIN_WEIGHTS], *[delta_w[n] for n in TWIN_WEIGHTS],
            *[new_m[n] for n in TWIN_WEIGHTS], *[new_v[n] for n in TWIN_WEIGHTS])
```

```python
import math

import jax
import jax.numpy as jnp
from jax import lax
from jax.experimental import pallas as pl
from jax.experimental.pallas import tpu as pltpu

F32 = jnp.float32
BF16 = jnp.bfloat16

LN_EPS = 1e-5
RMS_EPS = 1e-6
DEPTH = 2
DEEPNORM_ALPHA = (2.0 * DEPTH) ** 0.25
QK_NOPE_DIM = 128
QK_ROPE_DIM = 64
V_HEAD_DIM = 128
HEAD_PAD = 256
ROPE_BASE = 10000.0
ADAM_LR = 0.001
ADAM_B1 = 0.9
ADAM_B2 = 0.999
ADAM_EPS = 1e-08
ADAM_WD = 0.01
ADAM_STEP = 10

N_DEV = 8
N_CHIP = 4
LANE = 128
SUBLANE = 8
VMEM_LIMIT = 48 * 1024 * 1024
CONV_HALO = 32
MESH_ID = pl.DeviceIdType.MESH
NT = (((1,), (1,)), ((), ()))


def _cp(*sem):
    return pltpu.CompilerParams(dimension_semantics=sem, vmem_limit_bytes=VMEM_LIMIT)


def _tile(n, pref, align):
    if n <= pref:
        return n
    t = (pref // align) * align
    while t > align and n % t:
        t -= align
    assert n % t == 0, (n, pref, align)
    return t


def _silu(v):
    return v * jax.nn.sigmoid(v)


def _dsilu(v):
    s = jax.nn.sigmoid(v)
    return s * (1.0 + v * (1.0 - s))


def _mm(a, b, *, name, ta=False, tb=False, b_sh=False, o_sh=False, out_dtype=F32, tm=1024, tn=1024, tk=512):
    M, K = (a.shape[1], a.shape[0]) if ta else a.shape
    if b_sh:
        assert b.shape[0] == N_CHIP
        nq = b.shape[2]
        Kb, N = (nq * N_CHIP, b.shape[1]) if tb else (b.shape[1], nq * N_CHIP)
    else:
        Kb, N = (b.shape[1], b.shape[0]) if tb else b.shape
        nq = N // N_CHIP
    assert K == Kb, (a.shape, b.shape)
    tm = _tile(M, tm, LANE)
    tk = _tile(nq if (b_sh and tb) else K, tk, LANE)
    tn = _tile(nq if ((b_sh and not tb) or o_sh) else N, tn, LANE)
    nk = K // tk

    def body(a_ref, b_ref, o_ref, acc_ref):
        k = pl.program_id(2)

        @pl.when(k == 0)
        def _():
            acc_ref[...] = jnp.zeros_like(acc_ref)

        dn = (((0 if ta else 1,), (1 if tb else 0,)), ((), ()))
        acc_ref[...] += lax.dot_general(a_ref[...].astype(BF16), b_ref[...].astype(BF16), dn,
                                        preferred_element_type=F32)

        @pl.when(k == nk - 1)
        def _():
            o_ref[...] = acc_ref[...].astype(o_ref.dtype)

    a_spec = pl.BlockSpec((tk, tm), lambda i, j, k: (k, i)) if ta else pl.BlockSpec((tm, tk), lambda i, j, k: (i, k))
    if b_sh and not tb:
        per = nq // tn
        b_spec = pl.BlockSpec((None, tk, tn), lambda i, j, k: (j // per, k, j % per))
    elif b_sh and tb:
        per = nq // tk
        b_spec = pl.BlockSpec((None, tn, tk), lambda i, j, k: (k // per, j, k % per))
    elif tb:
        b_spec = pl.BlockSpec((tn, tk), lambda i, j, k: (j, k))
    else:
        b_spec = pl.BlockSpec((tk, tn), lambda i, j, k: (k, j))
    if o_sh:
        per_o = nq // tn
        o_spec = pl.BlockSpec((None, tm, tn), lambda i, j, k: (j // per_o, i, j % per_o))
        o_shape = jax.ShapeDtypeStruct((N_CHIP, M, nq), out_dtype)
    else:
        o_spec = pl.BlockSpec((tm, tn), lambda i, j, k: (i, j))
        o_shape = jax.ShapeDtypeStruct((M, N), out_dtype)
    return pl.pallas_call(
        body, name=name, grid=(M // tm, N // tn, nk), in_specs=[a_spec, b_spec], out_specs=o_spec, out_shape=o_shape,
        scratch_shapes=[pltpu.VMEM((tm, tn), F32)], compiler_params=_cp("parallel", "parallel", "arbitrary"),
    )(a, b)


def _mod(c_all, w_ada, b_sh, name):
    L, D, nq = w_ada.shape
    B = c_all.shape[0]
    tn = _tile(nq, 512, LANE)

    def body(c_ref, w_ref, b_ref, o_ref):
        sc = _silu(c_ref[...]).astype(BF16)
        o_ref[...] = jnp.dot(sc, w_ref[...].astype(BF16), preferred_element_type=F32) + b_ref[...]

    return pl.pallas_call(
        body, name=name, grid=(L, nq // tn),
        in_specs=[pl.BlockSpec((B, D), lambda l, j: (0, 0)), pl.BlockSpec((None, D, tn), lambda l, j: (l, 0, j)),
                  pl.BlockSpec((None, 1, tn), lambda l, j: (l, 0, j))],
        out_specs=pl.BlockSpec((None, B, tn), lambda l, j: (l, 0, j)),
        out_shape=jax.ShapeDtypeStruct((L, B, nq), F32), compiler_params=_cp("parallel", "parallel"),
    )(c_all, w_ada, b_sh)


def _wada_grad(c_all_t, dmod, name):
    D, B = c_all_t.shape
    L, _, nq = dmod.shape
    tm = _tile(D, 512, SUBLANE)
    tn = _tile(nq, 1024, LANE)

    def body(c_ref, d_ref, o_ref):
        sc = _silu(c_ref[...])
        dm = d_ref[...]
        acc = sc[:, 0:1] * dm[0:1, :]
        for b in range(1, B):
            acc = acc + sc[:, b:b + 1] * dm[b:b + 1, :]
        o_ref[...] = acc

    return pl.pallas_call(
        body, name=name, grid=(L, D // tm, nq // tn),
        in_specs=[pl.BlockSpec((tm, B), lambda l, i, j: (i, 0)), pl.BlockSpec((None, B, tn), lambda l, i, j: (l, 0, j))],
        out_specs=pl.BlockSpec((None, tm, tn), lambda l, i, j: (l, i, j)),
        out_shape=jax.ShapeDtypeStruct((L, D, nq), F32), compiler_params=_cp("parallel", "parallel", "parallel"),
    )(c_all_t, dmod)


ROW_TILE = 128


def _ln_stats(v):
    mu = jnp.mean(v, axis=-1, keepdims=True)
    vc = v - mu
    var = jnp.mean(vc * vc, axis=-1, keepdims=True)
    rstd = lax.rsqrt(var + LN_EPS)
    return vc * rstd, rstd


def _ln_bwd(dxhat, xhat, rstd):
    m1 = jnp.mean(dxhat, axis=-1, keepdims=True)
    m2 = jnp.mean(dxhat * xhat, axis=-1, keepdims=True)
    return rstd * (dxhat - m1 - xhat * m2)


def _row_spec(ts, D):
    return pl.BlockSpec((ts, D), lambda i: (i, 0))


def _vec_spec(D):
    return pl.BlockSpec((1, D), lambda i: (0, 0))


def _lnmod_fwd(x, scale, shift, name):
    S, D = x.shape
    ts = _tile(S, ROW_TILE, SUBLANE)

    def body(x_ref, sc_ref, sh_ref, h_ref):
        xn, _ = _ln_stats(x_ref[...])
        h_ref[...] = (xn * (1.0 + sc_ref[...]) + sh_ref[...]).astype(h_ref.dtype)

    return pl.pallas_call(
        body, name=name, grid=(S // ts,), in_specs=[_row_spec(ts, D), _vec_spec(D), _vec_spec(D)],
        out_specs=_row_spec(ts, D), out_shape=jax.ShapeDtypeStruct((S, D), BF16), compiler_params=_cp("parallel"),
    )(x, scale, shift)


def _lnmod_bwd(x, dh, scale, adds, name):
    S, D = x.shape
    ts = _tile(S, ROW_TILE, SUBLANE)
    na = len(adds)

    def body(*refs):
        x_ref, dh_ref, sc_ref = refs[:3]
        add_refs = refs[3:3 + na]
        dx_ref, dsc_ref, dsh_ref = refs[3 + na:]
        i = pl.program_id(0)

        @pl.when(i == 0)
        def _():
            dsc_ref[...] = jnp.zeros_like(dsc_ref)
            dsh_ref[...] = jnp.zeros_like(dsh_ref)

        xn, rstd = _ln_stats(x_ref[...])
        dh = dh_ref[...].astype(F32)
        dx = _ln_bwd(dh * (1.0 + sc_ref[...]), xn, rstd)
        for r in add_refs:
            dx = dx + r[...]
        dx_ref[...] = dx
        dsc_ref[...] += jnp.sum(dh * xn, axis=0, keepdims=True)
        dsh_ref[...] += jnp.sum(dh, axis=0, keepdims=True)

    return pl.pallas_call(
        body, name=name, grid=(S // ts,),
        in_specs=[_row_spec(ts, D), _row_spec(ts, D), _vec_spec(D)] + [_row_spec(ts, D)] * na,
        out_specs=[_row_spec(ts, D), _vec_spec(D), _vec_spec(D)],
        out_shape=[jax.ShapeDtypeStruct((S, D), F32), jax.ShapeDtypeStruct((1, D), F32), jax.ShapeDtypeStruct((1, D), F32)],
        compiler_params=_cp("arbitrary"),
    )(x, dh, scale, *adds)


def _resln_fwd(x, out, gate, g, b, name):
    S, D = x.shape
    ts = _tile(S, ROW_TILE, SUBLANE)

    def body(x_ref, o_ref, gt_ref, g_ref, b_ref, y_ref):
        r = DEEPNORM_ALPHA * x_ref[...] + (1.0 + gt_ref[...]) * o_ref[...]
        xhat, _ = _ln_stats(r)
        y_ref[...] = xhat * g_ref[...] + b_ref[...]

    return pl.pallas_call(
        body, name=name, grid=(S // ts,),
        in_specs=[_row_spec(ts, D), _row_spec(ts, D), _vec_spec(D), _vec_spec(D), _vec_spec(D)],
        out_specs=_row_spec(ts, D), out_shape=jax.ShapeDtypeStruct((S, D), F32), compiler_params=_cp("parallel"),
    )(x, out, gate, g, b)


def _resln_bwd(x, out, gate, g, b, dy_or_target, from_target, name):
    S, D = x.shape
    ts = _tile(S, ROW_TILE, SUBLANE)

    def body(x_ref, o_ref, gt_ref, g_ref, b_ref, t_ref, dxa_ref, dout_ref, dgt_ref, dg_ref, db_ref, *maybe_loss):
        i = pl.program_id(0)

        @pl.when(i == 0)
        def _():
            dgt_ref[...] = jnp.zeros_like(dgt_ref)
            dg_ref[...] = jnp.zeros_like(dg_ref)
            db_ref[...] = jnp.zeros_like(db_ref)
            if from_target:
                maybe_loss[0][...] = jnp.zeros_like(maybe_loss[0])

        ov = o_ref[...]
        g1 = 1.0 + gt_ref[...]
        r = DEEPNORM_ALPHA * x_ref[...] + g1 * ov
        xhat, rstd = _ln_stats(r)
        if from_target:
            err = xhat * g_ref[...] + b_ref[...] - t_ref[...]
            dy = err * (1.0 / D)
            maybe_loss[0][...] += jnp.sum(jnp.sum(err * err, axis=-1, keepdims=True), axis=0, keepdims=True) * (0.5 / D)
        else:
            dy = t_ref[...]
        dr = _ln_bwd(dy * g_ref[...], xhat, rstd)
        dxa_ref[...] = DEEPNORM_ALPHA * dr
        dout_ref[...] = (dr * g1).astype(dout_ref.dtype)
        dgt_ref[...] += jnp.sum(dr * ov, axis=0, keepdims=True)
        dg_ref[...] += jnp.sum(dy * xhat, axis=0, keepdims=True)
        db_ref[...] += jnp.sum(dy, axis=0, keepdims=True)

    vec = jax.ShapeDtypeStruct((1, D), F32)
    out_specs = [_row_spec(ts, D), _row_spec(ts, D), _vec_spec(D), _vec_spec(D), _vec_spec(D)]
    out_shape = [jax.ShapeDtypeStruct((S, D), F32), jax.ShapeDtypeStruct((S, D), BF16), vec, vec, vec]
    if from_target:
        out_specs.append(pl.BlockSpec((1, 1), lambda i: (0, 0)))
        out_shape.append(jax.ShapeDtypeStruct((1, 1), F32))
    return pl.pallas_call(
        body, name=name, grid=(S // ts,),
        in_specs=[_row_spec(ts, D), _row_spec(ts, D), _vec_spec(D), _vec_spec(D), _vec_spec(D), _row_spec(ts, D)],
        out_specs=out_specs, out_shape=out_shape, compiler_params=_cp("arbitrary"),
    )(x, out, gate, g, b, dy_or_target)


def _conv_tiles(S, C):
    tt = _tile(S, 256, CONV_HALO)
    tc = _tile(C, 512, LANE)
    return tt, tc


def _conv1_fwd(u, w_dw, b_dw, name):
    S, C3 = u.shape
    C = C3 // 3
    KS = w_dw.shape[0]
    tt, tc = _conv_tiles(S, C)
    ncb = C // tc
    hb = tt // CONV_HALO
    lead = CONV_HALO - (KS - 1)

    def body(a_ref, g_ref, ah_ref, gh_ref, w_ref, b_ref, o_ref, pad_ref):
        i = pl.program_id(0)
        halo = ah_ref[...] * jax.nn.sigmoid(gh_ref[...])
        pad_ref[0:CONV_HALO, :] = jnp.where(i > 0, halo, 0.0)
        pad_ref[CONV_HALO:, :] = a_ref[...] * jax.nn.sigmoid(g_ref[...])
        acc = jnp.broadcast_to(b_ref[...], (tt, tc))
        for k in range(KS):
            acc = acc + w_ref[k:k + 1, :] * pad_ref[lead + k:lead + k + tt, :]
        o_ref[...] = acc

    main = lambda off: pl.BlockSpec((tt, tc), lambda i, j: (i, off + j))
    halo = lambda off: pl.BlockSpec((CONV_HALO, tc), lambda i, j: (jnp.maximum(i * hb - 1, 0), off + j))
    return pl.pallas_call(
        body, name=name, grid=(S // tt, ncb),
        in_specs=[main(0), main(ncb), halo(0), halo(ncb), pl.BlockSpec((KS, tc), lambda i, j: (0, j)),
                  pl.BlockSpec((1, tc), lambda i, j: (0, j))],
        out_specs=pl.BlockSpec((tt, tc), lambda i, j: (i, j)), out_shape=jax.ShapeDtypeStruct((S, C), F32),
        scratch_shapes=[pltpu.VMEM((CONV_HALO + tt, tc), F32)], compiler_params=_cp("parallel", "parallel"),
    )(u, u, u, u, w_dw, b_dw)


def _conv2_fwd(v2, u, g_cn, b_cn, name):
    S, C = v2.shape
    ts = _tile(S, ROW_TILE, SUBLANE)

    def body(v_ref, z_ref, g_ref, b_ref, p_ref):
        xhat, _ = _ln_stats(v_ref[...])
        v3 = xhat * g_ref[...] + b_ref[...]
        p_ref[...] = (_silu(v3) * _silu(z_ref[...])).astype(p_ref.dtype)

    return pl.pallas_call(
        body, name=name, grid=(S // ts,),
        in_specs=[_row_spec(ts, C), pl.BlockSpec((ts, C), lambda i: (i, 2)), _vec_spec(C), _vec_spec(C)],
        out_specs=_row_spec(ts, C), out_shape=jax.ShapeDtypeStruct((S, C), BF16), compiler_params=_cp("parallel"),
    )(v2, u, g_cn, b_cn)


def _conv2_bwd(dp, v2, u, g_cn, b_cn, name):
    S, C = v2.shape
    ts = _tile(S, ROW_TILE, SUBLANE)

    def body(dp_ref, v_ref, z_ref, g_ref, b_ref, dv_ref, dz_ref, dg_ref, db_ref):
        i = pl.program_id(0)

        @pl.when(i == 0)
        def _():
            dg_ref[...] = jnp.zeros_like(dg_ref)
            db_ref[...] = jnp.zeros_like(db_ref)

        dp = dp_ref[...].astype(F32)
        z = z_ref[...]
        xhat, rstd = _ln_stats(v_ref[...])
        v3 = xhat * g_ref[...] + b_ref[...]
        dz_ref[...] = (dp * _silu(v3) * _dsilu(z)).astype(dz_ref.dtype)
        dv3 = dp * _silu(z) * _dsilu(v3)
        dv_ref[...] = _ln_bwd(dv3 * g_ref[...], xhat, rstd)
        dg_ref[...] += jnp.sum(dv3 * xhat, axis=0, keepdims=True)
        db_ref[...] += jnp.sum(dv3, axis=0, keepdims=True)

    vec = jax.ShapeDtypeStruct((1, C), F32)
    return pl.pallas_call(
        body, name=name, grid=(S // ts,),
        in_specs=[_row_spec(ts, C), _row_spec(ts, C), pl.BlockSpec((ts, C), lambda i: (i, 2)), _vec_spec(C), _vec_spec(C)],
        out_specs=[_row_spec(ts, C), _row_spec(ts, C), _vec_spec(C), _vec_spec(C)],
        out_shape=[jax.ShapeDtypeStruct((S, C), F32), jax.ShapeDtypeStruct((S, C), BF16), vec, vec],
        compiler_params=_cp("arbitrary"),
    )(dp, v2, u, g_cn, b_cn)


def _conv1_bwd(dv2, u, w_dw, name):
    S, C = dv2.shape
    KS = w_dw.shape[0]
    tt, tc = _conv_tiles(S, C)
    ncb = C // tc
    nt = S // tt
    hb = tt // CONV_HALO
    lead = CONV_HALO - (KS - 1)

    def body(dv_ref, dvh_ref, a_ref, g_ref, ah_ref, gh_ref, w_ref, da_ref, dg_ref, dw_ref, db_ref, pad_ref, fpad_ref):
        i = pl.program_id(1)

        @pl.when(i == 0)
        def _():
            dw_ref[...] = jnp.zeros_like(dw_ref)
            db_ref[...] = jnp.zeros_like(db_ref)

        dv = dv_ref[...]
        a = a_ref[...]
        sg = jax.nn.sigmoid(g_ref[...])
        halo = ah_ref[...] * jax.nn.sigmoid(gh_ref[...])
        pad_ref[0:CONV_HALO, :] = jnp.where(i > 0, halo, 0.0)
        pad_ref[CONV_HALO:, :] = a * sg
        fpad_ref[0:tt, :] = dv
        fpad_ref[tt:, :] = jnp.where(i < nt - 1, dvh_ref[...], 0.0)
        dv1 = jnp.zeros((tt, tc), F32)
        for k in range(KS):
            dv1 = dv1 + w_ref[k:k + 1, :] * fpad_ref[KS - 1 - k:KS - 1 - k + tt, :]
            dw_ref[k:k + 1, :] += jnp.sum(dv * pad_ref[lead + k:lead + k + tt, :], axis=0, keepdims=True)
        db_ref[...] += jnp.sum(dv, axis=0, keepdims=True)
        da_ref[...] = (dv1 * sg).astype(da_ref.dtype)
        dg_ref[...] = (dv1 * a * sg * (1.0 - sg)).astype(dg_ref.dtype)

    main = lambda off: pl.BlockSpec((tt, tc), lambda j, i: (i, off + j))
    halo = lambda off: pl.BlockSpec((CONV_HALO, tc), lambda j, i: (jnp.maximum(i * hb - 1, 0), off + j))
    fhalo = pl.BlockSpec((CONV_HALO, tc), lambda j, i: (jnp.minimum((i + 1) * hb, nt * hb - 1), j))
    return pl.pallas_call(
        body, name=name, grid=(ncb, nt),
        in_specs=[main(0), fhalo, main(0), main(ncb), halo(0), halo(ncb), pl.BlockSpec((KS, tc), lambda j, i: (0, j))],
        out_specs=[main(0), main(0), pl.BlockSpec((KS, tc), lambda j, i: (0, j)), pl.BlockSpec((1, tc), lambda j, i: (0, j))],
        out_shape=[jax.ShapeDtypeStruct((S, C), BF16), jax.ShapeDtypeStruct((S, C), BF16),
                   jax.ShapeDtypeStruct((KS, C), F32), jax.ShapeDtypeStruct((1, C), F32)],
        scratch_shapes=[pltpu.VMEM((CONV_HALO + tt, tc), F32), pltpu.VMEM((tt + CONV_HALO, tc), F32)],
        compiler_params=_cp("parallel", "arbitrary"),
    )(dv2, dv2, u, u, u, u, w_dw)


def _rope_tables(S):
    half = QK_ROPE_DIM // 2
    inv_freq = ROPE_BASE ** (-jnp.arange(half, dtype=F32) / half)
    ang = jnp.arange(S, dtype=jnp.int32).astype(F32)[:, None] * inv_freq[None, :]
    cos, sin, z = jnp.cos(ang), jnp.sin(ang), jnp.zeros((S, half), F32)
    tc = jnp.concatenate([cos, cos, z, z], axis=1)
    t1 = jnp.concatenate([-sin, z, z, z], axis=1)
    t2 = jnp.concatenate([z, sin, z, z], axis=1)
    return tc, t1, t2


def _rope128(r, tc, t1, t2, sign):
    return r * tc + sign * (pltpu.roll(r, LANE - QK_ROPE_DIM // 2, 1) * t1 + pltpu.roll(r, QK_ROPE_DIM // 2, 1) * t2)


def _rms_fwd(x, width, g, name):
    S = x.shape[0]
    ts = _tile(S, 256, SUBLANE)

    def body(x_ref, g_ref, o_ref):
        xv = x_ref[...]
        rr = lax.rsqrt(jnp.mean(xv * xv, axis=-1, keepdims=True) + RMS_EPS)
        o_ref[...] = (xv * rr * g_ref[...]).astype(o_ref.dtype)

    return pl.pallas_call(
        body, name=name, grid=(S // ts,), in_specs=[_row_spec(ts, width), _vec_spec(width)],
        out_specs=_row_spec(ts, width), out_shape=jax.ShapeDtypeStruct((S, width), BF16), compiler_params=_cp("parallel"),
    )(x, g)


def _rms_bwd_math(xv, dy, g):
    n = xv.shape[-1]
    rr = lax.rsqrt(jnp.mean(xv * xv, axis=-1, keepdims=True) + RMS_EPS)
    dyg = dy * g
    dx = rr * dyg - xv * (rr * rr * rr) * (jnp.sum(dyg * xv, axis=-1, keepdims=True) * (1.0 / n))
    dg = jnp.sum(dy * xv * rr, axis=0, keepdims=True)
    return dx, dg


def _rms_bwd(x, width, dy, g, du, name):
    S = x.shape[0]
    ts = _tile(S, 256, SUBLANE)

    def body(x_ref, dy_ref, g_ref, du_in, dx_ref, dg_ref):
        del du_in
        i = pl.program_id(0)

        @pl.when(i == 0)
        def _():
            dg_ref[...] = jnp.zeros_like(dg_ref)

        dx, dg = _rms_bwd_math(x_ref[...], dy_ref[...].astype(F32), g_ref[...])
        dx_ref[...] = dx.astype(dx_ref.dtype)
        dg_ref[...] += dg

    return pl.pallas_call(
        body, name=name, grid=(S // ts,),
        in_specs=[_row_spec(ts, width), _row_spec(ts, width), _vec_spec(width), pl.BlockSpec(memory_space=pl.ANY)],
        out_specs=[_row_spec(ts, width), _vec_spec(width)],
        out_shape=[jax.ShapeDtypeStruct(du.shape, du.dtype), jax.ShapeDtypeStruct((1, width), F32)],
        input_output_aliases={3: 0}, compiler_params=_cp("arbitrary"),
    )(x, dy, g, du)


def _kvprep_fwd(kva, g_kv, tabs, name):
    S, W = kva.shape
    KV = W - LANE
    ts = _tile(S, 256, SUBLANE)

    def body(x_ref, g_ref, tc_ref, t1_ref, t2_ref, c_ref, r_ref):
        xv = x_ref[:, 0:KV]
        rr = lax.rsqrt(jnp.mean(xv * xv, axis=-1, keepdims=True) + RMS_EPS)
        c_ref[...] = (xv * rr * g_ref[...]).astype(c_ref.dtype)
        r_ref[...] = _rope128(x_ref[:, KV:], tc_ref[...], t1_ref[...], t2_ref[...], 1.0).astype(r_ref.dtype)

    tab = _row_spec(ts, LANE)
    return pl.pallas_call(
        body, name=name, grid=(S // ts,), in_specs=[_row_spec(ts, W), _vec_spec(KV), tab, tab, tab],
        out_specs=[_row_spec(ts, KV), _row_spec(ts, LANE)],
        out_shape=[jax.ShapeDtypeStruct((S, KV), BF16), jax.ShapeDtypeStruct((S, LANE), BF16)], compiler_params=_cp("parallel"),
    )(kva, g_kv, *tabs)


def _kvprep_bwd(kva, dckv, dkr_h, g_kv, tabs, name):
    S, W = kva.shape
    KV = W - LANE
    H = dkr_h.shape[1] // LANE
    ts = _tile(S, 256, SUBLANE)

    def body(x_ref, dc_ref, dr_ref, g_ref, tc_ref, t1_ref, t2_ref, o_ref, dg_ref):
        i = pl.program_id(0)

        @pl.when(i == 0)
        def _():
            dg_ref[...] = jnp.zeros_like(dg_ref)

        dx, dg = _rms_bwd_math(x_ref[:, 0:KV], dc_ref[...].astype(F32), g_ref[...])
        o_ref[:, 0:KV] = dx.astype(o_ref.dtype)
        dg_ref[...] += dg
        dr = dr_ref[:, 0:LANE]
        for h in range(1, H):
            dr = dr + dr_ref[:, h * LANE:(h + 1) * LANE]
        o_ref[:, KV:] = _rope128(dr, tc_ref[...], t1_ref[...], t2_ref[...], -1.0).astype(o_ref.dtype)

    tab = _row_spec(ts, LANE)
    return pl.pallas_call(
        body, name=name, grid=(S // ts,),
        in_specs=[_row_spec(ts, W), _row_spec(ts, KV), _row_spec(ts, H * LANE), _vec_spec(KV), tab, tab, tab],
        out_specs=[_row_spec(ts, W), _vec_spec(KV)],
        out_shape=[jax.ShapeDtypeStruct((S, W), BF16), jax.ShapeDtypeStruct((1, KV), F32)], compiler_params=_cp("arbitrary"),
    )(kva, dckv, dkr_h, g_kv, *tabs)


def _qrope(q, tabs, sign, name):
    S, W = q.shape
    H = W // HEAD_PAD
    ts = _tile(S, 512, SUBLANE)

    def body(q_ref, tc_ref, t1_ref, t2_ref, o_ref):
        o_ref[:, 0:LANE] = q_ref[:, 0:LANE].astype(o_ref.dtype)
        o_ref[:, LANE:] = _rope128(q_ref[:, LANE:].astype(F32), tc_ref[...], t1_ref[...], t2_ref[...], sign).astype(o_ref.dtype)

    tab = pl.BlockSpec((ts, LANE), lambda i, h: (i, 0))
    return pl.pallas_call(
        body, name=name, grid=(S // ts, H), in_specs=[pl.BlockSpec((ts, HEAD_PAD), lambda i, h: (i, h)), tab, tab, tab],
        out_specs=pl.BlockSpec((ts, HEAD_PAD), lambda i, h: (i, h)), out_shape=jax.ShapeDtypeStruct((S, W), BF16),
        compiler_params=_cp("parallel", "parallel"),
    )(q, *tabs)


def _gate_fwd(o, u, zoff, name):
    S, W = o.shape
    ts = _tile(S, 256, SUBLANE)
    tw = _tile(math.gcd(W, zoff), 1024, LANE)
    assert zoff % tw == 0 and W % tw == 0
    zb = zoff // tw

    def body(o_ref, z_ref, p_ref):
        p_ref[...] = (o_ref[...] * _silu(z_ref[...])).astype(p_ref.dtype)

    return pl.pallas_call(
        body, name=name, grid=(S // ts, W // tw),
        in_specs=[pl.BlockSpec((ts, tw), lambda i, j: (i, j)), pl.BlockSpec((ts, tw), lambda i, j: (i, zb + j))],
        out_specs=pl.BlockSpec((ts, tw), lambda i, j: (i, j)), out_shape=jax.ShapeDtypeStruct((S, W), BF16),
        compiler_params=_cp("parallel", "parallel"),
    )(o, u)


def _gate_bwd(dp, o, u, zoff, name):
    S, W = o.shape
    H = W // V_HEAD_DIM
    U = u.shape[1]
    ts = _tile(S, 512, SUBLANE)
    zb = zoff // V_HEAD_DIM

    def body(dp_ref, o_ref, z_ref, do_ref, dz_ref, dl_ref):
        dp = dp_ref[...].astype(F32)
        ov = o_ref[...]
        z = z_ref[...]
        do = dp * _silu(z)
        do_ref[...] = do.astype(do_ref.dtype)
        dz_ref[...] = (dp * ov * _dsilu(z)).astype(dz_ref.dtype)
        dl_ref[...] = jnp.broadcast_to(jnp.sum(do * ov, axis=-1, keepdims=True), (ts, LANE))

    blk = pl.BlockSpec((ts, V_HEAD_DIM), lambda i, h: (i, h))
    zblk = pl.BlockSpec((ts, V_HEAD_DIM), lambda i, h: (i, zb + h))
    return pl.pallas_call(
        body, name=name, grid=(S // ts, H), in_specs=[blk, blk, zblk],
        out_specs=[blk, zblk, pl.BlockSpec((None, ts, LANE), lambda i, h: (h, i, 0))],
        out_shape=[jax.ShapeDtypeStruct((S, W), BF16), jax.ShapeDtypeStruct((S, U), BF16),
                   jax.ShapeDtypeStruct((H, S, LANE), F32)],
        compiler_params=_cp("parallel", "parallel"),
    )(dp, o, u)


ATT_TILE = 512


def _causal_mask(t, transposed):
    r = lax.broadcasted_iota(jnp.int32, (t, t), 0)
    c = lax.broadcasted_iota(jnp.int32, (t, t), 1)
    return (c >= r) if transposed else (r >= c)


def _attn_fwd(qr, kvh, krp, scale, name):
    S, W = qr.shape
    H = W // HEAD_PAD
    t = _tile(S, ATT_TILE, LANE)
    nq = S // t

    def body(q_ref, kv_ref, kr_ref, o_ref, lse_ref, m_sc, l_sc, acc_sc):
        i = pl.program_id(1)
        q = q_ref[...]
        m_sc[...] = jnp.full_like(m_sc, -jnp.inf)
        l_sc[...] = jnp.zeros_like(l_sc)
        acc_sc[...] = jnp.zeros_like(acc_sc)

        def step(j, masked):
            off = pl.multiple_of(j * t, t)
            kvb = kv_ref[pl.ds(off, t), :]
            k = jnp.concatenate([kvb[:, 0:QK_NOPE_DIM], kr_ref[pl.ds(off, t), :]], axis=-1)
            s = lax.dot_general(q, k, NT, preferred_element_type=F32) * scale
            if masked:
                s = jnp.where(_causal_mask(t, False), s, -jnp.inf)
            m_old = m_sc[...]
            m_new = jnp.maximum(m_old, jnp.max(s, axis=-1, keepdims=True))
            a = jnp.exp(m_old - m_new)
            p = jnp.exp(s - m_new)
            l_sc[...] = a * l_sc[...] + jnp.sum(p, axis=-1, keepdims=True)
            acc_sc[...] = a * acc_sc[...] + jnp.dot(p.astype(BF16), kvb[:, QK_NOPE_DIM:], preferred_element_type=F32)
            m_sc[...] = m_new

        def loop_body(j, carry):
            step(j, False)
            return carry

        lax.fori_loop(0, i, loop_body, 0)
        step(i, True)
        o_ref[...] = (acc_sc[...] / l_sc[...]).astype(o_ref.dtype)
        lse_ref[...] = jnp.broadcast_to(m_sc[...] + jnp.log(l_sc[...]), (t, LANE))

    return pl.pallas_call(
        body, name=name, grid=(H, nq),
        in_specs=[pl.BlockSpec((t, HEAD_PAD), lambda h, i: (i, h)), pl.BlockSpec((S, HEAD_PAD), lambda h, i: (0, h)),
                  pl.BlockSpec((S, LANE), lambda h, i: (0, 0))],
        out_specs=[pl.BlockSpec((t, V_HEAD_DIM), lambda h, i: (i, h)), pl.BlockSpec((None, t, LANE), lambda h, i: (h, i, 0))],
        out_shape=[jax.ShapeDtypeStruct((S, H * V_HEAD_DIM), F32), jax.ShapeDtypeStruct((H, S, LANE), F32)],
        scratch_shapes=[pltpu.VMEM((t, 1), F32), pltpu.VMEM((t, 1), F32), pltpu.VMEM((t, V_HEAD_DIM), F32)],
        compiler_params=_cp("parallel", "arbitrary"),
    )(qr, kvh, krp)


def _attn_dq(qr, kvh, krp, do, lse, delta, scale, name):
    S, W = qr.shape
    H = W // HEAD_PAD
    t = _tile(S, ATT_TILE, LANE)
    nq = S // t

    def body(q_ref, kv_ref, kr_ref, do_ref, lse_ref, dl_ref, dq_ref, acc_sc):
        i = pl.program_id(1)
        q = q_ref[...]
        dov = do_ref[...]
        lse_c = lse_ref[:, 0:1]
        dl_c = dl_ref[:, 0:1]
        acc_sc[...] = jnp.zeros_like(acc_sc)

        def step(j, masked):
            off = pl.multiple_of(j * t, t)
            kvb = kv_ref[pl.ds(off, t), :]
            k = jnp.concatenate([kvb[:, 0:QK_NOPE_DIM], kr_ref[pl.ds(off, t), :]], axis=-1)
            s = lax.dot_general(q, k, NT, preferred_element_type=F32) * scale
            p = jnp.exp(s - lse_c)
            if masked:
                p = jnp.where(_causal_mask(t, False), p, 0.0)
            dp = lax.dot_general(dov, kvb[:, QK_NOPE_DIM:], NT, preferred_element_type=F32)
            ds = p * (dp - dl_c) * scale
            acc_sc[...] += jnp.dot(ds.astype(BF16), k, preferred_element_type=F32)

        def loop_body(j, carry):
            step(j, False)
            return carry

        lax.fori_loop(0, i, loop_body, 0)
        step(i, True)
        dq_ref[...] = acc_sc[...].astype(dq_ref.dtype)

    col = pl.BlockSpec((None, t, LANE), lambda h, i: (h, i, 0))
    return pl.pallas_call(
        body, name=name, grid=(H, nq),
        in_specs=[pl.BlockSpec((t, HEAD_PAD), lambda h, i: (i, h)), pl.BlockSpec((S, HEAD_PAD), lambda h, i: (0, h)),
                  pl.BlockSpec((S, LANE), lambda h, i: (0, 0)), pl.BlockSpec((t, V_HEAD_DIM), lambda h, i: (i, h)), col, col],
        out_specs=pl.BlockSpec((t, HEAD_PAD), lambda h, i: (i, h)), out_shape=jax.ShapeDtypeStruct((S, W), F32),
        scratch_shapes=[pltpu.VMEM((t, HEAD_PAD), F32)], compiler_params=_cp("parallel", "arbitrary"),
    )(qr, kvh, krp, do, lse, delta)


def _attn_dkv(qr, kvh, krp, do, lse_r, delta_r, scale, name):
    S, W = qr.shape
    H = W // HEAD_PAD
    t = _tile(S, ATT_TILE, LANE)
    nq = S // t

    def body(q_ref, kv_ref, kr_ref, do_ref, lse_ref, dl_ref, dkv_ref, dkr_ref, dk_sc, dv_sc):
        j = pl.program_id(1)
        kvb = kv_ref[...]
        k = jnp.concatenate([kvb[:, 0:QK_NOPE_DIM], kr_ref[...]], axis=-1)
        v = kvb[:, QK_NOPE_DIM:]
        dk_sc[...] = jnp.zeros_like(dk_sc)
        dv_sc[...] = jnp.zeros_like(dv_sc)

        def step(i, masked):
            off = pl.multiple_of(i * t, t)
            q = q_ref[pl.ds(off, t), :]
            dov = do_ref[pl.ds(off, t), :]
            st = lax.dot_general(k, q, NT, preferred_element_type=F32) * scale
            pt = jnp.exp(st - lse_ref[pl.ds(i, 1), :])
            if masked:
                pt = jnp.where(_causal_mask(t, True), pt, 0.0)
            dv_sc[...] += jnp.dot(pt.astype(BF16), dov, preferred_element_type=F32)
            dpt = lax.dot_general(v, dov, NT, preferred_element_type=F32)
            dst = pt * (dpt - dl_ref[pl.ds(i, 1), :]) * scale
            dk_sc[...] += jnp.dot(dst.astype(BF16), q, preferred_element_type=F32)

        def loop_body(i, carry):
            step(i, False)
            return carry

        step(j, True)
        lax.fori_loop(j + 1, nq, loop_body, 0)
        dkv_ref[:, 0:QK_NOPE_DIM] = dk_sc[:, 0:QK_NOPE_DIM].astype(dkv_ref.dtype)
        dkv_ref[:, QK_NOPE_DIM:] = dv_sc[...].astype(dkv_ref.dtype)
        dkr_ref[...] = dk_sc[:, QK_NOPE_DIM:]

    row = pl.BlockSpec((None, nq, t), lambda h, j: (h, 0, 0))
    return pl.pallas_call(
        body, name=name, grid=(H, nq),
        in_specs=[pl.BlockSpec((S, HEAD_PAD), lambda h, j: (0, h)), pl.BlockSpec((t, HEAD_PAD), lambda h, j: (j, h)),
                  pl.BlockSpec((t, LANE), lambda h, j: (j, 0)), pl.BlockSpec((S, V_HEAD_DIM), lambda h, j: (0, h)), row, row],
        out_specs=[pl.BlockSpec((t, HEAD_PAD), lambda h, j: (j, h)), pl.BlockSpec((t, LANE), lambda h, j: (j, h))],
        out_shape=[jax.ShapeDtypeStruct((S, W), BF16), jax.ShapeDtypeStruct((S, H * LANE), F32)],
        scratch_shapes=[pltpu.VMEM((t, HEAD_PAD), F32), pltpu.VMEM((t, V_HEAD_DIM), F32)],
        compiler_params=_cp("parallel", "arbitrary"),
    )(qr, kvh, krp, do, lse_r, delta_r)


def _adamw_math(w, g, m, v):
    m = ADAM_B1 * m + (1.0 - ADAM_B1) * g
    v = ADAM_B2 * v + (1.0 - ADAM_B2) * (g * g)
    m_hat = m / (1.0 - ADAM_B1 ** ADAM_STEP)
    v_hat = v / (1.0 - ADAM_B2 ** ADAM_STEP)
    delta = -ADAM_LR * (m_hat / (jnp.sqrt(v_hat) + ADAM_EPS) + ADAM_WD * w)
    return delta, m, v


def _adamw(w, g, m, v, name):
    R, C = w.shape
    tr = _tile(R, 256, SUBLANE)
    tc = _tile(C, 1024, LANE)

    def body(w_ref, g_ref, m_ref, v_ref, d_ref, nm_ref, nv_ref):
        d, nm, nv = _adamw_math(w_ref[...], g_ref[...], m_ref[...], v_ref[...])
        d_ref[...] = d
        nm_ref[...] = nm
        nv_ref[...] = nv

    blk = pl.BlockSpec((tr, tc), lambda i, j: (i, j))
    sh = jax.ShapeDtypeStruct((R, C), F32)
    return pl.pallas_call(
        body, name=name, grid=(R // tr, C // tc), in_specs=[blk] * 4, out_specs=[blk] * 3, out_shape=[sh] * 3,
        compiler_params=_cp("parallel", "parallel"),
    )(w, g, m, v)


def _sum_leading(x, name):
    n, R, C = x.shape
    tr = _tile(R, 512, SUBLANE)

    def body(x_ref, o_ref):
        acc = x_ref[0]
        for k in range(1, n):
            acc = acc + x_ref[k]
        o_ref[...] = acc

    return pl.pallas_call(
        body, name=name, grid=(R // tr,), in_specs=[pl.BlockSpec((n, tr, C), lambda i: (0, i, 0))],
        out_specs=pl.BlockSpec((tr, C), lambda i: (i, 0)), out_shape=jax.ShapeDtypeStruct((R, C), F32),
        compiler_params=_cp("parallel"),
    )(x)


def _pair_add(full, recv, c_idx, name):
    n, R, C = full.shape
    h = R // 2
    tr = _tile(h, 256, 2 * SUBLANE)
    tc = _tile(C, 1024, LANE)
    nb = h // tr

    def body(c_ref, a_ref, b_ref, o_ref):
        del c_ref
        o_ref[...] = (a_ref[...].astype(F32) + b_ref[...].astype(F32)).astype(o_ref.dtype)

    return pl.pallas_call(
        body, name=name,
        grid_spec=pltpu.PrefetchScalarGridSpec(
            num_scalar_prefetch=1, grid=(n, nb, C // tc),
            in_specs=[pl.BlockSpec((None, tr, tc), lambda k, i, j, c: (k, c[0] * nb + i, j)),
                      pl.BlockSpec((None, tr, tc), lambda k, i, j, c: (k, i, j))],
            out_specs=pl.BlockSpec((None, tr, tc), lambda k, i, j, c: (k, i, j))),
        out_shape=jax.ShapeDtypeStruct((n, h, C), BF16), compiler_params=_cp("parallel", "parallel", "parallel"),
    )(c_idx, full, recv)


def _chip_sum(pair, recv, chip_idx, name):
    n, h, C = pair.shape
    tr = _tile(h, 256, 2 * SUBLANE)
    tc = _tile(C, 1024, LANE)

    def body(j_ref, a_ref, b_ref, o_ref):
        del j_ref
        acc = a_ref[...].astype(F32)
        for k in range(N_CHIP - 1):
            acc = acc + b_ref[k].astype(F32)
        o_ref[...] = acc

    return pl.pallas_call(
        body, name=name,
        grid_spec=pltpu.PrefetchScalarGridSpec(
            num_scalar_prefetch=1, grid=(h // tr, C // tc),
            in_specs=[pl.BlockSpec((None, tr, tc), lambda i, j, jc: (jc[0], i, j)),
                      pl.BlockSpec((N_CHIP - 1, tr, tc), lambda i, j, jc: (0, i, j))],
            out_specs=pl.BlockSpec((tr, tc), lambda i, j, jc: (i, j))),
        out_shape=jax.ShapeDtypeStruct((h, C), F32), compiler_params=_cp("parallel", "parallel"),
    )(chip_idx, pair, recv)


def _coords():
    return lax.axis_index("x"), lax.axis_index("y"), lax.axis_index("c")


def _allgather_small(x_shard, name):
    m_per, n = x_shard.shape

    def body(x_ref, out_ref, send_sems, recv_sems, local_sem):
        x, y, c = _coords()
        me, sibling = (x, y, c), (x, y, 1 - c)
        chips = [(1 - x, y), (x, 1 - y), (1 - x, 1 - y)]

        def rows(px, py, pc):
            return out_ref.at[pl.ds((4 * px + 2 * py + pc) * m_per, m_per), :]

        def copy(k, block, to, src=None):
            return pltpu.make_async_remote_copy(
                src_ref=rows(*block) if src is None else src, dst_ref=rows(*block), send_sem=send_sems.at[k],
                recv_sem=recv_sems.at[k], device_id=to, device_id_type=MESH_ID)

        mine = pltpu.make_async_copy(x_ref, rows(*me), local_sem)
        mine.start()
        first = [copy(0, me, sibling, src=x_ref)]
        first += [copy(1 + j, me, (*chip, c), src=x_ref) for j, chip in enumerate(chips)]
        for cp in first:
            cp.start()
        passed = [copy(4 + j, (*chip, c), sibling) for j, chip in enumerate(chips)]
        for j, chip in enumerate(chips):
            copy(1 + j, (*chip, c), me).wait_recv()
            passed[j].start()
        copy(0, sibling, me).wait_recv()
        for j, chip in enumerate(chips):
            copy(4 + j, (*chip, 1 - c), me).wait_recv()
        for cp in first + passed:
            cp.wait_send()
        mine.wait()

    return pl.pallas_call(
        body, name=name, out_shape=jax.ShapeDtypeStruct((N_DEV * m_per, n), x_shard.dtype),
        in_specs=[pl.BlockSpec(memory_space=pltpu.VMEM)], out_specs=pl.BlockSpec(memory_space=pltpu.VMEM),
        scratch_shapes=[pltpu.SemaphoreType.DMA((7,)), pltpu.SemaphoreType.DMA((7,)), pltpu.SemaphoreType.DMA],
        compiler_params=pltpu.CompilerParams(vmem_limit_bytes=VMEM_LIMIT),
    )(x_shard)


def _allgather_weights(shards, name):
    n = len(shards)

    def body(*refs):
        ins, outs = refs[:n], refs[n:2 * n]
        send_sems, recv_sems, local_sems = refs[2 * n:]
        x, y, c = _coords()
        me, sibling = (x, y, c), (x, y, 1 - c)
        chips = [(1 - x, y), (x, 1 - y), (1 - x, 1 - y)]

        def win(a, px, py, pc):
            h = shards[a].shape[0] // 2
            return outs[a].at[2 * px + py, pl.ds(pc * h, h), :]

        def own(a):
            h = shards[a].shape[0] // 2
            return ins[a].at[pl.ds(c * h, h), :]

        def copy(a, k, block, to, src=None):
            return pltpu.make_async_remote_copy(
                src_ref=win(a, *block) if src is None else src, dst_ref=win(a, *block), send_sem=send_sems.at[a, k],
                recv_sem=recv_sems.at[a, k], device_id=to, device_id_type=MESH_ID)

        mine = [pltpu.make_async_copy(own(a), win(a, *me), local_sems.at[a]) for a in range(n)]
        for cp in mine:
            cp.start()
        first = []
        for a in range(n):
            first.append(copy(a, 0, me, sibling, src=own(a)))
            first += [copy(a, 1 + j, me, (*chip, c), src=own(a)) for j, chip in enumerate(chips)]
        for cp in first:
            cp.start()
        passed = []
        for a in range(n):
            for j, chip in enumerate(chips):
                copy(a, 1 + j, (*chip, c), me).wait_recv()
                fwd = copy(a, 4 + j, (*chip, c), sibling)
                fwd.start()
                passed.append(fwd)
        for a in range(n):
            copy(a, 0, sibling, me).wait_recv()
            for j, chip in enumerate(chips):
                copy(a, 4 + j, (*chip, 1 - c), me).wait_recv()
        for cp in first + passed:
            cp.wait_send()
        for cp in mine:
            cp.wait()

    any_spec = pl.BlockSpec(memory_space=pl.ANY)
    return pl.pallas_call(
        body, name=name, out_shape=[jax.ShapeDtypeStruct((N_CHIP,) + s.shape, s.dtype) for s in shards],
        in_specs=[any_spec] * n, out_specs=[any_spec] * n,
        scratch_shapes=[pltpu.SemaphoreType.DMA((n, 7)), pltpu.SemaphoreType.DMA((n, 7)), pltpu.SemaphoreType.DMA((n,))],
    )(*shards)


def _pair_exchange(grads, name):
    n = len(grads)

    def body(*refs):
        ins, outs = refs[:n], refs[n:2 * n]
        send_sems, recv_sems = refs[2 * n:]
        x, y, c = _coords()
        sibling = (x, y, 1 - c)
        copies = []
        for a in range(n):
            h = grads[a].shape[1] // 2
            cp = pltpu.make_async_remote_copy(
                src_ref=ins[a].at[:, pl.ds((1 - c) * h, h), :], dst_ref=outs[a], send_sem=send_sems.at[a],
                recv_sem=recv_sems.at[a], device_id=sibling, device_id_type=MESH_ID)
            cp.start()
            copies.append(cp)
        for cp in copies:
            cp.wait()

    any_spec = pl.BlockSpec(memory_space=pl.ANY)
    return pl.pallas_call(
        body, name=name,
        out_shape=[jax.ShapeDtypeStruct((g.shape[0], g.shape[1] // 2, g.shape[2]), g.dtype) for g in grads],
        in_specs=[any_spec] * n, out_specs=[any_spec] * n,
        scratch_shapes=[pltpu.SemaphoreType.DMA((n,)), pltpu.SemaphoreType.DMA((n,))],
    )(*grads)


def _chip_exchange(pairs, name):
    n = len(pairs)

    def body(*refs):
        ins, outs = refs[:n], refs[n:2 * n]
        send_sems, recv_sems = refs[2 * n:]
        x, y, c = _coords()
        chips = [(1 - x, y), (x, 1 - y), (1 - x, 1 - y)]
        copies = []
        for a in range(n):
            for k, (px, py) in enumerate(chips):
                cp = pltpu.make_async_remote_copy(
                    src_ref=ins[a].at[2 * px + py], dst_ref=outs[a].at[k], send_sem=send_sems.at[a, k],
                    recv_sem=recv_sems.at[a, k], device_id=(px, py, c), device_id_type=MESH_ID)
                cp.start()
                copies.append(cp)
        for cp in copies:
            cp.wait()

    any_spec = pl.BlockSpec(memory_space=pl.ANY)
    return pl.pallas_call(
        body, name=name,
        out_shape=[jax.ShapeDtypeStruct((N_CHIP - 1,) + p.shape[1:], p.dtype) for p in pairs],
        in_specs=[any_spec] * n, out_specs=[any_spec] * n,
        scratch_shapes=[pltpu.SemaphoreType.DMA((n, N_CHIP - 1)), pltpu.SemaphoreType.DMA((n, N_CHIP - 1))],
    )(*pairs)


def _half_share(halves, name):
    n = len(halves)

    def body(*refs):
        ins, outs = refs[:n], refs[n:2 * n]
        send_sems, recv_sems, local_sems = refs[2 * n:]
        x, y, c = _coords()
        sibling = (x, y, 1 - c)
        copies = []
        for a in range(n):
            h = halves[a].shape[0]
            dst = outs[a].at[pl.ds(c * h, h), :]
            loc = pltpu.make_async_copy(ins[a], dst, local_sems.at[a])
            loc.start()
            cp = pltpu.make_async_remote_copy(
                src_ref=ins[a], dst_ref=dst, send_sem=send_sems.at[a], recv_sem=recv_sems.at[a], device_id=sibling,
                device_id_type=MESH_ID)
            cp.start()
            copies.append((loc, cp))
        for loc, cp in copies:
            loc.wait()
            cp.wait_send()
        for a in range(n):
            h = halves[a].shape[0]
            theirs = outs[a].at[pl.ds((1 - c) * h, h), :]
            pltpu.make_async_remote_copy(
                src_ref=ins[a], dst_ref=theirs, send_sem=send_sems.at[a], recv_sem=recv_sems.at[a], device_id=sibling,
                device_id_type=MESH_ID).wait_recv()

    any_spec = pl.BlockSpec(memory_space=pl.ANY)
    return pl.pallas_call(
        body, name=name, out_shape=[jax.ShapeDtypeStruct((2 * s.shape[0], s.shape[1]), s.dtype) for s in halves],
        in_specs=[any_spec] * n, out_specs=[any_spec] * n,
        scratch_shapes=[pltpu.SemaphoreType.DMA((n,)), pltpu.SemaphoreType.DMA((n,)), pltpu.SemaphoreType.DMA((n,))],
    )(*halves)


def _reduce_scatter(grads, c_idx, chip_idx):
    names = list(grads)
    full = [grads[k] for k in names]
    recv = _pair_exchange(full, "rs_pair_exchange")
    pairs = [_pair_add(f, r, c_idx, "rs_pair_add_" + k) for k, f, r in zip(names, full, recv)]
    got = _chip_exchange(pairs, "rs_chip_exchange")
    halves = [_chip_sum(p, g, chip_idx, "rs_chip_sum_" + k) for k, p, g in zip(names, pairs, got)]
    outs = _half_share(halves, "rs_half_share")
    return dict(zip(names, outs))


PACK_ALIGN = SUBLANE * LANE
PACK_ROWS_ALIGN = 256 * LANE


def _pack(parts):
    flat, offs, off = [], [], 0
    for p in parts:
        v = p.reshape(-1).astype(F32)
        n = v.shape[0]
        padded = -(-n // PACK_ALIGN) * PACK_ALIGN
        flat.append(jnp.pad(v, (0, padded - n)))
        offs.append((off, n))
        off += padded
    tail = -off % PACK_ROWS_ALIGN
    if tail:
        flat.append(jnp.zeros((tail,), F32))
    return jnp.concatenate(flat).reshape(-1, LANE), offs


def _unpack(flat, offs, shapes):
    return [flat[o:o + n].reshape(s) for (o, n), s in zip(offs, shapes)]


def _chipcat(g, per_dev_len, offs, shape, axis):
    o, n = offs
    parts = [g[2 * j, o:o + n].reshape(shape) for j in range(N_CHIP)]
    return jnp.concatenate(parts, axis=axis)


def kernel(x, c, w_ada, b_ada, ln_g, ln_b, a_w_in, a_w_dw, a_b_dw, a_norm_g, a_norm_b, a_w_out, b_w_in, b_q_norm_g, b_w_qb, b_w_out, kv_w_a, kv_norm_g, kv_w_b, loss_target, m_w_ada, m_b_ada, m_ln_g, m_ln_b, m_a_w_in, m_a_w_dw, m_a_b_dw, m_a_norm_g, m_a_norm_b, m_a_w_out, m_b_w_in, m_b_q_norm_g, m_b_w_qb, m_b_w_out, m_kv_w_a, m_kv_norm_g, m_kv_w_b, v_w_ada, v_b_ada, v_ln_g, v_ln_b, v_a_w_in, v_a_w_dw, v_a_b_dw, v_a_norm_g, v_a_norm_b, v_a_w_out, v_b_w_in, v_b_q_norm_g, v_b_w_qb, v_b_w_out, v_kv_w_a, v_kv_norm_g, v_kv_w_b):
    xi, yi, ci = _coords()
    chip = 2 * xi + yi
    dev = 4 * xi + 2 * yi + ci
    c_idx = jnp.reshape(ci, (1,)).astype(jnp.int32)
    chip_idx = jnp.reshape(chip, (1,)).astype(jnp.int32)

    x2 = x[0]
    tgt = loss_target[0]
    S, D = x2.shape
    C = a_w_out.shape[1] * N_CHIP
    Cq = C // N_CHIP
    KS = a_w_dw.shape[1]
    Q = b_q_norm_g.shape[1]
    KV = kv_norm_g.shape[0]
    Hq = kv_w_b.shape[1] // HEAD_PAD
    H = Hq * N_CHIP
    W = H * V_HEAD_DIM
    Nq = w_ada.shape[2]
    head_q = QK_NOPE_DIM + QK_ROPE_DIM
    scale = head_q ** -0.5
    tabs = _rope_tables(S)

    qb_pad = jnp.pad(b_w_qb[0].reshape(Q, Hq, head_q), ((0, 0), (0, 0), (0, HEAD_PAD - head_q))).reshape(Q, Hq * HEAD_PAD)
    kva_pad = jnp.pad(kv_w_a, ((0, 0), (0, LANE - QK_ROPE_DIM)))
    wg = _allgather_weights(
        [a_w_in[0].astype(BF16), a_w_out[0].astype(BF16), b_w_in[0].astype(BF16), qb_pad.astype(BF16),
         b_w_out[0].astype(BF16), kva_pad.astype(BF16), kv_w_b.astype(BF16)], "allgather_weights")
    W_ain, W_aout, W_bin, W_qb, W_bout, W_kva, W_kvb = wg
    W_aout = W_aout.reshape(C, D)
    W_bout = W_bout.reshape(W, D)
    W_kva = W_kva.reshape(D, KV + LANE)

    pack1, offs1 = _pack([c[0], a_w_dw[0], a_b_dw[0], a_norm_g[0], a_norm_b[0]])
    L1 = pack1.shape[0] * LANE
    g1 = _allgather_small(pack1, "allgather_small_in").reshape(N_DEV, L1)
    c_all = g1[:, :D]
    w_dw = _chipcat(g1, L1, offs1[1], (KS, Cq), 1)
    b_dw = _chipcat(g1, L1, offs1[2], (1, Cq), 1)
    g_cn = _chipcat(g1, L1, offs1[3], (1, Cq), 1)
    b_cn = _chipcat(g1, L1, offs1[4], (1, Cq), 1)

    b_ada_sh = lax.dynamic_slice_in_dim(b_ada, chip * Nq, Nq, axis=1)[:, None, :]
    mod_sh = _mod(c_all, w_ada, b_ada_sh, "adaln_mod")
    gm = _allgather_small(mod_sh.reshape(DEPTH * N_DEV, Nq), "allgather_small_mod").reshape(N_CHIP, 2, DEPTH, N_DEV, Nq)
    mod_rows = lax.dynamic_index_in_dim(gm[:, 0], dev, axis=2, keepdims=False)
    mod_me = jnp.transpose(mod_rows, (1, 0, 2)).reshape(DEPTH, N_CHIP * Nq)
    shift = [mod_me[l:l + 1, 0:D] for l in range(DEPTH)]
    scl = [mod_me[l:l + 1, D:2 * D] for l in range(DEPTH)]
    gate = [mod_me[l:l + 1, 2 * D:3 * D] for l in range(DEPTH)]

    h0 = _lnmod_fwd(x2, scl[0], shift[0], "a_lnmod_fwd")
    u0 = _mm(h0, W_ain, b_sh=True, name="a_in_fwd")
    v2 = _conv1_fwd(u0, w_dw, b_dw, "a_conv1_fwd")
    p0 = _conv2_fwd(v2, u0, g_cn, b_cn, "a_conv2_fwd")
    out0 = _mm(p0, W_aout, name="a_out_fwd")
    x1 = _resln_fwd(x2, out0, gate[0], ln_g[0:1], ln_b[0:1], "a_resln_fwd")

    kva = _mm(x1, W_kva, name="kv_a_fwd")
    ckv, krp = _kvprep_fwd(kva, kv_norm_g[None, :], tabs, "kv_prep_fwd")
    kvh = _mm(ckv, W_kvb, b_sh=True, out_dtype=BF16, name="kv_b_fwd")

    h1 = _lnmod_fwd(x1, scl[1], shift[1], "b_lnmod_fwd")
    u1 = _mm(h1, W_bin, b_sh=True, name="b_in_fwd")
    qn = _rms_fwd(u1, Q, b_q_norm_g, "b_qnorm_fwd")
    qraw = _mm(qn, W_qb, b_sh=True, name="b_qb_fwd")
    qr = _qrope(qraw, tabs, 1.0, "b_qrope_fwd")
    o, lse = _attn_fwd(qr, kvh, krp, scale, "b_attn_fwd")
    p1 = _gate_fwd(o, u1, Q, "b_gate_fwd")
    out1 = _mm(p1, W_bout, name="b_out_fwd")

    dxa1, dout1, dgate1, dlng1, dlnb1, loss_part = _resln_bwd(
        x1, out1, gate[1], ln_g[1:2], ln_b[1:2], tgt, True, "b_resln_bwd")
    dW_bout = _mm(p1, dout1, ta=True, out_dtype=BF16, name="b_out_dw").reshape(N_CHIP, W // N_CHIP, D)
    dp1 = _mm(dout1, W_bout, tb=True, name="b_out_dx")
    do, du1, delta = _gate_bwd(dp1, o, u1, Q, "b_gate_bwd")
    t_att = _tile(S, ATT_TILE, LANE)
    lse_r = lse[:, :, 0].reshape(H, S // t_att, t_att)
    delta_r = delta[:, :, 0].reshape(H, S // t_att, t_att)
    dqr = _attn_dq(qr, kvh, krp, do, lse, delta, scale, "b_attn_dq")
    dkvh, dkr_h = _attn_dkv(qr, kvh, krp, do, lse_r, delta_r, scale, "b_attn_dkv")
    dqraw = _qrope(dqr, tabs, -1.0, "b_qrope_bwd")
    dW_qb = _mm(qn, dqraw, ta=True, o_sh=True, out_dtype=BF16, name="b_qb_dw")
    dqn = _mm(dqraw, W_qb, tb=True, b_sh=True, name="b_qb_dx")
    du1, dgq = _rms_bwd(u1, Q, dqn, b_q_norm_g, du1, "b_qnorm_bwd")
    dW_bin = _mm(h1, du1, ta=True, o_sh=True, out_dtype=BF16, name="b_in_dw")
    dh1 = _mm(du1, W_bin, tb=True, b_sh=True, name="b_in_dx")

    dW_kvb = _mm(ckv, dkvh, ta=True, o_sh=True, out_dtype=BF16, name="kv_b_dw")
    dckv = _mm(dkvh, W_kvb, tb=True, b_sh=True, name="kv_b_dx")
    dkva, dgkv = _kvprep_bwd(kva, dckv, dkr_h, kv_norm_g[None, :], tabs, "kv_prep_bwd")
    dW_kva = _mm(x1, dkva, ta=True, out_dtype=BF16, name="kv_a_dw").reshape(N_CHIP, D // N_CHIP, KV + LANE)
    dx1_kv = _mm(dkva, W_kva, tb=True, name="kv_a_dx")
    dx1, dsc1, dsh1 = _lnmod_bwd(x1, dh1, scl[1], [dxa1, dx1_kv], "b_lnmod_bwd")

    dxa0, dout0, dgate0, dlng0, dlnb0 = _resln_bwd(x2, out0, gate[0], ln_g[0:1], ln_b[0:1], dx1, False, "a_resln_bwd")
    dW_aout = _mm(p0, dout0, ta=True, out_dtype=BF16, name="a_out_dw").reshape(N_CHIP, Cq, D)
    dp0 = _mm(dout0, W_aout, tb=True, name="a_out_dx")
    dv2, dz0, dgcn, dbcn = _conv2_bwd(dp0, v2, u0, g_cn, b_cn, "a_conv2_bwd")
    da0, dg0, dwdw, dbdw = _conv1_bwd(dv2, u0, w_dw, "a_conv1_bwd")
    du0 = jnp.concatenate([da0, dg0, dz0], axis=1)
    dW_ain = _mm(h0, du0, ta=True, o_sh=True, out_dtype=BF16, name="a_in_dw")
    dh0 = _mm(du0, W_ain, tb=True, b_sh=True, name="a_in_dx")
    dx, dsc0, dsh0 = _lnmod_bwd(x2, dh0, scl[0], [dxa0], "a_lnmod_bwd")
    grad_x = dx[None]

    dmod = jnp.concatenate([dsh0, dsc0, dgate0, dsh1, dsc1, dgate1], axis=1).reshape(DEPTH, 3 * D)
    small = [loss_part, dmod, jnp.concatenate([dlng0, dlng1], 0), jnp.concatenate([dlnb0, dlnb1], 0),
             dwdw, dbdw, dgcn, dbcn, dgq, dgkv]
    small_shapes = [p.shape for p in small]
    pack2, offs2 = _pack(small)
    R2 = pack2.shape[0]
    g2 = _allgather_small(pack2, "allgather_small_grads").reshape(N_DEV, R2, LANE)
    tot = _sum_leading(g2, "small_grad_sum").reshape(-1)
    (loss_t, g_b_ada, g_ln_g, g_ln_b, g_wdw_full, g_bdw_full, g_gcn_full, g_bcn_full, g_gq, g_gkv) = _unpack(
        tot, offs2, small_shapes)
    loss = loss_t.reshape(())
    colsl = lambda a: lax.dynamic_slice_in_dim(a, chip * Cq, Cq, axis=1)
    g_wdw, g_bdw, g_gcn, g_bcn = colsl(g_wdw_full), colsl(g_bdw_full), colsl(g_gcn_full), colsl(g_bcn_full)

    dmod_all = jnp.stack([g2[d].reshape(-1)[offs2[1][0]:offs2[1][0] + offs2[1][1]].reshape(DEPTH, 3 * D)
                          for d in range(N_DEV)], axis=1)
    dmod_sh = lax.dynamic_slice_in_dim(dmod_all, chip * Nq, Nq, axis=2)
    g_w_ada = _wada_grad(jnp.transpose(c_all), dmod_sh, "w_ada_grad")

    red = _reduce_scatter(
        {"a_w_in": dW_ain, "a_w_out": dW_aout, "b_w_in": dW_bin, "b_w_qb": dW_qb, "b_w_out": dW_bout,
         "kv_w_a": dW_kva, "kv_w_b": dW_kvb}, c_idx, chip_idx)
    g_a_w_in = red["a_w_in"]
    g_a_w_out = red["a_w_out"]
    g_b_w_in = red["b_w_in"]
    g_b_w_qb = red["b_w_qb"].reshape(Q, Hq, HEAD_PAD)[:, :, :head_q].reshape(Q, Hq * head_q)
    g_b_w_out = red["b_w_out"]
    g_kv_w_a = red["kv_w_a"][:, :KV + QK_ROPE_DIM]
    g_kv_w_b = red["kv_w_b"]

    grads = {
        "w_ada": g_w_ada, "b_ada": g_b_ada, "ln_g": g_ln_g, "ln_b": g_ln_b, "a_w_in": g_a_w_in[None],
        "a_w_dw": g_wdw[None], "a_b_dw": g_bdw, "a_norm_g": g_gcn, "a_norm_b": g_bcn, "a_w_out": g_a_w_out[None],
        "b_w_in": g_b_w_in[None], "b_q_norm_g": g_gq, "b_w_qb": g_b_w_qb[None], "b_w_out": g_b_w_out[None],
        "kv_w_a": g_kv_w_a, "kv_norm_g": g_gkv.reshape(KV), "kv_w_b": g_kv_w_b,
    }
    weights = {
        "w_ada": (w_ada, m_w_ada, v_w_ada), "b_ada": (b_ada, m_b_ada, v_b_ada), "ln_g": (ln_g, m_ln_g, v_ln_g),
        "ln_b": (ln_b, m_ln_b, v_ln_b), "a_w_in": (a_w_in, m_a_w_in, v_a_w_in), "a_w_dw": (a_w_dw, m_a_w_dw, v_a_w_dw),
        "a_b_dw": (a_b_dw, m_a_b_dw, v_a_b_dw), "a_norm_g": (a_norm_g, m_a_norm_g, v_a_norm_g),
        "a_norm_b": (a_norm_b, m_a_norm_b, v_a_norm_b), "a_w_out": (a_w_out, m_a_w_out, v_a_w_out),
        "b_w_in": (b_w_in, m_b_w_in, v_b_w_in), "b_q_norm_g": (b_q_norm_g, m_b_q_norm_g, v_b_q_norm_g),
        "b_w_qb": (b_w_qb, m_b_w_qb, v_b_w_qb), "b_w_out": (b_w_out, m_b_w_out, v_b_w_out),
        "kv_w_a": (kv_w_a, m_kv_w_a, v_kv_w_a), "kv_norm_g": (kv_norm_g, m_kv_norm_g, v_kv_norm_g),
        "kv_w_b": (kv_w_b, m_kv_w_b, v_kv_w_b),
    }
    order = list(weights)
    big = [k for k in order if weights[k][0].size >= (1 << 16) and weights[k][0].shape[-1] % LANE == 0]
    small_names = [k for k in order if k not in big]
    upd = {}
    for k in big:
        w, m, v = weights[k]
        shp = w.shape
        two = (-1, shp[-1])
        d_, m_, v_ = _adamw(w.reshape(two), grads[k].reshape(two), m.reshape(two), v.reshape(two), "adamw_" + k)
        upd[k] = (grads[k].reshape(shp), d_.reshape(shp), m_.reshape(shp), v_.reshape(shp))
    sw, offs3 = _pack([weights[k][0] for k in small_names])
    sg, _ = _pack([grads[k] for k in small_names])
    sm, _ = _pack([weights[k][1] for k in small_names])
    sv, _ = _pack([weights[k][2] for k in small_names])
    sd, snm, snv = _adamw(sw, sg, sm, sv, "adamw_small")
    shapes3 = [weights[k][0].shape for k in small_names]
    for k, d_, m_, v_ in zip(small_names, _unpack(sd.reshape(-1), offs3, shapes3), _unpack(snm.reshape(-1), offs3, shapes3),
                             _unpack(snv.reshape(-1), offs3, shapes3)):
        upd[k] = (grads[k].reshape(weights[k][0].shape), d_, m_, v_)

    return (loss, grad_x, *[upd[k][0] for k in order], *[upd[k][1] for k in order], *[upd[k][2] for k in order],
            *[upd[k][3] for k in order])
```

```python
import math

import jax
import jax.numpy as jnp
from jax import lax
from jax.experimental import pallas as pl
from jax.experimental.pallas import tpu as pltpu

F32 = jnp.float32
BF16 = jnp.bfloat16

LN_EPS = 1e-5
RMS_EPS = 1e-6
DEPTH = 2
DEEPNORM_ALPHA = (2.0 * DEPTH) ** 0.25
QK_NOPE_DIM = 128
QK_ROPE_DIM = 64
V_HEAD_DIM = 128
HEAD_PAD = 256
ROPE_BASE = 10000.0
ADAM_LR = 0.001
ADAM_B1 = 0.9
ADAM_B2 = 0.999
ADAM_EPS = 1e-08
ADAM_WD = 0.01
ADAM_STEP = 10

N_DEV = 8
N_CHIP = 4
LANE = 128
SUBLANE = 8
VMEM_LIMIT = 48 * 1024 * 1024
CONV_HALO = 32
MESH_ID = pl.DeviceIdType.MESH
NT = (((1,), (1,)), ((), ()))


def _cp(*sem):
    return pltpu.CompilerParams(dimension_semantics=sem, vmem_limit_bytes=VMEM_LIMIT)


def _tile(n, pref, align):
    if n <= pref:
        return n
    t = (pref // align) * align
    while t > align and n % t:
        t -= align
    assert n % t == 0, (n, pref, align)
    return t


def _silu(v):
    return v * jax.nn.sigmoid(v)


def _dsilu(v):
    s = jax.nn.sigmoid(v)
    return s * (1.0 + v * (1.0 - s))


def _mm(a, b, *, name, ta=False, tb=False, b_sh=False, o_sh=False, out_dtype=F32, tm=1024, tn=1024, tk=512):
    M, K = (a.shape[1], a.shape[0]) if ta else a.shape
    if b_sh:
        assert b.shape[0] == N_CHIP
        nq = b.shape[2]
        Kb, N = (nq * N_CHIP, b.shape[1]) if tb else (b.shape[1], nq * N_CHIP)
    else:
        Kb, N = (b.shape[1], b.shape[0]) if tb else b.shape
        nq = N // N_CHIP
    assert K == Kb, (a.shape, b.shape)
    tm = _tile(M, tm, LANE)
    tk = _tile(nq if (b_sh and tb) else K, tk, LANE)
    tn = _tile(nq if ((b_sh and not tb) or o_sh) else N, tn, LANE)
    nk = K // tk

    def body(a_ref, b_ref, o_ref, acc_ref):
        k = pl.program_id(2)

        @pl.when(k == 0)
        def _():
            acc_ref[...] = jnp.zeros_like(acc_ref)

        dn = (((0 if ta else 1,), (1 if tb else 0,)), ((), ()))
        acc_ref[...] += lax.dot_general(a_ref[...].astype(BF16), b_ref[...].astype(BF16), dn,
                                        preferred_element_type=F32)

        @pl.when(k == nk - 1)
        def _():
            o_ref[...] = acc_ref[...].astype(o_ref.dtype)

    a_spec = pl.BlockSpec((tk, tm), lambda i, j, k: (k, i)) if ta else pl.BlockSpec((tm, tk), lambda i, j, k: (i, k))
    if b_sh and not tb:
        per = nq // tn
        b_spec = pl.BlockSpec((None, tk, tn), lambda i, j, k: (j // per, k, j % per))
    elif b_sh and tb:
        per = nq // tk
        b_spec = pl.BlockSpec((None, tn, tk), lambda i, j, k: (k // per, j, k % per))
    elif tb:
        b_spec = pl.BlockSpec((tn, tk), lambda i, j, k: (j, k))
    else:
        b_spec = pl.BlockSpec((tk, tn), lambda i, j, k: (k, j))
    if o_sh:
        per_o = nq // tn
        o_spec = pl.BlockSpec((None, tm, tn), lambda i, j, k: (j // per_o, i, j % per_o))
        o_shape = jax.ShapeDtypeStruct((N_CHIP, M, nq), out_dtype)
    else:
        o_spec = pl.BlockSpec((tm, tn), lambda i, j, k: (i, j))
        o_shape = jax.ShapeDtypeStruct((M, N), out_dtype)
    return pl.pallas_call(
        body, name=name, grid=(M // tm, N // tn, nk), in_specs=[a_spec, b_spec], out_specs=o_spec, out_shape=o_shape,
        scratch_shapes=[pltpu.VMEM((tm, tn), F32)], compiler_params=_cp("parallel", "parallel", "arbitrary"),
    )(a, b)


def _mod(c_all, w_ada, b_sh, name):
    L, D, nq = w_ada.shape
    B = c_all.shape[0]
    tn = _tile(nq, 512, LANE)

    def body(c_ref, w_ref, b_ref, o_ref):
        sc = _silu(c_ref[...]).astype(BF16)
        o_ref[...] = jnp.dot(sc, w_ref[...].astype(BF16), preferred_element_type=F32) + b_ref[...]

    return pl.pallas_call(
        body, name=name, grid=(L, nq // tn),
        in_specs=[pl.BlockSpec((B, D), lambda l, j: (0, 0)), pl.BlockSpec((None, D, tn), lambda l, j: (l, 0, j)),
                  pl.BlockSpec((None, 1, tn), lambda l, j: (l, 0, j))],
        out_specs=pl.BlockSpec((None, B, tn), lambda l, j: (l, 0, j)),
        out_shape=jax.ShapeDtypeStruct((L, B, nq), F32), compiler_params=_cp("parallel", "parallel"),
    )(c_all, w_ada, b_sh)


def _wada_grad(c_all_t, dmod, name):
    D, B = c_all_t.shape
    L, _, nq = dmod.shape
    tm = _tile(D, 512, SUBLANE)
    tn = _tile(nq, 1024, LANE)

    def body(c_ref, d_ref, o_ref):
        sc = _silu(c_ref[...])
        dm = d_ref[...]
        acc = sc[:, 0:1] * dm[0:1, :]
        for b in range(1, B):
            acc = acc + sc[:, b:b + 1] * dm[b:b + 1, :]
        o_ref[...] = acc

    return pl.pallas_call(
        body, name=name, grid=(L, D // tm, nq // tn),
        in_specs=[pl.BlockSpec((tm, B), lambda l, i, j: (i, 0)), pl.BlockSpec((None, B, tn), lambda l, i, j: (l, 0, j))],
        out_specs=pl.BlockSpec((None, tm, tn), lambda l, i, j: (l, i, j)),
        out_shape=jax.ShapeDtypeStruct((L, D, nq), F32), compiler_params=_cp("parallel", "parallel", "parallel"),
    )(c_all_t, dmod)


ROW_TILE = 128


def _ln_stats(v):
    mu = jnp.mean(v, axis=-1, keepdims=True)
    vc = v - mu
    var = jnp.mean(vc * vc, axis=-1, keepdims=True)
    rstd = lax.rsqrt(var + LN_EPS)
    return vc * rstd, rstd


def _ln_bwd(dxhat, xhat, rstd):
    m1 = jnp.mean(dxhat, axis=-1, keepdims=True)
    m2 = jnp.mean(dxhat * xhat, axis=-1, keepdims=True)
    return rstd * (dxhat - m1 - xhat * m2)


def _row_spec(ts, D):
    return pl.BlockSpec((ts, D), lambda i: (i, 0))


def _vec_spec(D):
    return pl.BlockSpec((1, D), lambda i: (0, 0))


def _lnmod_fwd(x, scale, shift, name):
    S, D = x.shape
    ts = _tile(S, ROW_TILE, SUBLANE)

    def body(x_ref, sc_ref, sh_ref, h_ref):
        xn, _ = _ln_stats(x_ref[...])
        h_ref[...] = (xn * (1.0 + sc_ref[...]) + sh_ref[...]).astype(h_ref.dtype)

    return pl.pallas_call(
        body, name=name, grid=(S // ts,), in_specs=[_row_spec(ts, D), _vec_spec(D), _vec_spec(D)],
        out_specs=_row_spec(ts, D), out_shape=jax.ShapeDtypeStruct((S, D), BF16), compiler_params=_cp("parallel"),
    )(x, scale, shift)


def _lnmod_bwd(x, dh, scale, adds, name):
    S, D = x.shape
    ts = _tile(S, ROW_TILE, SUBLANE)
    na = len(adds)

    def body(*refs):
        x_ref, dh_ref, sc_ref = refs[:3]
        add_refs = refs[3:3 + na]
        dx_ref, dsc_ref, dsh_ref = refs[3 + na:]
        i = pl.program_id(0)

        @pl.when(i == 0)
        def _():
            dsc_ref[...] = jnp.zeros_like(dsc_ref)
            dsh_ref[...] = jnp.zeros_like(dsh_ref)

        xn, rstd = _ln_stats(x_ref[...])
        dh = dh_ref[...].astype(F32)
        dx = _ln_bwd(dh * (1.0 + sc_ref[...]), xn, rstd)
        for r in add_refs:
            dx = dx + r[...]
        dx_ref[...] = dx
        dsc_ref[...] += jnp.sum(dh * xn, axis=0, keepdims=True)
        dsh_ref[...] += jnp.sum(dh, axis=0, keepdims=True)

    return pl.pallas_call(
        body, name=name, grid=(S // ts,),
        in_specs=[_row_spec(ts, D), _row_spec(ts, D), _vec_spec(D)] + [_row_spec(ts, D)] * na,
        out_specs=[_row_spec(ts, D), _vec_spec(D), _vec_spec(D)],
        out_shape=[jax.ShapeDtypeStruct((S, D), F32), jax.ShapeDtypeStruct((1, D), F32), jax.ShapeDtypeStruct((1, D), F32)],
        compiler_params=_cp("arbitrary"),
    )(x, dh, scale, *adds)


def _resln_fwd(x, out, gate, g, b, name):
    S, D = x.shape
    ts = _tile(S, ROW_TILE, SUBLANE)

    def body(x_ref, o_ref, gt_ref, g_ref, b_ref, y_ref):
        r = DEEPNORM_ALPHA * x_ref[...] + (1.0 + gt_ref[...]) * o_ref[...]
        xhat, _ = _ln_stats(r)
        y_ref[...] = xhat * g_ref[...] + b_ref[...]

    return pl.pallas_call(
        body, name=name, grid=(S // ts,),
        in_specs=[_row_spec(ts, D), _row_spec(ts, D), _vec_spec(D), _vec_spec(D), _vec_spec(D)],
        out_specs=_row_spec(ts, D), out_shape=jax.ShapeDtypeStruct((S, D), F32), compiler_params=_cp("parallel"),
    )(x, out, gate, g, b)


def _resln_bwd(x, out, gate, g, b, dy_or_target, from_target, name):
    S, D = x.shape
    ts = _tile(S, ROW_TILE, SUBLANE)

    def body(x_ref, o_ref, gt_ref, g_ref, b_ref, t_ref, dxa_ref, dout_ref, dgt_ref, dg_ref, db_ref, *maybe_loss):
        i = pl.program_id(0)

        @pl.when(i == 0)
        def _():
            dgt_ref[...] = jnp.zeros_like(dgt_ref)
            dg_ref[...] = jnp.zeros_like(dg_ref)
            db_ref[...] = jnp.zeros_like(db_ref)
            if from_target:
                maybe_loss[0][...] = jnp.zeros_like(maybe_loss[0])

        ov = o_ref[...]
        g1 = 1.0 + gt_ref[...]
        r = DEEPNORM_ALPHA * x_ref[...] + g1 * ov
        xhat, rstd = _ln_stats(r)
        if from_target:
            err = xhat * g_ref[...] + b_ref[...] - t_ref[...]
            dy = err * (1.0 / D)
            maybe_loss[0][...] += jnp.sum(jnp.sum(err * err, axis=-1, keepdims=True), axis=0, keepdims=True) * (0.5 / D)
        else:
            dy = t_ref[...]
        dr = _ln_bwd(dy * g_ref[...], xhat, rstd)
        dxa_ref[...] = DEEPNORM_ALPHA * dr
        dout_ref[...] = (dr * g1).astype(dout_ref.dtype)
        dgt_ref[...] += jnp.sum(dr * ov, axis=0, keepdims=True)
        dg_ref[...] += jnp.sum(dy * xhat, axis=0, keepdims=True)
        db_ref[...] += jnp.sum(dy, axis=0, keepdims=True)

    vec = jax.ShapeDtypeStruct((1, D), F32)
    out_specs = [_row_spec(ts, D), _row_spec(ts, D), _vec_spec(D), _vec_spec(D), _vec_spec(D)]
    out_shape = [jax.ShapeDtypeStruct((S, D), F32), jax.ShapeDtypeStruct((S, D), BF16), vec, vec, vec]
    if from_target:
        out_specs.append(pl.BlockSpec((1, 1), lambda i: (0, 0)))
        out_shape.append(jax.ShapeDtypeStruct((1, 1), F32))
    return pl.pallas_call(
        body, name=name, grid=(S // ts,),
        in_specs=[_row_spec(ts, D), _row_spec(ts, D), _vec_spec(D), _vec_spec(D), _vec_spec(D), _row_spec(ts, D)],
        out_specs=out_specs, out_shape=out_shape, compiler_params=_cp("arbitrary"),
    )(x, out, gate, g, b, dy_or_target)


def _conv_tiles(S, C):
    tt = _tile(S, 256, CONV_HALO)
    tc = _tile(C, 512, LANE)
    return tt, tc


def _conv1_fwd(u, w_dw, b_dw, name):
    S, C3 = u.shape
    C = C3 // 3
    KS = w_dw.shape[0]
    tt, tc = _conv_tiles(S, C)
    ncb = C // tc
    hb = tt // CONV_HALO
    lead = CONV_HALO - (KS - 1)

    def body(a_ref, g_ref, ah_ref, gh_ref, w_ref, b_ref, o_ref, pad_ref):
        i = pl.program_id(0)
        halo = ah_ref[...] * jax.nn.sigmoid(gh_ref[...])
        pad_ref[0:CONV_HALO, :] = jnp.where(i > 0, halo, 0.0)
        pad_ref[CONV_HALO:, :] = a_ref[...] * jax.nn.sigmoid(g_ref[...])
        acc = jnp.broadcast_to(b_ref[...], (tt, tc))
        for k in range(KS):
            acc = acc + w_ref[k:k + 1, :] * pad_ref[lead + k:lead + k + tt, :]
        o_ref[...] = acc

    main = lambda off: pl.BlockSpec((tt, tc), lambda i, j: (i, off + j))
    halo = lambda off: pl.BlockSpec((CONV_HALO, tc), lambda i, j: (jnp.maximum(i * hb - 1, 0), off + j))
    return pl.pallas_call(
        body, name=name, grid=(S // tt, ncb),
        in_specs=[main(0), main(ncb), halo(0), halo(ncb), pl.BlockSpec((KS, tc), lambda i, j: (0, j)),
                  pl.BlockSpec((1, tc), lambda i, j: (0, j))],
        out_specs=pl.BlockSpec((tt, tc), lambda i, j: (i, j)), out_shape=jax.ShapeDtypeStruct((S, C), F32),
        scratch_shapes=[pltpu.VMEM((CONV_HALO + tt, tc), F32)], compiler_params=_cp("parallel", "parallel"),
    )(u, u, u, u, w_dw, b_dw)


def _conv2_fwd(v2, u, g_cn, b_cn, name):
    S, C = v2.shape
    ts = _tile(S, ROW_TILE, SUBLANE)

    def body(v_ref, z_ref, g_ref, b_ref, p_ref):
        xhat, _ = _ln_stats(v_ref[...])
        v3 = xhat * g_ref[...] + b_ref[...]
        p_ref[...] = (_silu(v3) * _silu(z_ref[...])).astype(p_ref.dtype)

    return pl.pallas_call(
        body, name=name, grid=(S // ts,),
        in_specs=[_row_spec(ts, C), pl.BlockSpec((ts, C), lambda i: (i, 2)), _vec_spec(C), _vec_spec(C)],
        out_specs=_row_spec(ts, C), out_shape=jax.ShapeDtypeStruct((S, C), BF16), compiler_params=_cp("parallel"),
    )(v2, u, g_cn, b_cn)


def _conv2_bwd(dp, v2, u, g_cn, b_cn, name):
    S, C = v2.shape
    ts = _tile(S, ROW_TILE, SUBLANE)

    def body(dp_ref, v_ref, z_ref, g_ref, b_ref, dv_ref, dz_ref, dg_ref, db_ref):
        i = pl.program_id(0)

        @pl.when(i == 0)
        def _():
            dg_ref[...] = jnp.zeros_like(dg_ref)
            db_ref[...] = jnp.zeros_like(db_ref)

        dp = dp_ref[...].astype(F32)
        z = z_ref[...]
        xhat, rstd = _ln_stats(v_ref[...])
        v3 = xhat * g_ref[...] + b_ref[...]
        dz_ref[...] = (dp * _silu(v3) * _dsilu(z)).astype(dz_ref.dtype)
        dv3 = dp * _silu(z) * _dsilu(v3)
        dv_ref[...] = _ln_bwd(dv3 * g_ref[...], xhat, rstd)
        dg_ref[...] += jnp.sum(dv3 * xhat, axis=0, keepdims=True)
        db_ref[...] += jnp.sum(dv3, axis=0, keepdims=True)

    vec = jax.ShapeDtypeStruct((1, C), F32)
    return pl.pallas_call(
        body, name=name, grid=(S // ts,),
        in_specs=[_row_spec(ts, C), _row_spec(ts, C), pl.BlockSpec((ts, C), lambda i: (i, 2)), _vec_spec(C), _vec_spec(C)],
        out_specs=[_row_spec(ts, C), _row_spec(ts, C), _vec_spec(C), _vec_spec(C)],
        out_shape=[jax.ShapeDtypeStruct((S, C), F32), jax.ShapeDtypeStruct((S, C), BF16), vec, vec],
        compiler_params=_cp("arbitrary"),
    )(dp, v2, u, g_cn, b_cn)


def _conv1_bwd(dv2, u, w_dw, name):
    S, C = dv2.shape
    KS = w_dw.shape[0]
    tt, tc = _conv_tiles(S, C)
    ncb = C // tc
    nt = S // tt
    hb = tt // CONV_HALO
    lead = CONV_HALO - (KS - 1)

    def body(dv_ref, dvh_ref, a_ref, g_ref, ah_ref, gh_ref, w_ref, da_ref, dg_ref, dw_ref, db_ref, pad_ref, fpad_ref):
        i = pl.program_id(1)

        @pl.when(i == 0)
        def _():
            dw_ref[...] = jnp.zeros_like(dw_ref)
            db_ref[...] = jnp.zeros_like(db_ref)

        dv = dv_ref[...]
        a = a_ref[...]
        sg = jax.nn.sigmoid(g_ref[...])
        halo = ah_ref[...] * jax.nn.sigmoid(gh_ref[...])
        pad_ref[0:CONV_HALO, :] = jnp.where(i > 0, halo, 0.0)
        pad_ref[CONV_HALO:, :] = a * sg
        fpad_ref[0:tt, :] = dv
        fpad_ref[tt:, :] = jnp.where(i < nt - 1, dvh_ref[...], 0.0)
        dv1 = jnp.zeros((tt, tc), F32)
        for k in range(KS):
            dv1 = dv1 + w_ref[k:k + 1, :] * fpad_ref[KS - 1 - k:KS - 1 - k + tt, :]
            dw_ref[k:k + 1, :] += jnp.sum(dv * pad_ref[lead + k:lead + k + tt, :], axis=0, keepdims=True)
        db_ref[...] += jnp.sum(dv, axis=0, keepdims=True)
        da_ref[...] = (dv1 * sg).astype(da_ref.dtype)
        dg_ref[...] = (dv1 * a * sg * (1.0 - sg)).astype(dg_ref.dtype)

    main = lambda off: pl.BlockSpec((tt, tc), lambda j, i: (i, off + j))
    halo = lambda off: pl.BlockSpec((CONV_HALO, tc), lambda j, i: (jnp.maximum(i * hb - 1, 0), off + j))
    fhalo = pl.BlockSpec((CONV_HALO, tc), lambda j, i: (jnp.minimum((i + 1) * hb, nt * hb - 1), j))
    return pl.pallas_call(
        body, name=name, grid=(ncb, nt),
        in_specs=[main(0), fhalo, main(0), main(ncb), halo(0), halo(ncb), pl.BlockSpec((KS, tc), lambda j, i: (0, j))],
        out_specs=[main(0), main(0), pl.BlockSpec((KS, tc), lambda j, i: (0, j)), pl.BlockSpec((1, tc), lambda j, i: (0, j))],
        out_shape=[jax.ShapeDtypeStruct((S, C), BF16), jax.ShapeDtypeStruct((S, C), BF16),
                   jax.ShapeDtypeStruct((KS, C), F32), jax.ShapeDtypeStruct((1, C), F32)],
        scratch_shapes=[pltpu.VMEM((CONV_HALO + tt, tc), F32), pltpu.VMEM((tt + CONV_HALO, tc), F32)],
        compiler_params=_cp("parallel", "arbitrary"),
    )(dv2, dv2, u, u, u, u, w_dw)


def _rope_tables(S):
    half = QK_ROPE_DIM // 2
    inv_freq = ROPE_BASE ** (-jnp.arange(half, dtype=F32) / half)
    ang = jnp.arange(S, dtype=jnp.int32).astype(F32)[:, None] * inv_freq[None, :]
    cos, sin, z = jnp.cos(ang), jnp.sin(ang), jnp.zeros((S, half), F32)
    tc = jnp.concatenate([cos, cos, z, z], axis=1)
    t1 = jnp.concatenate([-sin, z, z, z], axis=1)
    t2 = jnp.concatenate([z, sin, z, z], axis=1)
    return tc, t1, t2


def _rope128(r, tc, t1, t2, sign):
    return r * tc + sign * (pltpu.roll(r, LANE - QK_ROPE_DIM // 2, 1) * t1 + pltpu.roll(r, QK_ROPE_DIM // 2, 1) * t2)


def _rms_fwd(x, width, g, name):
    S = x.shape[0]
    ts = _tile(S, 256, SUBLANE)

    def body(x_ref, g_ref, o_ref):
        xv = x_ref[...]
        rr = lax.rsqrt(jnp.mean(xv * xv, axis=-1, keepdims=True) + RMS_EPS)
        o_ref[...] = (xv * rr * g_ref[...]).astype(o_ref.dtype)

    return pl.pallas_call(
        body, name=name, grid=(S // ts,), in_specs=[_row_spec(ts, width), _vec_spec(width)],
        out_specs=_row_spec(ts, width), out_shape=jax.ShapeDtypeStruct((S, width), BF16), compiler_params=_cp("parallel"),
    )(x, g)


def _rms_bwd_math(xv, dy, g):
    n = xv.shape[-1]
    rr = lax.rsqrt(jnp.mean(xv * xv, axis=-1, keepdims=True) + RMS_EPS)
    dyg = dy * g
    dx = rr * dyg - xv * (rr * rr * rr) * (jnp.sum(dyg * xv, axis=-1, keepdims=True) * (1.0 / n))
    dg = jnp.sum(dy * xv * rr, axis=0, keepdims=True)
    return dx, dg


def _rms_bwd(x, width, dy, g, du, name):
    S = x.shape[0]
    ts = _tile(S, 256, SUBLANE)

    def body(x_ref, dy_ref, g_ref, du_in, dx_ref, dg_ref):
        del du_in
        i = pl.program_id(0)

        @pl.when(i == 0)
        def _():
            dg_ref[...] = jnp.zeros_like(dg_ref)

        dx, dg = _rms_bwd_math(x_ref[...], dy_ref[...].astype(F32), g_ref[...])
        dx_ref[...] = dx.astype(dx_ref.dtype)
        dg_ref[...] += dg

    return pl.pallas_call(
        body, name=name, grid=(S // ts,),
        in_specs=[_row_spec(ts, width), _row_spec(ts, width), _vec_spec(width), pl.BlockSpec(memory_space=pl.ANY)],
        out_specs=[_row_spec(ts, width), _vec_spec(width)],
        out_shape=[jax.ShapeDtypeStruct(du.shape, du.dtype), jax.ShapeDtypeStruct((1, width), F32)],
        input_output_aliases={3: 0}, compiler_params=_cp("arbitrary"),
    )(x, dy, g, du)


def _kvprep_fwd(kva, g_kv, tabs, name):
    S, W = kva.shape
    KV = W - LANE
    ts = _tile(S, 256, SUBLANE)

    def body(x_ref, g_ref, tc_ref, t1_ref, t2_ref, c_ref, r_ref):
        xv = x_ref[:, 0:KV]
        rr = lax.rsqrt(jnp.mean(xv * xv, axis=-1, keepdims=True) + RMS_EPS)
        c_ref[...] = (xv * rr * g_ref[...]).astype(c_ref.dtype)
        r_ref[...] = _rope128(x_ref[:, KV:], tc_ref[...], t1_ref[...], t2_ref[...], 1.0).astype(r_ref.dtype)

    tab = _row_spec(ts, LANE)
    return pl.pallas_call(
        body, name=name, grid=(S // ts,), in_specs=[_row_spec(ts, W), _vec_spec(KV), tab, tab, tab],
        out_specs=[_row_spec(ts, KV), _row_spec(ts, LANE)],
        out_shape=[jax.ShapeDtypeStruct((S, KV), BF16), jax.ShapeDtypeStruct((S, LANE), BF16)], compiler_params=_cp("parallel"),
    )(kva, g_kv, *tabs)


def _kvprep_bwd(kva, dckv, dkr_h, g_kv, tabs, name):
    S, W = kva.shape
    KV = W - LANE
    H = dkr_h.shape[1] // LANE
    ts = _tile(S, 256, SUBLANE)

    def body(x_ref, dc_ref, dr_ref, g_ref, tc_ref, t1_ref, t2_ref, o_ref, dg_ref):
        i = pl.program_id(0)

        @pl.when(i == 0)
        def _():
            dg_ref[...] = jnp.zeros_like(dg_ref)

        dx, dg = _rms_bwd_math(x_ref[:, 0:KV], dc_ref[...].astype(F32), g_ref[...])
        o_ref[:, 0:KV] = dx.astype(o_ref.dtype)
        dg_ref[...] += dg
        dr = dr_ref[:, 0:LANE]
        for h in range(1, H):
            dr = dr + dr_ref[:, h * LANE:(h + 1) * LANE]
        o_ref[:, KV:] = _rope128(dr, tc_ref[...], t1_ref[...], t2_ref[...], -1.0).astype(o_ref.dtype)

    tab = _row_spec(ts, LANE)
    return pl.pallas_call(
        body, name=name, grid=(S // ts,),
        in_specs=[_row_spec(ts, W), _row_spec(ts, KV), _row_spec(ts, H * LANE), _vec_spec(KV), tab, tab, tab],
        out_specs=[_row_spec(ts, W), _vec_spec(KV)],
        out_shape=[jax.ShapeDtypeStruct((S, W), BF16), jax.ShapeDtypeStruct((1, KV), F32)], compiler_params=_cp("arbitrary"),
    )(kva, dckv, dkr_h, g_kv, *tabs)


ROPE_GROUP = 4


def _qrope_bwd(dq, tabs, name):
    S, W = dq.shape
    H = W // HEAD_PAD
    G = math.gcd(H, ROPE_GROUP)
    ts = _tile(S, 256, 2 * SUBLANE)

    def body(q_ref, tc_ref, t1_ref, t2_ref, o_ref):
        for g in range(G):
            lo = g * HEAD_PAD
            o_ref[:, lo:lo + LANE] = q_ref[:, lo:lo + LANE].astype(o_ref.dtype)
            o_ref[:, lo + LANE:lo + HEAD_PAD] = _rope128(
                q_ref[:, lo + LANE:lo + HEAD_PAD], tc_ref[...], t1_ref[...], t2_ref[...], -1.0).astype(o_ref.dtype)

    tab = pl.BlockSpec((ts, LANE), lambda i, h: (i, 0))
    blk = pl.BlockSpec((ts, G * HEAD_PAD), lambda i, h: (i, h))
    return pl.pallas_call(
        body, name=name, grid=(S // ts, H // G), in_specs=[blk, tab, tab, tab], out_specs=blk,
        out_shape=jax.ShapeDtypeStruct((S, W), BF16), compiler_params=_cp("parallel", "parallel"),
    )(dq, *tabs)


ATT_TILE = 512


def _causal_mask(t, transposed):
    r = lax.broadcasted_iota(jnp.int32, (t, t), 0)
    c = lax.broadcasted_iota(jnp.int32, (t, t), 1)
    return (c >= r) if transposed else (r >= c)


def _attn_fwd(qraw, kvh, krp, u, zoff, tabs, scale, name):
    S, W = qraw.shape
    H = W // HEAD_PAD
    t = _tile(S, ATT_TILE, LANE)
    nq = S // t
    zb = zoff // V_HEAD_DIM

    def body(q_ref, tc_ref, t1_ref, t2_ref, kv_ref, kr_ref, z_ref, o_ref, p_ref, qr_ref, lse_ref, vt_sc, m_sc, l_sc, acc_sc):
        i = pl.program_id(1)

        @pl.when(i == 0)
        def _():
            for jj in range(nq):
                vt_sc[jj] = kv_ref[jj * t:(jj + 1) * t, QK_NOPE_DIM:].astype(F32).T.astype(BF16)

        qrot = _rope128(q_ref[:, LANE:], tc_ref[...], t1_ref[...], t2_ref[...], 1.0)
        q = jnp.concatenate([q_ref[:, 0:LANE].astype(BF16), qrot.astype(BF16)], axis=-1)
        qr_ref[...] = q
        m_sc[...] = jnp.full_like(m_sc, -jnp.inf)
        l_sc[...] = jnp.zeros_like(l_sc)
        acc_sc[...] = jnp.zeros_like(acc_sc)

        def step(j, masked):
            off = pl.multiple_of(j * t, t)
            k = jnp.concatenate([kv_ref[pl.ds(off, t), 0:QK_NOPE_DIM], kr_ref[pl.ds(off, t), :]], axis=-1)
            st = lax.dot_general(k, q, NT, preferred_element_type=F32) * scale
            if masked:
                st = jnp.where(_causal_mask(t, True), st, -jnp.inf)
            m_old = m_sc[...]
            m_new = jnp.maximum(m_old, jnp.max(st, axis=0, keepdims=True))
            a = jnp.exp(m_old - m_new)
            pt = jnp.exp(st - m_new)
            l_sc[...] = a * l_sc[...] + jnp.sum(pt, axis=0, keepdims=True)
            acc_sc[...] = a * acc_sc[...] + jnp.dot(vt_sc[j], pt.astype(BF16), preferred_element_type=F32)
            m_sc[...] = m_new

        def loop_body(j, carry):
            step(j, False)
            return carry

        lax.fori_loop(0, i, loop_body, 0)
        step(i, True)
        ov = (acc_sc[...] / l_sc[...]).T
        o_ref[...] = ov
        p_ref[...] = (ov * _silu(z_ref[...])).astype(p_ref.dtype)
        lse_ref[...] = m_sc[...] + jnp.log(l_sc[...])

    tab = pl.BlockSpec((t, LANE), lambda h, i: (i, 0))
    head = pl.BlockSpec((t, V_HEAD_DIM), lambda h, i: (i, h))
    return pl.pallas_call(
        body, name=name, grid=(H, nq),
        in_specs=[pl.BlockSpec((t, HEAD_PAD), lambda h, i: (i, h)), tab, tab, tab,
                  pl.BlockSpec((S, HEAD_PAD), lambda h, i: (0, h)), pl.BlockSpec((S, LANE), lambda h, i: (0, 0)),
                  pl.BlockSpec((t, V_HEAD_DIM), lambda h, i: (i, zb + h))],
        out_specs=[head, head, pl.BlockSpec((t, HEAD_PAD), lambda h, i: (i, h)),
                   pl.BlockSpec((None, None, 1, t), lambda h, i: (h, i, 0, 0))],
        out_shape=[jax.ShapeDtypeStruct((S, H * V_HEAD_DIM), F32), jax.ShapeDtypeStruct((S, H * V_HEAD_DIM), BF16),
                   jax.ShapeDtypeStruct((S, W), BF16), jax.ShapeDtypeStruct((H, nq, 1, t), F32)],
        scratch_shapes=[pltpu.VMEM((nq, V_HEAD_DIM, t), BF16), pltpu.VMEM((1, t), F32), pltpu.VMEM((1, t), F32),
                        pltpu.VMEM((V_HEAD_DIM, t), F32)],
        compiler_params=_cp("parallel", "arbitrary"),
    )(qraw, *tabs, kvh, krp, u)


def _attn_bwd(qr, kvh, krp, dp, o, u, zoff, lse, scale, name):
    S, W = qr.shape
    H = W // HEAD_PAD
    U = u.shape[1]
    t = _tile(S, ATT_TILE, LANE)
    nq = S // t
    zb = zoff // V_HEAD_DIM

    def body(q_ref, kv_ref, kr_ref, dp_ref, o_ref, z_ref, lse_ref, dq_ref, dkv_ref, dkr_ref, dz_ref,
             do_sc, dl_sc, dqt_sc, dk_sc, dv_sc):
        j = pl.program_id(1)

        @pl.when(j == 0)
        def _():
            for ii in range(nq):
                rows = slice(ii * t, (ii + 1) * t)
                dpv, ov, z = dp_ref[rows, :], o_ref[rows, :], z_ref[rows, :]
                dov = dpv * _silu(z)
                do_sc[ii] = dov.astype(BF16)
                dz_ref[rows, :] = (dpv * ov * _dsilu(z)).astype(dz_ref.dtype)
                dl_sc[ii] = jnp.sum((dov * ov).T, axis=0, keepdims=True)
                dqt_sc[ii] = jnp.zeros((HEAD_PAD, t), F32)

        kvb = kv_ref[...]
        k = jnp.concatenate([kvb[:, 0:QK_NOPE_DIM], kr_ref[...]], axis=-1)
        v = kvb[:, QK_NOPE_DIM:]
        kt = k.astype(F32).T.astype(BF16)
        dk_sc[...] = jnp.zeros_like(dk_sc)
        dv_sc[...] = jnp.zeros_like(dv_sc)

        def step(i, masked):
            off = pl.multiple_of(i * t, t)
            q = q_ref[pl.ds(off, t), :]
            dov = do_sc[i]
            st = lax.dot_general(k, q, NT, preferred_element_type=F32) * scale
            pt = jnp.exp(st - lse_ref[i])
            if masked:
                pt = jnp.where(_causal_mask(t, True), pt, 0.0)
            dv_sc[...] += jnp.dot(pt.astype(BF16), dov, preferred_element_type=F32)
            dpt = lax.dot_general(v, dov, NT, preferred_element_type=F32)
            dst = (pt * (dpt - dl_sc[i]) * scale).astype(BF16)
            dk_sc[...] += jnp.dot(dst, q, preferred_element_type=F32)
            dqt_sc[i] += jnp.dot(kt, dst, preferred_element_type=F32)

        def loop_body(i, carry):
            step(i, False)
            return carry

        step(j, True)
        lax.fori_loop(j + 1, nq, loop_body, 0)
        dkv_ref[:, 0:QK_NOPE_DIM] = dk_sc[:, 0:QK_NOPE_DIM].astype(dkv_ref.dtype)
        dkv_ref[:, QK_NOPE_DIM:] = dv_sc[...].astype(dkv_ref.dtype)
        dkr_ref[...] = dk_sc[:, QK_NOPE_DIM:]

        @pl.when(j == nq - 1)
        def _():
            for ii in range(nq):
                dq_ref[ii * t:(ii + 1) * t, :] = dqt_sc[ii].T

    whole = lambda w, off: pl.BlockSpec((S, w), lambda h, j: (0, off + h))
    return pl.pallas_call(
        body, name=name, grid=(H, nq),
        in_specs=[whole(HEAD_PAD, 0), pl.BlockSpec((t, HEAD_PAD), lambda h, j: (j, h)),
                  pl.BlockSpec((t, LANE), lambda h, j: (j, 0)), whole(V_HEAD_DIM, 0), whole(V_HEAD_DIM, 0),
                  whole(V_HEAD_DIM, zb), pl.BlockSpec((None, nq, 1, t), lambda h, j: (h, 0, 0, 0))],
        out_specs=[whole(HEAD_PAD, 0), pl.BlockSpec((t, HEAD_PAD), lambda h, j: (j, h)),
                   pl.BlockSpec((t, LANE), lambda h, j: (j, h)), whole(V_HEAD_DIM, zb)],
        out_shape=[jax.ShapeDtypeStruct((S, W), F32), jax.ShapeDtypeStruct((S, W), BF16),
                   jax.ShapeDtypeStruct((S, H * LANE), F32), jax.ShapeDtypeStruct((S, U), BF16)],
        scratch_shapes=[pltpu.VMEM((nq, t, V_HEAD_DIM), BF16), pltpu.VMEM((nq, 1, t), F32),
                        pltpu.VMEM((nq, HEAD_PAD, t), F32), pltpu.VMEM((t, HEAD_PAD), F32), pltpu.VMEM((t, V_HEAD_DIM), F32)],
        compiler_params=_cp("parallel", "arbitrary"),
    )(qr, kvh, krp, dp, o, u, lse)


def _adamw_math(w, g, m, v):
    m = ADAM_B1 * m + (1.0 - ADAM_B1) * g
    v = ADAM_B2 * v + (1.0 - ADAM_B2) * (g * g)
    m_hat = m / (1.0 - ADAM_B1 ** ADAM_STEP)
    v_hat = v / (1.0 - ADAM_B2 ** ADAM_STEP)
    delta = -ADAM_LR * (m_hat / (jnp.sqrt(v_hat) + ADAM_EPS) + ADAM_WD * w)
    return delta, m, v


def _adamw(w, g, m, v, name):
    R, C = w.shape
    tr = _tile(R, 256, SUBLANE)
    tc = _tile(C, 1024, LANE)

    def body(w_ref, g_ref, m_ref, v_ref, d_ref, nm_ref, nv_ref):
        d, nm, nv = _adamw_math(w_ref[...], g_ref[...], m_ref[...], v_ref[...])
        d_ref[...] = d
        nm_ref[...] = nm
        nv_ref[...] = nv

    blk = pl.BlockSpec((tr, tc), lambda i, j: (i, j))
    sh = jax.ShapeDtypeStruct((R, C), F32)
    return pl.pallas_call(
        body, name=name, grid=(R // tr, C // tc), in_specs=[blk] * 4, out_specs=[blk] * 3, out_shape=[sh] * 3,
        compiler_params=_cp("parallel", "parallel"),
    )(w, g, m, v)


def _sum_leading(x, name):
    n, R, C = x.shape
    tr = _tile(R, 512, SUBLANE)

    def body(x_ref, o_ref):
        acc = x_ref[0]
        for k in range(1, n):
            acc = acc + x_ref[k]
        o_ref[...] = acc

    return pl.pallas_call(
        body, name=name, grid=(R // tr,), in_specs=[pl.BlockSpec((n, tr, C), lambda i: (0, i, 0))],
        out_specs=pl.BlockSpec((tr, C), lambda i: (i, 0)), out_shape=jax.ShapeDtypeStruct((R, C), F32),
        compiler_params=_cp("parallel"),
    )(x)


def _pair_add(full, recv, c_idx, name):
    n, R, C = full.shape
    h = R // 2
    tr = _tile(h, 256, 2 * SUBLANE)
    tc = _tile(C, 1024, LANE)
    nb = h // tr

    def body(c_ref, a_ref, b_ref, o_ref):
        del c_ref
        o_ref[...] = (a_ref[...].astype(F32) + b_ref[...].astype(F32)).astype(o_ref.dtype)

    return pl.pallas_call(
        body, name=name,
        grid_spec=pltpu.PrefetchScalarGridSpec(
            num_scalar_prefetch=1, grid=(n, nb, C // tc),
            in_specs=[pl.BlockSpec((None, tr, tc), lambda k, i, j, c: (k, c[0] * nb + i, j)),
                      pl.BlockSpec((None, tr, tc), lambda k, i, j, c: (k, i, j))],
            out_specs=pl.BlockSpec((None, tr, tc), lambda k, i, j, c: (k, i, j))),
        out_shape=jax.ShapeDtypeStruct((n, h, C), BF16), compiler_params=_cp("parallel", "parallel", "parallel"),
    )(c_idx, full, recv)


def _chip_sum(pair, recv, idx, name):
    n, h, C = pair.shape
    tr = _tile(h, 256, 2 * SUBLANE)
    tc = _tile(C, 1024, LANE)
    nb = h // tr

    def body(c_ref, chip_ref, a_ref, b_ref, o_ref):
        del c_ref, chip_ref
        acc = a_ref[...].astype(F32)
        for k in range(N_CHIP - 1):
            acc = acc + b_ref[k].astype(F32)
        o_ref[...] = acc

    return pl.pallas_call(
        body, name=name,
        grid_spec=pltpu.PrefetchScalarGridSpec(
            num_scalar_prefetch=2, grid=(nb, C // tc),
            in_specs=[pl.BlockSpec((None, tr, tc), lambda i, j, c, chip: (chip[0], i, j)),
                      pl.BlockSpec((N_CHIP - 1, tr, tc), lambda i, j, c, chip: (0, i, j))],
            out_specs=pl.BlockSpec((tr, tc), lambda i, j, c, chip: (c[0] * nb + i, j))),
        out_shape=jax.ShapeDtypeStruct((2 * h, C), F32), compiler_params=_cp("parallel", "parallel"),
    )(idx[0], idx[1], pair, recv)


def _cast_place(w, idx, name):
    R, C = w.shape
    h = R // 2
    tr = _tile(h, 256, 2 * SUBLANE)
    tc = _tile(C, 1024, LANE)
    nb = h // tr

    def body(c_ref, chip_ref, w_ref, o_ref):
        del c_ref, chip_ref
        o_ref[...] = w_ref[...].astype(o_ref.dtype)

    return pl.pallas_call(
        body, name=name,
        grid_spec=pltpu.PrefetchScalarGridSpec(
            num_scalar_prefetch=2, grid=(nb, C // tc),
            in_specs=[pl.BlockSpec((tr, tc), lambda i, j, c, chip: (c[0] * nb + i, j))],
            out_specs=pl.BlockSpec((None, tr, tc), lambda i, j, c, chip: (chip[0], c[0] * nb + i, j))),
        out_shape=jax.ShapeDtypeStruct((N_CHIP, R, C), BF16), compiler_params=_cp("parallel", "parallel"),
    )(idx[0], idx[1], w)


def _coords():
    return lax.axis_index("x"), lax.axis_index("y"), lax.axis_index("c")


def _allgather_small(x_shard, name):
    m_per, n = x_shard.shape

    def body(x_ref, out_ref, send_sems, recv_sems, local_sem):
        x, y, c = _coords()
        me, sibling = (x, y, c), (x, y, 1 - c)
        chips = [(1 - x, y), (x, 1 - y), (1 - x, 1 - y)]

        def rows(px, py, pc):
            return out_ref.at[pl.ds((4 * px + 2 * py + pc) * m_per, m_per), :]

        def copy(k, block, to, src=None):
            return pltpu.make_async_remote_copy(
                src_ref=rows(*block) if src is None else src, dst_ref=rows(*block), send_sem=send_sems.at[k],
                recv_sem=recv_sems.at[k], device_id=to, device_id_type=MESH_ID)

        mine = pltpu.make_async_copy(x_ref, rows(*me), local_sem)
        mine.start()
        first = [copy(0, me, sibling, src=x_ref)]
        first += [copy(1 + j, me, (*chip, c), src=x_ref) for j, chip in enumerate(chips)]
        for cp in first:
            cp.start()
        passed = [copy(4 + j, (*chip, c), sibling) for j, chip in enumerate(chips)]
        for j, chip in enumerate(chips):
            copy(1 + j, (*chip, c), me).wait_recv()
            passed[j].start()
        copy(0, sibling, me).wait_recv()
        for j, chip in enumerate(chips):
            copy(4 + j, (*chip, 1 - c), me).wait_recv()
        for cp in first + passed:
            cp.wait_send()
        mine.wait()

    return pl.pallas_call(
        body, name=name, out_shape=jax.ShapeDtypeStruct((N_DEV * m_per, n), x_shard.dtype),
        in_specs=[pl.BlockSpec(memory_space=pltpu.VMEM)], out_specs=pl.BlockSpec(memory_space=pltpu.VMEM),
        scratch_shapes=[pltpu.SemaphoreType.DMA((7,)), pltpu.SemaphoreType.DMA((7,)), pltpu.SemaphoreType.DMA],
        compiler_params=pltpu.CompilerParams(vmem_limit_bytes=VMEM_LIMIT),
    )(x_shard)


def _allgather_weights(bufs, name):
    n = len(bufs)

    def body(*refs):
        outs = refs[n:2 * n]
        send_sems, recv_sems = refs[2 * n:]
        x, y, c = _coords()
        me, sibling = (x, y, c), (x, y, 1 - c)
        chips = [(1 - x, y), (x, 1 - y), (1 - x, 1 - y)]

        def win(a, px, py, pc):
            h = bufs[a].shape[1] // 2
            return outs[a].at[2 * px + py, pl.ds(pc * h, h), :]

        def copy(a, k, block, to):
            return pltpu.make_async_remote_copy(
                src_ref=win(a, *block), dst_ref=win(a, *block), send_sem=send_sems.at[a, k],
                recv_sem=recv_sems.at[a, k], device_id=to, device_id_type=MESH_ID)

        first = []
        for a in range(n):
            first.append(copy(a, 0, me, sibling))
            first += [copy(a, 1 + j, me, (*chip, c)) for j, chip in enumerate(chips)]
        for cp in first:
            cp.start()
        passed = []
        for a in range(n):
            for j, chip in enumerate(chips):
                copy(a, 1 + j, (*chip, c), me).wait_recv()
                fwd = copy(a, 4 + j, (*chip, c), sibling)
                fwd.start()
                passed.append(fwd)
        for a in range(n):
            copy(a, 0, sibling, me).wait_recv()
            for j, chip in enumerate(chips):
                copy(a, 4 + j, (*chip, 1 - c), me).wait_recv()
        for cp in first + passed:
            cp.wait_send()

    any_spec = pl.BlockSpec(memory_space=pl.ANY)
    return pl.pallas_call(
        body, name=name, out_shape=[jax.ShapeDtypeStruct(b.shape, b.dtype) for b in bufs],
        in_specs=[any_spec] * n, out_specs=[any_spec] * n, input_output_aliases={a: a for a in range(n)},
        scratch_shapes=[pltpu.SemaphoreType.DMA((n, 7)), pltpu.SemaphoreType.DMA((n, 7))],
    )(*bufs)


def _pair_exchange(grads, name):
    n = len(grads)

    def body(*refs):
        ins, outs = refs[:n], refs[n:2 * n]
        send_sems, recv_sems = refs[2 * n:]
        x, y, c = _coords()
        sibling = (x, y, 1 - c)
        copies = []
        for a in range(n):
            h = grads[a].shape[1] // 2
            cp = pltpu.make_async_remote_copy(
                src_ref=ins[a].at[:, pl.ds((1 - c) * h, h), :], dst_ref=outs[a], send_sem=send_sems.at[a],
                recv_sem=recv_sems.at[a], device_id=sibling, device_id_type=MESH_ID)
            cp.start()
            copies.append(cp)
        for cp in copies:
            cp.wait()

    any_spec = pl.BlockSpec(memory_space=pl.ANY)
    return pl.pallas_call(
        body, name=name,
        out_shape=[jax.ShapeDtypeStruct((g.shape[0], g.shape[1] // 2, g.shape[2]), g.dtype) for g in grads],
        in_specs=[any_spec] * n, out_specs=[any_spec] * n,
        scratch_shapes=[pltpu.SemaphoreType.DMA((n,)), pltpu.SemaphoreType.DMA((n,))],
    )(*grads)


def _chip_exchange(pairs, name):
    n = len(pairs)

    def body(*refs):
        ins, outs = refs[:n], refs[n:2 * n]
        send_sems, recv_sems = refs[2 * n:]
        x, y, c = _coords()
        chips = [(1 - x, y), (x, 1 - y), (1 - x, 1 - y)]
        copies = []
        for a in range(n):
            for k, (px, py) in enumerate(chips):
                cp = pltpu.make_async_remote_copy(
                    src_ref=ins[a].at[2 * px + py], dst_ref=outs[a].at[k], send_sem=send_sems.at[a, k],
                    recv_sem=recv_sems.at[a, k], device_id=(px, py, c), device_id_type=MESH_ID)
                cp.start()
                copies.append(cp)
        for cp in copies:
            cp.wait()

    any_spec = pl.BlockSpec(memory_space=pl.ANY)
    return pl.pallas_call(
        body, name=name,
        out_shape=[jax.ShapeDtypeStruct((N_CHIP - 1,) + p.shape[1:], p.dtype) for p in pairs],
        in_specs=[any_spec] * n, out_specs=[any_spec] * n,
        scratch_shapes=[pltpu.SemaphoreType.DMA((n, N_CHIP - 1)), pltpu.SemaphoreType.DMA((n, N_CHIP - 1))],
    )(*pairs)


def _half_share(bufs, name):
    n = len(bufs)

    def body(*refs):
        outs = refs[n:2 * n]
        send_sems, recv_sems = refs[2 * n:]
        x, y, c = _coords()
        sibling = (x, y, 1 - c)

        def copy(a, pc):
            h = bufs[a].shape[0] // 2
            rows = outs[a].at[pl.ds(pc * h, h), :]
            return pltpu.make_async_remote_copy(
                src_ref=rows, dst_ref=rows, send_sem=send_sems.at[a], recv_sem=recv_sems.at[a], device_id=sibling,
                device_id_type=MESH_ID)

        for a in range(n):
            copy(a, c).start()
        for a in range(n):
            copy(a, c).wait_send()
            copy(a, 1 - c).wait_recv()

    any_spec = pl.BlockSpec(memory_space=pl.ANY)
    return pl.pallas_call(
        body, name=name, out_shape=[jax.ShapeDtypeStruct(b.shape, b.dtype) for b in bufs],
        in_specs=[any_spec] * n, out_specs=[any_spec] * n, input_output_aliases={a: a for a in range(n)},
        scratch_shapes=[pltpu.SemaphoreType.DMA((n,)), pltpu.SemaphoreType.DMA((n,))],
    )(*bufs)


def _reduce_scatter(grads, idx):
    names = list(grads)
    full = [grads[k] for k in names]
    recv = _pair_exchange(full, "rs_pair_exchange")
    pairs = [_pair_add(f, r, idx[0], "rs_pair_add_" + k) for k, f, r in zip(names, full, recv)]
    got = _chip_exchange(pairs, "rs_chip_exchange")
    halves = [_chip_sum(p, g, idx, "rs_chip_sum_" + k) for k, p, g in zip(names, pairs, got)]
    outs = _half_share(halves, "rs_half_share")
    return dict(zip(names, outs))


PACK_ALIGN = SUBLANE * LANE
PACK_ROWS_ALIGN = 256 * LANE


def _pack(parts):
    flat, offs, off = [], [], 0
    for p in parts:
        v = p.reshape(-1).astype(F32)
        n = v.shape[0]
        padded = -(-n // PACK_ALIGN) * PACK_ALIGN
        flat.append(jnp.pad(v, (0, padded - n)))
        offs.append((off, n))
        off += padded
    tail = -off % PACK_ROWS_ALIGN
    if tail:
        flat.append(jnp.zeros((tail,), F32))
    return jnp.concatenate(flat).reshape(-1, LANE), offs


def _unpack(flat, offs, shapes):
    return [flat[o:o + n].reshape(s) for (o, n), s in zip(offs, shapes)]


def _chipcat(g, per_dev_len, offs, shape, axis):
    o, n = offs
    parts = [g[2 * j, o:o + n].reshape(shape) for j in range(N_CHIP)]
    return jnp.concatenate(parts, axis=axis)


def kernel(x, c, w_ada, b_ada, ln_g, ln_b, a_w_in, a_w_dw, a_b_dw, a_norm_g, a_norm_b, a_w_out, b_w_in, b_q_norm_g, b_w_qb, b_w_out, kv_w_a, kv_norm_g, kv_w_b, loss_target, m_w_ada, m_b_ada, m_ln_g, m_ln_b, m_a_w_in, m_a_w_dw, m_a_b_dw, m_a_norm_g, m_a_norm_b, m_a_w_out, m_b_w_in, m_b_q_norm_g, m_b_w_qb, m_b_w_out, m_kv_w_a, m_kv_norm_g, m_kv_w_b, v_w_ada, v_b_ada, v_ln_g, v_ln_b, v_a_w_in, v_a_w_dw, v_a_b_dw, v_a_norm_g, v_a_norm_b, v_a_w_out, v_b_w_in, v_b_q_norm_g, v_b_w_qb, v_b_w_out, v_kv_w_a, v_kv_norm_g, v_kv_w_b):
    xi, yi, ci = _coords()
    chip = 2 * xi + yi
    dev = 4 * xi + 2 * yi + ci
    idx = (jnp.reshape(ci, (1,)).astype(jnp.int32), jnp.reshape(chip, (1,)).astype(jnp.int32))

    x2 = x[0]
    tgt = loss_target[0]
    S, D = x2.shape
    C = a_w_out.shape[1] * N_CHIP
    Cq = C // N_CHIP
    KS = a_w_dw.shape[1]
    Q = b_q_norm_g.shape[1]
    KV = kv_norm_g.shape[0]
    Hq = kv_w_b.shape[1] // HEAD_PAD
    H = Hq * N_CHIP
    W = H * V_HEAD_DIM
    Nq = w_ada.shape[2]
    head_q = QK_NOPE_DIM + QK_ROPE_DIM
    scale = head_q ** -0.5
    tabs = _rope_tables(S)

    qb_pad = jnp.pad(b_w_qb[0].reshape(Q, Hq, head_q), ((0, 0), (0, 0), (0, HEAD_PAD - head_q))).reshape(Q, Hq * HEAD_PAD)
    kva_pad = jnp.pad(kv_w_a, ((0, 0), (0, LANE - QK_ROPE_DIM)))
    shards = {"a_w_in": a_w_in[0], "a_w_out": a_w_out[0], "b_w_in": b_w_in[0], "b_w_qb": qb_pad, "b_w_out": b_w_out[0],
              "kv_w_a": kva_pad, "kv_w_b": kv_w_b}
    wg = _allgather_weights([_cast_place(w, idx, "cast_" + k) for k, w in shards.items()], "allgather_weights")
    W_ain, W_aout, W_bin, W_qb, W_bout, W_kva, W_kvb = wg
    W_aout = W_aout.reshape(C, D)
    W_bout = W_bout.reshape(W, D)
    W_kva = W_kva.reshape(D, KV + LANE)

    pack1, offs1 = _pack([c[0], a_w_dw[0], a_b_dw[0], a_norm_g[0], a_norm_b[0]])
    L1 = pack1.shape[0] * LANE
    g1 = _allgather_small(pack1, "allgather_small_in").reshape(N_DEV, L1)
    c_all = g1[:, :D]
    w_dw = _chipcat(g1, L1, offs1[1], (KS, Cq), 1)
    b_dw = _chipcat(g1, L1, offs1[2], (1, Cq), 1)
    g_cn = _chipcat(g1, L1, offs1[3], (1, Cq), 1)
    b_cn = _chipcat(g1, L1, offs1[4], (1, Cq), 1)

    b_ada_sh = lax.dynamic_slice_in_dim(b_ada, chip * Nq, Nq, axis=1)[:, None, :]
    mod_sh = _mod(c_all, w_ada, b_ada_sh, "adaln_mod")
    gm = _allgather_small(mod_sh.reshape(DEPTH * N_DEV, Nq), "allgather_small_mod").reshape(N_CHIP, 2, DEPTH, N_DEV, Nq)
    mod_rows = lax.dynamic_index_in_dim(gm[:, 0], dev, axis=2, keepdims=False)
    mod_me = jnp.transpose(mod_rows, (1, 0, 2)).reshape(DEPTH, N_CHIP * Nq)
    shift = [mod_me[l:l + 1, 0:D] for l in range(DEPTH)]
    scl = [mod_me[l:l + 1, D:2 * D] for l in range(DEPTH)]
    gate = [mod_me[l:l + 1, 2 * D:3 * D] for l in range(DEPTH)]

    h0 = _lnmod_fwd(x2, scl[0], shift[0], "a_lnmod_fwd")
    u0 = _mm(h0, W_ain, b_sh=True, name="a_in_fwd")
    v2 = _conv1_fwd(u0, w_dw, b_dw, "a_conv1_fwd")
    p0 = _conv2_fwd(v2, u0, g_cn, b_cn, "a_conv2_fwd")
    out0 = _mm(p0, W_aout, name="a_out_fwd")
    x1 = _resln_fwd(x2, out0, gate[0], ln_g[0:1], ln_b[0:1], "a_resln_fwd")

    kva = _mm(x1, W_kva, name="kv_a_fwd")
    ckv, krp = _kvprep_fwd(kva, kv_norm_g[None, :], tabs, "kv_prep_fwd")
    kvh = _mm(ckv, W_kvb, b_sh=True, out_dtype=BF16, name="kv_b_fwd")

    h1 = _lnmod_fwd(x1, scl[1], shift[1], "b_lnmod_fwd")
    u1 = _mm(h1, W_bin, b_sh=True, name="b_in_fwd")
    qn = _rms_fwd(u1, Q, b_q_norm_g, "b_qnorm_fwd")
    qraw = _mm(qn, W_qb, b_sh=True, name="b_qb_fwd")
    o, p1, qr, lse = _attn_fwd(qraw, kvh, krp, u1, Q, tabs, scale, "b_attn_fwd")
    out1 = _mm(p1, W_bout, name="b_out_fwd")

    dxa1, dout1, dgate1, dlng1, dlnb1, loss_part = _resln_bwd(
        x1, out1, gate[1], ln_g[1:2], ln_b[1:2], tgt, True, "b_resln_bwd")
    dW_bout = _mm(p1, dout1, ta=True, out_dtype=BF16, name="b_out_dw").reshape(N_CHIP, W // N_CHIP, D)
    dp1 = _mm(dout1, W_bout, tb=True, name="b_out_dx")
    dqr, dkvh, dkr_h, du1 = _attn_bwd(qr, kvh, krp, dp1, o, u1, Q, lse, scale, "b_attn_bwd")
    dqraw = _qrope_bwd(dqr, tabs, "b_qrope_bwd")
    dW_qb = _mm(qn, dqraw, ta=True, o_sh=True, out_dtype=BF16, name="b_qb_dw")
    dqn = _mm(dqraw, W_qb, tb=True, b_sh=True, name="b_qb_dx")
    du1, dgq = _rms_bwd(u1, Q, dqn, b_q_norm_g, du1, "b_qnorm_bwd")
    dW_bin = _mm(h1, du1, ta=True, o_sh=True, out_dtype=BF16, name="b_in_dw")
    dh1 = _mm(du1, W_bin, tb=True, b_sh=True, name="b_in_dx")

    dW_kvb = _mm(ckv, dkvh, ta=True, o_sh=True, out_dtype=BF16, name="kv_b_dw")
    dckv = _mm(dkvh, W_kvb, tb=True, b_sh=True, name="kv_b_dx")
    dkva, dgkv = _kvprep_bwd(kva, dckv, dkr_h, kv_norm_g[None, :], tabs, "kv_prep_bwd")
    dW_kva = _mm(x1, dkva, ta=True, out_dtype=BF16, name="kv_a_dw").reshape(N_CHIP, D // N_CHIP, KV + LANE)
    dx1_kv = _mm(dkva, W_kva, tb=True, name="kv_a_dx")
    dx1, dsc1, dsh1 = _lnmod_bwd(x1, dh1, scl[1], [dxa1, dx1_kv], "b_lnmod_bwd")

    dxa0, dout0, dgate0, dlng0, dlnb0 = _resln_bwd(x2, out0, gate[0], ln_g[0:1], ln_b[0:1], dx1, False, "a_resln_bwd")
    dW_aout = _mm(p0, dout0, ta=True, out_dtype=BF16, name="a_out_dw").reshape(N_CHIP, Cq, D)
    dp0 = _mm(dout0, W_aout, tb=True, name="a_out_dx")
    dv2, dz0, dgcn, dbcn = _conv2_bwd(dp0, v2, u0, g_cn, b_cn, "a_conv2_bwd")
    da0, dg0, dwdw, dbdw = _conv1_bwd(dv2, u0, w_dw, "a_conv1_bwd")
    du0 = jnp.concatenate([da0, dg0, dz0], axis=1)
    dW_ain = _mm(h0, du0, ta=True, o_sh=True, out_dtype=BF16, name="a_in_dw")
    dh0 = _mm(du0, W_ain, tb=True, b_sh=True, name="a_in_dx")
    dx, dsc0, dsh0 = _lnmod_bwd(x2, dh0, scl[0], [dxa0], "a_lnmod_bwd")
    grad_x = dx[None]

    dmod = jnp.concatenate([dsh0, dsc0, dgate0, dsh1, dsc1, dgate1], axis=1).reshape(DEPTH, 3 * D)
    small = [loss_part, dmod, jnp.concatenate([dlng0, dlng1], 0), jnp.concatenate([dlnb0, dlnb1], 0),
             dwdw, dbdw, dgcn, dbcn, dgq, dgkv]
    small_shapes = [p.shape for p in small]
    pack2, offs2 = _pack(small)
    R2 = pack2.shape[0]
    g2 = _allgather_small(pack2, "allgather_small_grads").reshape(N_DEV, R2, LANE)
    tot = _sum_leading(g2, "small_grad_sum").reshape(-1)
    (loss_t, g_b_ada, g_ln_g, g_ln_b, g_wdw_full, g_bdw_full, g_gcn_full, g_bcn_full, g_gq, g_gkv) = _unpack(
        tot, offs2, small_shapes)
    loss = loss_t.reshape(())
    colsl = lambda a: lax.dynamic_slice_in_dim(a, chip * Cq, Cq, axis=1)
    g_wdw, g_bdw, g_gcn, g_bcn = colsl(g_wdw_full), colsl(g_bdw_full), colsl(g_gcn_full), colsl(g_bcn_full)

    dmod_all = jnp.stack([g2[d].reshape(-1)[offs2[1][0]:offs2[1][0] + offs2[1][1]].reshape(DEPTH, 3 * D)
                          for d in range(N_DEV)], axis=1)
    dmod_sh = lax.dynamic_slice_in_dim(dmod_all, chip * Nq, Nq, axis=2)
    g_w_ada = _wada_grad(jnp.transpose(c_all), dmod_sh, "w_ada_grad")

    red = _reduce_scatter(
        {"a_w_in": dW_ain, "a_w_out": dW_aout, "b_w_in": dW_bin, "b_w_qb": dW_qb, "b_w_out": dW_bout,
         "kv_w_a": dW_kva, "kv_w_b": dW_kvb}, idx)
    g_a_w_in = red["a_w_in"]
    g_a_w_out = red["a_w_out"]
    g_b_w_in = red["b_w_in"]
    g_b_w_qb = red["b_w_qb"].reshape(Q, Hq, HEAD_PAD)[:, :, :head_q].reshape(Q, Hq * head_q)
    g_b_w_out = red["b_w_out"]
    g_kv_w_a = red["kv_w_a"][:, :KV + QK_ROPE_DIM]
    g_kv_w_b = red["kv_w_b"]

    grads = {
        "w_ada": g_w_ada, "b_ada": g_b_ada, "ln_g": g_ln_g, "ln_b": g_ln_b, "a_w_in": g_a_w_in[None],
        "a_w_dw": g_wdw[None], "a_b_dw": g_bdw, "a_norm_g": g_gcn, "a_norm_b": g_bcn, "a_w_out": g_a_w_out[None],
        "b_w_in": g_b_w_in[None], "b_q_norm_g": g_gq, "b_w_qb": g_b_w_qb[None], "b_w_out": g_b_w_out[None],
        "kv_w_a": g_kv_w_a, "kv_norm_g": g_gkv.reshape(KV), "kv_w_b": g_kv_w_b,
    }
    weights = {
        "w_ada": (w_ada, m_w_ada, v_w_ada), "b_ada": (b_ada, m_b_ada, v_b_ada), "ln_g": (ln_g, m_ln_g, v_ln_g),
        "ln_b": (ln_b, m_ln_b, v_ln_b), "a_w_in": (a_w_in, m_a_w_in, v_a_w_in), "a_w_dw": (a_w_dw, m_a_w_dw, v_a_w_dw),
        "a_b_dw": (a_b_dw, m_a_b_dw, v_a_b_dw), "a_norm_g": (a_norm_g, m_a_norm_g, v_a_norm_g),
        "a_norm_b": (a_norm_b, m_a_norm_b, v_a_norm_b), "a_w_out": (a_w_out, m_a_w_out, v_a_w_out),
        "b_w_in": (b_w_in, m_b_w_in, v_b_w_in), "b_q_norm_g": (b_q_norm_g, m_b_q_norm_g, v_b_q_norm_g),
        "b_w_qb": (b_w_qb, m_b_w_qb, v_b_w_qb), "b_w_out": (b_w_out, m_b_w_out, v_b_w_out),
        "kv_w_a": (kv_w_a, m_kv_w_a, v_kv_w_a), "kv_norm_g": (kv_norm_g, m_kv_norm_g, v_kv_norm_g),
        "kv_w_b": (kv_w_b, m_kv_w_b, v_kv_w_b),
    }
    order = list(weights)
    big = [k for k in order if weights[k][0].size >= (1 << 16) and weights[k][0].shape[-1] % LANE == 0]
    small_names = [k for k in order if k not in big]
    upd = {}
    for k in big:
        w, m, v = weights[k]
        shp = w.shape
        two = (-1, shp[-1])
        d_, m_, v_ = _adamw(w.reshape(two), grads[k].reshape(two), m.reshape(two), v.reshape(two), "adamw_" + k)
        upd[k] = (grads[k].reshape(shp), d_.reshape(shp), m_.reshape(shp), v_.reshape(shp))
    sw, offs3 = _pack([weights[k][0] for k in small_names])
    sg, _ = _pack([grads[k] for k in small_names])
    sm, _ = _pack([weights[k][1] for k in small_names])
    sv, _ = _pack([weights[k][2] for k in small_names])
    sd, snm, snv = _adamw(sw, sg, sm, sv, "adamw_small")
    shapes3 = [weights[k][0].shape for k in small_names]
    for k, d_, m_, v_ in zip(small_names, _unpack(sd.reshape(-1), offs3, shapes3), _unpack(snm.reshape(-1), offs3, shapes3),
                             _unpack(snv.reshape(-1), offs3, shapes3)):
        upd[k] = (grads[k].reshape(weights[k][0].shape), d_, m_, v_)

    return (loss, grad_x, *[upd[k][0] for k in order], *[upd[k][1] for k in order], *[upd[k][2] for k in order],
            *[upd[k][3] for k in order])
```

```python
import math
from typing import Callable, NamedTuple

import jax
import jax.numpy as jnp
from jax import lax
from jax.experimental import pallas as pl
from jax.experimental.pallas import tpu as pltpu

F32 = jnp.float32
BF16 = jnp.bfloat16

LN_EPS = 1e-5
RMS_EPS = 1e-6
DEPTH = 2
DEEPNORM_ALPHA = (2.0 * DEPTH) ** 0.25
QK_NOPE_DIM = 128
QK_ROPE_DIM = 64
V_HEAD_DIM = 128
HEAD_PAD = 256
ROPE_BASE = 10000.0
ADAM_LR = 0.001
ADAM_B1 = 0.9
ADAM_B2 = 0.999
ADAM_EPS = 1e-08
ADAM_WD = 0.01
ADAM_STEP = 10

N_DEV = 8
N_CHIP = 4
LANE = 128
SUBLANE = 8
VMEM_LIMIT = 48 * 1024 * 1024
CONV_HALO = 32
MESH_ID = pl.DeviceIdType.MESH
NT = (((1,), (1,)), ((), ()))


def _cp(*sem):
    return pltpu.CompilerParams(dimension_semantics=sem, vmem_limit_bytes=VMEM_LIMIT)


def _tile(n, pref, align):
    if n <= pref:
        return n
    t = (pref // align) * align
    while t > align and n % t:
        t -= align
    assert n % t == 0, (n, pref, align)
    return t


def _silu(v):
    return v * jax.nn.sigmoid(v)


def _dsilu(v):
    s = jax.nn.sigmoid(v)
    return s * (1.0 + v * (1.0 - s))


class _Carry(NamedTuple):
    bufs: tuple
    out_shapes: tuple
    aliases: dict
    sem_shape: tuple
    start: Callable
    finish: Callable


def _pcall(body, *, name, grid, in_specs, out_specs, out_shape, scratch_shapes, sem, args, carry=None):
    if carry is None:
        return pl.pallas_call(body, name=name, grid=grid, in_specs=in_specs, out_specs=out_specs, out_shape=out_shape,
                              scratch_shapes=scratch_shapes, compiler_params=_cp(*sem))(*args)
    n_in, n_out, n_sc = len(in_specs), len(out_specs), len(scratch_shapes)
    nc_in, nc_out = len(carry.bufs), len(carry.out_shapes)

    def wrapped(*refs):
        core_in, c_in = refs[:n_in], refs[n_in:n_in + nc_in]
        core_out = refs[n_in + nc_in:n_in + nc_in + n_out]
        c_out = refs[n_in + nc_in + n_out:n_in + nc_in + n_out + nc_out]
        core_sc = refs[n_in + nc_in + n_out + nc_out:n_in + nc_in + n_out + nc_out + n_sc]
        send_sems, recv_sems = refs[-2:]
        first = pl.program_id(0) == 0
        last = pl.program_id(0) == grid[0] - 1
        for d in range(1, len(grid)):
            first = jnp.logical_and(first, pl.program_id(d) == 0)
            last = jnp.logical_and(last, pl.program_id(d) == grid[d] - 1)

        @pl.when(first)
        def _():
            carry.start(c_in, c_out, send_sems, recv_sems)

        body(*core_in, *core_out, *core_sc)

        @pl.when(last)
        def _():
            carry.finish(c_in, c_out, send_sems, recv_sems)

    any_spec = pl.BlockSpec(memory_space=pl.ANY)
    res = pl.pallas_call(
        wrapped, name=name, grid=grid, in_specs=list(in_specs) + [any_spec] * nc_in,
        out_specs=list(out_specs) + [any_spec] * nc_out, out_shape=list(out_shape) + list(carry.out_shapes),
        input_output_aliases={n_in + i: n_out + o for i, o in carry.aliases.items()},
        scratch_shapes=list(scratch_shapes) + [pltpu.SemaphoreType.DMA(carry.sem_shape), pltpu.SemaphoreType.DMA(carry.sem_shape)],
        compiler_params=_cp(*(("arbitrary",) * len(grid))),
    )(*args, *carry.bufs)
    return res[:n_out], res[n_out:]


def _mm(a, b, *, name, ta=False, tb=False, b_sh=False, o_sh=False, out_dtype=F32, tm=1024, tn=1024, tk=512, carry=None):
    M, K = (a.shape[1], a.shape[0]) if ta else a.shape
    if b_sh:
        assert b.shape[0] == N_CHIP
        nq = b.shape[2]
        Kb, N = (nq * N_CHIP, b.shape[1]) if tb else (b.shape[1], nq * N_CHIP)
    else:
        Kb, N = (b.shape[1], b.shape[0]) if tb else b.shape
        nq = N // N_CHIP
    assert K == Kb, (a.shape, b.shape)
    tm = _tile(M, tm, LANE)
    tk = _tile(nq if (b_sh and tb) else K, tk, LANE)
    tn = _tile(nq if ((b_sh and not tb) or o_sh) else N, tn, LANE)
    nk = K // tk

    def body(a_ref, b_ref, o_ref, acc_ref):
        k = pl.program_id(2)

        @pl.when(k == 0)
        def _():
            acc_ref[...] = jnp.zeros_like(acc_ref)

        dn = (((0 if ta else 1,), (1 if tb else 0,)), ((), ()))
        acc_ref[...] += lax.dot_general(a_ref[...].astype(BF16), b_ref[...].astype(BF16), dn,
                                        preferred_element_type=F32)

        @pl.when(k == nk - 1)
        def _():
            o_ref[...] = acc_ref[...].astype(o_ref.dtype)

    a_spec = pl.BlockSpec((tk, tm), lambda i, j, k: (k, i)) if ta else pl.BlockSpec((tm, tk), lambda i, j, k: (i, k))
    if b_sh and not tb:
        per = nq // tn
        b_spec = pl.BlockSpec((None, tk, tn), lambda i, j, k: (j // per, k, j % per))
    elif b_sh and tb:
        per = nq // tk
        b_spec = pl.BlockSpec((None, tn, tk), lambda i, j, k: (k // per, j, k % per))
    elif tb:
        b_spec = pl.BlockSpec((tn, tk), lambda i, j, k: (j, k))
    else:
        b_spec = pl.BlockSpec((tk, tn), lambda i, j, k: (k, j))
    if o_sh:
        per_o = nq // tn
        o_spec = pl.BlockSpec((None, tm, tn), lambda i, j, k: (j // per_o, i, j % per_o))
        o_shape = jax.ShapeDtypeStruct((N_CHIP, M, nq), out_dtype)
    else:
        o_spec = pl.BlockSpec((tm, tn), lambda i, j, k: (i, j))
        o_shape = jax.ShapeDtypeStruct((M, N), out_dtype)
    res = _pcall(body, name=name, grid=(M // tm, N // tn, nk), in_specs=[a_spec, b_spec], out_specs=[o_spec],
                 out_shape=[o_shape], scratch_shapes=[pltpu.VMEM((tm, tn), F32)], sem=("parallel", "parallel", "arbitrary"),
                 args=(a, b), carry=carry)
    return res[0] if carry is None else (res[0][0], res[1])


def _mod(c_all, w_ada, b_sh, name):
    L, D, nq = w_ada.shape
    B = c_all.shape[0]
    tn = _tile(nq, 512, LANE)

    def body(c_ref, w_ref, b_ref, o_ref):
        sc = _silu(c_ref[...]).astype(BF16)
        o_ref[...] = jnp.dot(sc, w_ref[...].astype(BF16), preferred_element_type=F32) + b_ref[...]

    return pl.pallas_call(
        body, name=name, grid=(L, nq // tn),
        in_specs=[pl.BlockSpec((B, D), lambda l, j: (0, 0)), pl.BlockSpec((None, D, tn), lambda l, j: (l, 0, j)),
                  pl.BlockSpec((None, 1, tn), lambda l, j: (l, 0, j))],
        out_specs=pl.BlockSpec((None, B, tn), lambda l, j: (l, 0, j)),
        out_shape=jax.ShapeDtypeStruct((L, B, nq), F32), compiler_params=_cp("parallel", "parallel"),
    )(c_all, w_ada, b_sh)


def _wada_grad(c_all_t, dmod, name):
    D, B = c_all_t.shape
    L, _, nq = dmod.shape
    tm = _tile(D, 512, SUBLANE)
    tn = _tile(nq, 1024, LANE)

    def body(c_ref, d_ref, o_ref):
        sc = _silu(c_ref[...])
        dm = d_ref[...]
        acc = sc[:, 0:1] * dm[0:1, :]
        for b in range(1, B):
            acc = acc + sc[:, b:b + 1] * dm[b:b + 1, :]
        o_ref[...] = acc

    return pl.pallas_call(
        body, name=name, grid=(L, D // tm, nq // tn),
        in_specs=[pl.BlockSpec((tm, B), lambda l, i, j: (i, 0)), pl.BlockSpec((None, B, tn), lambda l, i, j: (l, 0, j))],
        out_specs=pl.BlockSpec((None, tm, tn), lambda l, i, j: (l, i, j)),
        out_shape=jax.ShapeDtypeStruct((L, D, nq), F32), compiler_params=_cp("parallel", "parallel", "parallel"),
    )(c_all_t, dmod)


ROW_TILE = 128


def _ln_stats(v):
    mu = jnp.mean(v, axis=-1, keepdims=True)
    vc = v - mu
    var = jnp.mean(vc * vc, axis=-1, keepdims=True)
    rstd = lax.rsqrt(var + LN_EPS)
    return vc * rstd, rstd


def _ln_bwd(dxhat, xhat, rstd):
    m1 = jnp.mean(dxhat, axis=-1, keepdims=True)
    m2 = jnp.mean(dxhat * xhat, axis=-1, keepdims=True)
    return rstd * (dxhat - m1 - xhat * m2)


def _row_spec(ts, D):
    return pl.BlockSpec((ts, D), lambda i: (i, 0))


def _vec_spec(D):
    return pl.BlockSpec((1, D), lambda i: (0, 0))


def _lnmod_fwd(x, scale, shift, name):
    S, D = x.shape
    ts = _tile(S, ROW_TILE, SUBLANE)

    def body(x_ref, sc_ref, sh_ref, h_ref):
        xn, _ = _ln_stats(x_ref[...])
        h_ref[...] = (xn * (1.0 + sc_ref[...]) + sh_ref[...]).astype(h_ref.dtype)

    return pl.pallas_call(
        body, name=name, grid=(S // ts,), in_specs=[_row_spec(ts, D), _vec_spec(D), _vec_spec(D)],
        out_specs=_row_spec(ts, D), out_shape=jax.ShapeDtypeStruct((S, D), BF16), compiler_params=_cp("parallel"),
    )(x, scale, shift)


def _lnmod_bwd(x, dh, scale, adds, name):
    S, D = x.shape
    ts = _tile(S, ROW_TILE, SUBLANE)
    na = len(adds)

    def body(*refs):
        x_ref, dh_ref, sc_ref = refs[:3]
        add_refs = refs[3:3 + na]
        dx_ref, dsc_ref, dsh_ref = refs[3 + na:]
        i = pl.program_id(0)

        @pl.when(i == 0)
        def _():
            dsc_ref[...] = jnp.zeros_like(dsc_ref)
            dsh_ref[...] = jnp.zeros_like(dsh_ref)

        xn, rstd = _ln_stats(x_ref[...])
        dh = dh_ref[...].astype(F32)
        dx = _ln_bwd(dh * (1.0 + sc_ref[...]), xn, rstd)
        for r in add_refs:
            dx = dx + r[...]
        dx_ref[...] = dx
        dsc_ref[...] += jnp.sum(dh * xn, axis=0, keepdims=True)
        dsh_ref[...] += jnp.sum(dh, axis=0, keepdims=True)

    return pl.pallas_call(
        body, name=name, grid=(S // ts,),
        in_specs=[_row_spec(ts, D), _row_spec(ts, D), _vec_spec(D)] + [_row_spec(ts, D)] * na,
        out_specs=[_row_spec(ts, D), _vec_spec(D), _vec_spec(D)],
        out_shape=[jax.ShapeDtypeStruct((S, D), F32), jax.ShapeDtypeStruct((1, D), F32), jax.ShapeDtypeStruct((1, D), F32)],
        compiler_params=_cp("arbitrary"),
    )(x, dh, scale, *adds)


def _resln_fwd(x, out, gate, g, b, name):
    S, D = x.shape
    ts = _tile(S, ROW_TILE, SUBLANE)

    def body(x_ref, o_ref, gt_ref, g_ref, b_ref, y_ref):
        r = DEEPNORM_ALPHA * x_ref[...] + (1.0 + gt_ref[...]) * o_ref[...]
        xhat, _ = _ln_stats(r)
        y_ref[...] = xhat * g_ref[...] + b_ref[...]

    return pl.pallas_call(
        body, name=name, grid=(S // ts,),
        in_specs=[_row_spec(ts, D), _row_spec(ts, D), _vec_spec(D), _vec_spec(D), _vec_spec(D)],
        out_specs=_row_spec(ts, D), out_shape=jax.ShapeDtypeStruct((S, D), F32), compiler_params=_cp("parallel"),
    )(x, out, gate, g, b)


def _resln_bwd(x, out, gate, g, b, dy_or_target, from_target, name):
    S, D = x.shape
    ts = _tile(S, ROW_TILE, SUBLANE)

    def body(x_ref, o_ref, gt_ref, g_ref, b_ref, t_ref, dxa_ref, dout_ref, dgt_ref, dg_ref, db_ref, *maybe_loss):
        i = pl.program_id(0)

        @pl.when(i == 0)
        def _():
            dgt_ref[...] = jnp.zeros_like(dgt_ref)
            dg_ref[...] = jnp.zeros_like(dg_ref)
            db_ref[...] = jnp.zeros_like(db_ref)
            if from_target:
                maybe_loss[0][...] = jnp.zeros_like(maybe_loss[0])

        ov = o_ref[...]
        g1 = 1.0 + gt_ref[...]
        r = DEEPNORM_ALPHA * x_ref[...] + g1 * ov
        xhat, rstd = _ln_stats(r)
        if from_target:
            err = xhat * g_ref[...] + b_ref[...] - t_ref[...]
            dy = err * (1.0 / D)
            maybe_loss[0][...] += jnp.sum(jnp.sum(err * err, axis=-1, keepdims=True), axis=0, keepdims=True) * (0.5 / D)
        else:
            dy = t_ref[...]
        dr = _ln_bwd(dy * g_ref[...], xhat, rstd)
        dxa_ref[...] = DEEPNORM_ALPHA * dr
        dout_ref[...] = (dr * g1).astype(dout_ref.dtype)
        dgt_ref[...] += jnp.sum(dr * ov, axis=0, keepdims=True)
        dg_ref[...] += jnp.sum(dy * xhat, axis=0, keepdims=True)
        db_ref[...] += jnp.sum(dy, axis=0, keepdims=True)

    vec = jax.ShapeDtypeStruct((1, D), F32)
    out_specs = [_row_spec(ts, D), _row_spec(ts, D), _vec_spec(D), _vec_spec(D), _vec_spec(D)]
    out_shape = [jax.ShapeDtypeStruct((S, D), F32), jax.ShapeDtypeStruct((S, D), BF16), vec, vec, vec]
    if from_target:
        out_specs.append(pl.BlockSpec((1, 1), lambda i: (0, 0)))
        out_shape.append(jax.ShapeDtypeStruct((1, 1), F32))
    return pl.pallas_call(
        body, name=name, grid=(S // ts,),
        in_specs=[_row_spec(ts, D), _row_spec(ts, D), _vec_spec(D), _vec_spec(D), _vec_spec(D), _row_spec(ts, D)],
        out_specs=out_specs, out_shape=out_shape, compiler_params=_cp("arbitrary"),
    )(x, out, gate, g, b, dy_or_target)


def _conv_tiles(S, C):
    tt = _tile(S, 256, CONV_HALO)
    tc = _tile(C, 512, LANE)
    return tt, tc


def _conv1_fwd(u, w_dw, b_dw, name, carry=None):
    S, C3 = u.shape
    C = C3 // 3
    KS = w_dw.shape[0]
    tt, tc = _conv_tiles(S, C)
    ncb = C // tc
    hb = tt // CONV_HALO
    lead = CONV_HALO - (KS - 1)

    def body(a_ref, g_ref, ah_ref, gh_ref, w_ref, b_ref, o_ref, pad_ref):
        i = pl.program_id(0)
        halo = ah_ref[...] * jax.nn.sigmoid(gh_ref[...])
        pad_ref[0:CONV_HALO, :] = jnp.where(i > 0, halo, 0.0)
        pad_ref[CONV_HALO:, :] = a_ref[...] * jax.nn.sigmoid(g_ref[...])
        acc = jnp.broadcast_to(b_ref[...], (tt, tc))
        for k in range(KS):
            acc = acc + w_ref[k:k + 1, :] * pad_ref[lead + k:lead + k + tt, :]
        o_ref[...] = acc

    main = lambda off: pl.BlockSpec((tt, tc), lambda i, j: (i, off + j))
    halo = lambda off: pl.BlockSpec((CONV_HALO, tc), lambda i, j: (jnp.maximum(i * hb - 1, 0), off + j))
    res = _pcall(
        body, name=name, grid=(S // tt, ncb),
        in_specs=[main(0), main(ncb), halo(0), halo(ncb), pl.BlockSpec((KS, tc), lambda i, j: (0, j)),
                  pl.BlockSpec((1, tc), lambda i, j: (0, j))],
        out_specs=[pl.BlockSpec((tt, tc), lambda i, j: (i, j))], out_shape=[jax.ShapeDtypeStruct((S, C), F32)],
        scratch_shapes=[pltpu.VMEM((CONV_HALO + tt, tc), F32)], sem=("parallel", "parallel"),
        args=(u, u, u, u, w_dw, b_dw), carry=carry)
    return res[0] if carry is None else (res[0][0], res[1])


def _conv2_fwd(v2, u, g_cn, b_cn, name):
    S, C = v2.shape
    ts = _tile(S, ROW_TILE, SUBLANE)

    def body(v_ref, z_ref, g_ref, b_ref, p_ref):
        xhat, _ = _ln_stats(v_ref[...])
        v3 = xhat * g_ref[...] + b_ref[...]
        p_ref[...] = (_silu(v3) * _silu(z_ref[...])).astype(p_ref.dtype)

    return pl.pallas_call(
        body, name=name, grid=(S // ts,),
        in_specs=[_row_spec(ts, C), pl.BlockSpec((ts, C), lambda i: (i, 2)), _vec_spec(C), _vec_spec(C)],
        out_specs=_row_spec(ts, C), out_shape=jax.ShapeDtypeStruct((S, C), BF16), compiler_params=_cp("parallel"),
    )(v2, u, g_cn, b_cn)


def _conv2_bwd(dp, v2, u, g_cn, b_cn, name):
    S, C = v2.shape
    ts = _tile(S, ROW_TILE, SUBLANE)

    def body(dp_ref, v_ref, z_ref, g_ref, b_ref, dv_ref, dz_ref, dg_ref, db_ref):
        i = pl.program_id(0)

        @pl.when(i == 0)
        def _():
            dg_ref[...] = jnp.zeros_like(dg_ref)
            db_ref[...] = jnp.zeros_like(db_ref)

        dp = dp_ref[...].astype(F32)
        z = z_ref[...]
        xhat, rstd = _ln_stats(v_ref[...])
        v3 = xhat * g_ref[...] + b_ref[...]
        dz_ref[...] = (dp * _silu(v3) * _dsilu(z)).astype(dz_ref.dtype)
        dv3 = dp * _silu(z) * _dsilu(v3)
        dv_ref[...] = _ln_bwd(dv3 * g_ref[...], xhat, rstd)
        dg_ref[...] += jnp.sum(dv3 * xhat, axis=0, keepdims=True)
        db_ref[...] += jnp.sum(dv3, axis=0, keepdims=True)

    vec = jax.ShapeDtypeStruct((1, C), F32)
    return pl.pallas_call(
        body, name=name, grid=(S // ts,),
        in_specs=[_row_spec(ts, C), _row_spec(ts, C), pl.BlockSpec((ts, C), lambda i: (i, 2)), _vec_spec(C), _vec_spec(C)],
        out_specs=[_row_spec(ts, C), _row_spec(ts, C), _vec_spec(C), _vec_spec(C)],
        out_shape=[jax.ShapeDtypeStruct((S, C), F32), jax.ShapeDtypeStruct((S, C), BF16), vec, vec],
        compiler_params=_cp("arbitrary"),
    )(dp, v2, u, g_cn, b_cn)


def _conv1_bwd(dv2, u, w_dw, name, carry=None):
    S, C = dv2.shape
    KS = w_dw.shape[0]
    tt, tc = _conv_tiles(S, C)
    ncb = C // tc
    nt = S // tt
    hb = tt // CONV_HALO
    lead = CONV_HALO - (KS - 1)

    def body(dv_ref, dvh_ref, a_ref, g_ref, ah_ref, gh_ref, w_ref, da_ref, dg_ref, dw_ref, db_ref, pad_ref, fpad_ref):
        i = pl.program_id(1)

        @pl.when(i == 0)
        def _():
            dw_ref[...] = jnp.zeros_like(dw_ref)
            db_ref[...] = jnp.zeros_like(db_ref)

        dv = dv_ref[...]
        a = a_ref[...]
        sg = jax.nn.sigmoid(g_ref[...])
        halo = ah_ref[...] * jax.nn.sigmoid(gh_ref[...])
        pad_ref[0:CONV_HALO, :] = jnp.where(i > 0, halo, 0.0)
        pad_ref[CONV_HALO:, :] = a * sg
        fpad_ref[0:tt, :] = dv
        fpad_ref[tt:, :] = jnp.where(i < nt - 1, dvh_ref[...], 0.0)
        dv1 = jnp.zeros((tt, tc), F32)
        for k in range(KS):
            dv1 = dv1 + w_ref[k:k + 1, :] * fpad_ref[KS - 1 - k:KS - 1 - k + tt, :]
            dw_ref[k:k + 1, :] += jnp.sum(dv * pad_ref[lead + k:lead + k + tt, :], axis=0, keepdims=True)
        db_ref[...] += jnp.sum(dv, axis=0, keepdims=True)
        da_ref[...] = (dv1 * sg).astype(da_ref.dtype)
        dg_ref[...] = (dv1 * a * sg * (1.0 - sg)).astype(dg_ref.dtype)

    main = lambda off: pl.BlockSpec((tt, tc), lambda j, i: (i, off + j))
    halo = lambda off: pl.BlockSpec((CONV_HALO, tc), lambda j, i: (jnp.maximum(i * hb - 1, 0), off + j))
    fhalo = pl.BlockSpec((CONV_HALO, tc), lambda j, i: (jnp.minimum((i + 1) * hb, nt * hb - 1), j))
    res = _pcall(
        body, name=name, grid=(ncb, nt),
        in_specs=[main(0), fhalo, main(0), main(ncb), halo(0), halo(ncb), pl.BlockSpec((KS, tc), lambda j, i: (0, j))],
        out_specs=[main(0), main(0), pl.BlockSpec((KS, tc), lambda j, i: (0, j)), pl.BlockSpec((1, tc), lambda j, i: (0, j))],
        out_shape=[jax.ShapeDtypeStruct((S, C), BF16), jax.ShapeDtypeStruct((S, C), BF16),
                   jax.ShapeDtypeStruct((KS, C), F32), jax.ShapeDtypeStruct((1, C), F32)],
        scratch_shapes=[pltpu.VMEM((CONV_HALO + tt, tc), F32), pltpu.VMEM((tt + CONV_HALO, tc), F32)],
        sem=("parallel", "arbitrary"), args=(dv2, dv2, u, u, u, u, w_dw), carry=carry)
    return res


def _rope_tables(S):
    half = QK_ROPE_DIM // 2
    inv_freq = ROPE_BASE ** (-jnp.arange(half, dtype=F32) / half)
    ang = jnp.arange(S, dtype=jnp.int32).astype(F32)[:, None] * inv_freq[None, :]
    cos, sin, z = jnp.cos(ang), jnp.sin(ang), jnp.zeros((S, half), F32)
    tc = jnp.concatenate([cos, cos, z, z], axis=1)
    t1 = jnp.concatenate([-sin, z, z, z], axis=1)
    t2 = jnp.concatenate([z, sin, z, z], axis=1)
    return tc, t1, t2


def _rope128(r, tc, t1, t2, sign):
    return r * tc + sign * (pltpu.roll(r, LANE - QK_ROPE_DIM // 2, 1) * t1 + pltpu.roll(r, QK_ROPE_DIM // 2, 1) * t2)


def _rms_fwd(x, width, g, name):
    S = x.shape[0]
    ts = _tile(S, 256, SUBLANE)

    def body(x_ref, g_ref, o_ref):
        xv = x_ref[...]
        rr = lax.rsqrt(jnp.mean(xv * xv, axis=-1, keepdims=True) + RMS_EPS)
        o_ref[...] = (xv * rr * g_ref[...]).astype(o_ref.dtype)

    return pl.pallas_call(
        body, name=name, grid=(S // ts,), in_specs=[_row_spec(ts, width), _vec_spec(width)],
        out_specs=_row_spec(ts, width), out_shape=jax.ShapeDtypeStruct((S, width), BF16), compiler_params=_cp("parallel"),
    )(x, g)


def _rms_bwd_math(xv, dy, g):
    n = xv.shape[-1]
    rr = lax.rsqrt(jnp.mean(xv * xv, axis=-1, keepdims=True) + RMS_EPS)
    dyg = dy * g
    dx = rr * dyg - xv * (rr * rr * rr) * (jnp.sum(dyg * xv, axis=-1, keepdims=True) * (1.0 / n))
    dg = jnp.sum(dy * xv * rr, axis=0, keepdims=True)
    return dx, dg


def _rms_bwd(x, width, dy, g, du, name):
    S = x.shape[0]
    ts = _tile(S, 256, SUBLANE)

    def body(x_ref, dy_ref, g_ref, du_in, dx_ref, dg_ref):
        del du_in
        i = pl.program_id(0)

        @pl.when(i == 0)
        def _():
            dg_ref[...] = jnp.zeros_like(dg_ref)

        dx, dg = _rms_bwd_math(x_ref[...], dy_ref[...].astype(F32), g_ref[...])
        dx_ref[...] = dx.astype(dx_ref.dtype)
        dg_ref[...] += dg

    return pl.pallas_call(
        body, name=name, grid=(S // ts,),
        in_specs=[_row_spec(ts, width), _row_spec(ts, width), _vec_spec(width), pl.BlockSpec(memory_space=pl.ANY)],
        out_specs=[_row_spec(ts, width), _vec_spec(width)],
        out_shape=[jax.ShapeDtypeStruct(du.shape, du.dtype), jax.ShapeDtypeStruct((1, width), F32)],
        input_output_aliases={3: 0}, compiler_params=_cp("arbitrary"),
    )(x, dy, g, du)


def _kvprep_fwd(kva, g_kv, tabs, name):
    S, W = kva.shape
    KV = W - LANE
    ts = _tile(S, 256, SUBLANE)

    def body(x_ref, g_ref, tc_ref, t1_ref, t2_ref, c_ref, r_ref):
        xv = x_ref[:, 0:KV]
        rr = lax.rsqrt(jnp.mean(xv * xv, axis=-1, keepdims=True) + RMS_EPS)
        c_ref[...] = (xv * rr * g_ref[...]).astype(c_ref.dtype)
        r_ref[...] = _rope128(x_ref[:, KV:], tc_ref[...], t1_ref[...], t2_ref[...], 1.0).astype(r_ref.dtype)

    tab = _row_spec(ts, LANE)
    return pl.pallas_call(
        body, name=name, grid=(S // ts,), in_specs=[_row_spec(ts, W), _vec_spec(KV), tab, tab, tab],
        out_specs=[_row_spec(ts, KV), _row_spec(ts, LANE)],
        out_shape=[jax.ShapeDtypeStruct((S, KV), BF16), jax.ShapeDtypeStruct((S, LANE), BF16)], compiler_params=_cp("parallel"),
    )(kva, g_kv, *tabs)


def _kvprep_bwd(kva, dckv, dkr_h, g_kv, tabs, name):
    S, W = kva.shape
    KV = W - LANE
    H = dkr_h.shape[1] // LANE
    ts = _tile(S, 256, SUBLANE)

    def body(x_ref, dc_ref, dr_ref, g_ref, tc_ref, t1_ref, t2_ref, o_ref, dg_ref):
        i = pl.program_id(0)

        @pl.when(i == 0)
        def _():
            dg_ref[...] = jnp.zeros_like(dg_ref)

        dx, dg = _rms_bwd_math(x_ref[:, 0:KV], dc_ref[...].astype(F32), g_ref[...])
        o_ref[:, 0:KV] = dx.astype(o_ref.dtype)
        dg_ref[...] += dg
        dr = dr_ref[:, 0:LANE]
        for h in range(1, H):
            dr = dr + dr_ref[:, h * LANE:(h + 1) * LANE]
        o_ref[:, KV:] = _rope128(dr, tc_ref[...], t1_ref[...], t2_ref[...], -1.0).astype(o_ref.dtype)

    tab = _row_spec(ts, LANE)
    return pl.pallas_call(
        body, name=name, grid=(S // ts,),
        in_specs=[_row_spec(ts, W), _row_spec(ts, KV), _row_spec(ts, H * LANE), _vec_spec(KV), tab, tab, tab],
        out_specs=[_row_spec(ts, W), _vec_spec(KV)],
        out_shape=[jax.ShapeDtypeStruct((S, W), BF16), jax.ShapeDtypeStruct((1, KV), F32)], compiler_params=_cp("arbitrary"),
    )(kva, dckv, dkr_h, g_kv, *tabs)


ROPE_GROUP = 4


def _qrope_bwd(dq, tabs, name):
    S, W = dq.shape
    H = W // HEAD_PAD
    G = math.gcd(H, ROPE_GROUP)
    ts = _tile(S, 256, 2 * SUBLANE)

    def body(q_ref, tc_ref, t1_ref, t2_ref, o_ref):
        for g in range(G):
            lo = g * HEAD_PAD
            o_ref[:, lo:lo + LANE] = q_ref[:, lo:lo + LANE].astype(o_ref.dtype)
            o_ref[:, lo + LANE:lo + HEAD_PAD] = _rope128(
                q_ref[:, lo + LANE:lo + HEAD_PAD], tc_ref[...], t1_ref[...], t2_ref[...], -1.0).astype(o_ref.dtype)

    tab = pl.BlockSpec((ts, LANE), lambda i, h: (i, 0))
    blk = pl.BlockSpec((ts, G * HEAD_PAD), lambda i, h: (i, h))
    return pl.pallas_call(
        body, name=name, grid=(S // ts, H // G), in_specs=[blk, tab, tab, tab], out_specs=blk,
        out_shape=jax.ShapeDtypeStruct((S, W), BF16), compiler_params=_cp("parallel", "parallel"),
    )(dq, *tabs)


ATT_TILE = 512


def _causal_mask(t, transposed):
    r = lax.broadcasted_iota(jnp.int32, (t, t), 0)
    c = lax.broadcasted_iota(jnp.int32, (t, t), 1)
    return (c >= r) if transposed else (r >= c)


def _attn_fwd(qraw, kvh, krp, u, zoff, tabs, scale, name):
    S, W = qraw.shape
    H = W // HEAD_PAD
    t = _tile(S, ATT_TILE, LANE)
    nq = S // t
    zb = zoff // V_HEAD_DIM

    def body(q_ref, tc_ref, t1_ref, t2_ref, kv_ref, kr_ref, z_ref, o_ref, p_ref, qr_ref, lse_ref, vt_sc, m_sc, l_sc, acc_sc):
        i = pl.program_id(1)

        @pl.when(i == 0)
        def _():
            for jj in range(nq):
                vt_sc[jj] = kv_ref[jj * t:(jj + 1) * t, QK_NOPE_DIM:].astype(F32).T.astype(BF16)

        qrot = _rope128(q_ref[:, LANE:], tc_ref[...], t1_ref[...], t2_ref[...], 1.0)
        q = jnp.concatenate([q_ref[:, 0:LANE].astype(BF16), qrot.astype(BF16)], axis=-1)
        qr_ref[...] = q
        m_sc[...] = jnp.full_like(m_sc, -jnp.inf)
        l_sc[...] = jnp.zeros_like(l_sc)
        acc_sc[...] = jnp.zeros_like(acc_sc)

        def step(j, masked):
            off = pl.multiple_of(j * t, t)
            k = jnp.concatenate([kv_ref[pl.ds(off, t), 0:QK_NOPE_DIM], kr_ref[pl.ds(off, t), :]], axis=-1)
            st = lax.dot_general(k, q, NT, preferred_element_type=F32) * scale
            if masked:
                st = jnp.where(_causal_mask(t, True), st, -jnp.inf)
            m_old = m_sc[...]
            m_new = jnp.maximum(m_old, jnp.max(st, axis=0, keepdims=True))
            a = jnp.exp(m_old - m_new)
            pt = jnp.exp(st - m_new)
            l_sc[...] = a * l_sc[...] + jnp.sum(pt, axis=0, keepdims=True)
            acc_sc[...] = a * acc_sc[...] + jnp.dot(vt_sc[j], pt.astype(BF16), preferred_element_type=F32)
            m_sc[...] = m_new

        def loop_body(j, carry):
            step(j, False)
            return carry

        lax.fori_loop(0, i, loop_body, 0)
        step(i, True)
        ov = (acc_sc[...] / l_sc[...]).T
        o_ref[...] = ov
        p_ref[...] = (ov * _silu(z_ref[...])).astype(p_ref.dtype)
        lse_ref[...] = m_sc[...] + jnp.log(l_sc[...])

    tab = pl.BlockSpec((t, LANE), lambda h, i: (i, 0))
    head = pl.BlockSpec((t, V_HEAD_DIM), lambda h, i: (i, h))
    return pl.pallas_call(
        body, name=name, grid=(H, nq),
        in_specs=[pl.BlockSpec((t, HEAD_PAD), lambda h, i: (i, h)), tab, tab, tab,
                  pl.BlockSpec((S, HEAD_PAD), lambda h, i: (0, h)), pl.BlockSpec((S, LANE), lambda h, i: (0, 0)),
                  pl.BlockSpec((t, V_HEAD_DIM), lambda h, i: (i, zb + h))],
        out_specs=[head, head, pl.BlockSpec((t, HEAD_PAD), lambda h, i: (i, h)),
                   pl.BlockSpec((None, None, 1, t), lambda h, i: (h, i, 0, 0))],
        out_shape=[jax.ShapeDtypeStruct((S, H * V_HEAD_DIM), F32), jax.ShapeDtypeStruct((S, H * V_HEAD_DIM), BF16),
                   jax.ShapeDtypeStruct((S, W), BF16), jax.ShapeDtypeStruct((H, nq, 1, t), F32)],
        scratch_shapes=[pltpu.VMEM((nq, V_HEAD_DIM, t), BF16), pltpu.VMEM((1, t), F32), pltpu.VMEM((1, t), F32),
                        pltpu.VMEM((V_HEAD_DIM, t), F32)],
        compiler_params=_cp("parallel", "arbitrary"),
    )(qraw, *tabs, kvh, krp, u)


def _attn_bwd(qr, kvh, krp, dp, o, u, zoff, lse, scale, name):
    S, W = qr.shape
    H = W // HEAD_PAD
    U = u.shape[1]
    t = _tile(S, ATT_TILE, LANE)
    nq = S // t
    zb = zoff // V_HEAD_DIM

    def body(q_ref, kv_ref, kr_ref, dp_ref, o_ref, z_ref, lse_ref, dq_ref, dkv_ref, dkr_ref, dz_ref,
             do_sc, dl_sc, dqt_sc, dk_sc, dv_sc):
        j = pl.program_id(1)

        @pl.when(j == 0)
        def _():
            for ii in range(nq):
                rows = slice(ii * t, (ii + 1) * t)
                dpv, ov, z = dp_ref[rows, :], o_ref[rows, :], z_ref[rows, :]
                dov = dpv * _silu(z)
                do_sc[ii] = dov.astype(BF16)
                dz_ref[rows, :] = (dpv * ov * _dsilu(z)).astype(dz_ref.dtype)
                dl_sc[ii] = jnp.sum((dov * ov).T, axis=0, keepdims=True)
                dqt_sc[ii] = jnp.zeros((HEAD_PAD, t), F32)

        kvb = kv_ref[...]
        k = jnp.concatenate([kvb[:, 0:QK_NOPE_DIM], kr_ref[...]], axis=-1)
        v = kvb[:, QK_NOPE_DIM:]
        kt = k.astype(F32).T.astype(BF16)
        dk_sc[...] = jnp.zeros_like(dk_sc)
        dv_sc[...] = jnp.zeros_like(dv_sc)

        def step(i, masked):
            off = pl.multiple_of(i * t, t)
            q = q_ref[pl.ds(off, t), :]
            dov = do_sc[i]
            st = lax.dot_general(k, q, NT, preferred_element_type=F32) * scale
            pt = jnp.exp(st - lse_ref[i])
            if masked:
                pt = jnp.where(_causal_mask(t, True), pt, 0.0)
            dv_sc[...] += jnp.dot(pt.astype(BF16), dov, preferred_element_type=F32)
            dpt = lax.dot_general(v, dov, NT, preferred_element_type=F32)
            dst = (pt * (dpt - dl_sc[i]) * scale).astype(BF16)
            dk_sc[...] += jnp.dot(dst, q, preferred_element_type=F32)
            dqt_sc[i] += jnp.dot(kt, dst, preferred_element_type=F32)

        def loop_body(i, carry):
            step(i, False)
            return carry

        step(j, True)
        lax.fori_loop(j + 1, nq, loop_body, 0)
        dkv_ref[:, 0:QK_NOPE_DIM] = dk_sc[:, 0:QK_NOPE_DIM].astype(dkv_ref.dtype)
        dkv_ref[:, QK_NOPE_DIM:] = dv_sc[...].astype(dkv_ref.dtype)
        dkr_ref[...] = dk_sc[:, QK_NOPE_DIM:]

        @pl.when(j == nq - 1)
        def _():
            for ii in range(nq):
                dq_ref[ii * t:(ii + 1) * t, :] = dqt_sc[ii].T

    whole = lambda w, off: pl.BlockSpec((S, w), lambda h, j: (0, off + h))
    return pl.pallas_call(
        body, name=name, grid=(H, nq),
        in_specs=[whole(HEAD_PAD, 0), pl.BlockSpec((t, HEAD_PAD), lambda h, j: (j, h)),
                  pl.BlockSpec((t, LANE), lambda h, j: (j, 0)), whole(V_HEAD_DIM, 0), whole(V_HEAD_DIM, 0),
                  whole(V_HEAD_DIM, zb), pl.BlockSpec((None, nq, 1, t), lambda h, j: (h, 0, 0, 0))],
        out_specs=[whole(HEAD_PAD, 0), pl.BlockSpec((t, HEAD_PAD), lambda h, j: (j, h)),
                   pl.BlockSpec((t, LANE), lambda h, j: (j, h)), whole(V_HEAD_DIM, zb)],
        out_shape=[jax.ShapeDtypeStruct((S, W), F32), jax.ShapeDtypeStruct((S, W), BF16),
                   jax.ShapeDtypeStruct((S, H * LANE), F32), jax.ShapeDtypeStruct((S, U), BF16)],
        scratch_shapes=[pltpu.VMEM((nq, t, V_HEAD_DIM), BF16), pltpu.VMEM((nq, 1, t), F32),
                        pltpu.VMEM((nq, HEAD_PAD, t), F32), pltpu.VMEM((t, HEAD_PAD), F32), pltpu.VMEM((t, V_HEAD_DIM), F32)],
        compiler_params=_cp("parallel", "arbitrary"),
    )(qr, kvh, krp, dp, o, u, lse)


def _adamw_math(w, g, m, v):
    m = ADAM_B1 * m + (1.0 - ADAM_B1) * g
    v = ADAM_B2 * v + (1.0 - ADAM_B2) * (g * g)
    m_hat = m / (1.0 - ADAM_B1 ** ADAM_STEP)
    v_hat = v / (1.0 - ADAM_B2 ** ADAM_STEP)
    delta = -ADAM_LR * (m_hat / (jnp.sqrt(v_hat) + ADAM_EPS) + ADAM_WD * w)
    return delta, m, v


def _adamw(w, g, m, v, name):
    R, C = w.shape
    tr = _tile(R, 256, SUBLANE)
    tc = _tile(C, 1024, LANE)

    def body(w_ref, g_ref, m_ref, v_ref, d_ref, nm_ref, nv_ref):
        d, nm, nv = _adamw_math(w_ref[...], g_ref[...], m_ref[...], v_ref[...])
        d_ref[...] = d
        nm_ref[...] = nm
        nv_ref[...] = nv

    blk = pl.BlockSpec((tr, tc), lambda i, j: (i, j))
    sh = jax.ShapeDtypeStruct((R, C), F32)
    return pl.pallas_call(
        body, name=name, grid=(R // tr, C // tc), in_specs=[blk] * 4, out_specs=[blk] * 3, out_shape=[sh] * 3,
        compiler_params=_cp("parallel", "parallel"),
    )(w, g, m, v)


def _sum_leading(x, name):
    n, R, C = x.shape
    tr = _tile(R, 512, SUBLANE)

    def body(x_ref, o_ref):
        acc = x_ref[0]
        for k in range(1, n):
            acc = acc + x_ref[k]
        o_ref[...] = acc

    return pl.pallas_call(
        body, name=name, grid=(R // tr,), in_specs=[pl.BlockSpec((n, tr, C), lambda i: (0, i, 0))],
        out_specs=pl.BlockSpec((tr, C), lambda i: (i, 0)), out_shape=jax.ShapeDtypeStruct((R, C), F32),
        compiler_params=_cp("parallel"),
    )(x)


def _pair_add(full, recv, c_idx, name):
    n, R, C = full.shape
    h = R // 2
    tr = _tile(h, 256, 2 * SUBLANE)
    tc = _tile(C, 1024, LANE)
    nb = h // tr

    def body(c_ref, a_ref, b_ref, o_ref):
        del c_ref
        o_ref[...] = (a_ref[...].astype(F32) + b_ref[...].astype(F32)).astype(o_ref.dtype)

    return pl.pallas_call(
        body, name=name,
        grid_spec=pltpu.PrefetchScalarGridSpec(
            num_scalar_prefetch=1, grid=(n, nb, C // tc),
            in_specs=[pl.BlockSpec((None, tr, tc), lambda k, i, j, c: (k, c[0] * nb + i, j)),
                      pl.BlockSpec((None, tr, tc), lambda k, i, j, c: (k, i, j))],
            out_specs=pl.BlockSpec((None, tr, tc), lambda k, i, j, c: (k, i, j))),
        out_shape=jax.ShapeDtypeStruct((n, h, C), BF16), compiler_params=_cp("parallel", "parallel", "parallel"),
    )(c_idx, full, recv)


def _chip_sum(pair, recv, idx, name):
    n, h, C = pair.shape
    tr = _tile(h, 256, 2 * SUBLANE)
    tc = _tile(C, 1024, LANE)
    nb = h // tr

    def body(c_ref, chip_ref, a_ref, b_ref, o_ref):
        del c_ref, chip_ref
        acc = a_ref[...].astype(F32)
        for k in range(N_CHIP - 1):
            acc = acc + b_ref[k].astype(F32)
        o_ref[...] = acc

    return pl.pallas_call(
        body, name=name,
        grid_spec=pltpu.PrefetchScalarGridSpec(
            num_scalar_prefetch=2, grid=(nb, C // tc),
            in_specs=[pl.BlockSpec((None, tr, tc), lambda i, j, c, chip: (chip[0], i, j)),
                      pl.BlockSpec((N_CHIP - 1, tr, tc), lambda i, j, c, chip: (0, i, j))],
            out_specs=pl.BlockSpec((tr, tc), lambda i, j, c, chip: (c[0] * nb + i, j))),
        out_shape=jax.ShapeDtypeStruct((2 * h, C), F32), compiler_params=_cp("parallel", "parallel"),
    )(idx[0], idx[1], pair, recv)


def _cast_place(w, idx, name):
    R, C = w.shape
    h = R // 2
    tr = _tile(h, 256, 2 * SUBLANE)
    tc = _tile(C, 1024, LANE)
    nb = h // tr

    def body(c_ref, chip_ref, w_ref, o_ref):
        del c_ref, chip_ref
        o_ref[...] = w_ref[...].astype(o_ref.dtype)

    return pl.pallas_call(
        body, name=name,
        grid_spec=pltpu.PrefetchScalarGridSpec(
            num_scalar_prefetch=2, grid=(nb, C // tc),
            in_specs=[pl.BlockSpec((tr, tc), lambda i, j, c, chip: (c[0] * nb + i, j))],
            out_specs=pl.BlockSpec((None, tr, tc), lambda i, j, c, chip: (chip[0], c[0] * nb + i, j))),
        out_shape=jax.ShapeDtypeStruct((N_CHIP, R, C), BF16), compiler_params=_cp("parallel", "parallel"),
    )(idx[0], idx[1], w)


def _coords():
    return lax.axis_index("x"), lax.axis_index("y"), lax.axis_index("c")


def _allgather_small(x_shard, name):
    m_per, n = x_shard.shape

    def body(x_ref, out_ref, send_sems, recv_sems, local_sem):
        x, y, c = _coords()
        me, sibling = (x, y, c), (x, y, 1 - c)
        chips = [(1 - x, y), (x, 1 - y), (1 - x, 1 - y)]

        def rows(px, py, pc):
            return out_ref.at[pl.ds((4 * px + 2 * py + pc) * m_per, m_per), :]

        def copy(k, block, to, src=None):
            return pltpu.make_async_remote_copy(
                src_ref=rows(*block) if src is None else src, dst_ref=rows(*block), send_sem=send_sems.at[k],
                recv_sem=recv_sems.at[k], device_id=to, device_id_type=MESH_ID)

        mine = pltpu.make_async_copy(x_ref, rows(*me), local_sem)
        mine.start()
        first = [copy(0, me, sibling, src=x_ref)]
        first += [copy(1 + j, me, (*chip, c), src=x_ref) for j, chip in enumerate(chips)]
        for cp in first:
            cp.start()
        passed = [copy(4 + j, (*chip, c), sibling) for j, chip in enumerate(chips)]
        for j, chip in enumerate(chips):
            copy(1 + j, (*chip, c), me).wait_recv()
            passed[j].start()
        copy(0, sibling, me).wait_recv()
        for j, chip in enumerate(chips):
            copy(4 + j, (*chip, 1 - c), me).wait_recv()
        for cp in first + passed:
            cp.wait_send()
        mine.wait()

    return pl.pallas_call(
        body, name=name, out_shape=jax.ShapeDtypeStruct((N_DEV * m_per, n), x_shard.dtype),
        in_specs=[pl.BlockSpec(memory_space=pltpu.VMEM)], out_specs=pl.BlockSpec(memory_space=pltpu.VMEM),
        scratch_shapes=[pltpu.SemaphoreType.DMA((7,)), pltpu.SemaphoreType.DMA((7,)), pltpu.SemaphoreType.DMA],
        compiler_params=pltpu.CompilerParams(vmem_limit_bytes=VMEM_LIMIT),
    )(x_shard)


def _allgather_carry(bufs):
    n = len(bufs)

    def plan(outs, send_sems, recv_sems):
        x, y, c = _coords()
        me, sibling = (x, y, c), (x, y, 1 - c)
        chips = [(1 - x, y), (x, 1 - y), (1 - x, 1 - y)]

        def win(a, px, py, pc):
            h = bufs[a].shape[1] // 2
            return outs[a].at[2 * px + py, pl.ds(pc * h, h), :]

        def copy(a, k, block, to):
            return pltpu.make_async_remote_copy(
                src_ref=win(a, *block), dst_ref=win(a, *block), send_sem=send_sems.at[a, k],
                recv_sem=recv_sems.at[a, k], device_id=to, device_id_type=MESH_ID)

        return c, me, sibling, chips, copy

    def start(_, outs, send_sems, recv_sems):
        c, me, sibling, chips, copy = plan(outs, send_sems, recv_sems)
        for a in range(n):
            copy(a, 0, me, sibling).start()
            for j, chip in enumerate(chips):
                copy(a, 1 + j, me, (*chip, c)).start()

    def finish(_, outs, send_sems, recv_sems):
        c, me, sibling, chips, copy = plan(outs, send_sems, recv_sems)
        for a in range(n):
            for j, chip in enumerate(chips):
                copy(a, 1 + j, (*chip, c), me).wait_recv()
                copy(a, 4 + j, (*chip, c), sibling).start()
        for a in range(n):
            copy(a, 0, sibling, me).wait_recv()
            for j, chip in enumerate(chips):
                copy(a, 4 + j, (*chip, 1 - c), me).wait_recv()
        for a in range(n):
            copy(a, 0, me, sibling).wait_send()
            for j, chip in enumerate(chips):
                copy(a, 1 + j, me, (*chip, c)).wait_send()
                copy(a, 4 + j, (*chip, c), sibling).wait_send()

    return _Carry(tuple(bufs), tuple(jax.ShapeDtypeStruct(b.shape, b.dtype) for b in bufs), {a: a for a in range(n)},
                  (n, 7), start, finish)


def _exchange_alone(carry, name):
    n_in, n_out = len(carry.bufs), len(carry.out_shapes)

    def body(*refs):
        ins, outs = refs[:n_in], refs[n_in:n_in + n_out]
        send_sems, recv_sems = refs[n_in + n_out:]
        carry.start(ins, outs, send_sems, recv_sems)
        carry.finish(ins, outs, send_sems, recv_sems)

    any_spec = pl.BlockSpec(memory_space=pl.ANY)
    return pl.pallas_call(
        body, name=name, out_shape=list(carry.out_shapes), in_specs=[any_spec] * n_in, out_specs=[any_spec] * n_out,
        input_output_aliases=dict(carry.aliases),
        scratch_shapes=[pltpu.SemaphoreType.DMA(carry.sem_shape), pltpu.SemaphoreType.DMA(carry.sem_shape)],
    )(*carry.bufs)


def _pair_exchange(grads, name):
    n = len(grads)

    def body(*refs):
        ins, outs = refs[:n], refs[n:2 * n]
        send_sems, recv_sems = refs[2 * n:]
        x, y, c = _coords()
        sibling = (x, y, 1 - c)
        copies = []
        for a in range(n):
            h = grads[a].shape[1] // 2
            cp = pltpu.make_async_remote_copy(
                src_ref=ins[a].at[:, pl.ds((1 - c) * h, h), :], dst_ref=outs[a], send_sem=send_sems.at[a],
                recv_sem=recv_sems.at[a], device_id=sibling, device_id_type=MESH_ID)
            cp.start()
            copies.append(cp)
        for cp in copies:
            cp.wait()

    any_spec = pl.BlockSpec(memory_space=pl.ANY)
    return pl.pallas_call(
        body, name=name,
        out_shape=[jax.ShapeDtypeStruct((g.shape[0], g.shape[1] // 2, g.shape[2]), g.dtype) for g in grads],
        in_specs=[any_spec] * n, out_specs=[any_spec] * n,
        scratch_shapes=[pltpu.SemaphoreType.DMA((n,)), pltpu.SemaphoreType.DMA((n,))],
    )(*grads)


def _chip_exchange_carry(pairs):
    n = len(pairs)

    def copies(ins, outs, send_sems, recv_sems):
        x, y, c = _coords()
        chips = [(1 - x, y), (x, 1 - y), (1 - x, 1 - y)]
        return [pltpu.make_async_remote_copy(
            src_ref=ins[a].at[2 * px + py], dst_ref=outs[a].at[k], send_sem=send_sems.at[a, k],
            recv_sem=recv_sems.at[a, k], device_id=(px, py, c), device_id_type=MESH_ID)
            for a in range(n) for k, (px, py) in enumerate(chips)]

    def start(ins, outs, send_sems, recv_sems):
        for cp in copies(ins, outs, send_sems, recv_sems):
            cp.start()

    def finish(ins, outs, send_sems, recv_sems):
        for cp in copies(ins, outs, send_sems, recv_sems):
            cp.wait()

    return _Carry(tuple(pairs), tuple(jax.ShapeDtypeStruct((N_CHIP - 1,) + p.shape[1:], p.dtype) for p in pairs), {},
                  (n, N_CHIP - 1), start, finish)


def _half_share(bufs, name):
    n = len(bufs)

    def body(*refs):
        outs = refs[n:2 * n]
        send_sems, recv_sems = refs[2 * n:]
        x, y, c = _coords()
        sibling = (x, y, 1 - c)

        def copy(a, pc):
            h = bufs[a].shape[0] // 2
            rows = outs[a].at[pl.ds(pc * h, h), :]
            return pltpu.make_async_remote_copy(
                src_ref=rows, dst_ref=rows, send_sem=send_sems.at[a], recv_sem=recv_sems.at[a], device_id=sibling,
                device_id_type=MESH_ID)

        for a in range(n):
            copy(a, c).start()
        for a in range(n):
            copy(a, c).wait_send()
            copy(a, 1 - c).wait_recv()

    any_spec = pl.BlockSpec(memory_space=pl.ANY)
    return pl.pallas_call(
        body, name=name, out_shape=[jax.ShapeDtypeStruct(b.shape, b.dtype) for b in bufs],
        in_specs=[any_spec] * n, out_specs=[any_spec] * n, input_output_aliases={a: a for a in range(n)},
        scratch_shapes=[pltpu.SemaphoreType.DMA((n,)), pltpu.SemaphoreType.DMA((n,))],
    )(*bufs)


def _pair_sums(grads, idx, tag):
    names = list(grads)
    full = [grads[k] for k in names]
    recv = _pair_exchange(full, "rs_pair_exchange_" + tag)
    return {k: _pair_add(f, r, idx[0], "rs_pair_add_" + k) for k, f, r in zip(names, full, recv)}


PACK_ALIGN = SUBLANE * LANE
PACK_ROWS_ALIGN = 256 * LANE


def _pack(parts):
    flat, offs, off = [], [], 0
    for p in parts:
        v = p.reshape(-1).astype(F32)
        n = v.shape[0]
        padded = -(-n // PACK_ALIGN) * PACK_ALIGN
        flat.append(jnp.pad(v, (0, padded - n)))
        offs.append((off, n))
        off += padded
    tail = -off % PACK_ROWS_ALIGN
    if tail:
        flat.append(jnp.zeros((tail,), F32))
    return jnp.concatenate(flat).reshape(-1, LANE), offs


def _unpack(flat, offs, shapes):
    return [flat[o:o + n].reshape(s) for (o, n), s in zip(offs, shapes)]


def _chipcat(g, per_dev_len, offs, shape, axis):
    o, n = offs
    parts = [g[2 * j, o:o + n].reshape(shape) for j in range(N_CHIP)]
    return jnp.concatenate(parts, axis=axis)


def kernel(x, c, w_ada, b_ada, ln_g, ln_b, a_w_in, a_w_dw, a_b_dw, a_norm_g, a_norm_b, a_w_out, b_w_in, b_q_norm_g, b_w_qb, b_w_out, kv_w_a, kv_norm_g, kv_w_b, loss_target, m_w_ada, m_b_ada, m_ln_g, m_ln_b, m_a_w_in, m_a_w_dw, m_a_b_dw, m_a_norm_g, m_a_norm_b, m_a_w_out, m_b_w_in, m_b_q_norm_g, m_b_w_qb, m_b_w_out, m_kv_w_a, m_kv_norm_g, m_kv_w_b, v_w_ada, v_b_ada, v_ln_g, v_ln_b, v_a_w_in, v_a_w_dw, v_a_b_dw, v_a_norm_g, v_a_norm_b, v_a_w_out, v_b_w_in, v_b_q_norm_g, v_b_w_qb, v_b_w_out, v_kv_w_a, v_kv_norm_g, v_kv_w_b):
    xi, yi, ci = _coords()
    chip = 2 * xi + yi
    dev = 4 * xi + 2 * yi + ci
    idx = (jnp.reshape(ci, (1,)).astype(jnp.int32), jnp.reshape(chip, (1,)).astype(jnp.int32))

    x2 = x[0]
    tgt = loss_target[0]
    S, D = x2.shape
    C = a_w_out.shape[1] * N_CHIP
    Cq = C // N_CHIP
    KS = a_w_dw.shape[1]
    Q = b_q_norm_g.shape[1]
    KV = kv_norm_g.shape[0]
    Hq = kv_w_b.shape[1] // HEAD_PAD
    H = Hq * N_CHIP
    W = H * V_HEAD_DIM
    Nq = w_ada.shape[2]
    head_q = QK_NOPE_DIM + QK_ROPE_DIM
    scale = head_q ** -0.5
    tabs = _rope_tables(S)

    qb_pad = jnp.pad(b_w_qb[0].reshape(Q, Hq, head_q), ((0, 0), (0, 0), (0, HEAD_PAD - head_q))).reshape(Q, Hq * HEAD_PAD)
    kva_pad = jnp.pad(kv_w_a, ((0, 0), (0, LANE - QK_ROPE_DIM)))
    shards = {"a_w_in": a_w_in[0], "a_w_out": a_w_out[0], "b_w_in": b_w_in[0], "b_w_qb": qb_pad, "b_w_out": b_w_out[0],
              "kv_w_a": kva_pad, "kv_w_b": kv_w_b}
    placed = {k: _cast_place(w, idx, "cast_" + k) for k, w in shards.items()}
    (W_ain,) = _exchange_alone(_allgather_carry([placed["a_w_in"]]), "allgather_a_w_in")

    pack1, offs1 = _pack([c[0], a_w_dw[0], a_b_dw[0], a_norm_g[0], a_norm_b[0]])
    L1 = pack1.shape[0] * LANE
    g1 = _allgather_small(pack1, "allgather_small_in").reshape(N_DEV, L1)
    c_all = g1[:, :D]
    w_dw = _chipcat(g1, L1, offs1[1], (KS, Cq), 1)
    b_dw = _chipcat(g1, L1, offs1[2], (1, Cq), 1)
    g_cn = _chipcat(g1, L1, offs1[3], (1, Cq), 1)
    b_cn = _chipcat(g1, L1, offs1[4], (1, Cq), 1)

    b_ada_sh = lax.dynamic_slice_in_dim(b_ada, chip * Nq, Nq, axis=1)[:, None, :]
    mod_sh = _mod(c_all, w_ada, b_ada_sh, "adaln_mod")
    gm = _allgather_small(mod_sh.reshape(DEPTH * N_DEV, Nq), "allgather_small_mod").reshape(N_CHIP, 2, DEPTH, N_DEV, Nq)
    mod_rows = lax.dynamic_index_in_dim(gm[:, 0], dev, axis=2, keepdims=False)
    mod_me = jnp.transpose(mod_rows, (1, 0, 2)).reshape(DEPTH, N_CHIP * Nq)
    shift = [mod_me[l:l + 1, 0:D] for l in range(DEPTH)]
    scl = [mod_me[l:l + 1, D:2 * D] for l in range(DEPTH)]
    gate = [mod_me[l:l + 1, 2 * D:3 * D] for l in range(DEPTH)]

    h0 = _lnmod_fwd(x2, scl[0], shift[0], "a_lnmod_fwd")
    u0, (W_aout, W_kva, W_kvb, W_bin) = _mm(
        h0, W_ain, b_sh=True, name="a_in_fwd",
        carry=_allgather_carry([placed["a_w_out"], placed["kv_w_a"], placed["kv_w_b"], placed["b_w_in"]]))
    v2, (W_qb, W_bout) = _conv1_fwd(u0, w_dw, b_dw, "a_conv1_fwd",
                                    carry=_allgather_carry([placed["b_w_qb"], placed["b_w_out"]]))
    W_aout = W_aout.reshape(C, D)
    W_bout = W_bout.reshape(W, D)
    W_kva = W_kva.reshape(D, KV + LANE)
    p0 = _conv2_fwd(v2, u0, g_cn, b_cn, "a_conv2_fwd")
    out0 = _mm(p0, W_aout, name="a_out_fwd")
    x1 = _resln_fwd(x2, out0, gate[0], ln_g[0:1], ln_b[0:1], "a_resln_fwd")

    kva = _mm(x1, W_kva, name="kv_a_fwd")
    ckv, krp = _kvprep_fwd(kva, kv_norm_g[None, :], tabs, "kv_prep_fwd")
    kvh = _mm(ckv, W_kvb, b_sh=True, out_dtype=BF16, name="kv_b_fwd")

    h1 = _lnmod_fwd(x1, scl[1], shift[1], "b_lnmod_fwd")
    u1 = _mm(h1, W_bin, b_sh=True, name="b_in_fwd")
    qn = _rms_fwd(u1, Q, b_q_norm_g, "b_qnorm_fwd")
    qraw = _mm(qn, W_qb, b_sh=True, name="b_qb_fwd")
    o, p1, qr, lse = _attn_fwd(qraw, kvh, krp, u1, Q, tabs, scale, "b_attn_fwd")
    out1 = _mm(p1, W_bout, name="b_out_fwd")

    dxa1, dout1, dgate1, dlng1, dlnb1, loss_part = _resln_bwd(
        x1, out1, gate[1], ln_g[1:2], ln_b[1:2], tgt, True, "b_resln_bwd")
    dW_bout = _mm(p1, dout1, ta=True, out_dtype=BF16, name="b_out_dw").reshape(N_CHIP, W // N_CHIP, D)
    pairs = _pair_sums({"b_w_out": dW_bout}, idx, "b_w_out")
    got = {}
    dp1, (got["b_w_out"],) = _mm(dout1, W_bout, tb=True, name="b_out_dx", carry=_chip_exchange_carry([pairs["b_w_out"]]))
    dqr, dkvh, dkr_h, du1 = _attn_bwd(qr, kvh, krp, dp1, o, u1, Q, lse, scale, "b_attn_bwd")
    dqraw = _qrope_bwd(dqr, tabs, "b_qrope_bwd")
    dW_qb = _mm(qn, dqraw, ta=True, o_sh=True, out_dtype=BF16, name="b_qb_dw")
    dqn = _mm(dqraw, W_qb, tb=True, b_sh=True, name="b_qb_dx")
    du1, dgq = _rms_bwd(u1, Q, dqn, b_q_norm_g, du1, "b_qnorm_bwd")
    dW_bin = _mm(h1, du1, ta=True, o_sh=True, out_dtype=BF16, name="b_in_dw")
    dh1 = _mm(du1, W_bin, tb=True, b_sh=True, name="b_in_dx")

    dW_kvb = _mm(ckv, dkvh, ta=True, o_sh=True, out_dtype=BF16, name="kv_b_dw")
    dckv = _mm(dkvh, W_kvb, tb=True, b_sh=True, name="kv_b_dx")
    dkva, dgkv = _kvprep_bwd(kva, dckv, dkr_h, kv_norm_g[None, :], tabs, "kv_prep_bwd")
    dW_kva = _mm(x1, dkva, ta=True, out_dtype=BF16, name="kv_a_dw").reshape(N_CHIP, D // N_CHIP, KV + LANE)
    dx1_kv = _mm(dkva, W_kva, tb=True, name="kv_a_dx")
    dx1, dsc1, dsh1 = _lnmod_bwd(x1, dh1, scl[1], [dxa1, dx1_kv], "b_lnmod_bwd")

    dxa0, dout0, dgate0, dlng0, dlnb0 = _resln_bwd(x2, out0, gate[0], ln_g[0:1], ln_b[0:1], dx1, False, "a_resln_bwd")
    dW_aout = _mm(p0, dout0, ta=True, out_dtype=BF16, name="a_out_dw").reshape(N_CHIP, Cq, D)
    dp0 = _mm(dout0, W_aout, tb=True, name="a_out_dx")
    dv2, dz0, dgcn, dbcn = _conv2_bwd(dp0, v2, u0, g_cn, b_cn, "a_conv2_bwd")
    mid = ["b_w_qb", "b_w_in", "kv_w_b", "kv_w_a"]
    pairs.update(_pair_sums({"b_w_qb": dW_qb, "b_w_in": dW_bin, "kv_w_b": dW_kvb, "kv_w_a": dW_kva}, idx, "mla"))
    (da0, dg0, dwdw, dbdw), got_mid = _conv1_bwd(dv2, u0, w_dw, "a_conv1_bwd",
                                                 carry=_chip_exchange_carry([pairs[k] for k in mid]))
    got.update(zip(mid, got_mid))
    du0 = jnp.concatenate([da0, dg0, dz0], axis=1)
    pairs.update(_pair_sums({"a_w_out": dW_aout}, idx, "a_w_out"))
    dW_ain, (got["a_w_out"],) = _mm(h0, du0, ta=True, o_sh=True, out_dtype=BF16, name="a_in_dw",
                                    carry=_chip_exchange_carry([pairs["a_w_out"]]))
    pairs.update(_pair_sums({"a_w_in": dW_ain}, idx, "a_w_in"))
    dh0, (got["a_w_in"],) = _mm(du0, W_ain, tb=True, b_sh=True, name="a_in_dx",
                                carry=_chip_exchange_carry([pairs["a_w_in"]]))
    dx, dsc0, dsh0 = _lnmod_bwd(x2, dh0, scl[0], [dxa0], "a_lnmod_bwd")
    grad_x = dx[None]

    dmod = jnp.concatenate([dsh0, dsc0, dgate0, dsh1, dsc1, dgate1], axis=1).reshape(DEPTH, 3 * D)
    small = [loss_part, dmod, jnp.concatenate([dlng0, dlng1], 0), jnp.concatenate([dlnb0, dlnb1], 0),
             dwdw, dbdw, dgcn, dbcn, dgq, dgkv]
    small_shapes = [p.shape for p in small]
    pack2, offs2 = _pack(small)
    R2 = pack2.shape[0]
    g2 = _allgather_small(pack2, "allgather_small_grads").reshape(N_DEV, R2, LANE)
    tot = _sum_leading(g2, "small_grad_sum").reshape(-1)
    (loss_t, g_b_ada, g_ln_g, g_ln_b, g_wdw_full, g_bdw_full, g_gcn_full, g_bcn_full, g_gq, g_gkv) = _unpack(
        tot, offs2, small_shapes)
    loss = loss_t.reshape(())
    colsl = lambda a: lax.dynamic_slice_in_dim(a, chip * Cq, Cq, axis=1)
    g_wdw, g_bdw, g_gcn, g_bcn = colsl(g_wdw_full), colsl(g_bdw_full), colsl(g_gcn_full), colsl(g_bcn_full)

    dmod_all = jnp.stack([g2[d].reshape(-1)[offs2[1][0]:offs2[1][0] + offs2[1][1]].reshape(DEPTH, 3 * D)
                          for d in range(N_DEV)], axis=1)
    dmod_sh = lax.dynamic_slice_in_dim(dmod_all, chip * Nq, Nq, axis=2)
    g_w_ada = _wada_grad(jnp.transpose(c_all), dmod_sh, "w_ada_grad")

    mats = ["a_w_in", "a_w_out", "b_w_in", "b_w_qb", "b_w_out", "kv_w_a", "kv_w_b"]
    halves = [_chip_sum(pairs[k], got[k], idx, "rs_chip_sum_" + k) for k in mats]
    red = dict(zip(mats, _half_share(halves, "rs_half_share")))
    g_a_w_in = red["a_w_in"]
    g_a_w_out = red["a_w_out"]
    g_b_w_in = red["b_w_in"]
    g_b_w_qb = red["b_w_qb"].reshape(Q, Hq, HEAD_PAD)[:, :, :head_q].reshape(Q, Hq * head_q)
    g_b_w_out = red["b_w_out"]
    g_kv_w_a = red["kv_w_a"][:, :KV + QK_ROPE_DIM]
    g_kv_w_b = red["kv_w_b"]

    grads = {
        "w_ada": g_w_ada, "b_ada": g_b_ada, "ln_g": g_ln_g, "ln_b": g_ln_b, "a_w_in": g_a_w_in[None],
        "a_w_dw": g_wdw[None], "a_b_dw": g_bdw, "a_norm_g": g_gcn, "a_norm_b": g_bcn, "a_w_out": g_a_w_out[None],
        "b_w_in": g_b_w_in[None], "b_q_norm_g": g_gq, "b_w_qb": g_b_w_qb[None], "b_w_out": g_b_w_out[None],
        "kv_w_a": g_kv_w_a, "kv_norm_g": g_gkv.reshape(KV), "kv_w_b": g_kv_w_b,
    }
    weights = {
        "w_ada": (w_ada, m_w_ada, v_w_ada), "b_ada": (b_ada, m_b_ada, v_b_ada), "ln_g": (ln_g, m_ln_g, v_ln_g),
        "ln_b": (ln_b, m_ln_b, v_ln_b), "a_w_in": (a_w_in, m_a_w_in, v_a_w_in), "a_w_dw": (a_w_dw, m_a_w_dw, v_a_w_dw),
        "a_b_dw": (a_b_dw, m_a_b_dw, v_a_b_dw), "a_norm_g": (a_norm_g, m_a_norm_g, v_a_norm_g),
        "a_norm_b": (a_norm_b, m_a_norm_b, v_a_norm_b), "a_w_out": (a_w_out, m_a_w_out, v_a_w_out),
        "b_w_in": (b_w_in, m_b_w_in, v_b_w_in), "b_q_norm_g": (b_q_norm_g, m_b_q_norm_g, v_b_q_norm_g),
        "b_w_qb": (b_w_qb, m_b_w_qb, v_b_w_qb), "b_w_out": (b_w_out, m_b_w_out, v_b_w_out),
        "kv_w_a": (kv_w_a, m_kv_w_a, v_kv_w_a), "kv_norm_g": (kv_norm_g, m_kv_norm_g, v_kv_norm_g),
        "kv_w_b": (kv_w_b, m_kv_w_b, v_kv_w_b),
    }
    order = list(weights)
    big = [k for k in order if weights[k][0].size >= (1 << 16) and weights[k][0].shape[-1] % LANE == 0]
    small_names = [k for k in order if k not in big]
    upd = {}
    for k in big:
        w, m, v = weights[k]
        shp = w.shape
        two = (-1, shp[-1])
        d_, m_, v_ = _adamw(w.reshape(two), grads[k].reshape(two), m.reshape(two), v.reshape(two), "adamw_" + k)
        upd[k] = (grads[k].reshape(shp), d_.reshape(shp), m_.reshape(shp), v_.reshape(shp))
    sw, offs3 = _pack([weights[k][0] for k in small_names])
    sg, _ = _pack([grads[k] for k in small_names])
    sm, _ = _pack([weights[k][1] for k in small_names])
    sv, _ = _pack([weights[k][2] for k in small_names])
    sd, snm, snv = _adamw(sw, sg, sm, sv, "adamw_small")
    shapes3 = [weights[k][0].shape for k in small_names]
    for k, d_, m_, v_ in zip(small_names, _unpack(sd.reshape(-1), offs3, shapes3), _unpack(snm.reshape(-1), offs3, shapes3),
                             _unpack(snv.reshape(-1), offs3, shapes3)):
        upd[k] = (grads[k].reshape(weights[k][0].shape), d_, m_, v_)

    return (loss, grad_x, *[upd[k][0] for k in order], *[upd[k][1] for k in order], *[upd[k][2] for k in order],
            *[upd[k][3] for k in order])
```

```python
import math
from typing import Callable, NamedTuple

import jax
import jax.numpy as jnp
from jax import lax
from jax.experimental import pallas as pl
from jax.experimental.pallas import tpu as pltpu

F32 = jnp.float32
BF16 = jnp.bfloat16

LN_EPS = 1e-5
RMS_EPS = 1e-6
DEPTH = 2
DEEPNORM_ALPHA = (2.0 * DEPTH) ** 0.25
QK_NOPE_DIM = 128
QK_ROPE_DIM = 64
V_HEAD_DIM = 128
HEAD_PAD = 256
ROPE_BASE = 10000.0
ADAM_LR = 0.001
ADAM_B1 = 0.9
ADAM_B2 = 0.999
ADAM_EPS = 1e-08
ADAM_WD = 0.01
ADAM_STEP = 10

N_DEV = 8
N_CHIP = 4
LANE = 128
SUBLANE = 8
VMEM_LIMIT = 48 * 1024 * 1024
CONV_HALO = 32
MESH_ID = pl.DeviceIdType.MESH
NT = (((1,), (1,)), ((), ()))


def _cp(*sem):
    return pltpu.CompilerParams(dimension_semantics=sem, vmem_limit_bytes=VMEM_LIMIT)


def _tile(n, pref, align):
    if n <= pref:
        return n
    t = (pref // align) * align
    while t > align and n % t:
        t -= align
    assert n % t == 0, (n, pref, align)
    return t


def _silu(v):
    return v * jax.nn.sigmoid(v)


def _dsilu(v):
    s = jax.nn.sigmoid(v)
    return s * (1.0 + v * (1.0 - s))


class _Carry(NamedTuple):
    bufs: tuple
    out_shapes: tuple
    aliases: dict
    sem_shape: tuple
    start: Callable
    finish: Callable


def _pcall(body, *, name, grid, in_specs, out_specs, out_shape, scratch_shapes, sem, args, carry=None):
    if carry is None:
        return pl.pallas_call(body, name=name, grid=grid, in_specs=in_specs, out_specs=out_specs, out_shape=out_shape,
                              scratch_shapes=scratch_shapes, compiler_params=_cp(*sem))(*args)
    n_in, n_out, n_sc = len(in_specs), len(out_specs), len(scratch_shapes)
    nc_in, nc_out = len(carry.bufs), len(carry.out_shapes)

    def wrapped(*refs):
        core_in, c_in = refs[:n_in], refs[n_in:n_in + nc_in]
        core_out = refs[n_in + nc_in:n_in + nc_in + n_out]
        c_out = refs[n_in + nc_in + n_out:n_in + nc_in + n_out + nc_out]
        core_sc = refs[n_in + nc_in + n_out + nc_out:n_in + nc_in + n_out + nc_out + n_sc]
        send_sems, recv_sems = refs[-2:]
        first = pl.program_id(0) == 0
        last = pl.program_id(0) == grid[0] - 1
        for d in range(1, len(grid)):
            first = jnp.logical_and(first, pl.program_id(d) == 0)
            last = jnp.logical_and(last, pl.program_id(d) == grid[d] - 1)

        @pl.when(first)
        def _():
            carry.start(c_in, c_out, send_sems, recv_sems)

        body(*core_in, *core_out, *core_sc)

        @pl.when(last)
        def _():
            carry.finish(c_in, c_out, send_sems, recv_sems)

    any_spec = pl.BlockSpec(memory_space=pl.ANY)
    res = pl.pallas_call(
        wrapped, name=name, grid=grid, in_specs=list(in_specs) + [any_spec] * nc_in,
        out_specs=list(out_specs) + [any_spec] * nc_out, out_shape=list(out_shape) + list(carry.out_shapes),
        input_output_aliases={n_in + i: n_out + o for i, o in carry.aliases.items()},
        scratch_shapes=list(scratch_shapes) + [pltpu.SemaphoreType.DMA(carry.sem_shape), pltpu.SemaphoreType.DMA(carry.sem_shape)],
        compiler_params=_cp(*(("arbitrary",) * len(grid))),
    )(*args, *carry.bufs)
    return res[:n_out], res[n_out:]


def _mm(a, b, *, name, ta=False, tb=False, b_sh=False, o_sh=False, out_dtype=F32, tm=1024, tn=1024, tk=2048, carry=None):
    M, K = (a.shape[1], a.shape[0]) if ta else a.shape
    if b_sh:
        assert b.shape[0] == N_CHIP
        nq = b.shape[2]
        Kb, N = (nq * N_CHIP, b.shape[1]) if tb else (b.shape[1], nq * N_CHIP)
    else:
        Kb, N = (b.shape[1], b.shape[0]) if tb else b.shape
        nq = N // N_CHIP
    assert K == Kb, (a.shape, b.shape)
    tm = _tile(M, tm, LANE)
    tk = _tile(nq if (b_sh and tb) else K, tk, LANE)
    tn = _tile(nq if ((b_sh and not tb) or o_sh) else N, tn, LANE)
    nk = K // tk

    def body(a_ref, b_ref, o_ref, acc_ref):
        k = pl.program_id(2)

        @pl.when(k == 0)
        def _():
            acc_ref[...] = jnp.zeros_like(acc_ref)

        dn = (((0 if ta else 1,), (1 if tb else 0,)), ((), ()))
        acc_ref[...] += lax.dot_general(a_ref[...].astype(BF16), b_ref[...].astype(BF16), dn,
                                        preferred_element_type=F32)

        @pl.when(k == nk - 1)
        def _():
            o_ref[...] = acc_ref[...].astype(o_ref.dtype)

    a_spec = pl.BlockSpec((tk, tm), lambda i, j, k: (k, i)) if ta else pl.BlockSpec((tm, tk), lambda i, j, k: (i, k))
    if b_sh and not tb:
        per = nq // tn
        b_spec = pl.BlockSpec((None, tk, tn), lambda i, j, k: (j // per, k, j % per))
    elif b_sh and tb:
        per = nq // tk
        b_spec = pl.BlockSpec((None, tn, tk), lambda i, j, k: (k // per, j, k % per))
    elif tb:
        b_spec = pl.BlockSpec((tn, tk), lambda i, j, k: (j, k))
    else:
        b_spec = pl.BlockSpec((tk, tn), lambda i, j, k: (k, j))
    if o_sh:
        per_o = nq // tn
        o_spec = pl.BlockSpec((None, tm, tn), lambda i, j, k: (j // per_o, i, j % per_o))
        o_shape = jax.ShapeDtypeStruct((N_CHIP, M, nq), out_dtype)
    else:
        o_spec = pl.BlockSpec((tm, tn), lambda i, j, k: (i, j))
        o_shape = jax.ShapeDtypeStruct((M, N), out_dtype)
    res = _pcall(body, name=name, grid=(M // tm, N // tn, nk), in_specs=[a_spec, b_spec], out_specs=[o_spec],
                 out_shape=[o_shape], scratch_shapes=[pltpu.VMEM((tm, tn), F32)], sem=("parallel", "parallel", "arbitrary"),
                 args=(a, b), carry=carry)
    return res[0] if carry is None else (res[0][0], res[1])


def _mod(c_all, w_ada, b_sh, name):
    L, D, nq = w_ada.shape
    B = c_all.shape[0]
    tn = _tile(nq, 512, LANE)

    def body(c_ref, w_ref, b_ref, o_ref):
        sc = _silu(c_ref[...]).astype(BF16)
        o_ref[...] = jnp.dot(sc, w_ref[...].astype(BF16), preferred_element_type=F32) + b_ref[...]

    return pl.pallas_call(
        body, name=name, grid=(L, nq // tn),
        in_specs=[pl.BlockSpec((B, D), lambda l, j: (0, 0)), pl.BlockSpec((None, D, tn), lambda l, j: (l, 0, j)),
                  pl.BlockSpec((None, 1, tn), lambda l, j: (l, 0, j))],
        out_specs=pl.BlockSpec((None, B, tn), lambda l, j: (l, 0, j)),
        out_shape=jax.ShapeDtypeStruct((L, B, nq), F32), compiler_params=_cp("parallel", "parallel"),
    )(c_all, w_ada, b_sh)


def _wada_grad(c_all_t, dmod, name):
    D, B = c_all_t.shape
    L, _, nq = dmod.shape
    tm = _tile(D, 512, SUBLANE)
    tn = _tile(nq, 1024, LANE)

    def body(c_ref, d_ref, o_ref):
        sc = _silu(c_ref[...])
        dm = d_ref[...]
        acc = sc[:, 0:1] * dm[0:1, :]
        for b in range(1, B):
            acc = acc + sc[:, b:b + 1] * dm[b:b + 1, :]
        o_ref[...] = acc

    return pl.pallas_call(
        body, name=name, grid=(L, D // tm, nq // tn),
        in_specs=[pl.BlockSpec((tm, B), lambda l, i, j: (i, 0)), pl.BlockSpec((None, B, tn), lambda l, i, j: (l, 0, j))],
        out_specs=pl.BlockSpec((None, tm, tn), lambda l, i, j: (l, i, j)),
        out_shape=jax.ShapeDtypeStruct((L, D, nq), F32), compiler_params=_cp("parallel", "parallel", "parallel"),
    )(c_all_t, dmod)


ROW_TILE = 128


def _ln_stats(v):
    mu = jnp.mean(v, axis=-1, keepdims=True)
    vc = v - mu
    var = jnp.mean(vc * vc, axis=-1, keepdims=True)
    rstd = lax.rsqrt(var + LN_EPS)
    return vc * rstd, rstd


def _ln_bwd(dxhat, xhat, rstd):
    m1 = jnp.mean(dxhat, axis=-1, keepdims=True)
    m2 = jnp.mean(dxhat * xhat, axis=-1, keepdims=True)
    return rstd * (dxhat - m1 - xhat * m2)


def _row_spec(ts, D):
    return pl.BlockSpec((ts, D), lambda i: (i, 0))


def _vec_spec(D):
    return pl.BlockSpec((1, D), lambda i: (0, 0))


def _lnmod_fwd(x, scale, shift, name):
    S, D = x.shape
    ts = _tile(S, ROW_TILE, SUBLANE)

    def body(x_ref, sc_ref, sh_ref, h_ref):
        xn, _ = _ln_stats(x_ref[...])
        h_ref[...] = (xn * (1.0 + sc_ref[...]) + sh_ref[...]).astype(h_ref.dtype)

    return pl.pallas_call(
        body, name=name, grid=(S // ts,), in_specs=[_row_spec(ts, D), _vec_spec(D), _vec_spec(D)],
        out_specs=_row_spec(ts, D), out_shape=jax.ShapeDtypeStruct((S, D), BF16), compiler_params=_cp("parallel"),
    )(x, scale, shift)


def _lnmod_bwd(x, dh, scale, adds, name):
    S, D = x.shape
    ts = _tile(S, ROW_TILE, SUBLANE)
    na = len(adds)

    def body(*refs):
        x_ref, dh_ref, sc_ref = refs[:3]
        add_refs = refs[3:3 + na]
        dx_ref, dsc_ref, dsh_ref = refs[3 + na:]
        i = pl.program_id(0)

        @pl.when(i == 0)
        def _():
            dsc_ref[...] = jnp.zeros_like(dsc_ref)
            dsh_ref[...] = jnp.zeros_like(dsh_ref)

        xn, rstd = _ln_stats(x_ref[...])
        dh = dh_ref[...].astype(F32)
        dx = _ln_bwd(dh * (1.0 + sc_ref[...]), xn, rstd)
        for r in add_refs:
            dx = dx + r[...]
        dx_ref[...] = dx
        dsc_ref[...] += jnp.sum(dh * xn, axis=0, keepdims=True)
        dsh_ref[...] += jnp.sum(dh, axis=0, keepdims=True)

    return pl.pallas_call(
        body, name=name, grid=(S // ts,),
        in_specs=[_row_spec(ts, D), _row_spec(ts, D), _vec_spec(D)] + [_row_spec(ts, D)] * na,
        out_specs=[_row_spec(ts, D), _vec_spec(D), _vec_spec(D)],
        out_shape=[jax.ShapeDtypeStruct((S, D), F32), jax.ShapeDtypeStruct((1, D), F32), jax.ShapeDtypeStruct((1, D), F32)],
        compiler_params=_cp("arbitrary"),
    )(x, dh, scale, *adds)


def _resln_fwd(x, out, gate, g, b, name):
    S, D = x.shape
    ts = _tile(S, ROW_TILE, SUBLANE)

    def body(x_ref, o_ref, gt_ref, g_ref, b_ref, y_ref):
        r = DEEPNORM_ALPHA * x_ref[...] + (1.0 + gt_ref[...]) * o_ref[...]
        xhat, _ = _ln_stats(r)
        y_ref[...] = xhat * g_ref[...] + b_ref[...]

    return pl.pallas_call(
        body, name=name, grid=(S // ts,),
        in_specs=[_row_spec(ts, D), _row_spec(ts, D), _vec_spec(D), _vec_spec(D), _vec_spec(D)],
        out_specs=_row_spec(ts, D), out_shape=jax.ShapeDtypeStruct((S, D), F32), compiler_params=_cp("parallel"),
    )(x, out, gate, g, b)


def _resln_bwd(x, out, gate, g, b, dy_or_target, from_target, name):
    S, D = x.shape
    ts = _tile(S, ROW_TILE, SUBLANE)

    def body(x_ref, o_ref, gt_ref, g_ref, b_ref, t_ref, dxa_ref, dout_ref, dgt_ref, dg_ref, db_ref, *maybe_loss):
        i = pl.program_id(0)

        @pl.when(i == 0)
        def _():
            dgt_ref[...] = jnp.zeros_like(dgt_ref)
            dg_ref[...] = jnp.zeros_like(dg_ref)
            db_ref[...] = jnp.zeros_like(db_ref)
            if from_target:
                maybe_loss[0][...] = jnp.zeros_like(maybe_loss[0])

        ov = o_ref[...]
        g1 = 1.0 + gt_ref[...]
        r = DEEPNORM_ALPHA * x_ref[...] + g1 * ov
        xhat, rstd = _ln_stats(r)
        if from_target:
            err = xhat * g_ref[...] + b_ref[...] - t_ref[...]
            dy = err * (1.0 / D)
            maybe_loss[0][...] += jnp.sum(jnp.sum(err * err, axis=-1, keepdims=True), axis=0, keepdims=True) * (0.5 / D)
        else:
            dy = t_ref[...]
        dr = _ln_bwd(dy * g_ref[...], xhat, rstd)
        dxa_ref[...] = DEEPNORM_ALPHA * dr
        dout_ref[...] = (dr * g1).astype(dout_ref.dtype)
        dgt_ref[...] += jnp.sum(dr * ov, axis=0, keepdims=True)
        dg_ref[...] += jnp.sum(dy * xhat, axis=0, keepdims=True)
        db_ref[...] += jnp.sum(dy, axis=0, keepdims=True)

    vec = jax.ShapeDtypeStruct((1, D), F32)
    out_specs = [_row_spec(ts, D), _row_spec(ts, D), _vec_spec(D), _vec_spec(D), _vec_spec(D)]
    out_shape = [jax.ShapeDtypeStruct((S, D), F32), jax.ShapeDtypeStruct((S, D), BF16), vec, vec, vec]
    if from_target:
        out_specs.append(pl.BlockSpec((1, 1), lambda i: (0, 0)))
        out_shape.append(jax.ShapeDtypeStruct((1, 1), F32))
    return pl.pallas_call(
        body, name=name, grid=(S // ts,),
        in_specs=[_row_spec(ts, D), _row_spec(ts, D), _vec_spec(D), _vec_spec(D), _vec_spec(D), _row_spec(ts, D)],
        out_specs=out_specs, out_shape=out_shape, compiler_params=_cp("arbitrary"),
    )(x, out, gate, g, b, dy_or_target)


def _conv_tiles(S, C):
    tt = _tile(S, 256, CONV_HALO)
    tc = _tile(C, 512, LANE)
    return tt, tc


def _shift_scratch(tt, tc):
    return pltpu.VMEM((SUBLANE - 1, tt + CONV_HALO - SUBLANE, tc), F32)


def _fill_shifts(src_ref, sh_ref, tt):
    rows = tt + CONV_HALO - SUBLANE
    for b in range(1, SUBLANE):
        sh_ref[b - 1] = src_ref[b:b + rows, :]


def _shifted(src_ref, sh_ref, q, tt):
    a8, b8 = divmod(q, SUBLANE)
    if b8 == 0:
        return src_ref[q:q + tt, :]
    return sh_ref[b8 - 1, a8 * SUBLANE:a8 * SUBLANE + tt, :]


def _conv1_fwd(u, w_dw, b_dw, name, carry=None):
    S, C3 = u.shape
    C = C3 // 3
    KS = w_dw.shape[0]
    tt, tc = _conv_tiles(S, C)
    ncb = C // tc
    hb = tt // CONV_HALO
    lead = CONV_HALO - (KS - 1)

    def body(a_ref, g_ref, ah_ref, gh_ref, w_ref, b_ref, o_ref, pad_ref, sh_ref):
        i = pl.program_id(0)
        halo = ah_ref[...] * jax.nn.sigmoid(gh_ref[...])
        pad_ref[0:CONV_HALO, :] = jnp.where(i > 0, halo, 0.0)
        pad_ref[CONV_HALO:, :] = a_ref[...] * jax.nn.sigmoid(g_ref[...])
        _fill_shifts(pad_ref, sh_ref, tt)
        acc = jnp.broadcast_to(b_ref[...], (tt, tc))
        for k in range(KS):
            acc = acc + w_ref[k:k + 1, :] * _shifted(pad_ref, sh_ref, lead + k, tt)
        o_ref[...] = acc

    main = lambda off: pl.BlockSpec((tt, tc), lambda i, j: (i, off + j))
    halo = lambda off: pl.BlockSpec((CONV_HALO, tc), lambda i, j: (jnp.maximum(i * hb - 1, 0), off + j))
    res = _pcall(
        body, name=name, grid=(S // tt, ncb),
        in_specs=[main(0), main(ncb), halo(0), halo(ncb), pl.BlockSpec((KS, tc), lambda i, j: (0, j)),
                  pl.BlockSpec((1, tc), lambda i, j: (0, j))],
        out_specs=[pl.BlockSpec((tt, tc), lambda i, j: (i, j))], out_shape=[jax.ShapeDtypeStruct((S, C), F32)],
        scratch_shapes=[pltpu.VMEM((CONV_HALO + tt, tc), F32), _shift_scratch(tt, tc)], sem=("parallel", "parallel"),
        args=(u, u, u, u, w_dw, b_dw), carry=carry)
    return res[0] if carry is None else (res[0][0], res[1])


def _conv2_fwd(v2, u, g_cn, b_cn, name):
    S, C = v2.shape
    ts = _tile(S, ROW_TILE, SUBLANE)

    def body(v_ref, z_ref, g_ref, b_ref, p_ref):
        xhat, _ = _ln_stats(v_ref[...])
        v3 = xhat * g_ref[...] + b_ref[...]
        p_ref[...] = (_silu(v3) * _silu(z_ref[...])).astype(p_ref.dtype)

    return pl.pallas_call(
        body, name=name, grid=(S // ts,),
        in_specs=[_row_spec(ts, C), pl.BlockSpec((ts, C), lambda i: (i, 2)), _vec_spec(C), _vec_spec(C)],
        out_specs=_row_spec(ts, C), out_shape=jax.ShapeDtypeStruct((S, C), BF16), compiler_params=_cp("parallel"),
    )(v2, u, g_cn, b_cn)


def _conv2_bwd(dp, v2, u, g_cn, b_cn, name):
    S, C = v2.shape
    ts = _tile(S, ROW_TILE, SUBLANE)

    def body(dp_ref, v_ref, z_ref, g_ref, b_ref, dv_ref, dz_ref, dg_ref, db_ref):
        i = pl.program_id(0)

        @pl.when(i == 0)
        def _():
            dg_ref[...] = jnp.zeros_like(dg_ref)
            db_ref[...] = jnp.zeros_like(db_ref)

        dp = dp_ref[...].astype(F32)
        z = z_ref[...]
        xhat, rstd = _ln_stats(v_ref[...])
        v3 = xhat * g_ref[...] + b_ref[...]
        dz_ref[...] = (dp * _silu(v3) * _dsilu(z)).astype(dz_ref.dtype)
        dv3 = dp * _silu(z) * _dsilu(v3)
        dv_ref[...] = _ln_bwd(dv3 * g_ref[...], xhat, rstd)
        dg_ref[...] += jnp.sum(dv3 * xhat, axis=0, keepdims=True)
        db_ref[...] += jnp.sum(dv3, axis=0, keepdims=True)

    vec = jax.ShapeDtypeStruct((1, C), F32)
    return pl.pallas_call(
        body, name=name, grid=(S // ts,),
        in_specs=[_row_spec(ts, C), _row_spec(ts, C), pl.BlockSpec((ts, C), lambda i: (i, 2)), _vec_spec(C), _vec_spec(C)],
        out_specs=[_row_spec(ts, C), _row_spec(ts, C), _vec_spec(C), _vec_spec(C)],
        out_shape=[jax.ShapeDtypeStruct((S, C), F32), jax.ShapeDtypeStruct((S, C), BF16), vec, vec],
        compiler_params=_cp("arbitrary"),
    )(dp, v2, u, g_cn, b_cn)


def _conv1_bwd(dv2, u, w_dw, name, carry=None):
    S, C = dv2.shape
    KS = w_dw.shape[0]
    tt, tc = _conv_tiles(S, C)
    ncb = C // tc
    nt = S // tt
    hb = tt // CONV_HALO
    lead = CONV_HALO - (KS - 1)

    def body(dv_ref, dvh_ref, a_ref, g_ref, ah_ref, gh_ref, w_ref, da_ref, dg_ref, dw_ref, db_ref, pad_ref, fpad_ref,
             sh_ref, fsh_ref):
        i = pl.program_id(1)

        @pl.when(i == 0)
        def _():
            dw_ref[...] = jnp.zeros_like(dw_ref)
            db_ref[...] = jnp.zeros_like(db_ref)

        dv = dv_ref[...]
        a = a_ref[...]
        sg = jax.nn.sigmoid(g_ref[...])
        halo = ah_ref[...] * jax.nn.sigmoid(gh_ref[...])
        pad_ref[0:CONV_HALO, :] = jnp.where(i > 0, halo, 0.0)
        pad_ref[CONV_HALO:, :] = a * sg
        fpad_ref[0:tt, :] = dv
        fpad_ref[tt:, :] = jnp.where(i < nt - 1, dvh_ref[...], 0.0)
        _fill_shifts(pad_ref, sh_ref, tt)
        _fill_shifts(fpad_ref, fsh_ref, tt)
        dv1 = jnp.zeros((tt, tc), F32)
        for k in range(KS):
            dv1 = dv1 + w_ref[k:k + 1, :] * _shifted(fpad_ref, fsh_ref, KS - 1 - k, tt)
            dw_ref[k:k + 1, :] += jnp.sum(dv * _shifted(pad_ref, sh_ref, lead + k, tt), axis=0, keepdims=True)
        db_ref[...] += jnp.sum(dv, axis=0, keepdims=True)
        da_ref[...] = (dv1 * sg).astype(da_ref.dtype)
        dg_ref[...] = (dv1 * a * sg * (1.0 - sg)).astype(dg_ref.dtype)

    main = lambda off: pl.BlockSpec((tt, tc), lambda j, i: (i, off + j))
    halo = lambda off: pl.BlockSpec((CONV_HALO, tc), lambda j, i: (jnp.maximum(i * hb - 1, 0), off + j))
    fhalo = pl.BlockSpec((CONV_HALO, tc), lambda j, i: (jnp.minimum((i + 1) * hb, nt * hb - 1), j))
    res = _pcall(
        body, name=name, grid=(ncb, nt),
        in_specs=[main(0), fhalo, main(0), main(ncb), halo(0), halo(ncb), pl.BlockSpec((KS, tc), lambda j, i: (0, j))],
        out_specs=[main(0), main(0), pl.BlockSpec((KS, tc), lambda j, i: (0, j)), pl.BlockSpec((1, tc), lambda j, i: (0, j))],
        out_shape=[jax.ShapeDtypeStruct((S, C), BF16), jax.ShapeDtypeStruct((S, C), BF16),
                   jax.ShapeDtypeStruct((KS, C), F32), jax.ShapeDtypeStruct((1, C), F32)],
        scratch_shapes=[pltpu.VMEM((CONV_HALO + tt, tc), F32), pltpu.VMEM((tt + CONV_HALO, tc), F32),
                        _shift_scratch(tt, tc), _shift_scratch(tt, tc)],
        sem=("parallel", "arbitrary"), args=(dv2, dv2, u, u, u, u, w_dw), carry=carry)
    return res


def _rope_tables(S):
    half = QK_ROPE_DIM // 2
    inv_freq = ROPE_BASE ** (-jnp.arange(half, dtype=F32) / half)
    ang = jnp.arange(S, dtype=jnp.int32).astype(F32)[:, None] * inv_freq[None, :]
    cos, sin, z = jnp.cos(ang), jnp.sin(ang), jnp.zeros((S, half), F32)
    tc = jnp.concatenate([cos, cos, z, z], axis=1)
    t1 = jnp.concatenate([-sin, z, z, z], axis=1)
    t2 = jnp.concatenate([z, sin, z, z], axis=1)
    return tc, t1, t2


def _rope128(r, tc, t1, t2, sign):
    return r * tc + sign * (pltpu.roll(r, LANE - QK_ROPE_DIM // 2, 1) * t1 + pltpu.roll(r, QK_ROPE_DIM // 2, 1) * t2)


def _rms_fwd(x, width, g, name):
    S = x.shape[0]
    ts = _tile(S, 256, SUBLANE)

    def body(x_ref, g_ref, o_ref):
        xv = x_ref[...]
        rr = lax.rsqrt(jnp.mean(xv * xv, axis=-1, keepdims=True) + RMS_EPS)
        o_ref[...] = (xv * rr * g_ref[...]).astype(o_ref.dtype)

    return pl.pallas_call(
        body, name=name, grid=(S // ts,), in_specs=[_row_spec(ts, width), _vec_spec(width)],
        out_specs=_row_spec(ts, width), out_shape=jax.ShapeDtypeStruct((S, width), BF16), compiler_params=_cp("parallel"),
    )(x, g)


def _rms_bwd_math(xv, dy, g):
    n = xv.shape[-1]
    rr = lax.rsqrt(jnp.mean(xv * xv, axis=-1, keepdims=True) + RMS_EPS)
    dyg = dy * g
    dx = rr * dyg - xv * (rr * rr * rr) * (jnp.sum(dyg * xv, axis=-1, keepdims=True) * (1.0 / n))
    dg = jnp.sum(dy * xv * rr, axis=0, keepdims=True)
    return dx, dg


def _rms_bwd(x, width, dy, g, du, name):
    S = x.shape[0]
    ts = _tile(S, 256, SUBLANE)

    def body(x_ref, dy_ref, g_ref, du_in, dx_ref, dg_ref):
        del du_in
        i = pl.program_id(0)

        @pl.when(i == 0)
        def _():
            dg_ref[...] = jnp.zeros_like(dg_ref)

        dx, dg = _rms_bwd_math(x_ref[...], dy_ref[...].astype(F32), g_ref[...])
        dx_ref[...] = dx.astype(dx_ref.dtype)
        dg_ref[...] += dg

    return pl.pallas_call(
        body, name=name, grid=(S // ts,),
        in_specs=[_row_spec(ts, width), _row_spec(ts, width), _vec_spec(width), pl.BlockSpec(memory_space=pl.ANY)],
        out_specs=[_row_spec(ts, width), _vec_spec(width)],
        out_shape=[jax.ShapeDtypeStruct(du.shape, du.dtype), jax.ShapeDtypeStruct((1, width), F32)],
        input_output_aliases={3: 0}, compiler_params=_cp("arbitrary"),
    )(x, dy, g, du)


def _kvprep_fwd(kva, g_kv, tabs, name):
    S, W = kva.shape
    KV = W - LANE
    ts = _tile(S, 256, SUBLANE)

    def body(x_ref, g_ref, tc_ref, t1_ref, t2_ref, c_ref, r_ref):
        xv = x_ref[:, 0:KV]
        rr = lax.rsqrt(jnp.mean(xv * xv, axis=-1, keepdims=True) + RMS_EPS)
        c_ref[...] = (xv * rr * g_ref[...]).astype(c_ref.dtype)
        r_ref[...] = _rope128(x_ref[:, KV:], tc_ref[...], t1_ref[...], t2_ref[...], 1.0).astype(r_ref.dtype)

    tab = _row_spec(ts, LANE)
    return pl.pallas_call(
        body, name=name, grid=(S // ts,), in_specs=[_row_spec(ts, W), _vec_spec(KV), tab, tab, tab],
        out_specs=[_row_spec(ts, KV), _row_spec(ts, LANE)],
        out_shape=[jax.ShapeDtypeStruct((S, KV), BF16), jax.ShapeDtypeStruct((S, LANE), BF16)], compiler_params=_cp("parallel"),
    )(kva, g_kv, *tabs)


def _kvprep_bwd(kva, dckv, dkr_h, g_kv, tabs, name):
    S, W = kva.shape
    KV = W - LANE
    H = dkr_h.shape[1] // LANE
    ts = _tile(S, 256, SUBLANE)

    def body(x_ref, dc_ref, dr_ref, g_ref, tc_ref, t1_ref, t2_ref, o_ref, dg_ref):
        i = pl.program_id(0)

        @pl.when(i == 0)
        def _():
            dg_ref[...] = jnp.zeros_like(dg_ref)

        dx, dg = _rms_bwd_math(x_ref[:, 0:KV], dc_ref[...].astype(F32), g_ref[...])
        o_ref[:, 0:KV] = dx.astype(o_ref.dtype)
        dg_ref[...] += dg
        dr = dr_ref[:, 0:LANE]
        for h in range(1, H):
            dr = dr + dr_ref[:, h * LANE:(h + 1) * LANE]
        o_ref[:, KV:] = _rope128(dr, tc_ref[...], t1_ref[...], t2_ref[...], -1.0).astype(o_ref.dtype)

    tab = _row_spec(ts, LANE)
    return pl.pallas_call(
        body, name=name, grid=(S // ts,),
        in_specs=[_row_spec(ts, W), _row_spec(ts, KV), _row_spec(ts, H * LANE), _vec_spec(KV), tab, tab, tab],
        out_specs=[_row_spec(ts, W), _vec_spec(KV)],
        out_shape=[jax.ShapeDtypeStruct((S, W), BF16), jax.ShapeDtypeStruct((1, KV), F32)], compiler_params=_cp("arbitrary"),
    )(kva, dckv, dkr_h, g_kv, *tabs)


ROPE_GROUP = 8


def _qrope_bwd(dq, tabs, name):
    S, W = dq.shape
    H = W // HEAD_PAD
    G = math.gcd(H, ROPE_GROUP)
    ts = _tile(S, 512, 2 * SUBLANE)

    def body(q_ref, tc_ref, t1_ref, t2_ref, o_ref):
        for g in range(G):
            lo = g * HEAD_PAD
            o_ref[:, lo:lo + LANE] = q_ref[:, lo:lo + LANE].astype(o_ref.dtype)
            o_ref[:, lo + LANE:lo + HEAD_PAD] = _rope128(
                q_ref[:, lo + LANE:lo + HEAD_PAD], tc_ref[...], t1_ref[...], t2_ref[...], -1.0).astype(o_ref.dtype)

    tab = pl.BlockSpec((ts, LANE), lambda i, h: (i, 0))
    blk = pl.BlockSpec((ts, G * HEAD_PAD), lambda i, h: (i, h))
    return pl.pallas_call(
        body, name=name, grid=(S // ts, H // G), in_specs=[blk, tab, tab, tab], out_specs=blk,
        out_shape=jax.ShapeDtypeStruct((S, W), BF16), compiler_params=_cp("parallel", "parallel"),
    )(dq, *tabs)


ATT_TILE = 512
ATT_FWD_HEADS = 4


def _causal_mask(t, transposed):
    r = lax.broadcasted_iota(jnp.int32, (t, t), 0)
    c = lax.broadcasted_iota(jnp.int32, (t, t), 1)
    return (c >= r) if transposed else (r >= c)


def _attn_fwd(qraw, kvh, krp, u, zoff, tabs, scale, name):
    S, W = qraw.shape
    H = W // HEAD_PAD
    G = math.gcd(H, ATT_FWD_HEADS)
    t = _tile(S, ATT_TILE, LANE)
    nq = S // t
    assert zoff % (G * V_HEAD_DIM) == 0
    zb = zoff // (G * V_HEAD_DIM)

    def body(q_ref, tc_ref, t1_ref, t2_ref, kv_ref, kr_ref, z_ref, o_ref, p_ref, qr_ref, lse_ref, vt_sc, m_sc, l_sc, acc_sc):
        i = pl.program_id(1)

        @pl.when(i == 0)
        def _():
            for g in range(G):
                lo = g * HEAD_PAD + QK_NOPE_DIM
                for jj in range(nq):
                    vt_sc[g, jj] = kv_ref[jj * t:(jj + 1) * t, lo:lo + V_HEAD_DIM].astype(F32).T.astype(BF16)

        qs = []
        for g in range(G):
            lo = g * HEAD_PAD
            qrot = _rope128(q_ref[:, lo + LANE:lo + HEAD_PAD], tc_ref[...], t1_ref[...], t2_ref[...], 1.0)
            q = jnp.concatenate([q_ref[:, lo:lo + LANE].astype(BF16), qrot.astype(BF16)], axis=-1)
            qr_ref[:, lo:lo + HEAD_PAD] = q
            qs.append(q)
        m_sc[...] = jnp.full_like(m_sc, -jnp.inf)
        l_sc[...] = jnp.zeros_like(l_sc)
        acc_sc[...] = jnp.zeros_like(acc_sc)

        def step(j, masked):
            off = pl.multiple_of(j * t, t)
            kr = kr_ref[pl.ds(off, t), :]
            for g in range(G):
                k = jnp.concatenate([kv_ref[pl.ds(off, t), g * HEAD_PAD:g * HEAD_PAD + QK_NOPE_DIM], kr], axis=-1)
                st = lax.dot_general(k, qs[g], NT, preferred_element_type=F32) * scale
                if masked:
                    st = jnp.where(_causal_mask(t, True), st, -jnp.inf)
                m_old = m_sc[g]
                m_new = jnp.maximum(m_old, jnp.max(st, axis=0, keepdims=True))
                a = jnp.exp(m_old - m_new)
                pt = jnp.exp(st - m_new)
                l_sc[g] = a * l_sc[g] + jnp.sum(pt, axis=0, keepdims=True)
                acc_sc[g] = a * acc_sc[g] + jnp.dot(vt_sc[g, j], pt.astype(BF16), preferred_element_type=F32)
                m_sc[g] = m_new

        def loop_body(j, carry):
            step(j, False)
            return carry

        lax.fori_loop(0, i, loop_body, 0)
        step(i, True)
        for g in range(G):
            ov = (acc_sc[g] / l_sc[g]).T
            cols = slice(g * V_HEAD_DIM, (g + 1) * V_HEAD_DIM)
            o_ref[:, cols] = ov
            p_ref[:, cols] = (ov * _silu(z_ref[:, cols])).astype(p_ref.dtype)
            lse_ref[g] = m_sc[g] + jnp.log(l_sc[g])

    tab = pl.BlockSpec((t, LANE), lambda h, i: (i, 0))
    head = pl.BlockSpec((t, G * V_HEAD_DIM), lambda h, i: (i, h))
    return pl.pallas_call(
        body, name=name, grid=(H // G, nq),
        in_specs=[pl.BlockSpec((t, G * HEAD_PAD), lambda h, i: (i, h)), tab, tab, tab,
                  pl.BlockSpec((S, G * HEAD_PAD), lambda h, i: (0, h)), pl.BlockSpec((S, LANE), lambda h, i: (0, 0)),
                  pl.BlockSpec((t, G * V_HEAD_DIM), lambda h, i: (i, zb + h))],
        out_specs=[head, head, pl.BlockSpec((t, G * HEAD_PAD), lambda h, i: (i, h)),
                   pl.BlockSpec((G, None, 1, t), lambda h, i: (h, i, 0, 0))],
        out_shape=[jax.ShapeDtypeStruct((S, H * V_HEAD_DIM), F32), jax.ShapeDtypeStruct((S, H * V_HEAD_DIM), BF16),
                   jax.ShapeDtypeStruct((S, W), BF16), jax.ShapeDtypeStruct((H, nq, 1, t), F32)],
        scratch_shapes=[pltpu.VMEM((G, nq, V_HEAD_DIM, t), BF16), pltpu.VMEM((G, 1, t), F32), pltpu.VMEM((G, 1, t), F32),
                        pltpu.VMEM((G, V_HEAD_DIM, t), F32)],
        compiler_params=_cp("parallel", "arbitrary"),
    )(qraw, *tabs, kvh, krp, u)


def _attn_bwd(qr, kvh, krp, dp, o, u, zoff, lse, scale, name):
    S, W = qr.shape
    H = W // HEAD_PAD
    U = u.shape[1]
    t = _tile(S, ATT_TILE, LANE)
    nq = S // t
    zb = zoff // V_HEAD_DIM

    def body(q_ref, kv_ref, kr_ref, dp_ref, o_ref, z_ref, lse_ref, dq_ref, dkv_ref, dkr_ref, dz_ref,
             do_sc, dl_sc, dqt_sc, dk_sc, dv_sc):
        j = pl.program_id(1)

        @pl.when(j == 0)
        def _():
            for ii in range(nq):
                rows = slice(ii * t, (ii + 1) * t)
                dpv, ov, z = dp_ref[rows, :], o_ref[rows, :], z_ref[rows, :]
                dov = dpv * _silu(z)
                do_sc[ii] = dov.astype(BF16)
                dz_ref[rows, :] = (dpv * ov * _dsilu(z)).astype(dz_ref.dtype)
                dl_sc[ii] = jnp.sum((dov * ov).T, axis=0, keepdims=True)
                dqt_sc[ii] = jnp.zeros((HEAD_PAD, t), F32)

        kvb = kv_ref[...]
        k = jnp.concatenate([kvb[:, 0:QK_NOPE_DIM], kr_ref[...]], axis=-1)
        v = kvb[:, QK_NOPE_DIM:]
        kt = k.astype(F32).T.astype(BF16)
        dk_sc[...] = jnp.zeros_like(dk_sc)
        dv_sc[...] = jnp.zeros_like(dv_sc)

        def step(i, masked):
            off = pl.multiple_of(i * t, t)
            q = q_ref[pl.ds(off, t), :]
            dov = do_sc[i]
            st = lax.dot_general(k, q, NT, preferred_element_type=F32) * scale
            pt = jnp.exp(st - lse_ref[i])
            if masked:
                pt = jnp.where(_causal_mask(t, True), pt, 0.0)
            dv_sc[...] += jnp.dot(pt.astype(BF16), dov, preferred_element_type=F32)
            dpt = lax.dot_general(v, dov, NT, preferred_element_type=F32)
            dst = (pt * (dpt - dl_sc[i]) * scale).astype(BF16)
            dk_sc[...] += jnp.dot(dst, q, preferred_element_type=F32)
            dqt_sc[i] += jnp.dot(kt, dst, preferred_element_type=F32)

        def loop_body(i, carry):
            step(i, False)
            return carry

        step(j, True)
        lax.fori_loop(j + 1, nq, loop_body, 0)
        dkv_ref[:, 0:QK_NOPE_DIM] = dk_sc[:, 0:QK_NOPE_DIM].astype(dkv_ref.dtype)
        dkv_ref[:, QK_NOPE_DIM:] = dv_sc[...].astype(dkv_ref.dtype)
        dkr_ref[...] = dk_sc[:, QK_NOPE_DIM:]

        @pl.when(j == nq - 1)
        def _():
            for ii in range(nq):
                dq_ref[ii * t:(ii + 1) * t, :] = dqt_sc[ii].T

    whole = lambda w, off: pl.BlockSpec((S, w), lambda h, j: (0, off + h))
    return pl.pallas_call(
        body, name=name, grid=(H, nq),
        in_specs=[whole(HEAD_PAD, 0), pl.BlockSpec((t, HEAD_PAD), lambda h, j: (j, h)),
                  pl.BlockSpec((t, LANE), lambda h, j: (j, 0)), whole(V_HEAD_DIM, 0), whole(V_HEAD_DIM, 0),
                  whole(V_HEAD_DIM, zb), pl.BlockSpec((None, nq, 1, t), lambda h, j: (h, 0, 0, 0))],
        out_specs=[whole(HEAD_PAD, 0), pl.BlockSpec((t, HEAD_PAD), lambda h, j: (j, h)),
                   pl.BlockSpec((t, LANE), lambda h, j: (j, h)), whole(V_HEAD_DIM, zb)],
        out_shape=[jax.ShapeDtypeStruct((S, W), F32), jax.ShapeDtypeStruct((S, W), BF16),
                   jax.ShapeDtypeStruct((S, H * LANE), F32), jax.ShapeDtypeStruct((S, U), BF16)],
        scratch_shapes=[pltpu.VMEM((nq, t, V_HEAD_DIM), BF16), pltpu.VMEM((nq, 1, t), F32),
                        pltpu.VMEM((nq, HEAD_PAD, t), F32), pltpu.VMEM((t, HEAD_PAD), F32), pltpu.VMEM((t, V_HEAD_DIM), F32)],
        compiler_params=_cp("parallel", "arbitrary"),
    )(qr, kvh, krp, dp, o, u, lse)


def _adamw_math(w, g, m, v):
    m = ADAM_B1 * m + (1.0 - ADAM_B1) * g
    v = ADAM_B2 * v + (1.0 - ADAM_B2) * (g * g)
    m_hat = m / (1.0 - ADAM_B1 ** ADAM_STEP)
    v_hat = v / (1.0 - ADAM_B2 ** ADAM_STEP)
    delta = -ADAM_LR * (m_hat / (jnp.sqrt(v_hat) + ADAM_EPS) + ADAM_WD * w)
    return delta, m, v


def _adamw(w, g, m, v, name):
    R, C = w.shape
    tr = _tile(R, 256, SUBLANE)
    tc = _tile(C, 1024, LANE)

    def body(w_ref, g_ref, m_ref, v_ref, d_ref, nm_ref, nv_ref):
        d, nm, nv = _adamw_math(w_ref[...], g_ref[...], m_ref[...], v_ref[...])
        d_ref[...] = d
        nm_ref[...] = nm
        nv_ref[...] = nv

    blk = pl.BlockSpec((tr, tc), lambda i, j: (i, j))
    sh = jax.ShapeDtypeStruct((R, C), F32)
    return pl.pallas_call(
        body, name=name, grid=(R // tr, C // tc), in_specs=[blk] * 4, out_specs=[blk] * 3, out_shape=[sh] * 3,
        compiler_params=_cp("parallel", "parallel"),
    )(w, g, m, v)


def _sum_leading(x, name):
    n, R, C = x.shape
    tr = _tile(R, 512, SUBLANE)

    def body(x_ref, o_ref):
        acc = x_ref[0]
        for k in range(1, n):
            acc = acc + x_ref[k]
        o_ref[...] = acc

    return pl.pallas_call(
        body, name=name, grid=(R // tr,), in_specs=[pl.BlockSpec((n, tr, C), lambda i: (0, i, 0))],
        out_specs=pl.BlockSpec((tr, C), lambda i: (i, 0)), out_shape=jax.ShapeDtypeStruct((R, C), F32),
        compiler_params=_cp("parallel"),
    )(x)


def _pair_add(full, recv, c_idx, name):
    n, R, C = full.shape
    h = R // 2
    tr = _tile(h, 256, 2 * SUBLANE)
    tc = _tile(C, 1024, LANE)
    nb = h // tr

    def body(c_ref, a_ref, b_ref, o_ref):
        del c_ref
        o_ref[...] = (a_ref[...].astype(F32) + b_ref[...].astype(F32)).astype(o_ref.dtype)

    return pl.pallas_call(
        body, name=name,
        grid_spec=pltpu.PrefetchScalarGridSpec(
            num_scalar_prefetch=1, grid=(n, nb, C // tc),
            in_specs=[pl.BlockSpec((None, tr, tc), lambda k, i, j, c: (k, c[0] * nb + i, j)),
                      pl.BlockSpec((None, tr, tc), lambda k, i, j, c: (k, i, j))],
            out_specs=pl.BlockSpec((None, tr, tc), lambda k, i, j, c: (k, i, j))),
        out_shape=jax.ShapeDtypeStruct((n, h, C), BF16), compiler_params=_cp("parallel", "parallel", "parallel"),
    )(c_idx, full, recv)


def _chip_sum(pair, recv, idx, name):
    n, h, C = pair.shape
    tr = _tile(h, 256, 2 * SUBLANE)
    tc = _tile(C, 1024, LANE)
    nb = h // tr

    def body(c_ref, chip_ref, a_ref, b_ref, o_ref):
        del c_ref, chip_ref
        acc = a_ref[...].astype(F32)
        for k in range(N_CHIP - 1):
            acc = acc + b_ref[k].astype(F32)
        o_ref[...] = acc

    return pl.pallas_call(
        body, name=name,
        grid_spec=pltpu.PrefetchScalarGridSpec(
            num_scalar_prefetch=2, grid=(nb, C // tc),
            in_specs=[pl.BlockSpec((None, tr, tc), lambda i, j, c, chip: (chip[0], i, j)),
                      pl.BlockSpec((N_CHIP - 1, tr, tc), lambda i, j, c, chip: (0, i, j))],
            out_specs=pl.BlockSpec((tr, tc), lambda i, j, c, chip: (c[0] * nb + i, j))),
        out_shape=jax.ShapeDtypeStruct((2 * h, C), F32), compiler_params=_cp("parallel", "parallel"),
    )(idx[0], idx[1], pair, recv)


def _cast_place(w, idx, name):
    R, C = w.shape
    h = R // 2
    tr = _tile(h, 256, 2 * SUBLANE)
    tc = _tile(C, 1024, LANE)
    nb = h // tr

    def body(c_ref, chip_ref, w_ref, o_ref):
        del c_ref, chip_ref
        o_ref[...] = w_ref[...].astype(o_ref.dtype)

    return pl.pallas_call(
        body, name=name,
        grid_spec=pltpu.PrefetchScalarGridSpec(
            num_scalar_prefetch=2, grid=(nb, C // tc),
            in_specs=[pl.BlockSpec((tr, tc), lambda i, j, c, chip: (c[0] * nb + i, j))],
            out_specs=pl.BlockSpec((None, tr, tc), lambda i, j, c, chip: (chip[0], c[0] * nb + i, j))),
        out_shape=jax.ShapeDtypeStruct((N_CHIP, R, C), BF16), compiler_params=_cp("parallel", "parallel"),
    )(idx[0], idx[1], w)


def _coords():
    return lax.axis_index("x"), lax.axis_index("y"), lax.axis_index("c")


def _allgather_small(x_shard, name):
    m_per, n = x_shard.shape

    def body(x_ref, out_ref, send_sems, recv_sems, local_sem):
        x, y, c = _coords()
        me, sibling = (x, y, c), (x, y, 1 - c)
        chips = [(1 - x, y), (x, 1 - y), (1 - x, 1 - y)]

        def rows(px, py, pc):
            return out_ref.at[pl.ds((4 * px + 2 * py + pc) * m_per, m_per), :]

        def copy(k, block, to, src=None):
            return pltpu.make_async_remote_copy(
                src_ref=rows(*block) if src is None else src, dst_ref=rows(*block), send_sem=send_sems.at[k],
                recv_sem=recv_sems.at[k], device_id=to, device_id_type=MESH_ID)

        mine = pltpu.make_async_copy(x_ref, rows(*me), local_sem)
        mine.start()
        first = [copy(0, me, sibling, src=x_ref)]
        first += [copy(1 + j, me, (*chip, c), src=x_ref) for j, chip in enumerate(chips)]
        for cp in first:
            cp.start()
        passed = [copy(4 + j, (*chip, c), sibling) for j, chip in enumerate(chips)]
        for j, chip in enumerate(chips):
            copy(1 + j, (*chip, c), me).wait_recv()
            passed[j].start()
        copy(0, sibling, me).wait_recv()
        for j, chip in enumerate(chips):
            copy(4 + j, (*chip, 1 - c), me).wait_recv()
        for cp in first + passed:
            cp.wait_send()
        mine.wait()

    return pl.pallas_call(
        body, name=name, out_shape=jax.ShapeDtypeStruct((N_DEV * m_per, n), x_shard.dtype),
        in_specs=[pl.BlockSpec(memory_space=pltpu.VMEM)], out_specs=pl.BlockSpec(memory_space=pltpu.VMEM),
        scratch_shapes=[pltpu.SemaphoreType.DMA((7,)), pltpu.SemaphoreType.DMA((7,)), pltpu.SemaphoreType.DMA],
        compiler_params=pltpu.CompilerParams(vmem_limit_bytes=VMEM_LIMIT),
    )(x_shard)


def _allgather_carry(bufs):
    n = len(bufs)

    def plan(outs, send_sems, recv_sems):
        x, y, c = _coords()
        me, sibling = (x, y, c), (x, y, 1 - c)
        chips = [(1 - x, y), (x, 1 - y), (1 - x, 1 - y)]

        def win(a, px, py, pc):
            h = bufs[a].shape[1] // 2
            return outs[a].at[2 * px + py, pl.ds(pc * h, h), :]

        def copy(a, k, block, to):
            return pltpu.make_async_remote_copy(
                src_ref=win(a, *block), dst_ref=win(a, *block), send_sem=send_sems.at[a, k],
                recv_sem=recv_sems.at[a, k], device_id=to, device_id_type=MESH_ID)

        return c, me, sibling, chips, copy

    def start(_, outs, send_sems, recv_sems):
        c, me, sibling, chips, copy = plan(outs, send_sems, recv_sems)
        for a in range(n):
            copy(a, 0, me, sibling).start()
            for j, chip in enumerate(chips):
                copy(a, 1 + j, me, (*chip, c)).start()

    def finish(_, outs, send_sems, recv_sems):
        c, me, sibling, chips, copy = plan(outs, send_sems, recv_sems)
        for a in range(n):
            for j, chip in enumerate(chips):
                copy(a, 1 + j, (*chip, c), me).wait_recv()
                copy(a, 4 + j, (*chip, c), sibling).start()
        for a in range(n):
            copy(a, 0, sibling, me).wait_recv()
            for j, chip in enumerate(chips):
                copy(a, 4 + j, (*chip, 1 - c), me).wait_recv()
        for a in range(n):
            copy(a, 0, me, sibling).wait_send()
            for j, chip in enumerate(chips):
                copy(a, 1 + j, me, (*chip, c)).wait_send()
                copy(a, 4 + j, (*chip, c), sibling).wait_send()

    return _Carry(tuple(bufs), tuple(jax.ShapeDtypeStruct(b.shape, b.dtype) for b in bufs), {a: a for a in range(n)},
                  (n, 7), start, finish)


def _exchange_alone(carry, name):
    n_in, n_out = len(carry.bufs), len(carry.out_shapes)

    def body(*refs):
        ins, outs = refs[:n_in], refs[n_in:n_in + n_out]
        send_sems, recv_sems = refs[n_in + n_out:]
        carry.start(ins, outs, send_sems, recv_sems)
        carry.finish(ins, outs, send_sems, recv_sems)

    any_spec = pl.BlockSpec(memory_space=pl.ANY)
    return pl.pallas_call(
        body, name=name, out_shape=list(carry.out_shapes), in_specs=[any_spec] * n_in, out_specs=[any_spec] * n_out,
        input_output_aliases=dict(carry.aliases),
        scratch_shapes=[pltpu.SemaphoreType.DMA(carry.sem_shape), pltpu.SemaphoreType.DMA(carry.sem_shape)],
    )(*carry.bufs)


def _pair_exchange(grads, name):
    n = len(grads)

    def body(*refs):
        ins, outs = refs[:n], refs[n:2 * n]
        send_sems, recv_sems = refs[2 * n:]
        x, y, c = _coords()
        sibling = (x, y, 1 - c)
        copies = []
        for a in range(n):
            h = grads[a].shape[1] // 2
            cp = pltpu.make_async_remote_copy(
                src_ref=ins[a].at[:, pl.ds((1 - c) * h, h), :], dst_ref=outs[a], send_sem=send_sems.at[a],
                recv_sem=recv_sems.at[a], device_id=sibling, device_id_type=MESH_ID)
            cp.start()
            copies.append(cp)
        for cp in copies:
            cp.wait()

    any_spec = pl.BlockSpec(memory_space=pl.ANY)
    return pl.pallas_call(
        body, name=name,
        out_shape=[jax.ShapeDtypeStruct((g.shape[0], g.shape[1] // 2, g.shape[2]), g.dtype) for g in grads],
        in_specs=[any_spec] * n, out_specs=[any_spec] * n,
        scratch_shapes=[pltpu.SemaphoreType.DMA((n,)), pltpu.SemaphoreType.DMA((n,))],
    )(*grads)


def _chip_exchange_carry(pairs):
    n = len(pairs)

    def copies(ins, outs, send_sems, recv_sems):
        x, y, c = _coords()
        chips = [(1 - x, y), (x, 1 - y), (1 - x, 1 - y)]
        return [pltpu.make_async_remote_copy(
            src_ref=ins[a].at[2 * px + py], dst_ref=outs[a].at[k], send_sem=send_sems.at[a, k],
            recv_sem=recv_sems.at[a, k], device_id=(px, py, c), device_id_type=MESH_ID)
            for a in range(n) for k, (px, py) in enumerate(chips)]

    def start(ins, outs, send_sems, recv_sems):
        for cp in copies(ins, outs, send_sems, recv_sems):
            cp.start()

    def finish(ins, outs, send_sems, recv_sems):
        for cp in copies(ins, outs, send_sems, recv_sems):
            cp.wait()

    return _Carry(tuple(pairs), tuple(jax.ShapeDtypeStruct((N_CHIP - 1,) + p.shape[1:], p.dtype) for p in pairs), {},
                  (n, N_CHIP - 1), start, finish)


def _half_share(bufs, name):
    n = len(bufs)

    def body(*refs):
        outs = refs[n:2 * n]
        send_sems, recv_sems = refs[2 * n:]
        x, y, c = _coords()
        sibling = (x, y, 1 - c)

        def copy(a, pc):
            h = bufs[a].shape[0] // 2
            rows = outs[a].at[pl.ds(pc * h, h), :]
            return pltpu.make_async_remote_copy(
                src_ref=rows, dst_ref=rows, send_sem=send_sems.at[a], recv_sem=recv_sems.at[a], device_id=sibling,
                device_id_type=MESH_ID)

        for a in range(n):
            copy(a, c).start()
        for a in range(n):
            copy(a, c).wait_send()
            copy(a, 1 - c).wait_recv()

    any_spec = pl.BlockSpec(memory_space=pl.ANY)
    return pl.pallas_call(
        body, name=name, out_shape=[jax.ShapeDtypeStruct(b.shape, b.dtype) for b in bufs],
        in_specs=[any_spec] * n, out_specs=[any_spec] * n, input_output_aliases={a: a for a in range(n)},
        scratch_shapes=[pltpu.SemaphoreType.DMA((n,)), pltpu.SemaphoreType.DMA((n,))],
    )(*bufs)


def _pair_sums(grads, idx, tag):
    names = list(grads)
    full = [grads[k] for k in names]
    recv = _pair_exchange(full, "rs_pair_exchange_" + tag)
    return {k: _pair_add(f, r, idx[0], "rs_pair_add_" + k) for k, f, r in zip(names, full, recv)}


PACK_ALIGN = SUBLANE * LANE
PACK_ROWS_ALIGN = 256 * LANE


def _pack(parts):
    flat, offs, off = [], [], 0
    for p in parts:
        v = p.reshape(-1).astype(F32)
        n = v.shape[0]
        padded = -(-n // PACK_ALIGN) * PACK_ALIGN
        flat.append(jnp.pad(v, (0, padded - n)))
        offs.append((off, n))
        off += padded
    tail = -off % PACK_ROWS_ALIGN
    if tail:
        flat.append(jnp.zeros((tail,), F32))
    return jnp.concatenate(flat).reshape(-1, LANE), offs


def _unpack(flat, offs, shapes):
    return [flat[o:o + n].reshape(s) for (o, n), s in zip(offs, shapes)]


def _chipcat(g, per_dev_len, offs, shape, axis):
    o, n = offs
    parts = [g[2 * j, o:o + n].reshape(shape) for j in range(N_CHIP)]
    return jnp.concatenate(parts, axis=axis)


def kernel(x, c, w_ada, b_ada, ln_g, ln_b, a_w_in, a_w_dw, a_b_dw, a_norm_g, a_norm_b, a_w_out, b_w_in, b_q_norm_g, b_w_qb, b_w_out, kv_w_a, kv_norm_g, kv_w_b, loss_target, m_w_ada, m_b_ada, m_ln_g, m_ln_b, m_a_w_in, m_a_w_dw, m_a_b_dw, m_a_norm_g, m_a_norm_b, m_a_w_out, m_b_w_in, m_b_q_norm_g, m_b_w_qb, m_b_w_out, m_kv_w_a, m_kv_norm_g, m_kv_w_b, v_w_ada, v_b_ada, v_ln_g, v_ln_b, v_a_w_in, v_a_w_dw, v_a_b_dw, v_a_norm_g, v_a_norm_b, v_a_w_out, v_b_w_in, v_b_q_norm_g, v_b_w_qb, v_b_w_out, v_kv_w_a, v_kv_norm_g, v_kv_w_b):
    xi, yi, ci = _coords()
    chip = 2 * xi + yi
    dev = 4 * xi + 2 * yi + ci
    idx = (jnp.reshape(ci, (1,)).astype(jnp.int32), jnp.reshape(chip, (1,)).astype(jnp.int32))

    x2 = x[0]
    tgt = loss_target[0]
    S, D = x2.shape
    C = a_w_out.shape[1] * N_CHIP
    Cq = C // N_CHIP
    KS = a_w_dw.shape[1]
    Q = b_q_norm_g.shape[1]
    KV = kv_norm_g.shape[0]
    Hq = kv_w_b.shape[1] // HEAD_PAD
    H = Hq * N_CHIP
    W = H * V_HEAD_DIM
    Nq = w_ada.shape[2]
    head_q = QK_NOPE_DIM + QK_ROPE_DIM
    scale = head_q ** -0.5
    tabs = _rope_tables(S)

    qb_pad = jnp.pad(b_w_qb[0].reshape(Q, Hq, head_q), ((0, 0), (0, 0), (0, HEAD_PAD - head_q))).reshape(Q, Hq * HEAD_PAD)
    kva_pad = jnp.pad(kv_w_a, ((0, 0), (0, LANE - QK_ROPE_DIM)))
    shards = {"a_w_in": a_w_in[0], "a_w_out": a_w_out[0], "b_w_in": b_w_in[0], "b_w_qb": qb_pad, "b_w_out": b_w_out[0],
              "kv_w_a": kva_pad, "kv_w_b": kv_w_b}
    placed = {k: _cast_place(w, idx, "cast_" + k) for k, w in shards.items()}
    (W_ain,) = _exchange_alone(_allgather_carry([placed["a_w_in"]]), "allgather_a_w_in")

    pack1, offs1 = _pack([c[0], a_w_dw[0], a_b_dw[0], a_norm_g[0], a_norm_b[0]])
    L1 = pack1.shape[0] * LANE
    g1 = _allgather_small(pack1, "allgather_small_in").reshape(N_DEV, L1)
    c_all = g1[:, :D]
    w_dw = _chipcat(g1, L1, offs1[1], (KS, Cq), 1)
    b_dw = _chipcat(g1, L1, offs1[2], (1, Cq), 1)
    g_cn = _chipcat(g1, L1, offs1[3], (1, Cq), 1)
    b_cn = _chipcat(g1, L1, offs1[4], (1, Cq), 1)

    b_ada_sh = lax.dynamic_slice_in_dim(b_ada, chip * Nq, Nq, axis=1)[:, None, :]
    mod_sh = _mod(c_all, w_ada, b_ada_sh, "adaln_mod")
    gm = _allgather_small(mod_sh.reshape(DEPTH * N_DEV, Nq), "allgather_small_mod").reshape(N_CHIP, 2, DEPTH, N_DEV, Nq)
    mod_rows = lax.dynamic_index_in_dim(gm[:, 0], dev, axis=2, keepdims=False)
    mod_me = jnp.transpose(mod_rows, (1, 0, 2)).reshape(DEPTH, N_CHIP * Nq)
    shift = [mod_me[l:l + 1, 0:D] for l in range(DEPTH)]
    scl = [mod_me[l:l + 1, D:2 * D] for l in range(DEPTH)]
    gate = [mod_me[l:l + 1, 2 * D:3 * D] for l in range(DEPTH)]

    h0 = _lnmod_fwd(x2, scl[0], shift[0], "a_lnmod_fwd")
    u0, (W_aout, W_bin) = _mm(h0, W_ain, b_sh=True, name="a_in_fwd",
                              carry=_allgather_carry([placed["a_w_out"], placed["b_w_in"]]))
    v2, (W_kva, W_kvb) = _conv1_fwd(u0, w_dw, b_dw, "a_conv1_fwd",
                                    carry=_allgather_carry([placed["kv_w_a"], placed["kv_w_b"]]))
    W_aout = W_aout.reshape(C, D)
    W_kva = W_kva.reshape(D, KV + LANE)
    p0 = _conv2_fwd(v2, u0, g_cn, b_cn, "a_conv2_fwd")
    out0 = _mm(p0, W_aout, name="a_out_fwd")
    x1 = _resln_fwd(x2, out0, gate[0], ln_g[0:1], ln_b[0:1], "a_resln_fwd")

    kva = _mm(x1, W_kva, name="kv_a_fwd")
    ckv, krp = _kvprep_fwd(kva, kv_norm_g[None, :], tabs, "kv_prep_fwd")
    kvh = _mm(ckv, W_kvb, b_sh=True, out_dtype=BF16, name="kv_b_fwd")

    h1 = _lnmod_fwd(x1, scl[1], shift[1], "b_lnmod_fwd")
    u1, (W_qb, W_bout) = _mm(h1, W_bin, b_sh=True, name="b_in_fwd",
                             carry=_allgather_carry([placed["b_w_qb"], placed["b_w_out"]]))
    W_bout = W_bout.reshape(W, D)
    qn = _rms_fwd(u1, Q, b_q_norm_g, "b_qnorm_fwd")
    qraw = _mm(qn, W_qb, b_sh=True, name="b_qb_fwd")
    o, p1, qr, lse = _attn_fwd(qraw, kvh, krp, u1, Q, tabs, scale, "b_attn_fwd")
    out1 = _mm(p1, W_bout, name="b_out_fwd")

    dxa1, dout1, dgate1, dlng1, dlnb1, loss_part = _resln_bwd(
        x1, out1, gate[1], ln_g[1:2], ln_b[1:2], tgt, True, "b_resln_bwd")
    dW_bout = _mm(p1, dout1, ta=True, out_dtype=BF16, name="b_out_dw").reshape(N_CHIP, W // N_CHIP, D)
    pairs = _pair_sums({"b_w_out": dW_bout}, idx, "b_w_out")
    got = {}
    dp1, (got["b_w_out"],) = _mm(dout1, W_bout, tb=True, name="b_out_dx", carry=_chip_exchange_carry([pairs["b_w_out"]]))
    dqr, dkvh, dkr_h, du1 = _attn_bwd(qr, kvh, krp, dp1, o, u1, Q, lse, scale, "b_attn_bwd")
    dqraw = _qrope_bwd(dqr, tabs, "b_qrope_bwd")
    dW_qb = _mm(qn, dqraw, ta=True, o_sh=True, out_dtype=BF16, name="b_qb_dw")
    dqn = _mm(dqraw, W_qb, tb=True, b_sh=True, name="b_qb_dx")
    du1, dgq = _rms_bwd(u1, Q, dqn, b_q_norm_g, du1, "b_qnorm_bwd")
    dW_bin = _mm(h1, du1, ta=True, o_sh=True, out_dtype=BF16, name="b_in_dw")
    dh1 = _mm(du1, W_bin, tb=True, b_sh=True, name="b_in_dx")

    dW_kvb = _mm(ckv, dkvh, ta=True, o_sh=True, out_dtype=BF16, name="kv_b_dw")
    dckv = _mm(dkvh, W_kvb, tb=True, b_sh=True, name="kv_b_dx")
    dkva, dgkv = _kvprep_bwd(kva, dckv, dkr_h, kv_norm_g[None, :], tabs, "kv_prep_bwd")
    dW_kva = _mm(x1, dkva, ta=True, out_dtype=BF16, name="kv_a_dw").reshape(N_CHIP, D // N_CHIP, KV + LANE)
    dx1_kv = _mm(dkva, W_kva, tb=True, name="kv_a_dx")
    dx1, dsc1, dsh1 = _lnmod_bwd(x1, dh1, scl[1], [dxa1, dx1_kv], "b_lnmod_bwd")

    dxa0, dout0, dgate0, dlng0, dlnb0 = _resln_bwd(x2, out0, gate[0], ln_g[0:1], ln_b[0:1], dx1, False, "a_resln_bwd")
    dW_aout = _mm(p0, dout0, ta=True, out_dtype=BF16, name="a_out_dw").reshape(N_CHIP, Cq, D)
    dp0 = _mm(dout0, W_aout, tb=True, name="a_out_dx")
    dv2, dz0, dgcn, dbcn = _conv2_bwd(dp0, v2, u0, g_cn, b_cn, "a_conv2_bwd")
    mid = ["b_w_qb", "b_w_in", "kv_w_b", "kv_w_a"]
    pairs.update(_pair_sums({"b_w_qb": dW_qb, "b_w_in": dW_bin, "kv_w_b": dW_kvb, "kv_w_a": dW_kva}, idx, "mla"))
    (da0, dg0, dwdw, dbdw), got_mid = _conv1_bwd(dv2, u0, w_dw, "a_conv1_bwd",
                                                 carry=_chip_exchange_carry([pairs[k] for k in mid]))
    got.update(zip(mid, got_mid))
    du0 = jnp.concatenate([da0, dg0, dz0], axis=1)
    pairs.update(_pair_sums({"a_w_out": dW_aout}, idx, "a_w_out"))
    dW_ain, (got["a_w_out"],) = _mm(h0, du0, ta=True, o_sh=True, out_dtype=BF16, name="a_in_dw",
                                    carry=_chip_exchange_carry([pairs["a_w_out"]]))
    pairs.update(_pair_sums({"a_w_in": dW_ain}, idx, "a_w_in"))
    dh0, (got["a_w_in"],) = _mm(du0, W_ain, tb=True, b_sh=True, name="a_in_dx",
                                carry=_chip_exchange_carry([pairs["a_w_in"]]))
    dx, dsc0, dsh0 = _lnmod_bwd(x2, dh0, scl[0], [dxa0], "a_lnmod_bwd")
    grad_x = dx[None]

    dmod = jnp.concatenate([dsh0, dsc0, dgate0, dsh1, dsc1, dgate1], axis=1).reshape(DEPTH, 3 * D)
    small = [loss_part, dmod, jnp.concatenate([dlng0, dlng1], 0), jnp.concatenate([dlnb0, dlnb1], 0),
             dwdw, dbdw, dgcn, dbcn, dgq, dgkv]
    small_shapes = [p.shape for p in small]
    pack2, offs2 = _pack(small)
    R2 = pack2.shape[0]
    g2 = _allgather_small(pack2, "allgather_small_grads").reshape(N_DEV, R2, LANE)
    tot = _sum_leading(g2, "small_grad_sum").reshape(-1)
    (loss_t, g_b_ada, g_ln_g, g_ln_b, g_wdw_full, g_bdw_full, g_gcn_full, g_bcn_full, g_gq, g_gkv) = _unpack(
        tot, offs2, small_shapes)
    loss = loss_t.reshape(())
    colsl = lambda a: lax.dynamic_slice_in_dim(a, chip * Cq, Cq, axis=1)
    g_wdw, g_bdw, g_gcn, g_bcn = colsl(g_wdw_full), colsl(g_bdw_full), colsl(g_gcn_full), colsl(g_bcn_full)

    dmod_all = jnp.stack([g2[d].reshape(-1)[offs2[1][0]:offs2[1][0] + offs2[1][1]].reshape(DEPTH, 3 * D)
                          for d in range(N_DEV)], axis=1)
    dmod_sh = lax.dynamic_slice_in_dim(dmod_all, chip * Nq, Nq, axis=2)
    g_w_ada = _wada_grad(jnp.transpose(c_all), dmod_sh, "w_ada_grad")

    mats = ["a_w_in", "a_w_out", "b_w_in", "b_w_qb", "b_w_out", "kv_w_a", "kv_w_b"]
    halves = [_chip_sum(pairs[k], got[k], idx, "rs_chip_sum_" + k) for k in mats]
    red = dict(zip(mats, _half_share(halves, "rs_half_share")))
    g_a_w_in = red["a_w_in"]
    g_a_w_out = red["a_w_out"]
    g_b_w_in = red["b_w_in"]
    g_b_w_qb = red["b_w_qb"].reshape(Q, Hq, HEAD_PAD)[:, :, :head_q].reshape(Q, Hq * head_q)
    g_b_w_out = red["b_w_out"]
    g_kv_w_a = red["kv_w_a"][:, :KV + QK_ROPE_DIM]
    g_kv_w_b = red["kv_w_b"]

    grads = {
        "w_ada": g_w_ada, "b_ada": g_b_ada, "ln_g": g_ln_g, "ln_b": g_ln_b, "a_w_in": g_a_w_in[None],
        "a_w_dw": g_wdw[None], "a_b_dw": g_bdw, "a_norm_g": g_gcn, "a_norm_b": g_bcn, "a_w_out": g_a_w_out[None],
        "b_w_in": g_b_w_in[None], "b_q_norm_g": g_gq, "b_w_qb": g_b_w_qb[None], "b_w_out": g_b_w_out[None],
        "kv_w_a": g_kv_w_a, "kv_norm_g": g_gkv.reshape(KV), "kv_w_b": g_kv_w_b,
    }
    weights = {
        "w_ada": (w_ada, m_w_ada, v_w_ada), "b_ada": (b_ada, m_b_ada, v_b_ada), "ln_g": (ln_g, m_ln_g, v_ln_g),
        "ln_b": (ln_b, m_ln_b, v_ln_b), "a_w_in": (a_w_in, m_a_w_in, v_a_w_in), "a_w_dw": (a_w_dw, m_a_w_dw, v_a_w_dw),
        "a_b_dw": (a_b_dw, m_a_b_dw, v_a_b_dw), "a_norm_g": (a_norm_g, m_a_norm_g, v_a_norm_g),
        "a_norm_b": (a_norm_b, m_a_norm_b, v_a_norm_b), "a_w_out": (a_w_out, m_a_w_out, v_a_w_out),
        "b_w_in": (b_w_in, m_b_w_in, v_b_w_in), "b_q_norm_g": (b_q_norm_g, m_b_q_norm_g, v_b_q_norm_g),
        "b_w_qb": (b_w_qb, m_b_w_qb, v_b_w_qb), "b_w_out": (b_w_out, m_b_w_out, v_b_w_out),
        "kv_w_a": (kv_w_a, m_kv_w_a, v_kv_w_a), "kv_norm_g": (kv_norm_g, m_kv_norm_g, v_kv_norm_g),
        "kv_w_b": (kv_w_b, m_kv_w_b, v_kv_w_b),
    }
    order = list(weights)
    big = [k for k in order if weights[k][0].size >= (1 << 16) and weights[k][0].shape[-1] % LANE == 0]
    small_names = [k for k in order if k not in big]
    upd = {}
    for k in big:
        w, m, v = weights[k]
        shp = w.shape
        two = (-1, shp[-1])
        d_, m_, v_ = _adamw(w.reshape(two), grads[k].reshape(two), m.reshape(two), v.reshape(two), "adamw_" + k)
        upd[k] = (grads[k].reshape(shp), d_.reshape(shp), m_.reshape(shp), v_.reshape(shp))
    sw, offs3 = _pack([weights[k][0] for k in small_names])
    sg, _ = _pack([grads[k] for k in small_names])
    sm, _ = _pack([weights[k][1] for k in small_names])
    sv, _ = _pack([weights[k][2] for k in small_names])
    sd, snm, snv = _adamw(sw, sg, sm, sv, "adamw_small")
    shapes3 = [weights[k][0].shape for k in small_names]
    for k, d_, m_, v_ in zip(small_names, _unpack(sd.reshape(-1), offs3, shapes3), _unpack(snm.reshape(-1), offs3, shapes3),
                             _unpack(snv.reshape(-1), offs3, shapes3)):
        upd[k] = (grads[k].reshape(weights[k][0].shape), d_, m_, v_)

    return (loss, grad_x, *[upd[k][0] for k in order], *[upd[k][1] for k in order], *[upd[k][2] for k in order],
            *[upd[k][3] for k in order])
```

```python
import math
from typing import Callable, NamedTuple

import jax
import jax.numpy as jnp
from jax import lax
from jax.experimental import pallas as pl
from jax.experimental.pallas import tpu as pltpu

F32 = jnp.float32
BF16 = jnp.bfloat16

LN_EPS = 1e-5
RMS_EPS = 1e-6
DEPTH = 2
DEEPNORM_ALPHA = (2.0 * DEPTH) ** 0.25
QK_NOPE_DIM = 128
QK_ROPE_DIM = 64
V_HEAD_DIM = 128
HEAD_PAD = 256
ROPE_BASE = 10000.0
ADAM_LR = 0.001
ADAM_B1 = 0.9
ADAM_B2 = 0.999
ADAM_EPS = 1e-08
ADAM_WD = 0.01
ADAM_STEP = 10

N_DEV = 8
N_CHIP = 4
LANE = 128
SUBLANE = 8
VMEM_LIMIT = 48 * 1024 * 1024
CONV_HALO = 32
MESH_ID = pl.DeviceIdType.MESH
NT = (((1,), (1,)), ((), ()))


def _cp(*sem):
    return pltpu.CompilerParams(dimension_semantics=sem, vmem_limit_bytes=VMEM_LIMIT)


def _tile(n, pref, align):
    if n <= pref:
        return n
    t = (pref // align) * align
    while t > align and n % t:
        t -= align
    assert n % t == 0, (n, pref, align)
    return t


def _silu(v):
    return v * jax.nn.sigmoid(v)


def _dsilu(v):
    s = jax.nn.sigmoid(v)
    return s * (1.0 + v * (1.0 - s))


class _Carry(NamedTuple):
    bufs: tuple
    out_shapes: tuple
    aliases: dict
    sem_shape: tuple
    start: Callable
    finish: Callable


def _pcall(body, *, name, grid, in_specs, out_specs, out_shape, scratch_shapes, sem, args, carry=None):
    if carry is None:
        return pl.pallas_call(body, name=name, grid=grid, in_specs=in_specs, out_specs=out_specs, out_shape=out_shape,
                              scratch_shapes=scratch_shapes, compiler_params=_cp(*sem))(*args)
    n_in, n_out, n_sc = len(in_specs), len(out_specs), len(scratch_shapes)
    nc_in, nc_out = len(carry.bufs), len(carry.out_shapes)

    def wrapped(*refs):
        core_in, c_in = refs[:n_in], refs[n_in:n_in + nc_in]
        core_out = refs[n_in + nc_in:n_in + nc_in + n_out]
        c_out = refs[n_in + nc_in + n_out:n_in + nc_in + n_out + nc_out]
        core_sc = refs[n_in + nc_in + n_out + nc_out:n_in + nc_in + n_out + nc_out + n_sc]
        send_sems, recv_sems = refs[-2:]
        first = pl.program_id(0) == 0
        last = pl.program_id(0) == grid[0] - 1
        for d in range(1, len(grid)):
            first = jnp.logical_and(first, pl.program_id(d) == 0)
            last = jnp.logical_and(last, pl.program_id(d) == grid[d] - 1)

        @pl.when(first)
        def _():
            carry.start(c_in, c_out, send_sems, recv_sems)

        body(*core_in, *core_out, *core_sc)

        @pl.when(last)
        def _():
            carry.finish(c_in, c_out, send_sems, recv_sems)

    any_spec = pl.BlockSpec(memory_space=pl.ANY)
    res = pl.pallas_call(
        wrapped, name=name, grid=grid, in_specs=list(in_specs) + [any_spec] * nc_in,
        out_specs=list(out_specs) + [any_spec] * nc_out, out_shape=list(out_shape) + list(carry.out_shapes),
        input_output_aliases={n_in + i: n_out + o for i, o in carry.aliases.items()},
        scratch_shapes=list(scratch_shapes) + [pltpu.SemaphoreType.DMA(carry.sem_shape), pltpu.SemaphoreType.DMA(carry.sem_shape)],
        compiler_params=_cp(*(("arbitrary",) * len(grid))),
    )(*args, *carry.bufs)
    return res[:n_out], res[n_out:]


def _mm(a, b, *, name, ta=False, tb=False, b_sh=False, o_sh=False, out_dtype=F32, tm=1024, tn=1024, tk=2048, carry=None):
    M, K = (a.shape[1], a.shape[0]) if ta else a.shape
    if b_sh:
        assert b.shape[0] == N_CHIP
        nq = b.shape[2]
        Kb, N = (nq * N_CHIP, b.shape[1]) if tb else (b.shape[1], nq * N_CHIP)
    else:
        Kb, N = (b.shape[1], b.shape[0]) if tb else b.shape
        nq = N // N_CHIP
    assert K == Kb, (a.shape, b.shape)
    tm = _tile(M, tm, LANE)
    tk = _tile(nq if (b_sh and tb) else K, tk, LANE)
    tn = _tile(nq if ((b_sh and not tb) or o_sh) else N, tn, LANE)
    nk = K // tk

    def body(a_ref, b_ref, o_ref, acc_ref):
        k = pl.program_id(2)

        @pl.when(k == 0)
        def _():
            acc_ref[...] = jnp.zeros_like(acc_ref)

        dn = (((0 if ta else 1,), (1 if tb else 0,)), ((), ()))
        acc_ref[...] += lax.dot_general(a_ref[...].astype(BF16), b_ref[...].astype(BF16), dn,
                                        preferred_element_type=F32)

        @pl.when(k == nk - 1)
        def _():
            o_ref[...] = acc_ref[...].astype(o_ref.dtype)

    a_spec = pl.BlockSpec((tk, tm), lambda i, j, k: (k, i)) if ta else pl.BlockSpec((tm, tk), lambda i, j, k: (i, k))
    if b_sh and not tb:
        per = nq // tn
        b_spec = pl.BlockSpec((None, tk, tn), lambda i, j, k: (j // per, k, j % per))
    elif b_sh and tb:
        per = nq // tk
        b_spec = pl.BlockSpec((None, tn, tk), lambda i, j, k: (k // per, j, k % per))
    elif tb:
        b_spec = pl.BlockSpec((tn, tk), lambda i, j, k: (j, k))
    else:
        b_spec = pl.BlockSpec((tk, tn), lambda i, j, k: (k, j))
    if o_sh:
        per_o = nq // tn
        o_spec = pl.BlockSpec((None, tm, tn), lambda i, j, k: (j // per_o, i, j % per_o))
        o_shape = jax.ShapeDtypeStruct((N_CHIP, M, nq), out_dtype)
    else:
        o_spec = pl.BlockSpec((tm, tn), lambda i, j, k: (i, j))
        o_shape = jax.ShapeDtypeStruct((M, N), out_dtype)
    res = _pcall(body, name=name, grid=(M // tm, N // tn, nk), in_specs=[a_spec, b_spec], out_specs=[o_spec],
                 out_shape=[o_shape], scratch_shapes=[pltpu.VMEM((tm, tn), F32)], sem=("parallel", "parallel", "arbitrary"),
                 args=(a, b), carry=carry)
    return res[0] if carry is None else (res[0][0], res[1])


def _mod(c_all, w_ada, b_sh, name):
    L, D, nq = w_ada.shape
    B = c_all.shape[0]
    tn = _tile(nq, 512, LANE)

    def body(c_ref, w_ref, b_ref, o_ref):
        sc = _silu(c_ref[...]).astype(BF16)
        o_ref[...] = jnp.dot(sc, w_ref[...].astype(BF16), preferred_element_type=F32) + b_ref[...]

    return pl.pallas_call(
        body, name=name, grid=(L, nq // tn),
        in_specs=[pl.BlockSpec((B, D), lambda l, j: (0, 0)), pl.BlockSpec((None, D, tn), lambda l, j: (l, 0, j)),
                  pl.BlockSpec((None, 1, tn), lambda l, j: (l, 0, j))],
        out_specs=pl.BlockSpec((None, B, tn), lambda l, j: (l, 0, j)),
        out_shape=jax.ShapeDtypeStruct((L, B, nq), F32), compiler_params=_cp("parallel", "parallel"),
    )(c_all, w_ada, b_sh)


def _wada_grad(c_all_t, dmod, name):
    D, B = c_all_t.shape
    L, _, nq = dmod.shape
    tm = _tile(D, 512, SUBLANE)
    tn = _tile(nq, 1024, LANE)

    def body(c_ref, d_ref, o_ref):
        sc = _silu(c_ref[...])
        dm = d_ref[...]
        acc = sc[:, 0:1] * dm[0:1, :]
        for b in range(1, B):
            acc = acc + sc[:, b:b + 1] * dm[b:b + 1, :]
        o_ref[...] = acc

    return pl.pallas_call(
        body, name=name, grid=(L, D // tm, nq // tn),
        in_specs=[pl.BlockSpec((tm, B), lambda l, i, j: (i, 0)), pl.BlockSpec((None, B, tn), lambda l, i, j: (l, 0, j))],
        out_specs=pl.BlockSpec((None, tm, tn), lambda l, i, j: (l, i, j)),
        out_shape=jax.ShapeDtypeStruct((L, D, nq), F32), compiler_params=_cp("parallel", "parallel", "parallel"),
    )(c_all_t, dmod)


ROW_TILE = 128


def _ln_stats(v):
    mu = jnp.mean(v, axis=-1, keepdims=True)
    vc = v - mu
    var = jnp.mean(vc * vc, axis=-1, keepdims=True)
    rstd = lax.rsqrt(var + LN_EPS)
    return vc * rstd, rstd


def _ln_bwd(dxhat, xhat, rstd):
    m1 = jnp.mean(dxhat, axis=-1, keepdims=True)
    m2 = jnp.mean(dxhat * xhat, axis=-1, keepdims=True)
    return rstd * (dxhat - m1 - xhat * m2)


def _row_spec(ts, D):
    return pl.BlockSpec((ts, D), lambda i: (i, 0))


def _vec_spec(D):
    return pl.BlockSpec((1, D), lambda i: (0, 0))


def _lnmod_fwd(x, scale, shift, name):
    S, D = x.shape
    ts = _tile(S, ROW_TILE, SUBLANE)

    def body(x_ref, sc_ref, sh_ref, h_ref):
        xn, _ = _ln_stats(x_ref[...])
        h_ref[...] = (xn * (1.0 + sc_ref[...]) + sh_ref[...]).astype(h_ref.dtype)

    return pl.pallas_call(
        body, name=name, grid=(S // ts,), in_specs=[_row_spec(ts, D), _vec_spec(D), _vec_spec(D)],
        out_specs=_row_spec(ts, D), out_shape=jax.ShapeDtypeStruct((S, D), BF16), compiler_params=_cp("parallel"),
    )(x, scale, shift)


def _lnmod_bwd(x, dh, scale, adds, name):
    S, D = x.shape
    ts = _tile(S, ROW_TILE, SUBLANE)
    na = len(adds)

    def body(*refs):
        x_ref, dh_ref, sc_ref = refs[:3]
        add_refs = refs[3:3 + na]
        dx_ref, dsc_ref, dsh_ref = refs[3 + na:]
        i = pl.program_id(0)

        @pl.when(i == 0)
        def _():
            dsc_ref[...] = jnp.zeros_like(dsc_ref)
            dsh_ref[...] = jnp.zeros_like(dsh_ref)

        xn, rstd = _ln_stats(x_ref[...])
        dh = dh_ref[...].astype(F32)
        dx = _ln_bwd(dh * (1.0 + sc_ref[...]), xn, rstd)
        for r in add_refs:
            dx = dx + r[...]
        dx_ref[...] = dx
        dsc_ref[...] += jnp.sum(dh * xn, axis=0, keepdims=True)
        dsh_ref[...] += jnp.sum(dh, axis=0, keepdims=True)

    return pl.pallas_call(
        body, name=name, grid=(S // ts,),
        in_specs=[_row_spec(ts, D), _row_spec(ts, D), _vec_spec(D)] + [_row_spec(ts, D)] * na,
        out_specs=[_row_spec(ts, D), _vec_spec(D), _vec_spec(D)],
        out_shape=[jax.ShapeDtypeStruct((S, D), F32), jax.ShapeDtypeStruct((1, D), F32), jax.ShapeDtypeStruct((1, D), F32)],
        compiler_params=_cp("arbitrary"),
    )(x, dh, scale, *adds)


def _resln_fwd(x, out, gate, g, b, name):
    S, D = x.shape
    ts = _tile(S, ROW_TILE, SUBLANE)

    def body(x_ref, o_ref, gt_ref, g_ref, b_ref, y_ref):
        r = DEEPNORM_ALPHA * x_ref[...] + (1.0 + gt_ref[...]) * o_ref[...]
        xhat, _ = _ln_stats(r)
        y_ref[...] = xhat * g_ref[...] + b_ref[...]

    return pl.pallas_call(
        body, name=name, grid=(S // ts,),
        in_specs=[_row_spec(ts, D), _row_spec(ts, D), _vec_spec(D), _vec_spec(D), _vec_spec(D)],
        out_specs=_row_spec(ts, D), out_shape=jax.ShapeDtypeStruct((S, D), F32), compiler_params=_cp("parallel"),
    )(x, out, gate, g, b)


def _resln_bwd(x, out, gate, g, b, dy_or_target, from_target, name):
    S, D = x.shape
    ts = _tile(S, ROW_TILE, SUBLANE)

    def body(x_ref, o_ref, gt_ref, g_ref, b_ref, t_ref, dxa_ref, dout_ref, dgt_ref, dg_ref, db_ref, *maybe_loss):
        i = pl.program_id(0)

        @pl.when(i == 0)
        def _():
            dgt_ref[...] = jnp.zeros_like(dgt_ref)
            dg_ref[...] = jnp.zeros_like(dg_ref)
            db_ref[...] = jnp.zeros_like(db_ref)
            if from_target:
                maybe_loss[0][...] = jnp.zeros_like(maybe_loss[0])

        ov = o_ref[...]
        g1 = 1.0 + gt_ref[...]
        r = DEEPNORM_ALPHA * x_ref[...] + g1 * ov
        xhat, rstd = _ln_stats(r)
        if from_target:
            err = xhat * g_ref[...] + b_ref[...] - t_ref[...]
            dy = err * (1.0 / D)
            maybe_loss[0][...] += jnp.sum(jnp.sum(err * err, axis=-1, keepdims=True), axis=0, keepdims=True) * (0.5 / D)
        else:
            dy = t_ref[...]
        dr = _ln_bwd(dy * g_ref[...], xhat, rstd)
        dxa_ref[...] = DEEPNORM_ALPHA * dr
        dout_ref[...] = (dr * g1).astype(dout_ref.dtype)
        dgt_ref[...] += jnp.sum(dr * ov, axis=0, keepdims=True)
        dg_ref[...] += jnp.sum(dy * xhat, axis=0, keepdims=True)
        db_ref[...] += jnp.sum(dy, axis=0, keepdims=True)

    vec = jax.ShapeDtypeStruct((1, D), F32)
    out_specs = [_row_spec(ts, D), _row_spec(ts, D), _vec_spec(D), _vec_spec(D), _vec_spec(D)]
    out_shape = [jax.ShapeDtypeStruct((S, D), F32), jax.ShapeDtypeStruct((S, D), BF16), vec, vec, vec]
    if from_target:
        out_specs.append(pl.BlockSpec((1, 1), lambda i: (0, 0)))
        out_shape.append(jax.ShapeDtypeStruct((1, 1), F32))
    return pl.pallas_call(
        body, name=name, grid=(S // ts,),
        in_specs=[_row_spec(ts, D), _row_spec(ts, D), _vec_spec(D), _vec_spec(D), _vec_spec(D), _row_spec(ts, D)],
        out_specs=out_specs, out_shape=out_shape, compiler_params=_cp("arbitrary"),
    )(x, out, gate, g, b, dy_or_target)


def _conv_tiles(S, C):
    tt = _tile(S, 256, CONV_HALO)
    tc = _tile(C, 512, LANE)
    return tt, tc


def _shift_scratch(tt, tc):
    return pltpu.VMEM((SUBLANE - 1, tt + CONV_HALO - SUBLANE, tc), F32)


def _fill_shifts(src_ref, sh_ref, tt):
    rows = tt + CONV_HALO - SUBLANE
    for b in range(1, SUBLANE):
        sh_ref[b - 1] = src_ref[b:b + rows, :]


def _shifted(src_ref, sh_ref, q, tt):
    a8, b8 = divmod(q, SUBLANE)
    if b8 == 0:
        return src_ref[q:q + tt, :]
    return sh_ref[b8 - 1, a8 * SUBLANE:a8 * SUBLANE + tt, :]


def _conv1_fwd(u, w_dw, b_dw, name, carry=None):
    S, C3 = u.shape
    C = C3 // 3
    KS = w_dw.shape[0]
    tt, tc = _conv_tiles(S, C)
    ncb = C // tc
    hb = tt // CONV_HALO
    lead = CONV_HALO - (KS - 1)

    def body(a_ref, g_ref, ah_ref, gh_ref, w_ref, b_ref, o_ref, pad_ref, sh_ref):
        i = pl.program_id(0)
        halo = ah_ref[...] * jax.nn.sigmoid(gh_ref[...])
        pad_ref[0:CONV_HALO, :] = jnp.where(i > 0, halo, 0.0)
        pad_ref[CONV_HALO:, :] = a_ref[...] * jax.nn.sigmoid(g_ref[...])
        _fill_shifts(pad_ref, sh_ref, tt)
        acc = jnp.broadcast_to(b_ref[...], (tt, tc))
        for k in range(KS):
            acc = acc + w_ref[k:k + 1, :] * _shifted(pad_ref, sh_ref, lead + k, tt)
        o_ref[...] = acc

    main = lambda off: pl.BlockSpec((tt, tc), lambda i, j: (i, off + j))
    halo = lambda off: pl.BlockSpec((CONV_HALO, tc), lambda i, j: (jnp.maximum(i * hb - 1, 0), off + j))
    res = _pcall(
        body, name=name, grid=(S // tt, ncb),
        in_specs=[main(0), main(ncb), halo(0), halo(ncb), pl.BlockSpec((KS, tc), lambda i, j: (0, j)),
                  pl.BlockSpec((1, tc), lambda i, j: (0, j))],
        out_specs=[pl.BlockSpec((tt, tc), lambda i, j: (i, j))], out_shape=[jax.ShapeDtypeStruct((S, C), F32)],
        scratch_shapes=[pltpu.VMEM((CONV_HALO + tt, tc), F32), _shift_scratch(tt, tc)], sem=("parallel", "parallel"),
        args=(u, u, u, u, w_dw, b_dw), carry=carry)
    return res[0] if carry is None else (res[0][0], res[1])


def _conv2_fwd(v2, u, g_cn, b_cn, name):
    S, C = v2.shape
    ts = _tile(S, ROW_TILE, SUBLANE)

    def body(v_ref, z_ref, g_ref, b_ref, p_ref):
        xhat, _ = _ln_stats(v_ref[...])
        v3 = xhat * g_ref[...] + b_ref[...]
        p_ref[...] = (_silu(v3) * _silu(z_ref[...])).astype(p_ref.dtype)

    return pl.pallas_call(
        body, name=name, grid=(S // ts,),
        in_specs=[_row_spec(ts, C), pl.BlockSpec((ts, C), lambda i: (i, 2)), _vec_spec(C), _vec_spec(C)],
        out_specs=_row_spec(ts, C), out_shape=jax.ShapeDtypeStruct((S, C), BF16), compiler_params=_cp("parallel"),
    )(v2, u, g_cn, b_cn)


def _conv2_bwd(dp, v2, u, g_cn, b_cn, name):
    S, C = v2.shape
    ts = _tile(S, ROW_TILE, SUBLANE)

    def body(dp_ref, v_ref, z_ref, g_ref, b_ref, dv_ref, dz_ref, dg_ref, db_ref):
        i = pl.program_id(0)

        @pl.when(i == 0)
        def _():
            dg_ref[...] = jnp.zeros_like(dg_ref)
            db_ref[...] = jnp.zeros_like(db_ref)

        dp = dp_ref[...].astype(F32)
        z = z_ref[...]
        xhat, rstd = _ln_stats(v_ref[...])
        v3 = xhat * g_ref[...] + b_ref[...]
        dz_ref[...] = (dp * _silu(v3) * _dsilu(z)).astype(dz_ref.dtype)
        dv3 = dp * _silu(z) * _dsilu(v3)
        dv_ref[...] = _ln_bwd(dv3 * g_ref[...], xhat, rstd)
        dg_ref[...] += jnp.sum(dv3 * xhat, axis=0, keepdims=True)
        db_ref[...] += jnp.sum(dv3, axis=0, keepdims=True)

    vec = jax.ShapeDtypeStruct((1, C), F32)
    return pl.pallas_call(
        body, name=name, grid=(S // ts,),
        in_specs=[_row_spec(ts, C), _row_spec(ts, C), pl.BlockSpec((ts, C), lambda i: (i, 2)), _vec_spec(C), _vec_spec(C)],
        out_specs=[_row_spec(ts, C), _row_spec(ts, C), _vec_spec(C), _vec_spec(C)],
        out_shape=[jax.ShapeDtypeStruct((S, C), F32), jax.ShapeDtypeStruct((S, C), BF16), vec, vec],
        compiler_params=_cp("arbitrary"),
    )(dp, v2, u, g_cn, b_cn)


def _conv1_bwd(dv2, u, w_dw, name, carry=None):
    S, C = dv2.shape
    KS = w_dw.shape[0]
    tt, tc = _conv_tiles(S, C)
    ncb = C // tc
    nt = S // tt
    hb = tt // CONV_HALO
    lead = CONV_HALO - (KS - 1)

    def body(dv_ref, dvh_ref, a_ref, g_ref, ah_ref, gh_ref, w_ref, da_ref, dg_ref, dw_ref, db_ref, pad_ref, fpad_ref,
             sh_ref, fsh_ref):
        i = pl.program_id(1)

        @pl.when(i == 0)
        def _():
            dw_ref[...] = jnp.zeros_like(dw_ref)
            db_ref[...] = jnp.zeros_like(db_ref)

        dv = dv_ref[...]
        a = a_ref[...]
        sg = jax.nn.sigmoid(g_ref[...])
        halo = ah_ref[...] * jax.nn.sigmoid(gh_ref[...])
        pad_ref[0:CONV_HALO, :] = jnp.where(i > 0, halo, 0.0)
        pad_ref[CONV_HALO:, :] = a * sg
        fpad_ref[0:tt, :] = dv
        fpad_ref[tt:, :] = jnp.where(i < nt - 1, dvh_ref[...], 0.0)
        _fill_shifts(pad_ref, sh_ref, tt)
        _fill_shifts(fpad_ref, fsh_ref, tt)
        dv1 = jnp.zeros((tt, tc), F32)
        for k in range(KS):
            dv1 = dv1 + w_ref[k:k + 1, :] * _shifted(fpad_ref, fsh_ref, KS - 1 - k, tt)
            dw_ref[k:k + 1, :] += jnp.sum(dv * _shifted(pad_ref, sh_ref, lead + k, tt), axis=0, keepdims=True)
        db_ref[...] += jnp.sum(dv, axis=0, keepdims=True)
        da_ref[...] = (dv1 * sg).astype(da_ref.dtype)
        dg_ref[...] = (dv1 * a * sg * (1.0 - sg)).astype(dg_ref.dtype)

    main = lambda off: pl.BlockSpec((tt, tc), lambda j, i: (i, off + j))
    halo = lambda off: pl.BlockSpec((CONV_HALO, tc), lambda j, i: (jnp.maximum(i * hb - 1, 0), off + j))
    fhalo = pl.BlockSpec((CONV_HALO, tc), lambda j, i: (jnp.minimum((i + 1) * hb, nt * hb - 1), j))
    res = _pcall(
        body, name=name, grid=(ncb, nt),
        in_specs=[main(0), fhalo, main(0), main(ncb), halo(0), halo(ncb), pl.BlockSpec((KS, tc), lambda j, i: (0, j))],
        out_specs=[main(0), main(0), pl.BlockSpec((KS, tc), lambda j, i: (0, j)), pl.BlockSpec((1, tc), lambda j, i: (0, j))],
        out_shape=[jax.ShapeDtypeStruct((S, C), BF16), jax.ShapeDtypeStruct((S, C), BF16),
                   jax.ShapeDtypeStruct((KS, C), F32), jax.ShapeDtypeStruct((1, C), F32)],
        scratch_shapes=[pltpu.VMEM((CONV_HALO + tt, tc), F32), pltpu.VMEM((tt + CONV_HALO, tc), F32),
                        _shift_scratch(tt, tc), _shift_scratch(tt, tc)],
        sem=("parallel", "arbitrary"), args=(dv2, dv2, u, u, u, u, w_dw), carry=carry)
    return res


def _rope_tables(S):
    half = QK_ROPE_DIM // 2
    inv_freq = ROPE_BASE ** (-jnp.arange(half, dtype=F32) / half)
    ang = jnp.arange(S, dtype=jnp.int32).astype(F32)[:, None] * inv_freq[None, :]
    cos, sin, z = jnp.cos(ang), jnp.sin(ang), jnp.zeros((S, half), F32)
    tc = jnp.concatenate([cos, cos, z, z], axis=1)
    t1 = jnp.concatenate([-sin, z, z, z], axis=1)
    t2 = jnp.concatenate([z, sin, z, z], axis=1)
    return tc, t1, t2


def _rope128(r, tc, t1, t2, sign):
    return r * tc + sign * (pltpu.roll(r, LANE - QK_ROPE_DIM // 2, 1) * t1 + pltpu.roll(r, QK_ROPE_DIM // 2, 1) * t2)


def _rms_fwd(x, width, g, name):
    S = x.shape[0]
    ts = _tile(S, 256, SUBLANE)

    def body(x_ref, g_ref, o_ref):
        xv = x_ref[...]
        rr = lax.rsqrt(jnp.mean(xv * xv, axis=-1, keepdims=True) + RMS_EPS)
        o_ref[...] = (xv * rr * g_ref[...]).astype(o_ref.dtype)

    return pl.pallas_call(
        body, name=name, grid=(S // ts,), in_specs=[_row_spec(ts, width), _vec_spec(width)],
        out_specs=_row_spec(ts, width), out_shape=jax.ShapeDtypeStruct((S, width), BF16), compiler_params=_cp("parallel"),
    )(x, g)


def _rms_bwd_math(xv, dy, g):
    n = xv.shape[-1]
    rr = lax.rsqrt(jnp.mean(xv * xv, axis=-1, keepdims=True) + RMS_EPS)
    dyg = dy * g
    dx = rr * dyg - xv * (rr * rr * rr) * (jnp.sum(dyg * xv, axis=-1, keepdims=True) * (1.0 / n))
    dg = jnp.sum(dy * xv * rr, axis=0, keepdims=True)
    return dx, dg


def _rms_bwd(x, width, dy, g, du, name):
    S = x.shape[0]
    ts = _tile(S, 256, SUBLANE)

    def body(x_ref, dy_ref, g_ref, du_in, dx_ref, dg_ref):
        del du_in
        i = pl.program_id(0)

        @pl.when(i == 0)
        def _():
            dg_ref[...] = jnp.zeros_like(dg_ref)

        dx, dg = _rms_bwd_math(x_ref[...], dy_ref[...].astype(F32), g_ref[...])
        dx_ref[...] = dx.astype(dx_ref.dtype)
        dg_ref[...] += dg

    return pl.pallas_call(
        body, name=name, grid=(S // ts,),
        in_specs=[_row_spec(ts, width), _row_spec(ts, width), _vec_spec(width), pl.BlockSpec(memory_space=pl.ANY)],
        out_specs=[_row_spec(ts, width), _vec_spec(width)],
        out_shape=[jax.ShapeDtypeStruct(du.shape, du.dtype), jax.ShapeDtypeStruct((1, width), F32)],
        input_output_aliases={3: 0}, compiler_params=_cp("arbitrary"),
    )(x, dy, g, du)


def _kvprep_fwd(kva, g_kv, tabs, name):
    S, W = kva.shape
    KV = W - LANE
    ts = _tile(S, 256, SUBLANE)

    def body(x_ref, g_ref, tc_ref, t1_ref, t2_ref, c_ref, r_ref):
        xv = x_ref[:, 0:KV]
        rr = lax.rsqrt(jnp.mean(xv * xv, axis=-1, keepdims=True) + RMS_EPS)
        c_ref[...] = (xv * rr * g_ref[...]).astype(c_ref.dtype)
        r_ref[...] = _rope128(x_ref[:, KV:], tc_ref[...], t1_ref[...], t2_ref[...], 1.0).astype(r_ref.dtype)

    tab = _row_spec(ts, LANE)
    return pl.pallas_call(
        body, name=name, grid=(S // ts,), in_specs=[_row_spec(ts, W), _vec_spec(KV), tab, tab, tab],
        out_specs=[_row_spec(ts, KV), _row_spec(ts, LANE)],
        out_shape=[jax.ShapeDtypeStruct((S, KV), BF16), jax.ShapeDtypeStruct((S, LANE), BF16)], compiler_params=_cp("parallel"),
    )(kva, g_kv, *tabs)


def _kvprep_bwd(kva, dckv, dkr_h, g_kv, tabs, name):
    S, W = kva.shape
    KV = W - LANE
    H = dkr_h.shape[1] // LANE
    ts = _tile(S, 256, SUBLANE)

    def body(x_ref, dc_ref, dr_ref, g_ref, tc_ref, t1_ref, t2_ref, o_ref, dg_ref):
        i = pl.program_id(0)

        @pl.when(i == 0)
        def _():
            dg_ref[...] = jnp.zeros_like(dg_ref)

        dx, dg = _rms_bwd_math(x_ref[:, 0:KV], dc_ref[...].astype(F32), g_ref[...])
        o_ref[:, 0:KV] = dx.astype(o_ref.dtype)
        dg_ref[...] += dg
        dr = dr_ref[:, 0:LANE]
        for h in range(1, H):
            dr = dr + dr_ref[:, h * LANE:(h + 1) * LANE]
        o_ref[:, KV:] = _rope128(dr, tc_ref[...], t1_ref[...], t2_ref[...], -1.0).astype(o_ref.dtype)

    tab = _row_spec(ts, LANE)
    return pl.pallas_call(
        body, name=name, grid=(S // ts,),
        in_specs=[_row_spec(ts, W), _row_spec(ts, KV), _row_spec(ts, H * LANE), _vec_spec(KV), tab, tab, tab],
        out_specs=[_row_spec(ts, W), _vec_spec(KV)],
        out_shape=[jax.ShapeDtypeStruct((S, W), BF16), jax.ShapeDtypeStruct((1, KV), F32)], compiler_params=_cp("arbitrary"),
    )(kva, dckv, dkr_h, g_kv, *tabs)


ROPE_GROUP = 8


def _qrope_bwd(dq, tabs, name):
    S, W = dq.shape
    H = W // HEAD_PAD
    G = math.gcd(H, ROPE_GROUP)
    ts = _tile(S, 512, 2 * SUBLANE)

    def body(q_ref, tc_ref, t1_ref, t2_ref, o_ref):
        for g in range(G):
            lo = g * HEAD_PAD
            o_ref[:, lo:lo + LANE] = q_ref[:, lo:lo + LANE].astype(o_ref.dtype)
            o_ref[:, lo + LANE:lo + HEAD_PAD] = _rope128(
                q_ref[:, lo + LANE:lo + HEAD_PAD], tc_ref[...], t1_ref[...], t2_ref[...], -1.0).astype(o_ref.dtype)

    tab = pl.BlockSpec((ts, LANE), lambda i, h: (i, 0))
    blk = pl.BlockSpec((ts, G * HEAD_PAD), lambda i, h: (i, h))
    return pl.pallas_call(
        body, name=name, grid=(S // ts, H // G), in_specs=[blk, tab, tab, tab], out_specs=blk,
        out_shape=jax.ShapeDtypeStruct((S, W), BF16), compiler_params=_cp("parallel", "parallel"),
    )(dq, *tabs)


ATT_TILE = 512
LOG2E = math.log2(math.e)
LN2 = math.log(2.0)
ATT_FWD_HEADS = 4


def _causal_mask(nk, nq, q0):
    r = lax.broadcasted_iota(jnp.int32, (nk, nq), 0)
    c = lax.broadcasted_iota(jnp.int32, (nk, nq), 1)
    return c + q0 >= r


def _attn_fwd(qraw, kvh, krp, u, zoff, tabs, scale, name, carry=None):
    S, W = qraw.shape
    H = W // HEAD_PAD
    G = math.gcd(H, ATT_FWD_HEADS)
    t = _tile(S, ATT_TILE, LANE)
    nq = S // t
    assert zoff % (G * V_HEAD_DIM) == 0
    zb = zoff // (G * V_HEAD_DIM)
    qc = scale * LOG2E
    t2 = t // 2

    def body(q_ref, tc_ref, t1_ref, t2_ref, kv_ref, kr_ref, z_ref, o_ref, p_ref, qr_ref, lse_ref, vt_sc, m_sc, l_sc, acc_sc):
        i = pl.program_id(1)

        @pl.when(i == 0)
        def _():
            for g in range(G):
                lo = g * HEAD_PAD + QK_NOPE_DIM
                for jj in range(nq):
                    vt_sc[g, jj] = kv_ref[jj * t:(jj + 1) * t, lo:lo + V_HEAD_DIM].astype(F32).T.astype(BF16)

        qs = []
        for g in range(G):
            lo = g * HEAD_PAD
            qrot = _rope128(q_ref[:, lo + LANE:lo + HEAD_PAD], tc_ref[...], t1_ref[...], t2_ref[...], 1.0)
            q = jnp.concatenate([(q_ref[:, lo:lo + LANE] * qc).astype(BF16), (qrot * qc).astype(BF16)], axis=-1)
            qr_ref[:, lo:lo + HEAD_PAD] = q
            qs.append(q)
        m_sc[...] = jnp.full_like(m_sc, -jnp.inf)
        l_sc[...] = jnp.zeros_like(l_sc)
        acc_sc[...] = jnp.zeros_like(acc_sc)

        def update(g, cols, st, vt):
            m_old = m_sc[g, :, cols]
            m_new = jnp.maximum(m_old, jnp.max(st, axis=0, keepdims=True))
            a = jnp.exp2(m_old - m_new)
            pt = jnp.exp2(st - m_new)
            l_sc[g, :, cols] = a * l_sc[g, :, cols] + jnp.sum(pt, axis=0, keepdims=True)
            acc_sc[g, :, cols] = a * acc_sc[g, :, cols] + jnp.dot(vt, pt.astype(BF16), preferred_element_type=F32)
            m_sc[g, :, cols] = m_new

        def keys(g, off):
            return jnp.concatenate([kv_ref[pl.ds(off, t), g * HEAD_PAD:g * HEAD_PAD + QK_NOPE_DIM],
                                    kr_ref[pl.ds(off, t), :]], axis=-1)

        def loop_body(j, carry):
            off = pl.multiple_of(j * t, t)
            for g in range(G):
                st = lax.dot_general(keys(g, off), qs[g], NT, preferred_element_type=F32)
                update(g, slice(0, t), st, vt_sc[g, j])
            return carry

        lax.fori_loop(0, i, loop_body, 0)
        off = pl.multiple_of(i * t, t)
        for g in range(G):
            k = keys(g, off)
            vt = vt_sc[g, i]
            st0 = lax.dot_general(k[0:t2], qs[g][0:t2], NT, preferred_element_type=F32)
            update(g, slice(0, t2), jnp.where(_causal_mask(t2, t2, 0), st0, -jnp.inf), vt[:, 0:t2])
            st1 = lax.dot_general(k, qs[g][t2:], NT, preferred_element_type=F32)
            update(g, slice(t2, t), jnp.where(_causal_mask(t, t2, t2), st1, -jnp.inf), vt)
        for g in range(G):
            ov = (acc_sc[g] / l_sc[g]).T
            cols = slice(g * V_HEAD_DIM, (g + 1) * V_HEAD_DIM)
            o_ref[:, cols] = ov
            p_ref[:, cols] = (ov * _silu(z_ref[:, cols])).astype(p_ref.dtype)
            lse_ref[g] = m_sc[g] + jnp.log2(l_sc[g])

    tab = pl.BlockSpec((t, LANE), lambda h, i: (i, 0))
    head = pl.BlockSpec((t, G * V_HEAD_DIM), lambda h, i: (i, h))
    return _pcall(
        body, name=name, grid=(H // G, nq),
        in_specs=[pl.BlockSpec((t, G * HEAD_PAD), lambda h, i: (i, h)), tab, tab, tab,
                  pl.BlockSpec((S, G * HEAD_PAD), lambda h, i: (0, h)), pl.BlockSpec((S, LANE), lambda h, i: (0, 0)),
                  pl.BlockSpec((t, G * V_HEAD_DIM), lambda h, i: (i, zb + h))],
        out_specs=[head, head, pl.BlockSpec((t, G * HEAD_PAD), lambda h, i: (i, h)),
                   pl.BlockSpec((G, None, 1, t), lambda h, i: (h, i, 0, 0))],
        out_shape=[jax.ShapeDtypeStruct((S, H * V_HEAD_DIM), F32), jax.ShapeDtypeStruct((S, H * V_HEAD_DIM), BF16),
                   jax.ShapeDtypeStruct((S, W), BF16), jax.ShapeDtypeStruct((H, nq, 1, t), F32)],
        scratch_shapes=[pltpu.VMEM((G, nq, V_HEAD_DIM, t), BF16), pltpu.VMEM((G, 1, t), F32), pltpu.VMEM((G, 1, t), F32),
                        pltpu.VMEM((G, V_HEAD_DIM, t), F32)],
        sem=("parallel", "arbitrary"), args=(qraw, *tabs, kvh, krp, u), carry=carry)


def _attn_bwd(qr, kvh, krp, dp, o, u, zoff, lse, scale, name):
    S, W = qr.shape
    H = W // HEAD_PAD
    U = u.shape[1]
    t = _tile(S, ATT_TILE, LANE)
    nq = S // t
    zb = zoff // V_HEAD_DIM

    def body(q_ref, kv_ref, kr_ref, dp_ref, o_ref, z_ref, lse_ref, dq_ref, dkv_ref, dkr_ref, dz_ref,
             do_sc, dl_sc, dqt_sc, dk_sc, dv_sc):
        j = pl.program_id(1)

        @pl.when(j == 0)
        def _():
            for ii in range(nq):
                rows = slice(ii * t, (ii + 1) * t)
                dpv, ov, z = dp_ref[rows, :], o_ref[rows, :], z_ref[rows, :]
                dov = dpv * _silu(z)
                do_sc[ii] = dov.astype(BF16)
                dz_ref[rows, :] = (dpv * ov * _dsilu(z)).astype(dz_ref.dtype)
                dl_sc[ii] = jnp.sum((dov * ov).T, axis=0, keepdims=True)
                dqt_sc[ii] = jnp.zeros((HEAD_PAD, t), F32)

        kvb = kv_ref[...]
        k = jnp.concatenate([kvb[:, 0:QK_NOPE_DIM], kr_ref[...]], axis=-1)
        v = kvb[:, QK_NOPE_DIM:]
        kt = k.astype(F32).T.astype(BF16)
        t2 = t // 2
        dk_sc[...] = jnp.zeros_like(dk_sc)
        dv_sc[...] = jnp.zeros_like(dv_sc)

        def part(i, krows, qcols, mask):
            nkr = krows.stop - krows.start
            q = q_ref[pl.ds(pl.multiple_of(i * t + qcols.start, t2), qcols.stop - qcols.start), :]
            dov = do_sc[i, qcols, :]
            st = lax.dot_general(k[krows], q, NT, preferred_element_type=F32)
            pt = jnp.exp2(st - lse_ref[i, :, qcols])
            if mask is not None:
                pt = jnp.where(mask, pt, 0.0)
            dv_sc[krows, :] += jnp.dot(pt.astype(BF16), dov, preferred_element_type=F32)
            dpt = lax.dot_general(v[krows], dov, NT, preferred_element_type=F32)
            dst = (pt * (dpt - dl_sc[i, :, qcols])).astype(BF16)
            dk_sc[krows, :] += jnp.dot(dst, q, preferred_element_type=F32)
            dqt_sc[i, :, qcols] += jnp.dot(kt[:, krows] if nkr < t else kt, dst, preferred_element_type=F32)

        def loop_body(i, carry):
            part(i, slice(0, t), slice(0, t), None)
            return carry

        part(j, slice(0, t2), slice(0, t2), _causal_mask(t2, t2, 0))
        part(j, slice(0, t), slice(t2, t), _causal_mask(t, t2, t2))
        lax.fori_loop(j + 1, nq, loop_body, 0)
        dkv_ref[:, 0:QK_NOPE_DIM] = (dk_sc[:, 0:QK_NOPE_DIM] * LN2).astype(dkv_ref.dtype)
        dkv_ref[:, QK_NOPE_DIM:] = dv_sc[...].astype(dkv_ref.dtype)
        dkr_ref[...] = dk_sc[:, QK_NOPE_DIM:] * LN2

        @pl.when(j == nq - 1)
        def _():
            for ii in range(nq):
                dq_ref[ii * t:(ii + 1) * t, :] = dqt_sc[ii].T * scale

    whole = lambda w, off: pl.BlockSpec((S, w), lambda h, j: (0, off + h))
    return pl.pallas_call(
        body, name=name, grid=(H, nq),
        in_specs=[whole(HEAD_PAD, 0), pl.BlockSpec((t, HEAD_PAD), lambda h, j: (j, h)),
                  pl.BlockSpec((t, LANE), lambda h, j: (j, 0)), whole(V_HEAD_DIM, 0), whole(V_HEAD_DIM, 0),
                  whole(V_HEAD_DIM, zb), pl.BlockSpec((None, nq, 1, t), lambda h, j: (h, 0, 0, 0))],
        out_specs=[whole(HEAD_PAD, 0), pl.BlockSpec((t, HEAD_PAD), lambda h, j: (j, h)),
                   pl.BlockSpec((t, LANE), lambda h, j: (j, h)), whole(V_HEAD_DIM, zb)],
        out_shape=[jax.ShapeDtypeStruct((S, W), F32), jax.ShapeDtypeStruct((S, W), BF16),
                   jax.ShapeDtypeStruct((S, H * LANE), F32), jax.ShapeDtypeStruct((S, U), BF16)],
        scratch_shapes=[pltpu.VMEM((nq, t, V_HEAD_DIM), BF16), pltpu.VMEM((nq, 1, t), F32),
                        pltpu.VMEM((nq, HEAD_PAD, t), F32), pltpu.VMEM((t, HEAD_PAD), F32), pltpu.VMEM((t, V_HEAD_DIM), F32)],
        compiler_params=_cp("parallel", "arbitrary"),
    )(qr, kvh, krp, dp, o, u, lse)


def _adamw_math(w, g, m, v):
    m = ADAM_B1 * m + (1.0 - ADAM_B1) * g
    v = ADAM_B2 * v + (1.0 - ADAM_B2) * (g * g)
    m_hat = m / (1.0 - ADAM_B1 ** ADAM_STEP)
    v_hat = v / (1.0 - ADAM_B2 ** ADAM_STEP)
    delta = -ADAM_LR * (m_hat / (jnp.sqrt(v_hat) + ADAM_EPS) + ADAM_WD * w)
    return delta, m, v


def _adamw(w, g, m, v, name):
    R, C = w.shape
    tr = _tile(R, 256, SUBLANE)
    tc = _tile(C, 1024, LANE)

    def body(w_ref, g_ref, m_ref, v_ref, d_ref, nm_ref, nv_ref):
        d, nm, nv = _adamw_math(w_ref[...], g_ref[...], m_ref[...], v_ref[...])
        d_ref[...] = d
        nm_ref[...] = nm
        nv_ref[...] = nv

    blk = pl.BlockSpec((tr, tc), lambda i, j: (i, j))
    sh = jax.ShapeDtypeStruct((R, C), F32)
    return pl.pallas_call(
        body, name=name, grid=(R // tr, C // tc), in_specs=[blk] * 4, out_specs=[blk] * 3, out_shape=[sh] * 3,
        compiler_params=_cp("parallel", "parallel"),
    )(w, g, m, v)


def _sum_leading(x, name):
    n, R, C = x.shape
    tr = _tile(R, 512, SUBLANE)

    def body(x_ref, o_ref):
        acc = x_ref[0]
        for k in range(1, n):
            acc = acc + x_ref[k]
        o_ref[...] = acc

    return pl.pallas_call(
        body, name=name, grid=(R // tr,), in_specs=[pl.BlockSpec((n, tr, C), lambda i: (0, i, 0))],
        out_specs=pl.BlockSpec((tr, C), lambda i: (i, 0)), out_shape=jax.ShapeDtypeStruct((R, C), F32),
        compiler_params=_cp("parallel"),
    )(x)


def _pair_add(full, recv, c_idx, name):
    n, R, C = full.shape
    h = R // 2
    tr = _tile(h, 256, 2 * SUBLANE)
    tc = _tile(C, 1024, LANE)
    nb = h // tr

    def body(c_ref, a_ref, b_ref, o_ref):
        del c_ref
        o_ref[...] = (a_ref[...].astype(F32) + b_ref[...].astype(F32)).astype(o_ref.dtype)

    return pl.pallas_call(
        body, name=name,
        grid_spec=pltpu.PrefetchScalarGridSpec(
            num_scalar_prefetch=1, grid=(n, nb, C // tc),
            in_specs=[pl.BlockSpec((None, tr, tc), lambda k, i, j, c: (k, c[0] * nb + i, j)),
                      pl.BlockSpec((None, tr, tc), lambda k, i, j, c: (k, i, j))],
            out_specs=pl.BlockSpec((None, tr, tc), lambda k, i, j, c: (k, i, j))),
        out_shape=jax.ShapeDtypeStruct((n, h, C), BF16), compiler_params=_cp("parallel", "parallel", "parallel"),
    )(c_idx, full, recv)


def _chip_sum(pair, recv, idx, name):
    n, h, C = pair.shape
    tr = _tile(h, 256, 2 * SUBLANE)
    tc = _tile(C, 1024, LANE)
    nb = h // tr

    def body(c_ref, chip_ref, a_ref, b_ref, o_ref):
        del c_ref, chip_ref
        acc = a_ref[...].astype(F32)
        for k in range(N_CHIP - 1):
            acc = acc + b_ref[k].astype(F32)
        o_ref[...] = acc

    return pl.pallas_call(
        body, name=name,
        grid_spec=pltpu.PrefetchScalarGridSpec(
            num_scalar_prefetch=2, grid=(nb, C // tc),
            in_specs=[pl.BlockSpec((None, tr, tc), lambda i, j, c, chip: (chip[0], i, j)),
                      pl.BlockSpec((N_CHIP - 1, tr, tc), lambda i, j, c, chip: (0, i, j))],
            out_specs=pl.BlockSpec((tr, tc), lambda i, j, c, chip: (c[0] * nb + i, j))),
        out_shape=jax.ShapeDtypeStruct((2 * h, C), F32), compiler_params=_cp("parallel", "parallel"),
    )(idx[0], idx[1], pair, recv)


def _cast_place(w, idx, name):
    R, C = w.shape
    h = R // 2
    tr = _tile(h, 256, 2 * SUBLANE)
    tc = _tile(C, 1024, LANE)
    nb = h // tr

    def body(c_ref, chip_ref, w_ref, o_ref):
        del c_ref, chip_ref
        o_ref[...] = w_ref[...].astype(o_ref.dtype)

    return pl.pallas_call(
        body, name=name,
        grid_spec=pltpu.PrefetchScalarGridSpec(
            num_scalar_prefetch=2, grid=(nb, C // tc),
            in_specs=[pl.BlockSpec((tr, tc), lambda i, j, c, chip: (c[0] * nb + i, j))],
            out_specs=pl.BlockSpec((None, tr, tc), lambda i, j, c, chip: (chip[0], c[0] * nb + i, j))),
        out_shape=jax.ShapeDtypeStruct((N_CHIP, R, C), BF16), compiler_params=_cp("parallel", "parallel"),
    )(idx[0], idx[1], w)


def _coords():
    return lax.axis_index("x"), lax.axis_index("y"), lax.axis_index("c")


def _allgather_small(x_shard, name):
    m_per, n = x_shard.shape

    def body(x_ref, out_ref, send_sems, recv_sems, local_sem):
        x, y, c = _coords()
        me, sibling = (x, y, c), (x, y, 1 - c)
        chips = [(1 - x, y), (x, 1 - y), (1 - x, 1 - y)]

        def rows(px, py, pc):
            return out_ref.at[pl.ds((4 * px + 2 * py + pc) * m_per, m_per), :]

        def copy(k, block, to, src=None):
            return pltpu.make_async_remote_copy(
                src_ref=rows(*block) if src is None else src, dst_ref=rows(*block), send_sem=send_sems.at[k],
                recv_sem=recv_sems.at[k], device_id=to, device_id_type=MESH_ID)

        mine = pltpu.make_async_copy(x_ref, rows(*me), local_sem)
        mine.start()
        first = [copy(0, me, sibling, src=x_ref)]
        first += [copy(1 + j, me, (*chip, c), src=x_ref) for j, chip in enumerate(chips)]
        for cp in first:
            cp.start()
        passed = [copy(4 + j, (*chip, c), sibling) for j, chip in enumerate(chips)]
        for j, chip in enumerate(chips):
            copy(1 + j, (*chip, c), me).wait_recv()
            passed[j].start()
        copy(0, sibling, me).wait_recv()
        for j, chip in enumerate(chips):
            copy(4 + j, (*chip, 1 - c), me).wait_recv()
        for cp in first + passed:
            cp.wait_send()
        mine.wait()

    return pl.pallas_call(
        body, name=name, out_shape=jax.ShapeDtypeStruct((N_DEV * m_per, n), x_shard.dtype),
        in_specs=[pl.BlockSpec(memory_space=pltpu.VMEM)], out_specs=pl.BlockSpec(memory_space=pltpu.VMEM),
        scratch_shapes=[pltpu.SemaphoreType.DMA((7,)), pltpu.SemaphoreType.DMA((7,)), pltpu.SemaphoreType.DMA],
        compiler_params=pltpu.CompilerParams(vmem_limit_bytes=VMEM_LIMIT),
    )(x_shard)


def _allgather_carry(bufs):
    n = len(bufs)

    def plan(outs, send_sems, recv_sems):
        x, y, c = _coords()
        me, sibling = (x, y, c), (x, y, 1 - c)
        chips = [(1 - x, y), (x, 1 - y), (1 - x, 1 - y)]

        def win(a, px, py, pc):
            h = bufs[a].shape[1] // 2
            return outs[a].at[2 * px + py, pl.ds(pc * h, h), :]

        def copy(a, k, block, to):
            return pltpu.make_async_remote_copy(
                src_ref=win(a, *block), dst_ref=win(a, *block), send_sem=send_sems.at[a, k],
                recv_sem=recv_sems.at[a, k], device_id=to, device_id_type=MESH_ID)

        return c, me, sibling, chips, copy

    def start(_, outs, send_sems, recv_sems):
        c, me, sibling, chips, copy = plan(outs, send_sems, recv_sems)
        for a in range(n):
            copy(a, 0, me, sibling).start()
            for j, chip in enumerate(chips):
                copy(a, 1 + j, me, (*chip, c)).start()

    def finish(_, outs, send_sems, recv_sems):
        c, me, sibling, chips, copy = plan(outs, send_sems, recv_sems)
        for a in range(n):
            for j, chip in enumerate(chips):
                copy(a, 1 + j, (*chip, c), me).wait_recv()
                copy(a, 4 + j, (*chip, c), sibling).start()
        for a in range(n):
            copy(a, 0, sibling, me).wait_recv()
            for j, chip in enumerate(chips):
                copy(a, 4 + j, (*chip, 1 - c), me).wait_recv()
        for a in range(n):
            copy(a, 0, me, sibling).wait_send()
            for j, chip in enumerate(chips):
                copy(a, 1 + j, me, (*chip, c)).wait_send()
                copy(a, 4 + j, (*chip, c), sibling).wait_send()

    return _Carry(tuple(bufs), tuple(jax.ShapeDtypeStruct(b.shape, b.dtype) for b in bufs), {a: a for a in range(n)},
                  (n, 7), start, finish)


def _exchange_alone(carry, name):
    n_in, n_out = len(carry.bufs), len(carry.out_shapes)

    def body(*refs):
        ins, outs = refs[:n_in], refs[n_in:n_in + n_out]
        send_sems, recv_sems = refs[n_in + n_out:]
        carry.start(ins, outs, send_sems, recv_sems)
        carry.finish(ins, outs, send_sems, recv_sems)

    any_spec = pl.BlockSpec(memory_space=pl.ANY)
    return pl.pallas_call(
        body, name=name, out_shape=list(carry.out_shapes), in_specs=[any_spec] * n_in, out_specs=[any_spec] * n_out,
        input_output_aliases=dict(carry.aliases),
        scratch_shapes=[pltpu.SemaphoreType.DMA(carry.sem_shape), pltpu.SemaphoreType.DMA(carry.sem_shape)],
    )(*carry.bufs)


def _pair_exchange(grads, name):
    n = len(grads)

    def body(*refs):
        ins, outs = refs[:n], refs[n:2 * n]
        send_sems, recv_sems = refs[2 * n:]
        x, y, c = _coords()
        sibling = (x, y, 1 - c)
        copies = []
        for a in range(n):
            h = grads[a].shape[1] // 2
            cp = pltpu.make_async_remote_copy(
                src_ref=ins[a].at[:, pl.ds((1 - c) * h, h), :], dst_ref=outs[a], send_sem=send_sems.at[a],
                recv_sem=recv_sems.at[a], device_id=sibling, device_id_type=MESH_ID)
            cp.start()
            copies.append(cp)
        for cp in copies:
            cp.wait()

    any_spec = pl.BlockSpec(memory_space=pl.ANY)
    return pl.pallas_call(
        body, name=name,
        out_shape=[jax.ShapeDtypeStruct((g.shape[0], g.shape[1] // 2, g.shape[2]), g.dtype) for g in grads],
        in_specs=[any_spec] * n, out_specs=[any_spec] * n,
        scratch_shapes=[pltpu.SemaphoreType.DMA((n,)), pltpu.SemaphoreType.DMA((n,))],
    )(*grads)


def _chip_exchange_carry(pairs):
    n = len(pairs)

    def copies(ins, outs, send_sems, recv_sems):
        x, y, c = _coords()
        chips = [(1 - x, y), (x, 1 - y), (1 - x, 1 - y)]
        return [pltpu.make_async_remote_copy(
            src_ref=ins[a].at[2 * px + py], dst_ref=outs[a].at[k], send_sem=send_sems.at[a, k],
            recv_sem=recv_sems.at[a, k], device_id=(px, py, c), device_id_type=MESH_ID)
            for a in range(n) for k, (px, py) in enumerate(chips)]

    def start(ins, outs, send_sems, recv_sems):
        for cp in copies(ins, outs, send_sems, recv_sems):
            cp.start()

    def finish(ins, outs, send_sems, recv_sems):
        for cp in copies(ins, outs, send_sems, recv_sems):
            cp.wait()

    return _Carry(tuple(pairs), tuple(jax.ShapeDtypeStruct((N_CHIP - 1,) + p.shape[1:], p.dtype) for p in pairs), {},
                  (n, N_CHIP - 1), start, finish)


def _half_share(bufs, name):
    n = len(bufs)

    def body(*refs):
        outs = refs[n:2 * n]
        send_sems, recv_sems = refs[2 * n:]
        x, y, c = _coords()
        sibling = (x, y, 1 - c)

        def copy(a, pc):
            h = bufs[a].shape[0] // 2
            rows = outs[a].at[pl.ds(pc * h, h), :]
            return pltpu.make_async_remote_copy(
                src_ref=rows, dst_ref=rows, send_sem=send_sems.at[a], recv_sem=recv_sems.at[a], device_id=sibling,
                device_id_type=MESH_ID)

        for a in range(n):
            copy(a, c).start()
        for a in range(n):
            copy(a, c).wait_send()
            copy(a, 1 - c).wait_recv()

    any_spec = pl.BlockSpec(memory_space=pl.ANY)
    return pl.pallas_call(
        body, name=name, out_shape=[jax.ShapeDtypeStruct(b.shape, b.dtype) for b in bufs],
        in_specs=[any_spec] * n, out_specs=[any_spec] * n, input_output_aliases={a: a for a in range(n)},
        scratch_shapes=[pltpu.SemaphoreType.DMA((n,)), pltpu.SemaphoreType.DMA((n,))],
    )(*bufs)


def _pair_sums(grads, idx, tag):
    names = list(grads)
    full = [grads[k] for k in names]
    recv = _pair_exchange(full, "rs_pair_exchange_" + tag)
    return {k: _pair_add(f, r, idx[0], "rs_pair_add_" + k) for k, f, r in zip(names, full, recv)}


PACK_ALIGN = SUBLANE * LANE
PACK_ROWS_ALIGN = 256 * LANE


def _pack(parts):
    flat, offs, off = [], [], 0
    for p in parts:
        v = p.reshape(-1).astype(F32)
        n = v.shape[0]
        padded = -(-n // PACK_ALIGN) * PACK_ALIGN
        flat.append(jnp.pad(v, (0, padded - n)))
        offs.append((off, n))
        off += padded
    tail = -off % PACK_ROWS_ALIGN
    if tail:
        flat.append(jnp.zeros((tail,), F32))
    return jnp.concatenate(flat).reshape(-1, LANE), offs


def _unpack(flat, offs, shapes):
    return [flat[o:o + n].reshape(s) for (o, n), s in zip(offs, shapes)]


def _chipcat(g, per_dev_len, offs, shape, axis):
    o, n = offs
    parts = [g[2 * j, o:o + n].reshape(shape) for j in range(N_CHIP)]
    return jnp.concatenate(parts, axis=axis)


def kernel(x, c, w_ada, b_ada, ln_g, ln_b, a_w_in, a_w_dw, a_b_dw, a_norm_g, a_norm_b, a_w_out, b_w_in, b_q_norm_g, b_w_qb, b_w_out, kv_w_a, kv_norm_g, kv_w_b, loss_target, m_w_ada, m_b_ada, m_ln_g, m_ln_b, m_a_w_in, m_a_w_dw, m_a_b_dw, m_a_norm_g, m_a_norm_b, m_a_w_out, m_b_w_in, m_b_q_norm_g, m_b_w_qb, m_b_w_out, m_kv_w_a, m_kv_norm_g, m_kv_w_b, v_w_ada, v_b_ada, v_ln_g, v_ln_b, v_a_w_in, v_a_w_dw, v_a_b_dw, v_a_norm_g, v_a_norm_b, v_a_w_out, v_b_w_in, v_b_q_norm_g, v_b_w_qb, v_b_w_out, v_kv_w_a, v_kv_norm_g, v_kv_w_b):
    xi, yi, ci = _coords()
    chip = 2 * xi + yi
    dev = 4 * xi + 2 * yi + ci
    idx = (jnp.reshape(ci, (1,)).astype(jnp.int32), jnp.reshape(chip, (1,)).astype(jnp.int32))

    x2 = x[0]
    tgt = loss_target[0]
    S, D = x2.shape
    C = a_w_out.shape[1] * N_CHIP
    Cq = C // N_CHIP
    KS = a_w_dw.shape[1]
    Q = b_q_norm_g.shape[1]
    KV = kv_norm_g.shape[0]
    Hq = kv_w_b.shape[1] // HEAD_PAD
    H = Hq * N_CHIP
    W = H * V_HEAD_DIM
    Nq = w_ada.shape[2]
    head_q = QK_NOPE_DIM + QK_ROPE_DIM
    scale = head_q ** -0.5
    tabs = _rope_tables(S)

    qb_pad = jnp.pad(b_w_qb[0].reshape(Q, Hq, head_q), ((0, 0), (0, 0), (0, HEAD_PAD - head_q))).reshape(Q, Hq * HEAD_PAD)
    kva_pad = jnp.pad(kv_w_a, ((0, 0), (0, LANE - QK_ROPE_DIM)))
    shards = {"a_w_in": a_w_in[0], "a_w_out": a_w_out[0], "b_w_in": b_w_in[0], "b_w_qb": qb_pad, "b_w_out": b_w_out[0],
              "kv_w_a": kva_pad, "kv_w_b": kv_w_b}
    placed = {k: _cast_place(w, idx, "cast_" + k) for k, w in shards.items()}
    (W_ain,) = _exchange_alone(_allgather_carry([placed["a_w_in"]]), "allgather_a_w_in")

    pack1, offs1 = _pack([c[0], a_w_dw[0], a_b_dw[0], a_norm_g[0], a_norm_b[0]])
    L1 = pack1.shape[0] * LANE
    g1 = _allgather_small(pack1, "allgather_small_in").reshape(N_DEV, L1)
    c_all = g1[:, :D]
    w_dw = _chipcat(g1, L1, offs1[1], (KS, Cq), 1)
    b_dw = _chipcat(g1, L1, offs1[2], (1, Cq), 1)
    g_cn = _chipcat(g1, L1, offs1[3], (1, Cq), 1)
    b_cn = _chipcat(g1, L1, offs1[4], (1, Cq), 1)

    b_ada_sh = lax.dynamic_slice_in_dim(b_ada, chip * Nq, Nq, axis=1)[:, None, :]
    mod_sh = _mod(c_all, w_ada, b_ada_sh, "adaln_mod")
    gm = _allgather_small(mod_sh.reshape(DEPTH * N_DEV, Nq), "allgather_small_mod").reshape(N_CHIP, 2, DEPTH, N_DEV, Nq)
    mod_rows = lax.dynamic_index_in_dim(gm[:, 0], dev, axis=2, keepdims=False)
    mod_me = jnp.transpose(mod_rows, (1, 0, 2)).reshape(DEPTH, N_CHIP * Nq)
    shift = [mod_me[l:l + 1, 0:D] for l in range(DEPTH)]
    scl = [mod_me[l:l + 1, D:2 * D] for l in range(DEPTH)]
    gate = [mod_me[l:l + 1, 2 * D:3 * D] for l in range(DEPTH)]

    h0 = _lnmod_fwd(x2, scl[0], shift[0], "a_lnmod_fwd")
    u0, (W_bin,) = _mm(h0, W_ain, b_sh=True, name="a_in_fwd", carry=_allgather_carry([placed["b_w_in"]]))
    v2, (W_aout, W_kva, W_kvb) = _conv1_fwd(
        u0, w_dw, b_dw, "a_conv1_fwd",
        carry=_allgather_carry([placed["a_w_out"], placed["kv_w_a"], placed["kv_w_b"]]))
    W_aout = W_aout.reshape(C, D)
    W_kva = W_kva.reshape(D, KV + LANE)
    p0 = _conv2_fwd(v2, u0, g_cn, b_cn, "a_conv2_fwd")
    out0 = _mm(p0, W_aout, name="a_out_fwd")
    x1 = _resln_fwd(x2, out0, gate[0], ln_g[0:1], ln_b[0:1], "a_resln_fwd")

    kva = _mm(x1, W_kva, name="kv_a_fwd")
    ckv, krp = _kvprep_fwd(kva, kv_norm_g[None, :], tabs, "kv_prep_fwd")
    kvh = _mm(ckv, W_kvb, b_sh=True, out_dtype=BF16, name="kv_b_fwd")

    h1 = _lnmod_fwd(x1, scl[1], shift[1], "b_lnmod_fwd")
    u1, (W_qb,) = _mm(h1, W_bin, b_sh=True, name="b_in_fwd", carry=_allgather_carry([placed["b_w_qb"]]))
    qn = _rms_fwd(u1, Q, b_q_norm_g, "b_qnorm_fwd")
    qraw = _mm(qn, W_qb, b_sh=True, name="b_qb_fwd")
    (o, p1, qr, lse), (W_bout,) = _attn_fwd(qraw, kvh, krp, u1, Q, tabs, scale, "b_attn_fwd",
                                            carry=_allgather_carry([placed["b_w_out"]]))
    W_bout = W_bout.reshape(W, D)
    out1 = _mm(p1, W_bout, name="b_out_fwd")

    dxa1, dout1, dgate1, dlng1, dlnb1, loss_part = _resln_bwd(
        x1, out1, gate[1], ln_g[1:2], ln_b[1:2], tgt, True, "b_resln_bwd")
    dW_bout = _mm(p1, dout1, ta=True, out_dtype=BF16, name="b_out_dw").reshape(N_CHIP, W // N_CHIP, D)
    pairs = _pair_sums({"b_w_out": dW_bout}, idx, "b_w_out")
    got = {}
    dp1, (got["b_w_out"],) = _mm(dout1, W_bout, tb=True, name="b_out_dx", carry=_chip_exchange_carry([pairs["b_w_out"]]))
    dqr, dkvh, dkr_h, du1 = _attn_bwd(qr, kvh, krp, dp1, o, u1, Q, lse, scale, "b_attn_bwd")
    dqraw = _qrope_bwd(dqr, tabs, "b_qrope_bwd")
    dW_qb = _mm(qn, dqraw, ta=True, o_sh=True, out_dtype=BF16, name="b_qb_dw")
    dqn = _mm(dqraw, W_qb, tb=True, b_sh=True, name="b_qb_dx")
    du1, dgq = _rms_bwd(u1, Q, dqn, b_q_norm_g, du1, "b_qnorm_bwd")
    dW_bin = _mm(h1, du1, ta=True, o_sh=True, out_dtype=BF16, name="b_in_dw")
    dh1 = _mm(du1, W_bin, tb=True, b_sh=True, name="b_in_dx")

    dW_kvb = _mm(ckv, dkvh, ta=True, o_sh=True, out_dtype=BF16, name="kv_b_dw")
    dckv = _mm(dkvh, W_kvb, tb=True, b_sh=True, name="kv_b_dx")
    dkva, dgkv = _kvprep_bwd(kva, dckv, dkr_h, kv_norm_g[None, :], tabs, "kv_prep_bwd")
    dW_kva = _mm(x1, dkva, ta=True, out_dtype=BF16, name="kv_a_dw").reshape(N_CHIP, D // N_CHIP, KV + LANE)
    dx1_kv = _mm(dkva, W_kva, tb=True, name="kv_a_dx")
    dx1, dsc1, dsh1 = _lnmod_bwd(x1, dh1, scl[1], [dxa1, dx1_kv], "b_lnmod_bwd")

    dxa0, dout0, dgate0, dlng0, dlnb0 = _resln_bwd(x2, out0, gate[0], ln_g[0:1], ln_b[0:1], dx1, False, "a_resln_bwd")
    dW_aout = _mm(p0, dout0, ta=True, out_dtype=BF16, name="a_out_dw").reshape(N_CHIP, Cq, D)
    dp0 = _mm(dout0, W_aout, tb=True, name="a_out_dx")
    dv2, dz0, dgcn, dbcn = _conv2_bwd(dp0, v2, u0, g_cn, b_cn, "a_conv2_bwd")
    mid = ["b_w_qb", "b_w_in", "kv_w_b", "kv_w_a"]
    pairs.update(_pair_sums({"b_w_qb": dW_qb, "b_w_in": dW_bin, "kv_w_b": dW_kvb, "kv_w_a": dW_kva}, idx, "mla"))
    (da0, dg0, dwdw, dbdw), got_mid = _conv1_bwd(dv2, u0, w_dw, "a_conv1_bwd",
                                                 carry=_chip_exchange_carry([pairs[k] for k in mid]))
    got.update(zip(mid, got_mid))
    du0 = jnp.concatenate([da0, dg0, dz0], axis=1)
    pairs.update(_pair_sums({"a_w_out": dW_aout}, idx, "a_w_out"))
    dW_ain, (got["a_w_out"],) = _mm(h0, du0, ta=True, o_sh=True, out_dtype=BF16, name="a_in_dw",
                                    carry=_chip_exchange_carry([pairs["a_w_out"]]))
    pairs.update(_pair_sums({"a_w_in": dW_ain}, idx, "a_w_in"))
    dh0, (got["a_w_in"],) = _mm(du0, W_ain, tb=True, b_sh=True, name="a_in_dx",
                                carry=_chip_exchange_carry([pairs["a_w_in"]]))
    dx, dsc0, dsh0 = _lnmod_bwd(x2, dh0, scl[0], [dxa0], "a_lnmod_bwd")
    grad_x = dx[None]

    dmod = jnp.concatenate([dsh0, dsc0, dgate0, dsh1, dsc1, dgate1], axis=1).reshape(DEPTH, 3 * D)
    small = [loss_part, dmod, jnp.concatenate([dlng0, dlng1], 0), jnp.concatenate([dlnb0, dlnb1], 0),
             dwdw, dbdw, dgcn, dbcn, dgq, dgkv]
    small_shapes = [p.shape for p in small]
    pack2, offs2 = _pack(small)
    R2 = pack2.shape[0]
    g2 = _allgather_small(pack2, "allgather_small_grads").reshape(N_DEV, R2, LANE)
    tot = _sum_leading(g2, "small_grad_sum").reshape(-1)
    (loss_t, g_b_ada, g_ln_g, g_ln_b, g_wdw_full, g_bdw_full, g_gcn_full, g_bcn_full, g_gq, g_gkv) = _unpack(
        tot, offs2, small_shapes)
    loss = loss_t.reshape(())
    colsl = lambda a: lax.dynamic_slice_in_dim(a, chip * Cq, Cq, axis=1)
    g_wdw, g_bdw, g_gcn, g_bcn = colsl(g_wdw_full), colsl(g_bdw_full), colsl(g_gcn_full), colsl(g_bcn_full)

    dmod_all = jnp.stack([g2[d].reshape(-1)[offs2[1][0]:offs2[1][0] + offs2[1][1]].reshape(DEPTH, 3 * D)
                          for d in range(N_DEV)], axis=1)
    dmod_sh = lax.dynamic_slice_in_dim(dmod_all, chip * Nq, Nq, axis=2)
    g_w_ada = _wada_grad(jnp.transpose(c_all), dmod_sh, "w_ada_grad")

    mats = ["a_w_in", "a_w_out", "b_w_in", "b_w_qb", "b_w_out", "kv_w_a", "kv_w_b"]
    halves = [_chip_sum(pairs[k], got[k], idx, "rs_chip_sum_" + k) for k in mats]
    red = dict(zip(mats, _half_share(halves, "rs_half_share")))
    g_a_w_in = red["a_w_in"]
    g_a_w_out = red["a_w_out"]
    g_b_w_in = red["b_w_in"]
    g_b_w_qb = red["b_w_qb"].reshape(Q, Hq, HEAD_PAD)[:, :, :head_q].reshape(Q, Hq * head_q)
    g_b_w_out = red["b_w_out"]
    g_kv_w_a = red["kv_w_a"][:, :KV + QK_ROPE_DIM]
    g_kv_w_b = red["kv_w_b"]

    grads = {
        "w_ada": g_w_ada, "b_ada": g_b_ada, "ln_g": g_ln_g, "ln_b": g_ln_b, "a_w_in": g_a_w_in[None],
        "a_w_dw": g_wdw[None], "a_b_dw": g_bdw, "a_norm_g": g_gcn, "a_norm_b": g_bcn, "a_w_out": g_a_w_out[None],
        "b_w_in": g_b_w_in[None], "b_q_norm_g": g_gq, "b_w_qb": g_b_w_qb[None], "b_w_out": g_b_w_out[None],
        "kv_w_a": g_kv_w_a, "kv_norm_g": g_gkv.reshape(KV), "kv_w_b": g_kv_w_b,
    }
    weights = {
        "w_ada": (w_ada, m_w_ada, v_w_ada), "b_ada": (b_ada, m_b_ada, v_b_ada), "ln_g": (ln_g, m_ln_g, v_ln_g),
        "ln_b": (ln_b, m_ln_b, v_ln_b), "a_w_in": (a_w_in, m_a_w_in, v_a_w_in), "a_w_dw": (a_w_dw, m_a_w_dw, v_a_w_dw),
        "a_b_dw": (a_b_dw, m_a_b_dw, v_a_b_dw), "a_norm_g": (a_norm_g, m_a_norm_g, v_a_norm_g),
        "a_norm_b": (a_norm_b, m_a_norm_b, v_a_norm_b), "a_w_out": (a_w_out, m_a_w_out, v_a_w_out),
        "b_w_in": (b_w_in, m_b_w_in, v_b_w_in), "b_q_norm_g": (b_q_norm_g, m_b_q_norm_g, v_b_q_norm_g),
        "b_w_qb": (b_w_qb, m_b_w_qb, v_b_w_qb), "b_w_out": (b_w_out, m_b_w_out, v_b_w_out),
        "kv_w_a": (kv_w_a, m_kv_w_a, v_kv_w_a), "kv_norm_g": (kv_norm_g, m_kv_norm_g, v_kv_norm_g),
        "kv_w_b": (kv_w_b, m_kv_w_b, v_kv_w_b),
    }
    order = list(weights)
    big = [k for k in order if weights[k][0].size >= (1 << 16) and weights[k][0].shape[-1] % LANE == 0]
    small_names = [k for k in order if k not in big]
    upd = {}
    for k in big:
        w, m, v = weights[k]
        shp = w.shape
        two = (-1, shp[-1])
        d_, m_, v_ = _adamw(w.reshape(two), grads[k].reshape(two), m.reshape(two), v.reshape(two), "adamw_" + k)
        upd[k] = (grads[k].reshape(shp), d_.reshape(shp), m_.reshape(shp), v_.reshape(shp))
    sw, offs3 = _pack([weights[k][0] for k in small_names])
    sg, _ = _pack([grads[k] for k in small_names])
    sm, _ = _pack([weights[k][1] for k in small_names])
    sv, _ = _pack([weights[k][2] for k in small_names])
    sd, snm, snv = _adamw(sw, sg, sm, sv, "adamw_small")
    shapes3 = [weights[k][0].shape for k in small_names]
    for k, d_, m_, v_ in zip(small_names, _unpack(sd.reshape(-1), offs3, shapes3), _unpack(snm.reshape(-1), offs3, shapes3),
                             _unpack(snv.reshape(-1), offs3, shapes3)):
        upd[k] = (grads[k].reshape(weights[k][0].shape), d_, m_, v_)

    return (loss, grad_x, *[upd[k][0] for k in order], *[upd[k][1] for k in order], *[upd[k][2] for k in order],
            *[upd[k][3] for k in order])
```

```python
import math
from typing import Callable, NamedTuple

import jax
import jax.numpy as jnp
from jax import lax
from jax.experimental import pallas as pl
from jax.experimental.pallas import tpu as pltpu

F32 = jnp.float32
BF16 = jnp.bfloat16

LN_EPS = 1e-5
RMS_EPS = 1e-6
DEPTH = 2
DEEPNORM_ALPHA = (2.0 * DEPTH) ** 0.25
QK_NOPE_DIM = 128
QK_ROPE_DIM = 64
V_HEAD_DIM = 128
HEAD_PAD = 256
ROPE_BASE = 10000.0
ADAM_LR = 0.001
ADAM_B1 = 0.9
ADAM_B2 = 0.999
ADAM_EPS = 1e-08
ADAM_WD = 0.01
ADAM_STEP = 10

N_DEV = 8
N_CHIP = 4
LANE = 128
SUBLANE = 8
VMEM_LIMIT = 48 * 1024 * 1024
CONV_HALO = 32
MESH_ID = pl.DeviceIdType.MESH
NT = (((1,), (1,)), ((), ()))


def _cp(*sem):
    return pltpu.CompilerParams(dimension_semantics=sem, vmem_limit_bytes=VMEM_LIMIT)


def _tile(n, pref, align):
    if n <= pref:
        return n
    t = (pref // align) * align
    while t > align and n % t:
        t -= align
    assert n % t == 0, (n, pref, align)
    return t


def _silu(v):
    return v * jax.nn.sigmoid(v)


def _dsilu(v):
    s = jax.nn.sigmoid(v)
    return s * (1.0 + v * (1.0 - s))


class _Carry(NamedTuple):
    bufs: tuple
    out_shapes: tuple
    aliases: dict
    sem_shape: tuple
    start: Callable
    finish: Callable


def _pcall(body, *, name, grid, in_specs, out_specs, out_shape, scratch_shapes, sem, args, carry=None):
    if carry is None:
        return pl.pallas_call(body, name=name, grid=grid, in_specs=in_specs, out_specs=out_specs, out_shape=out_shape,
                              scratch_shapes=scratch_shapes, compiler_params=_cp(*sem))(*args)
    n_in, n_out, n_sc = len(in_specs), len(out_specs), len(scratch_shapes)
    nc_in, nc_out = len(carry.bufs), len(carry.out_shapes)

    def wrapped(*refs):
        core_in, c_in = refs[:n_in], refs[n_in:n_in + nc_in]
        core_out = refs[n_in + nc_in:n_in + nc_in + n_out]
        c_out = refs[n_in + nc_in + n_out:n_in + nc_in + n_out + nc_out]
        core_sc = refs[n_in + nc_in + n_out + nc_out:n_in + nc_in + n_out + nc_out + n_sc]
        send_sems, recv_sems = refs[-2:]
        first = pl.program_id(0) == 0
        last = pl.program_id(0) == grid[0] - 1
        for d in range(1, len(grid)):
            first = jnp.logical_and(first, pl.program_id(d) == 0)
            last = jnp.logical_and(last, pl.program_id(d) == grid[d] - 1)

        @pl.when(first)
        def _():
            carry.start(c_in, c_out, send_sems, recv_sems)

        body(*core_in, *core_out, *core_sc)

        @pl.when(last)
        def _():
            carry.finish(c_in, c_out, send_sems, recv_sems)

    any_spec = pl.BlockSpec(memory_space=pl.ANY)
    res = pl.pallas_call(
        wrapped, name=name, grid=grid, in_specs=list(in_specs) + [any_spec] * nc_in,
        out_specs=list(out_specs) + [any_spec] * nc_out, out_shape=list(out_shape) + list(carry.out_shapes),
        input_output_aliases={n_in + i: n_out + o for i, o in carry.aliases.items()},
        scratch_shapes=list(scratch_shapes) + [pltpu.SemaphoreType.DMA(carry.sem_shape), pltpu.SemaphoreType.DMA(carry.sem_shape)],
        compiler_params=_cp(*(("arbitrary",) * len(grid))),
    )(*args, *carry.bufs)
    return res[:n_out], res[n_out:]


def _mm(a, b, *, name, ta=False, tb=False, b_sh=False, o_sh=False, out_dtype=F32, tm=1024, tn=1024, tk=2048, carry=None):
    M, K = (a.shape[1], a.shape[0]) if ta else a.shape
    if b_sh:
        assert b.shape[0] == N_CHIP
        nq = b.shape[2]
        Kb, N = (nq * N_CHIP, b.shape[1]) if tb else (b.shape[1], nq * N_CHIP)
    else:
        Kb, N = (b.shape[1], b.shape[0]) if tb else b.shape
        nq = N // N_CHIP
    assert K == Kb, (a.shape, b.shape)
    tm = _tile(M, tm, LANE)
    tk = _tile(nq if (b_sh and tb) else K, tk, LANE)
    tn = _tile(nq if ((b_sh and not tb) or o_sh) else N, tn, LANE)
    nk = K // tk

    def body(a_ref, b_ref, o_ref, acc_ref):
        k = pl.program_id(2)

        @pl.when(k == 0)
        def _():
            acc_ref[...] = jnp.zeros_like(acc_ref)

        dn = (((0 if ta else 1,), (1 if tb else 0,)), ((), ()))
        acc_ref[...] += lax.dot_general(a_ref[...].astype(BF16), b_ref[...].astype(BF16), dn,
                                        preferred_element_type=F32)

        @pl.when(k == nk - 1)
        def _():
            o_ref[...] = acc_ref[...].astype(o_ref.dtype)

    a_spec = pl.BlockSpec((tk, tm), lambda i, j, k: (k, i)) if ta else pl.BlockSpec((tm, tk), lambda i, j, k: (i, k))
    if b_sh and not tb:
        per = nq // tn
        b_spec = pl.BlockSpec((None, tk, tn), lambda i, j, k: (j // per, k, j % per))
    elif b_sh and tb:
        per = nq // tk
        b_spec = pl.BlockSpec((None, tn, tk), lambda i, j, k: (k // per, j, k % per))
    elif tb:
        b_spec = pl.BlockSpec((tn, tk), lambda i, j, k: (j, k))
    else:
        b_spec = pl.BlockSpec((tk, tn), lambda i, j, k: (k, j))
    if o_sh:
        per_o = nq // tn
        o_spec = pl.BlockSpec((None, tm, tn), lambda i, j, k: (j // per_o, i, j % per_o))
        o_shape = jax.ShapeDtypeStruct((N_CHIP, M, nq), out_dtype)
    else:
        o_spec = pl.BlockSpec((tm, tn), lambda i, j, k: (i, j))
        o_shape = jax.ShapeDtypeStruct((M, N), out_dtype)
    res = _pcall(body, name=name, grid=(M // tm, N // tn, nk), in_specs=[a_spec, b_spec], out_specs=[o_spec],
                 out_shape=[o_shape], scratch_shapes=[pltpu.VMEM((tm, tn), F32)], sem=("parallel", "parallel", "arbitrary"),
                 args=(a, b), carry=carry)
    return res[0] if carry is None else (res[0][0], res[1])


def _mod(c_all, w_ada, b_sh, name):
    L, D, nq = w_ada.shape
    B = c_all.shape[0]
    tn = _tile(nq, 512, LANE)

    def body(c_ref, w_ref, b_ref, o_ref):
        sc = _silu(c_ref[...]).astype(BF16)
        o_ref[...] = jnp.dot(sc, w_ref[...].astype(BF16), preferred_element_type=F32) + b_ref[...]

    return pl.pallas_call(
        body, name=name, grid=(L, nq // tn),
        in_specs=[pl.BlockSpec((B, D), lambda l, j: (0, 0)), pl.BlockSpec((None, D, tn), lambda l, j: (l, 0, j)),
                  pl.BlockSpec((None, 1, tn), lambda l, j: (l, 0, j))],
        out_specs=pl.BlockSpec((None, B, tn), lambda l, j: (l, 0, j)),
        out_shape=jax.ShapeDtypeStruct((L, B, nq), F32), compiler_params=_cp("parallel", "parallel"),
    )(c_all, w_ada, b_sh)


def _wada_grad(c_all_t, dmod, name):
    D, B = c_all_t.shape
    L, _, nq = dmod.shape
    tm = _tile(D, 512, SUBLANE)
    tn = _tile(nq, 1024, LANE)

    def body(c_ref, d_ref, o_ref):
        sc = _silu(c_ref[...])
        dm = d_ref[...]
        acc = sc[:, 0:1] * dm[0:1, :]
        for b in range(1, B):
            acc = acc + sc[:, b:b + 1] * dm[b:b + 1, :]
        o_ref[...] = acc

    return pl.pallas_call(
        body, name=name, grid=(L, D // tm, nq // tn),
        in_specs=[pl.BlockSpec((tm, B), lambda l, i, j: (i, 0)), pl.BlockSpec((None, B, tn), lambda l, i, j: (l, 0, j))],
        out_specs=pl.BlockSpec((None, tm, tn), lambda l, i, j: (l, i, j)),
        out_shape=jax.ShapeDtypeStruct((L, D, nq), F32), compiler_params=_cp("parallel", "parallel", "parallel"),
    )(c_all_t, dmod)


ROW_TILE = 128


def _ln_stats(v):
    mu = jnp.mean(v, axis=-1, keepdims=True)
    vc = v - mu
    var = jnp.mean(vc * vc, axis=-1, keepdims=True)
    rstd = lax.rsqrt(var + LN_EPS)
    return vc * rstd, rstd


def _ln_bwd(dxhat, xhat, rstd):
    m1 = jnp.mean(dxhat, axis=-1, keepdims=True)
    m2 = jnp.mean(dxhat * xhat, axis=-1, keepdims=True)
    return rstd * (dxhat - m1 - xhat * m2)


def _row_spec(ts, D):
    return pl.BlockSpec((ts, D), lambda i: (i, 0))


def _vec_spec(D):
    return pl.BlockSpec((1, D), lambda i: (0, 0))


def _lnmod_fwd(x, scale, shift, name):
    S, D = x.shape
    ts = _tile(S, ROW_TILE, SUBLANE)

    def body(x_ref, sc_ref, sh_ref, h_ref):
        xn, _ = _ln_stats(x_ref[...])
        h_ref[...] = (xn * (1.0 + sc_ref[...]) + sh_ref[...]).astype(h_ref.dtype)

    return pl.pallas_call(
        body, name=name, grid=(S // ts,), in_specs=[_row_spec(ts, D), _vec_spec(D), _vec_spec(D)],
        out_specs=_row_spec(ts, D), out_shape=jax.ShapeDtypeStruct((S, D), BF16), compiler_params=_cp("parallel"),
    )(x, scale, shift)


def _lnmod_bwd(x, dh, scale, adds, name):
    S, D = x.shape
    ts = _tile(S, ROW_TILE, SUBLANE)
    na = len(adds)

    def body(*refs):
        x_ref, dh_ref, sc_ref = refs[:3]
        add_refs = refs[3:3 + na]
        dx_ref, dsc_ref, dsh_ref = refs[3 + na:]
        i = pl.program_id(0)

        @pl.when(i == 0)
        def _():
            dsc_ref[...] = jnp.zeros_like(dsc_ref)
            dsh_ref[...] = jnp.zeros_like(dsh_ref)

        xn, rstd = _ln_stats(x_ref[...])
        dh = dh_ref[...].astype(F32)
        dx = _ln_bwd(dh * (1.0 + sc_ref[...]), xn, rstd)
        for r in add_refs:
            dx = dx + r[...]
        dx_ref[...] = dx
        dsc_ref[...] += jnp.sum(dh * xn, axis=0, keepdims=True)
        dsh_ref[...] += jnp.sum(dh, axis=0, keepdims=True)

    return pl.pallas_call(
        body, name=name, grid=(S // ts,),
        in_specs=[_row_spec(ts, D), _row_spec(ts, D), _vec_spec(D)] + [_row_spec(ts, D)] * na,
        out_specs=[_row_spec(ts, D), _vec_spec(D), _vec_spec(D)],
        out_shape=[jax.ShapeDtypeStruct((S, D), F32), jax.ShapeDtypeStruct((1, D), F32), jax.ShapeDtypeStruct((1, D), F32)],
        compiler_params=_cp("arbitrary"),
    )(x, dh, scale, *adds)


def _resln_fwd(x, out, gate, g, b, name):
    S, D = x.shape
    ts = _tile(S, ROW_TILE, SUBLANE)

    def body(x_ref, o_ref, gt_ref, g_ref, b_ref, y_ref):
        r = DEEPNORM_ALPHA * x_ref[...] + (1.0 + gt_ref[...]) * o_ref[...]
        xhat, _ = _ln_stats(r)
        y_ref[...] = xhat * g_ref[...] + b_ref[...]

    return pl.pallas_call(
        body, name=name, grid=(S // ts,),
        in_specs=[_row_spec(ts, D), _row_spec(ts, D), _vec_spec(D), _vec_spec(D), _vec_spec(D)],
        out_specs=_row_spec(ts, D), out_shape=jax.ShapeDtypeStruct((S, D), F32), compiler_params=_cp("parallel"),
    )(x, out, gate, g, b)


def _resln_bwd(x, out, gate, g, b, dy_or_target, from_target, name):
    S, D = x.shape
    ts = _tile(S, ROW_TILE, SUBLANE)

    def body(x_ref, o_ref, gt_ref, g_ref, b_ref, t_ref, dxa_ref, dout_ref, dgt_ref, dg_ref, db_ref, *maybe_loss):
        i = pl.program_id(0)

        @pl.when(i == 0)
        def _():
            dgt_ref[...] = jnp.zeros_like(dgt_ref)
            dg_ref[...] = jnp.zeros_like(dg_ref)
            db_ref[...] = jnp.zeros_like(db_ref)
            if from_target:
                maybe_loss[0][...] = jnp.zeros_like(maybe_loss[0])

        ov = o_ref[...]
        g1 = 1.0 + gt_ref[...]
        r = DEEPNORM_ALPHA * x_ref[...] + g1 * ov
        xhat, rstd = _ln_stats(r)
        if from_target:
            err = xhat * g_ref[...] + b_ref[...] - t_ref[...]
            dy = err * (1.0 / D)
            maybe_loss[0][...] += jnp.sum(jnp.sum(err * err, axis=-1, keepdims=True), axis=0, keepdims=True) * (0.5 / D)
        else:
            dy = t_ref[...]
        dr = _ln_bwd(dy * g_ref[...], xhat, rstd)
        dxa_ref[...] = DEEPNORM_ALPHA * dr
        dout_ref[...] = (dr * g1).astype(dout_ref.dtype)
        dgt_ref[...] += jnp.sum(dr * ov, axis=0, keepdims=True)
        dg_ref[...] += jnp.sum(dy * xhat, axis=0, keepdims=True)
        db_ref[...] += jnp.sum(dy, axis=0, keepdims=True)

    vec = jax.ShapeDtypeStruct((1, D), F32)
    out_specs = [_row_spec(ts, D), _row_spec(ts, D), _vec_spec(D), _vec_spec(D), _vec_spec(D)]
    out_shape = [jax.ShapeDtypeStruct((S, D), F32), jax.ShapeDtypeStruct((S, D), BF16), vec, vec, vec]
    if from_target:
        out_specs.append(pl.BlockSpec((1, 1), lambda i: (0, 0)))
        out_shape.append(jax.ShapeDtypeStruct((1, 1), F32))
    return pl.pallas_call(
        body, name=name, grid=(S // ts,),
        in_specs=[_row_spec(ts, D), _row_spec(ts, D), _vec_spec(D), _vec_spec(D), _vec_spec(D), _row_spec(ts, D)],
        out_specs=out_specs, out_shape=out_shape, compiler_params=_cp("arbitrary"),
    )(x, out, gate, g, b, dy_or_target)


def _conv_tiles(S, C):
    tt = _tile(S, 256, CONV_HALO)
    tc = _tile(C, 512, LANE)
    return tt, tc


def _shift_scratch(tt, tc):
    return pltpu.VMEM((SUBLANE - 1, tt + CONV_HALO - SUBLANE, tc), F32)


def _fill_shifts(src_ref, sh_ref, tt):
    rows = tt + CONV_HALO - SUBLANE
    for b in range(1, SUBLANE):
        sh_ref[b - 1] = src_ref[b:b + rows, :]


def _shifted(src_ref, sh_ref, q, tt):
    a8, b8 = divmod(q, SUBLANE)
    if b8 == 0:
        return src_ref[q:q + tt, :]
    return sh_ref[b8 - 1, a8 * SUBLANE:a8 * SUBLANE + tt, :]


def _conv1_fwd(u, w_dw, b_dw, name, carry=None):
    S, C3 = u.shape
    C = C3 // 3
    KS = w_dw.shape[0]
    tt, tc = _conv_tiles(S, C)
    ncb = C // tc
    hb = tt // CONV_HALO
    lead = CONV_HALO - (KS - 1)

    def body(a_ref, g_ref, ah_ref, gh_ref, w_ref, b_ref, o_ref, pad_ref, sh_ref):
        i = pl.program_id(0)
        halo = ah_ref[...] * jax.nn.sigmoid(gh_ref[...])
        pad_ref[0:CONV_HALO, :] = jnp.where(i > 0, halo, 0.0)
        pad_ref[CONV_HALO:, :] = a_ref[...] * jax.nn.sigmoid(g_ref[...])
        _fill_shifts(pad_ref, sh_ref, tt)
        acc = jnp.broadcast_to(b_ref[...], (tt, tc))
        for k in range(KS):
            acc = acc + w_ref[k:k + 1, :] * _shifted(pad_ref, sh_ref, lead + k, tt)
        o_ref[...] = acc

    main = lambda off: pl.BlockSpec((tt, tc), lambda i, j: (i, off + j))
    halo = lambda off: pl.BlockSpec((CONV_HALO, tc), lambda i, j: (jnp.maximum(i * hb - 1, 0), off + j))
    res = _pcall(
        body, name=name, grid=(S // tt, ncb),
        in_specs=[main(0), main(ncb), halo(0), halo(ncb), pl.BlockSpec((KS, tc), lambda i, j: (0, j)),
                  pl.BlockSpec((1, tc), lambda i, j: (0, j))],
        out_specs=[pl.BlockSpec((tt, tc), lambda i, j: (i, j))], out_shape=[jax.ShapeDtypeStruct((S, C), F32)],
        scratch_shapes=[pltpu.VMEM((CONV_HALO + tt, tc), F32), _shift_scratch(tt, tc)], sem=("parallel", "parallel"),
        args=(u, u, u, u, w_dw, b_dw), carry=carry)
    return res[0] if carry is None else (res[0][0], res[1])


def _conv2_fwd(v2, u, g_cn, b_cn, name):
    S, C = v2.shape
    ts = _tile(S, ROW_TILE, SUBLANE)

    def body(v_ref, z_ref, g_ref, b_ref, p_ref):
        xhat, _ = _ln_stats(v_ref[...])
        v3 = xhat * g_ref[...] + b_ref[...]
        p_ref[...] = (_silu(v3) * _silu(z_ref[...])).astype(p_ref.dtype)

    return pl.pallas_call(
        body, name=name, grid=(S // ts,),
        in_specs=[_row_spec(ts, C), pl.BlockSpec((ts, C), lambda i: (i, 2)), _vec_spec(C), _vec_spec(C)],
        out_specs=_row_spec(ts, C), out_shape=jax.ShapeDtypeStruct((S, C), BF16), compiler_params=_cp("parallel"),
    )(v2, u, g_cn, b_cn)


def _conv2_bwd(dp, v2, u, g_cn, b_cn, name):
    S, C = v2.shape
    ts = _tile(S, ROW_TILE, SUBLANE)

    def body(dp_ref, v_ref, z_ref, g_ref, b_ref, dv_ref, dz_ref, dg_ref, db_ref):
        i = pl.program_id(0)

        @pl.when(i == 0)
        def _():
            dg_ref[...] = jnp.zeros_like(dg_ref)
            db_ref[...] = jnp.zeros_like(db_ref)

        dp = dp_ref[...].astype(F32)
        z = z_ref[...]
        xhat, rstd = _ln_stats(v_ref[...])
        v3 = xhat * g_ref[...] + b_ref[...]
        dz_ref[...] = (dp * _silu(v3) * _dsilu(z)).astype(dz_ref.dtype)
        dv3 = dp * _silu(z) * _dsilu(v3)
        dv_ref[...] = _ln_bwd(dv3 * g_ref[...], xhat, rstd)
        dg_ref[...] += jnp.sum(dv3 * xhat, axis=0, keepdims=True)
        db_ref[...] += jnp.sum(dv3, axis=0, keepdims=True)

    vec = jax.ShapeDtypeStruct((1, C), F32)
    return pl.pallas_call(
        body, name=name, grid=(S // ts,),
        in_specs=[_row_spec(ts, C), _row_spec(ts, C), pl.BlockSpec((ts, C), lambda i: (i, 2)), _vec_spec(C), _vec_spec(C)],
        out_specs=[_row_spec(ts, C), _row_spec(ts, C), _vec_spec(C), _vec_spec(C)],
        out_shape=[jax.ShapeDtypeStruct((S, C), F32), jax.ShapeDtypeStruct((S, C), BF16), vec, vec],
        compiler_params=_cp("arbitrary"),
    )(dp, v2, u, g_cn, b_cn)


def _conv1_bwd(dv2, u, w_dw, name, carry=None):
    S, C = dv2.shape
    KS = w_dw.shape[0]
    tt, tc = _conv_tiles(S, C)
    ncb = C // tc
    nt = S // tt
    hb = tt // CONV_HALO
    lead = CONV_HALO - (KS - 1)

    def body(dv_ref, dvh_ref, a_ref, g_ref, ah_ref, gh_ref, w_ref, da_ref, dg_ref, dw_ref, db_ref, pad_ref, fpad_ref,
             sh_ref, fsh_ref):
        i = pl.program_id(1)

        @pl.when(i == 0)
        def _():
            dw_ref[...] = jnp.zeros_like(dw_ref)
            db_ref[...] = jnp.zeros_like(db_ref)

        dv = dv_ref[...]
        a = a_ref[...]
        sg = jax.nn.sigmoid(g_ref[...])
        halo = ah_ref[...] * jax.nn.sigmoid(gh_ref[...])
        pad_ref[0:CONV_HALO, :] = jnp.where(i > 0, halo, 0.0)
        pad_ref[CONV_HALO:, :] = a * sg
        fpad_ref[0:tt, :] = dv
        fpad_ref[tt:, :] = jnp.where(i < nt - 1, dvh_ref[...], 0.0)
        _fill_shifts(pad_ref, sh_ref, tt)
        _fill_shifts(fpad_ref, fsh_ref, tt)
        dv1 = jnp.zeros((tt, tc), F32)
        for k in range(KS):
            dv1 = dv1 + w_ref[k:k + 1, :] * _shifted(fpad_ref, fsh_ref, KS - 1 - k, tt)
            dw_ref[k:k + 1, :] += jnp.sum(dv * _shifted(pad_ref, sh_ref, lead + k, tt), axis=0, keepdims=True)
        db_ref[...] += jnp.sum(dv, axis=0, keepdims=True)
        da_ref[...] = (dv1 * sg).astype(da_ref.dtype)
        dg_ref[...] = (dv1 * a * sg * (1.0 - sg)).astype(dg_ref.dtype)

    main = lambda off: pl.BlockSpec((tt, tc), lambda j, i: (i, off + j))
    halo = lambda off: pl.BlockSpec((CONV_HALO, tc), lambda j, i: (jnp.maximum(i * hb - 1, 0), off + j))
    fhalo = pl.BlockSpec((CONV_HALO, tc), lambda j, i: (jnp.minimum((i + 1) * hb, nt * hb - 1), j))
    res = _pcall(
        body, name=name, grid=(ncb, nt),
        in_specs=[main(0), fhalo, main(0), main(ncb), halo(0), halo(ncb), pl.BlockSpec((KS, tc), lambda j, i: (0, j))],
        out_specs=[main(0), main(0), pl.BlockSpec((KS, tc), lambda j, i: (0, j)), pl.BlockSpec((1, tc), lambda j, i: (0, j))],
        out_shape=[jax.ShapeDtypeStruct((S, C), BF16), jax.ShapeDtypeStruct((S, C), BF16),
                   jax.ShapeDtypeStruct((KS, C), F32), jax.ShapeDtypeStruct((1, C), F32)],
        scratch_shapes=[pltpu.VMEM((CONV_HALO + tt, tc), F32), pltpu.VMEM((tt + CONV_HALO, tc), F32),
                        _shift_scratch(tt, tc), _shift_scratch(tt, tc)],
        sem=("parallel", "arbitrary"), args=(dv2, dv2, u, u, u, u, w_dw), carry=carry)
    return res


def _rope_tables(S):
    half = QK_ROPE_DIM // 2
    inv_freq = ROPE_BASE ** (-jnp.arange(half, dtype=F32) / half)
    ang = jnp.arange(S, dtype=jnp.int32).astype(F32)[:, None] * inv_freq[None, :]
    cos, sin, z = jnp.cos(ang), jnp.sin(ang), jnp.zeros((S, half), F32)
    tc = jnp.concatenate([cos, cos, z, z], axis=1)
    t1 = jnp.concatenate([-sin, z, z, z], axis=1)
    t2 = jnp.concatenate([z, sin, z, z], axis=1)
    return tc, t1, t2


def _rope128(r, tc, t1, t2, sign):
    return r * tc + sign * (pltpu.roll(r, LANE - QK_ROPE_DIM // 2, 1) * t1 + pltpu.roll(r, QK_ROPE_DIM // 2, 1) * t2)


def _rms_fwd(x, width, g, name):
    S = x.shape[0]
    ts = _tile(S, 256, SUBLANE)

    def body(x_ref, g_ref, o_ref):
        xv = x_ref[...]
        rr = lax.rsqrt(jnp.mean(xv * xv, axis=-1, keepdims=True) + RMS_EPS)
        o_ref[...] = (xv * rr * g_ref[...]).astype(o_ref.dtype)

    return pl.pallas_call(
        body, name=name, grid=(S // ts,), in_specs=[_row_spec(ts, width), _vec_spec(width)],
        out_specs=_row_spec(ts, width), out_shape=jax.ShapeDtypeStruct((S, width), BF16), compiler_params=_cp("parallel"),
    )(x, g)


def _rms_bwd_math(xv, dy, g):
    n = xv.shape[-1]
    rr = lax.rsqrt(jnp.mean(xv * xv, axis=-1, keepdims=True) + RMS_EPS)
    dyg = dy * g
    dx = rr * dyg - xv * (rr * rr * rr) * (jnp.sum(dyg * xv, axis=-1, keepdims=True) * (1.0 / n))
    dg = jnp.sum(dy * xv * rr, axis=0, keepdims=True)
    return dx, dg


def _rms_bwd(x, width, dy, g, du, name):
    S = x.shape[0]
    ts = _tile(S, 256, SUBLANE)

    def body(x_ref, dy_ref, g_ref, du_in, dx_ref, dg_ref):
        del du_in
        i = pl.program_id(0)

        @pl.when(i == 0)
        def _():
            dg_ref[...] = jnp.zeros_like(dg_ref)

        dx, dg = _rms_bwd_math(x_ref[...], dy_ref[...].astype(F32), g_ref[...])
        dx_ref[...] = dx.astype(dx_ref.dtype)
        dg_ref[...] += dg

    return pl.pallas_call(
        body, name=name, grid=(S // ts,),
        in_specs=[_row_spec(ts, width), _row_spec(ts, width), _vec_spec(width), pl.BlockSpec(memory_space=pl.ANY)],
        out_specs=[_row_spec(ts, width), _vec_spec(width)],
        out_shape=[jax.ShapeDtypeStruct(du.shape, du.dtype), jax.ShapeDtypeStruct((1, width), F32)],
        input_output_aliases={3: 0}, compiler_params=_cp("arbitrary"),
    )(x, dy, g, du)


def _kvprep_fwd(kva, g_kv, tabs, name):
    S, W = kva.shape
    KV = W - LANE
    ts = _tile(S, 256, SUBLANE)

    def body(x_ref, g_ref, tc_ref, t1_ref, t2_ref, c_ref, r_ref):
        xv = x_ref[:, 0:KV]
        rr = lax.rsqrt(jnp.mean(xv * xv, axis=-1, keepdims=True) + RMS_EPS)
        c_ref[...] = (xv * rr * g_ref[...]).astype(c_ref.dtype)
        r_ref[...] = _rope128(x_ref[:, KV:], tc_ref[...], t1_ref[...], t2_ref[...], 1.0).astype(r_ref.dtype)

    tab = _row_spec(ts, LANE)
    return pl.pallas_call(
        body, name=name, grid=(S // ts,), in_specs=[_row_spec(ts, W), _vec_spec(KV), tab, tab, tab],
        out_specs=[_row_spec(ts, KV), _row_spec(ts, LANE)],
        out_shape=[jax.ShapeDtypeStruct((S, KV), BF16), jax.ShapeDtypeStruct((S, LANE), BF16)], compiler_params=_cp("parallel"),
    )(kva, g_kv, *tabs)


def _kvprep_bwd(kva, dckv, dkr_h, g_kv, tabs, name):
    S, W = kva.shape
    KV = W - LANE
    H = dkr_h.shape[1] // LANE
    ts = _tile(S, 256, SUBLANE)

    def body(x_ref, dc_ref, dr_ref, g_ref, tc_ref, t1_ref, t2_ref, o_ref, dg_ref):
        i = pl.program_id(0)

        @pl.when(i == 0)
        def _():
            dg_ref[...] = jnp.zeros_like(dg_ref)

        dx, dg = _rms_bwd_math(x_ref[:, 0:KV], dc_ref[...].astype(F32), g_ref[...])
        o_ref[:, 0:KV] = dx.astype(o_ref.dtype)
        dg_ref[...] += dg
        dr = dr_ref[:, 0:LANE]
        for h in range(1, H):
            dr = dr + dr_ref[:, h * LANE:(h + 1) * LANE]
        o_ref[:, KV:] = _rope128(dr, tc_ref[...], t1_ref[...], t2_ref[...], -1.0).astype(o_ref.dtype)

    tab = _row_spec(ts, LANE)
    return pl.pallas_call(
        body, name=name, grid=(S // ts,),
        in_specs=[_row_spec(ts, W), _row_spec(ts, KV), _row_spec(ts, H * LANE), _vec_spec(KV), tab, tab, tab],
        out_specs=[_row_spec(ts, W), _vec_spec(KV)],
        out_shape=[jax.ShapeDtypeStruct((S, W), BF16), jax.ShapeDtypeStruct((1, KV), F32)], compiler_params=_cp("arbitrary"),
    )(kva, dckv, dkr_h, g_kv, *tabs)


ROPE_GROUP = 8


def _qrope_bwd(dq, tabs, name):
    S, W = dq.shape
    H = W // HEAD_PAD
    G = math.gcd(H, ROPE_GROUP)
    ts = _tile(S, 512, 2 * SUBLANE)

    def body(q_ref, tc_ref, t1_ref, t2_ref, o_ref):
        for g in range(G):
            lo = g * HEAD_PAD
            o_ref[:, lo:lo + LANE] = q_ref[:, lo:lo + LANE].astype(o_ref.dtype)
            o_ref[:, lo + LANE:lo + HEAD_PAD] = _rope128(
                q_ref[:, lo + LANE:lo + HEAD_PAD], tc_ref[...], t1_ref[...], t2_ref[...], -1.0).astype(o_ref.dtype)

    tab = pl.BlockSpec((ts, LANE), lambda i, h: (i, 0))
    blk = pl.BlockSpec((ts, G * HEAD_PAD), lambda i, h: (i, h))
    return pl.pallas_call(
        body, name=name, grid=(S // ts, H // G), in_specs=[blk, tab, tab, tab], out_specs=blk,
        out_shape=jax.ShapeDtypeStruct((S, W), BF16), compiler_params=_cp("parallel", "parallel"),
    )(dq, *tabs)


ATT_TILE = 512
LOG2E = math.log2(math.e)
LN2 = math.log(2.0)
ATT_FWD_HEADS = 4
ATT_BWD_KEY_TILES = 2


def _causal_mask(nk, nq, q0):
    r = lax.broadcasted_iota(jnp.int32, (nk, nq), 0)
    c = lax.broadcasted_iota(jnp.int32, (nk, nq), 1)
    return c + q0 >= r


def _attn_fwd(qraw, kvh, krp, u, zoff, tabs, scale, name, carry=None):
    S, W = qraw.shape
    H = W // HEAD_PAD
    G = math.gcd(H, ATT_FWD_HEADS)
    t = _tile(S, ATT_TILE, LANE)
    nq = S // t
    assert zoff % (G * V_HEAD_DIM) == 0
    zb = zoff // (G * V_HEAD_DIM)
    qc = scale * LOG2E
    t2 = t // 2

    def body(q_ref, tc_ref, t1_ref, t2_ref, kv_ref, kr_ref, z_ref, o_ref, p_ref, qr_ref, lse_ref, vt_sc, m_sc, l_sc, acc_sc):
        i = pl.program_id(1)

        @pl.when(i == 0)
        def _():
            for g in range(G):
                lo = g * HEAD_PAD + QK_NOPE_DIM
                for jj in range(nq):
                    vt_sc[g, jj] = kv_ref[jj * t:(jj + 1) * t, lo:lo + V_HEAD_DIM].astype(F32).T.astype(BF16)

        qs = []
        for g in range(G):
            lo = g * HEAD_PAD
            qrot = _rope128(q_ref[:, lo + LANE:lo + HEAD_PAD], tc_ref[...], t1_ref[...], t2_ref[...], 1.0)
            q = jnp.concatenate([(q_ref[:, lo:lo + LANE] * qc).astype(BF16), (qrot * qc).astype(BF16)], axis=-1)
            qr_ref[:, lo:lo + HEAD_PAD] = q
            qs.append(q)
        m_sc[...] = jnp.full_like(m_sc, -jnp.inf)
        l_sc[...] = jnp.zeros_like(l_sc)
        acc_sc[...] = jnp.zeros_like(acc_sc)

        def update(g, cols, st, vt):
            m_old = m_sc[g, :, cols]
            m_new = jnp.maximum(m_old, jnp.max(st, axis=0, keepdims=True))
            a = jnp.exp2(m_old - m_new)
            pt = jnp.exp2(st - m_new)
            l_sc[g, :, cols] = a * l_sc[g, :, cols] + jnp.sum(pt, axis=0, keepdims=True)
            acc_sc[g, :, cols] = a * acc_sc[g, :, cols] + jnp.dot(vt, pt.astype(BF16), preferred_element_type=F32)
            m_sc[g, :, cols] = m_new

        def keys(g, off):
            return jnp.concatenate([kv_ref[pl.ds(off, t), g * HEAD_PAD:g * HEAD_PAD + QK_NOPE_DIM],
                                    kr_ref[pl.ds(off, t), :]], axis=-1)

        def loop_body(j, carry):
            off = pl.multiple_of(j * t, t)
            for g in range(G):
                st = lax.dot_general(keys(g, off), qs[g], NT, preferred_element_type=F32)
                update(g, slice(0, t), st, vt_sc[g, j])
            return carry

        lax.fori_loop(0, i, loop_body, 0)
        off = pl.multiple_of(i * t, t)
        for g in range(G):
            k = keys(g, off)
            vt = vt_sc[g, i]
            st0 = lax.dot_general(k[0:t2], qs[g][0:t2], NT, preferred_element_type=F32)
            update(g, slice(0, t2), jnp.where(_causal_mask(t2, t2, 0), st0, -jnp.inf), vt[:, 0:t2])
            st1 = lax.dot_general(k, qs[g][t2:], NT, preferred_element_type=F32)
            update(g, slice(t2, t), jnp.where(_causal_mask(t, t2, t2), st1, -jnp.inf), vt)
        for g in range(G):
            ov = (acc_sc[g] / l_sc[g]).T
            cols = slice(g * V_HEAD_DIM, (g + 1) * V_HEAD_DIM)
            o_ref[:, cols] = ov
            p_ref[:, cols] = (ov * _silu(z_ref[:, cols])).astype(p_ref.dtype)
            lse_ref[g] = m_sc[g] + jnp.log2(l_sc[g])

    tab = pl.BlockSpec((t, LANE), lambda h, i: (i, 0))
    head = pl.BlockSpec((t, G * V_HEAD_DIM), lambda h, i: (i, h))
    return _pcall(
        body, name=name, grid=(H // G, nq),
        in_specs=[pl.BlockSpec((t, G * HEAD_PAD), lambda h, i: (i, h)), tab, tab, tab,
                  pl.BlockSpec((S, G * HEAD_PAD), lambda h, i: (0, h)), pl.BlockSpec((S, LANE), lambda h, i: (0, 0)),
                  pl.BlockSpec((t, G * V_HEAD_DIM), lambda h, i: (i, zb + h))],
        out_specs=[head, head, pl.BlockSpec((t, G * HEAD_PAD), lambda h, i: (i, h)),
                   pl.BlockSpec((G, None, 1, t), lambda h, i: (h, i, 0, 0))],
        out_shape=[jax.ShapeDtypeStruct((S, H * V_HEAD_DIM), F32), jax.ShapeDtypeStruct((S, H * V_HEAD_DIM), BF16),
                   jax.ShapeDtypeStruct((S, W), BF16), jax.ShapeDtypeStruct((H, nq, 1, t), F32)],
        scratch_shapes=[pltpu.VMEM((G, nq, V_HEAD_DIM, t), BF16), pltpu.VMEM((G, 1, t), F32), pltpu.VMEM((G, 1, t), F32),
                        pltpu.VMEM((G, V_HEAD_DIM, t), F32)],
        sem=("parallel", "arbitrary"), args=(qraw, *tabs, kvh, krp, u), carry=carry)


def _attn_bwd(qr, kvh, krp, dp, o, u, zoff, lse, scale, name):
    S, W = qr.shape
    H = W // HEAD_PAD
    U = u.shape[1]
    t = _tile(S, ATT_TILE, LANE)
    nq = S // t
    R = math.gcd(nq, ATT_BWD_KEY_TILES)
    tk = R * t
    nk = nq // R
    zb = zoff // V_HEAD_DIM

    def body(q_ref, kv_ref, kr_ref, dp_ref, o_ref, z_ref, lse_ref, dq_ref, dkv_ref, dkr_ref, dz_ref,
             do_sc, dl_sc, dqt_sc, dk_sc, dv_sc):
        j = pl.program_id(1)

        @pl.when(j == 0)
        def _():
            for ii in range(nq):
                rows = slice(ii * t, (ii + 1) * t)
                dpv, ov, z = dp_ref[rows, :], o_ref[rows, :], z_ref[rows, :]
                dov = dpv * _silu(z)
                do_sc[ii] = dov.astype(BF16)
                dz_ref[rows, :] = (dpv * ov * _dsilu(z)).astype(dz_ref.dtype)
                dl_sc[ii] = jnp.sum((dov * ov).T, axis=0, keepdims=True)
                dqt_sc[ii] = jnp.zeros((HEAD_PAD, t), F32)

        kvb = kv_ref[...]
        k = jnp.concatenate([kvb[:, 0:QK_NOPE_DIM], kr_ref[...]], axis=-1)
        v = kvb[:, QK_NOPE_DIM:]
        kt = k.astype(F32).T.astype(BF16)
        dk_sc[...] = jnp.zeros_like(dk_sc)
        dv_sc[...] = jnp.zeros_like(dv_sc)

        def part(i, nkr, mask):
            q = q_ref[pl.ds(pl.multiple_of(i * t, t), t), :]
            dov = do_sc[i]
            st = lax.dot_general(k[0:nkr], q, NT, preferred_element_type=F32)
            pt = jnp.exp2(st - lse_ref[i])
            if mask is not None:
                pt = jnp.where(mask, pt, 0.0)
            dv_sc[0:nkr, :] += jnp.dot(pt.astype(BF16), dov, preferred_element_type=F32)
            dpt = lax.dot_general(v[0:nkr], dov, NT, preferred_element_type=F32)
            dst = (pt * (dpt - dl_sc[i])).astype(BF16)
            dk_sc[0:nkr, :] += jnp.dot(dst, q, preferred_element_type=F32)
            dqt_sc[i] += jnp.dot(kt[:, 0:nkr], dst, preferred_element_type=F32)

        def loop_body(i, carry):
            part(i, tk, None)
            return carry

        for r in range(R):
            part(R * j + r, (r + 1) * t, _causal_mask((r + 1) * t, t, r * t))
        lax.fori_loop(R * (j + 1), nq, loop_body, 0)
        dkv_ref[:, 0:QK_NOPE_DIM] = (dk_sc[:, 0:QK_NOPE_DIM] * LN2).astype(dkv_ref.dtype)
        dkv_ref[:, QK_NOPE_DIM:] = dv_sc[...].astype(dkv_ref.dtype)
        dkr_ref[...] = dk_sc[:, QK_NOPE_DIM:] * LN2

        @pl.when(j == nk - 1)
        def _():
            for ii in range(nq):
                dq_ref[ii * t:(ii + 1) * t, :] = dqt_sc[ii].T * scale

    whole = lambda w, off: pl.BlockSpec((S, w), lambda h, j: (0, off + h))
    return pl.pallas_call(
        body, name=name, grid=(H, nk),
        in_specs=[whole(HEAD_PAD, 0), pl.BlockSpec((tk, HEAD_PAD), lambda h, j: (j, h)),
                  pl.BlockSpec((tk, LANE), lambda h, j: (j, 0)), whole(V_HEAD_DIM, 0), whole(V_HEAD_DIM, 0),
                  whole(V_HEAD_DIM, zb), pl.BlockSpec((None, nq, 1, t), lambda h, j: (h, 0, 0, 0))],
        out_specs=[whole(HEAD_PAD, 0), pl.BlockSpec((tk, HEAD_PAD), lambda h, j: (j, h)),
                   pl.BlockSpec((tk, LANE), lambda h, j: (j, h)), whole(V_HEAD_DIM, zb)],
        out_shape=[jax.ShapeDtypeStruct((S, W), F32), jax.ShapeDtypeStruct((S, W), BF16),
                   jax.ShapeDtypeStruct((S, H * LANE), F32), jax.ShapeDtypeStruct((S, U), BF16)],
        scratch_shapes=[pltpu.VMEM((nq, t, V_HEAD_DIM), BF16), pltpu.VMEM((nq, 1, t), F32),
                        pltpu.VMEM((nq, HEAD_PAD, t), F32), pltpu.VMEM((tk, HEAD_PAD), F32), pltpu.VMEM((tk, V_HEAD_DIM), F32)],
        compiler_params=_cp("parallel", "arbitrary"),
    )(qr, kvh, krp, dp, o, u, lse)


def _adamw_math(w, g, m, v):
    m = ADAM_B1 * m + (1.0 - ADAM_B1) * g
    v = ADAM_B2 * v + (1.0 - ADAM_B2) * (g * g)
    m_hat = m / (1.0 - ADAM_B1 ** ADAM_STEP)
    v_hat = v / (1.0 - ADAM_B2 ** ADAM_STEP)
    delta = -ADAM_LR * (m_hat / (jnp.sqrt(v_hat) + ADAM_EPS) + ADAM_WD * w)
    return delta, m, v


def _adamw(w, g, m, v, name):
    R, C = w.shape
    tr = _tile(R, 256, SUBLANE)
    tc = _tile(C, 1024, LANE)

    def body(w_ref, g_ref, m_ref, v_ref, d_ref, nm_ref, nv_ref):
        d, nm, nv = _adamw_math(w_ref[...], g_ref[...], m_ref[...], v_ref[...])
        d_ref[...] = d
        nm_ref[...] = nm
        nv_ref[...] = nv

    blk = pl.BlockSpec((tr, tc), lambda i, j: (i, j))
    sh = jax.ShapeDtypeStruct((R, C), F32)
    return pl.pallas_call(
        body, name=name, grid=(R // tr, C // tc), in_specs=[blk] * 4, out_specs=[blk] * 3, out_shape=[sh] * 3,
        compiler_params=_cp("parallel", "parallel"),
    )(w, g, m, v)


def _sum_leading(x, name):
    n, R, C = x.shape
    tr = _tile(R, 512, SUBLANE)

    def body(x_ref, o_ref):
        acc = x_ref[0]
        for k in range(1, n):
            acc = acc + x_ref[k]
        o_ref[...] = acc

    return pl.pallas_call(
        body, name=name, grid=(R // tr,), in_specs=[pl.BlockSpec((n, tr, C), lambda i: (0, i, 0))],
        out_specs=pl.BlockSpec((tr, C), lambda i: (i, 0)), out_shape=jax.ShapeDtypeStruct((R, C), F32),
        compiler_params=_cp("parallel"),
    )(x)


def _pair_add(full, recv, c_idx, name):
    n, R, C = full.shape
    h = R // 2
    tr = _tile(h, 256, 2 * SUBLANE)
    tc = _tile(C, 1024, LANE)
    nb = h // tr

    def body(c_ref, a_ref, b_ref, o_ref):
        del c_ref
        o_ref[...] = (a_ref[...].astype(F32) + b_ref[...].astype(F32)).astype(o_ref.dtype)

    return pl.pallas_call(
        body, name=name,
        grid_spec=pltpu.PrefetchScalarGridSpec(
            num_scalar_prefetch=1, grid=(n, nb, C // tc),
            in_specs=[pl.BlockSpec((None, tr, tc), lambda k, i, j, c: (k, c[0] * nb + i, j)),
                      pl.BlockSpec((None, tr, tc), lambda k, i, j, c: (k, i, j))],
            out_specs=pl.BlockSpec((None, tr, tc), lambda k, i, j, c: (k, i, j))),
        out_shape=jax.ShapeDtypeStruct((n, h, C), BF16), compiler_params=_cp("parallel", "parallel", "parallel"),
    )(c_idx, full, recv)


def _chip_sum(pair, recv, idx, name):
    n, h, C = pair.shape
    tr = _tile(h, 256, 2 * SUBLANE)
    tc = _tile(C, 1024, LANE)
    nb = h // tr

    def body(c_ref, chip_ref, a_ref, b_ref, o_ref):
        del c_ref, chip_ref
        acc = a_ref[...].astype(F32)
        for k in range(N_CHIP - 1):
            acc = acc + b_ref[k].astype(F32)
        o_ref[...] = acc

    return pl.pallas_call(
        body, name=name,
        grid_spec=pltpu.PrefetchScalarGridSpec(
            num_scalar_prefetch=2, grid=(nb, C // tc),
            in_specs=[pl.BlockSpec((None, tr, tc), lambda i, j, c, chip: (chip[0], i, j)),
                      pl.BlockSpec((N_CHIP - 1, tr, tc), lambda i, j, c, chip: (0, i, j))],
            out_specs=pl.BlockSpec((tr, tc), lambda i, j, c, chip: (c[0] * nb + i, j))),
        out_shape=jax.ShapeDtypeStruct((2 * h, C), F32), compiler_params=_cp("parallel", "parallel"),
    )(idx[0], idx[1], pair, recv)


def _cast_place(w, idx, name):
    R, C = w.shape
    h = R // 2
    tr = _tile(h, 256, 2 * SUBLANE)
    tc = _tile(C, 1024, LANE)
    nb = h // tr

    def body(c_ref, chip_ref, w_ref, o_ref):
        del c_ref, chip_ref
        o_ref[...] = w_ref[...].astype(o_ref.dtype)

    return pl.pallas_call(
        body, name=name,
        grid_spec=pltpu.PrefetchScalarGridSpec(
            num_scalar_prefetch=2, grid=(nb, C // tc),
            in_specs=[pl.BlockSpec((tr, tc), lambda i, j, c, chip: (c[0] * nb + i, j))],
            out_specs=pl.BlockSpec((None, tr, tc), lambda i, j, c, chip: (chip[0], c[0] * nb + i, j))),
        out_shape=jax.ShapeDtypeStruct((N_CHIP, R, C), BF16), compiler_params=_cp("parallel", "parallel"),
    )(idx[0], idx[1], w)


def _coords():
    return lax.axis_index("x"), lax.axis_index("y"), lax.axis_index("c")


def _allgather_small(x_shard, name):
    m_per, n = x_shard.shape

    def body(x_ref, out_ref, send_sems, recv_sems, local_sem):
        x, y, c = _coords()
        me, sibling = (x, y, c), (x, y, 1 - c)
        chips = [(1 - x, y), (x, 1 - y), (1 - x, 1 - y)]

        def rows(px, py, pc):
            return out_ref.at[pl.ds((4 * px + 2 * py + pc) * m_per, m_per), :]

        def copy(k, block, to, src=None):
            return pltpu.make_async_remote_copy(
                src_ref=rows(*block) if src is None else src, dst_ref=rows(*block), send_sem=send_sems.at[k],
                recv_sem=recv_sems.at[k], device_id=to, device_id_type=MESH_ID)

        mine = pltpu.make_async_copy(x_ref, rows(*me), local_sem)
        mine.start()
        first = [copy(0, me, sibling, src=x_ref)]
        first += [copy(1 + j, me, (*chip, c), src=x_ref) for j, chip in enumerate(chips)]
        for cp in first:
            cp.start()
        passed = [copy(4 + j, (*chip, c), sibling) for j, chip in enumerate(chips)]
        for j, chip in enumerate(chips):
            copy(1 + j, (*chip, c), me).wait_recv()
            passed[j].start()
        copy(0, sibling, me).wait_recv()
        for j, chip in enumerate(chips):
            copy(4 + j, (*chip, 1 - c), me).wait_recv()
        for cp in first + passed:
            cp.wait_send()
        mine.wait()

    return pl.pallas_call(
        body, name=name, out_shape=jax.ShapeDtypeStruct((N_DEV * m_per, n), x_shard.dtype),
        in_specs=[pl.BlockSpec(memory_space=pltpu.VMEM)], out_specs=pl.BlockSpec(memory_space=pltpu.VMEM),
        scratch_shapes=[pltpu.SemaphoreType.DMA((7,)), pltpu.SemaphoreType.DMA((7,)), pltpu.SemaphoreType.DMA],
        compiler_params=pltpu.CompilerParams(vmem_limit_bytes=VMEM_LIMIT),
    )(x_shard)


def _allgather_carry(bufs):
    n = len(bufs)

    def plan(outs, send_sems, recv_sems):
        x, y, c = _coords()
        me, sibling = (x, y, c), (x, y, 1 - c)
        chips = [(1 - x, y), (x, 1 - y), (1 - x, 1 - y)]

        def win(a, px, py, pc):
            h = bufs[a].shape[1] // 2
            return outs[a].at[2 * px + py, pl.ds(pc * h, h), :]

        def copy(a, k, block, to):
            return pltpu.make_async_remote_copy(
                src_ref=win(a, *block), dst_ref=win(a, *block), send_sem=send_sems.at[a, k],
                recv_sem=recv_sems.at[a, k], device_id=to, device_id_type=MESH_ID)

        return c, me, sibling, chips, copy

    def start(_, outs, send_sems, recv_sems):
        c, me, sibling, chips, copy = plan(outs, send_sems, recv_sems)
        for a in range(n):
            copy(a, 0, me, sibling).start()
            for j, chip in enumerate(chips):
                copy(a, 1 + j, me, (*chip, c)).start()

    def finish(_, outs, send_sems, recv_sems):
        c, me, sibling, chips, copy = plan(outs, send_sems, recv_sems)
        for a in range(n):
            for j, chip in enumerate(chips):
                copy(a, 1 + j, (*chip, c), me).wait_recv()
                copy(a, 4 + j, (*chip, c), sibling).start()
        for a in range(n):
            copy(a, 0, sibling, me).wait_recv()
            for j, chip in enumerate(chips):
                copy(a, 4 + j, (*chip, 1 - c), me).wait_recv()
        for a in range(n):
            copy(a, 0, me, sibling).wait_send()
            for j, chip in enumerate(chips):
                copy(a, 1 + j, me, (*chip, c)).wait_send()
                copy(a, 4 + j, (*chip, c), sibling).wait_send()

    return _Carry(tuple(bufs), tuple(jax.ShapeDtypeStruct(b.shape, b.dtype) for b in bufs), {a: a for a in range(n)},
                  (n, 7), start, finish)


def _exchange_alone(carry, name):
    n_in, n_out = len(carry.bufs), len(carry.out_shapes)

    def body(*refs):
        ins, outs = refs[:n_in], refs[n_in:n_in + n_out]
        send_sems, recv_sems = refs[n_in + n_out:]
        carry.start(ins, outs, send_sems, recv_sems)
        carry.finish(ins, outs, send_sems, recv_sems)

    any_spec = pl.BlockSpec(memory_space=pl.ANY)
    return pl.pallas_call(
        body, name=name, out_shape=list(carry.out_shapes), in_specs=[any_spec] * n_in, out_specs=[any_spec] * n_out,
        input_output_aliases=dict(carry.aliases),
        scratch_shapes=[pltpu.SemaphoreType.DMA(carry.sem_shape), pltpu.SemaphoreType.DMA(carry.sem_shape)],
    )(*carry.bufs)


def _pair_exchange(grads, name):
    n = len(grads)

    def body(*refs):
        ins, outs = refs[:n], refs[n:2 * n]
        send_sems, recv_sems = refs[2 * n:]
        x, y, c = _coords()
        sibling = (x, y, 1 - c)
        copies = []
        for a in range(n):
            h = grads[a].shape[1] // 2
            cp = pltpu.make_async_remote_copy(
                src_ref=ins[a].at[:, pl.ds((1 - c) * h, h), :], dst_ref=outs[a], send_sem=send_sems.at[a],
                recv_sem=recv_sems.at[a], device_id=sibling, device_id_type=MESH_ID)
            cp.start()
            copies.append(cp)
        for cp in copies:
            cp.wait()

    any_spec = pl.BlockSpec(memory_space=pl.ANY)
    return pl.pallas_call(
        body, name=name,
        out_shape=[jax.ShapeDtypeStruct((g.shape[0], g.shape[1] // 2, g.shape[2]), g.dtype) for g in grads],
        in_specs=[any_spec] * n, out_specs=[any_spec] * n,
        scratch_shapes=[pltpu.SemaphoreType.DMA((n,)), pltpu.SemaphoreType.DMA((n,))],
    )(*grads)


def _chip_exchange_carry(pairs):
    n = len(pairs)

    def copies(ins, outs, send_sems, recv_sems):
        x, y, c = _coords()
        chips = [(1 - x, y), (x, 1 - y), (1 - x, 1 - y)]
        return [pltpu.make_async_remote_copy(
            src_ref=ins[a].at[2 * px + py], dst_ref=outs[a].at[k], send_sem=send_sems.at[a, k],
            recv_sem=recv_sems.at[a, k], device_id=(px, py, c), device_id_type=MESH_ID)
            for a in range(n) for k, (px, py) in enumerate(chips)]

    def start(ins, outs, send_sems, recv_sems):
        for cp in copies(ins, outs, send_sems, recv_sems):
            cp.start()

    def finish(ins, outs, send_sems, recv_sems):
        for cp in copies(ins, outs, send_sems, recv_sems):
            cp.wait()

    return _Carry(tuple(pairs), tuple(jax.ShapeDtypeStruct((N_CHIP - 1,) + p.shape[1:], p.dtype) for p in pairs), {},
                  (n, N_CHIP - 1), start, finish)


def _half_share(bufs, name):
    n = len(bufs)

    def body(*refs):
        outs = refs[n:2 * n]
        send_sems, recv_sems = refs[2 * n:]
        x, y, c = _coords()
        sibling = (x, y, 1 - c)

        def copy(a, pc):
            h = bufs[a].shape[0] // 2
            rows = outs[a].at[pl.ds(pc * h, h), :]
            return pltpu.make_async_remote_copy(
                src_ref=rows, dst_ref=rows, send_sem=send_sems.at[a], recv_sem=recv_sems.at[a], device_id=sibling,
                device_id_type=MESH_ID)

        for a in range(n):
            copy(a, c).start()
        for a in range(n):
            copy(a, c).wait_send()
            copy(a, 1 - c).wait_recv()

    any_spec = pl.BlockSpec(memory_space=pl.ANY)
    return pl.pallas_call(
        body, name=name, out_shape=[jax.ShapeDtypeStruct(b.shape, b.dtype) for b in bufs],
        in_specs=[any_spec] * n, out_specs=[any_spec] * n, input_output_aliases={a: a for a in range(n)},
        scratch_shapes=[pltpu.SemaphoreType.DMA((n,)), pltpu.SemaphoreType.DMA((n,))],
    )(*bufs)


def _pair_sums(grads, idx, tag):
    names = list(grads)
    full = [grads[k] for k in names]
    recv = _pair_exchange(full, "rs_pair_exchange_" + tag)
    return {k: _pair_add(f, r, idx[0], "rs_pair_add_" + k) for k, f, r in zip(names, full, recv)}


PACK_ALIGN = SUBLANE * LANE
PACK_ROWS_ALIGN = 256 * LANE


def _pack(parts):
    flat, offs, off = [], [], 0
    for p in parts:
        v = p.reshape(-1).astype(F32)
        n = v.shape[0]
        padded = -(-n // PACK_ALIGN) * PACK_ALIGN
        flat.append(jnp.pad(v, (0, padded - n)))
        offs.append((off, n))
        off += padded
    tail = -off % PACK_ROWS_ALIGN
    if tail:
        flat.append(jnp.zeros((tail,), F32))
    return jnp.concatenate(flat).reshape(-1, LANE), offs


def _unpack(flat, offs, shapes):
    return [flat[o:o + n].reshape(s) for (o, n), s in zip(offs, shapes)]


def _chipcat(g, per_dev_len, offs, shape, axis):
    o, n = offs
    parts = [g[2 * j, o:o + n].reshape(shape) for j in range(N_CHIP)]
    return jnp.concatenate(parts, axis=axis)


def kernel(x, c, w_ada, b_ada, ln_g, ln_b, a_w_in, a_w_dw, a_b_dw, a_norm_g, a_norm_b, a_w_out, b_w_in, b_q_norm_g, b_w_qb, b_w_out, kv_w_a, kv_norm_g, kv_w_b, loss_target, m_w_ada, m_b_ada, m_ln_g, m_ln_b, m_a_w_in, m_a_w_dw, m_a_b_dw, m_a_norm_g, m_a_norm_b, m_a_w_out, m_b_w_in, m_b_q_norm_g, m_b_w_qb, m_b_w_out, m_kv_w_a, m_kv_norm_g, m_kv_w_b, v_w_ada, v_b_ada, v_ln_g, v_ln_b, v_a_w_in, v_a_w_dw, v_a_b_dw, v_a_norm_g, v_a_norm_b, v_a_w_out, v_b_w_in, v_b_q_norm_g, v_b_w_qb, v_b_w_out, v_kv_w_a, v_kv_norm_g, v_kv_w_b):
    xi, yi, ci = _coords()
    chip = 2 * xi + yi
    dev = 4 * xi + 2 * yi + ci
    idx = (jnp.reshape(ci, (1,)).astype(jnp.int32), jnp.reshape(chip, (1,)).astype(jnp.int32))

    x2 = x[0]
    tgt = loss_target[0]
    S, D = x2.shape
    C = a_w_out.shape[1] * N_CHIP
    Cq = C // N_CHIP
    KS = a_w_dw.shape[1]
    Q = b_q_norm_g.shape[1]
    KV = kv_norm_g.shape[0]
    Hq = kv_w_b.shape[1] // HEAD_PAD
    H = Hq * N_CHIP
    W = H * V_HEAD_DIM
    Nq = w_ada.shape[2]
    head_q = QK_NOPE_DIM + QK_ROPE_DIM
    scale = head_q ** -0.5
    tabs = _rope_tables(S)

    qb_pad = jnp.pad(b_w_qb[0].reshape(Q, Hq, head_q), ((0, 0), (0, 0), (0, HEAD_PAD - head_q))).reshape(Q, Hq * HEAD_PAD)
    kva_pad = jnp.pad(kv_w_a, ((0, 0), (0, LANE - QK_ROPE_DIM)))
    shards = {"a_w_in": a_w_in[0], "a_w_out": a_w_out[0], "b_w_in": b_w_in[0], "b_w_qb": qb_pad, "b_w_out": b_w_out[0],
              "kv_w_a": kva_pad, "kv_w_b": kv_w_b}
    placed = {k: _cast_place(w, idx, "cast_" + k) for k, w in shards.items()}
    (W_ain,) = _exchange_alone(_allgather_carry([placed["a_w_in"]]), "allgather_a_w_in")

    pack1, offs1 = _pack([c[0], a_w_dw[0], a_b_dw[0], a_norm_g[0], a_norm_b[0]])
    L1 = pack1.shape[0] * LANE
    g1 = _allgather_small(pack1, "allgather_small_in").reshape(N_DEV, L1)
    c_all = g1[:, :D]
    w_dw = _chipcat(g1, L1, offs1[1], (KS, Cq), 1)
    b_dw = _chipcat(g1, L1, offs1[2], (1, Cq), 1)
    g_cn = _chipcat(g1, L1, offs1[3], (1, Cq), 1)
    b_cn = _chipcat(g1, L1, offs1[4], (1, Cq), 1)

    b_ada_sh = lax.dynamic_slice_in_dim(b_ada, chip * Nq, Nq, axis=1)[:, None, :]
    mod_sh = _mod(c_all, w_ada, b_ada_sh, "adaln_mod")
    gm = _allgather_small(mod_sh.reshape(DEPTH * N_DEV, Nq), "allgather_small_mod").reshape(N_CHIP, 2, DEPTH, N_DEV, Nq)
    mod_rows = lax.dynamic_index_in_dim(gm[:, 0], dev, axis=2, keepdims=False)
    mod_me = jnp.transpose(mod_rows, (1, 0, 2)).reshape(DEPTH, N_CHIP * Nq)
    shift = [mod_me[l:l + 1, 0:D] for l in range(DEPTH)]
    scl = [mod_me[l:l + 1, D:2 * D] for l in range(DEPTH)]
    gate = [mod_me[l:l + 1, 2 * D:3 * D] for l in range(DEPTH)]

    h0 = _lnmod_fwd(x2, scl[0], shift[0], "a_lnmod_fwd")
    u0, (W_bin,) = _mm(h0, W_ain, b_sh=True, name="a_in_fwd", carry=_allgather_carry([placed["b_w_in"]]))
    v2, (W_aout, W_kva, W_kvb) = _conv1_fwd(
        u0, w_dw, b_dw, "a_conv1_fwd",
        carry=_allgather_carry([placed["a_w_out"], placed["kv_w_a"], placed["kv_w_b"]]))
    W_aout = W_aout.reshape(C, D)
    W_kva = W_kva.reshape(D, KV + LANE)
    p0 = _conv2_fwd(v2, u0, g_cn, b_cn, "a_conv2_fwd")
    out0 = _mm(p0, W_aout, name="a_out_fwd")
    x1 = _resln_fwd(x2, out0, gate[0], ln_g[0:1], ln_b[0:1], "a_resln_fwd")

    kva = _mm(x1, W_kva, name="kv_a_fwd")
    ckv, krp = _kvprep_fwd(kva, kv_norm_g[None, :], tabs, "kv_prep_fwd")
    kvh = _mm(ckv, W_kvb, b_sh=True, out_dtype=BF16, name="kv_b_fwd")

    h1 = _lnmod_fwd(x1, scl[1], shift[1], "b_lnmod_fwd")
    u1, (W_qb,) = _mm(h1, W_bin, b_sh=True, name="b_in_fwd", carry=_allgather_carry([placed["b_w_qb"]]))
    qn = _rms_fwd(u1, Q, b_q_norm_g, "b_qnorm_fwd")
    qraw = _mm(qn, W_qb, b_sh=True, name="b_qb_fwd")
    (o, p1, qr, lse), (W_bout,) = _attn_fwd(qraw, kvh, krp, u1, Q, tabs, scale, "b_attn_fwd",
                                            carry=_allgather_carry([placed["b_w_out"]]))
    W_bout = W_bout.reshape(W, D)
    out1 = _mm(p1, W_bout, name="b_out_fwd")

    dxa1, dout1, dgate1, dlng1, dlnb1, loss_part = _resln_bwd(
        x1, out1, gate[1], ln_g[1:2], ln_b[1:2], tgt, True, "b_resln_bwd")
    dW_bout = _mm(p1, dout1, ta=True, out_dtype=BF16, name="b_out_dw").reshape(N_CHIP, W // N_CHIP, D)
    pairs = _pair_sums({"b_w_out": dW_bout}, idx, "b_w_out")
    got = {}
    dp1, (got["b_w_out"],) = _mm(dout1, W_bout, tb=True, name="b_out_dx", carry=_chip_exchange_carry([pairs["b_w_out"]]))
    dqr, dkvh, dkr_h, du1 = _attn_bwd(qr, kvh, krp, dp1, o, u1, Q, lse, scale, "b_attn_bwd")
    dqraw = _qrope_bwd(dqr, tabs, "b_qrope_bwd")
    dW_qb = _mm(qn, dqraw, ta=True, o_sh=True, out_dtype=BF16, name="b_qb_dw")
    dqn = _mm(dqraw, W_qb, tb=True, b_sh=True, name="b_qb_dx")
    du1, dgq = _rms_bwd(u1, Q, dqn, b_q_norm_g, du1, "b_qnorm_bwd")
    dW_bin = _mm(h1, du1, ta=True, o_sh=True, out_dtype=BF16, name="b_in_dw")
    dh1 = _mm(du1, W_bin, tb=True, b_sh=True, name="b_in_dx")

    dW_kvb = _mm(ckv, dkvh, ta=True, o_sh=True, out_dtype=BF16, name="kv_b_dw")
    dckv = _mm(dkvh, W_kvb, tb=True, b_sh=True, name="kv_b_dx")
    dkva, dgkv = _kvprep_bwd(kva, dckv, dkr_h, kv_norm_g[None, :], tabs, "kv_prep_bwd")
    dW_kva = _mm(x1, dkva, ta=True, out_dtype=BF16, name="kv_a_dw").reshape(N_CHIP, D // N_CHIP, KV + LANE)
    dx1_kv = _mm(dkva, W_kva, tb=True, name="kv_a_dx")
    dx1, dsc1, dsh1 = _lnmod_bwd(x1, dh1, scl[1], [dxa1, dx1_kv], "b_lnmod_bwd")

    dxa0, dout0, dgate0, dlng0, dlnb0 = _resln_bwd(x2, out0, gate[0], ln_g[0:1], ln_b[0:1], dx1, False, "a_resln_bwd")
    dW_aout = _mm(p0, dout0, ta=True, out_dtype=BF16, name="a_out_dw").reshape(N_CHIP, Cq, D)
    dp0 = _mm(dout0, W_aout, tb=True, name="a_out_dx")
    dv2, dz0, dgcn, dbcn = _conv2_bwd(dp0, v2, u0, g_cn, b_cn, "a_conv2_bwd")
    mid = ["b_w_qb", "b_w_in", "kv_w_b", "kv_w_a"]
    pairs.update(_pair_sums({"b_w_qb": dW_qb, "b_w_in": dW_bin, "kv_w_b": dW_kvb, "kv_w_a": dW_kva}, idx, "mla"))
    (da0, dg0, dwdw, dbdw), got_mid = _conv1_bwd(dv2, u0, w_dw, "a_conv1_bwd",
                                                 carry=_chip_exchange_carry([pairs[k] for k in mid]))
    got.update(zip(mid, got_mid))
    du0 = jnp.concatenate([da0, dg0, dz0], axis=1)
    pairs.update(_pair_sums({"a_w_out": dW_aout}, idx, "a_w_out"))
    dW_ain, (got["a_w_out"],) = _mm(h0, du0, ta=True, o_sh=True, out_dtype=BF16, name="a_in_dw",
                                    carry=_chip_exchange_carry([pairs["a_w_out"]]))
    pairs.update(_pair_sums({"a_w_in": dW_ain}, idx, "a_w_in"))
    dh0, (got["a_w_in"],) = _mm(du0, W_ain, tb=True, b_sh=True, name="a_in_dx",
                                carry=_chip_exchange_carry([pairs["a_w_in"]]))
    dx, dsc0, dsh0 = _lnmod_bwd(x2, dh0, scl[0], [dxa0], "a_lnmod_bwd")
    grad_x = dx[None]

    dmod = jnp.concatenate([dsh0, dsc0, dgate0, dsh1, dsc1, dgate1], axis=1).reshape(DEPTH, 3 * D)
    small = [loss_part, dmod, jnp.concatenate([dlng0, dlng1], 0), jnp.concatenate([dlnb0, dlnb1], 0),
             dwdw, dbdw, dgcn, dbcn, dgq, dgkv]
    small_shapes = [p.shape for p in small]
    pack2, offs2 = _pack(small)
    R2 = pack2.shape[0]
    g2 = _allgather_small(pack2, "allgather_small_grads").reshape(N_DEV, R2, LANE)
    tot = _sum_leading(g2, "small_grad_sum").reshape(-1)
    (loss_t, g_b_ada, g_ln_g, g_ln_b, g_wdw_full, g_bdw_full, g_gcn_full, g_bcn_full, g_gq, g_gkv) = _unpack(
        tot, offs2, small_shapes)
    loss = loss_t.reshape(())
    colsl = lambda a: lax.dynamic_slice_in_dim(a, chip * Cq, Cq, axis=1)
    g_wdw, g_bdw, g_gcn, g_bcn = colsl(g_wdw_full), colsl(g_bdw_full), colsl(g_gcn_full), colsl(g_bcn_full)

    dmod_all = jnp.stack([g2[d].reshape(-1)[offs2[1][0]:offs2[1][0] + offs2[1][1]].reshape(DEPTH, 3 * D)
                          for d in range(N_DEV)], axis=1)
    dmod_sh = lax.dynamic_slice_in_dim(dmod_all, chip * Nq, Nq, axis=2)
    g_w_ada = _wada_grad(jnp.transpose(c_all), dmod_sh, "w_ada_grad")

    mats = ["a_w_in", "a_w_out", "b_w_in", "b_w_qb", "b_w_out", "kv_w_a", "kv_w_b"]
    halves = [_chip_sum(pairs[k], got[k], idx, "rs_chip_sum_" + k) for k in mats]
    red = dict(zip(mats, _half_share(halves, "rs_half_share")))
    g_a_w_in = red["a_w_in"]
    g_a_w_out = red["a_w_out"]
    g_b_w_in = red["b_w_in"]
    g_b_w_qb = red["b_w_qb"].reshape(Q, Hq, HEAD_PAD)[:, :, :head_q].reshape(Q, Hq * head_q)
    g_b_w_out = red["b_w_out"]
    g_kv_w_a = red["kv_w_a"][:, :KV + QK_ROPE_DIM]
    g_kv_w_b = red["kv_w_b"]

    grads = {
        "w_ada": g_w_ada, "b_ada": g_b_ada, "ln_g": g_ln_g, "ln_b": g_ln_b, "a_w_in": g_a_w_in[None],
        "a_w_dw": g_wdw[None], "a_b_dw": g_bdw, "a_norm_g": g_gcn, "a_norm_b": g_bcn, "a_w_out": g_a_w_out[None],
        "b_w_in": g_b_w_in[None], "b_q_norm_g": g_gq, "b_w_qb": g_b_w_qb[None], "b_w_out": g_b_w_out[None],
        "kv_w_a": g_kv_w_a, "kv_norm_g": g_gkv.reshape(KV), "kv_w_b": g_kv_w_b,
    }
    weights = {
        "w_ada": (w_ada, m_w_ada, v_w_ada), "b_ada": (b_ada, m_b_ada, v_b_ada), "ln_g": (ln_g, m_ln_g, v_ln_g),
        "ln_b": (ln_b, m_ln_b, v_ln_b), "a_w_in": (a_w_in, m_a_w_in, v_a_w_in), "a_w_dw": (a_w_dw, m_a_w_dw, v_a_w_dw),
        "a_b_dw": (a_b_dw, m_a_b_dw, v_a_b_dw), "a_norm_g": (a_norm_g, m_a_norm_g, v_a_norm_g),
        "a_norm_b": (a_norm_b, m_a_norm_b, v_a_norm_b), "a_w_out": (a_w_out, m_a_w_out, v_a_w_out),
        "b_w_in": (b_w_in, m_b_w_in, v_b_w_in), "b_q_norm_g": (b_q_norm_g, m_b_q_norm_g, v_b_q_norm_g),
        "b_w_qb": (b_w_qb, m_b_w_qb, v_b_w_qb), "b_w_out": (b_w_out, m_b_w_out, v_b_w_out),
        "kv_w_a": (kv_w_a, m_kv_w_a, v_kv_w_a), "kv_norm_g": (kv_norm_g, m_kv_norm_g, v_kv_norm_g),
        "kv_w_b": (kv_w_b, m_kv_w_b, v_kv_w_b),
    }
    order = list(weights)
    big = [k for k in order if weights[k][0].size >= (1 << 16) and weights[k][0].shape[-1] % LANE == 0]
    small_names = [k for k in order if k not in big]
    upd = {}
    for k in big:
        w, m, v = weights[k]
        shp = w.shape
        two = (-1, shp[-1])
        d_, m_, v_ = _adamw(w.reshape(two), grads[k].reshape(two), m.reshape(two), v.reshape(two), "adamw_" + k)
        upd[k] = (grads[k].reshape(shp), d_.reshape(shp), m_.reshape(shp), v_.reshape(shp))
    sw, offs3 = _pack([weights[k][0] for k in small_names])
    sg, _ = _pack([grads[k] for k in small_names])
    sm, _ = _pack([weights[k][1] for k in small_names])
    sv, _ = _pack([weights[k][2] for k in small_names])
    sd, snm, snv = _adamw(sw, sg, sm, sv, "adamw_small")
    shapes3 = [weights[k][0].shape for k in small_names]
    for k, d_, m_, v_ in zip(small_names, _unpack(sd.reshape(-1), offs3, shapes3), _unpack(snm.reshape(-1), offs3, shapes3),
                             _unpack(snv.reshape(-1), offs3, shapes3)):
        upd[k] = (grads[k].reshape(weights[k][0].shape), d_, m_, v_)

    return (loss, grad_x, *[upd[k][0] for k in order], *[upd[k][1] for k in order], *[upd[k][2] for k in order],
            *[upd[k][3] for k in order])
```

```python
import math
from typing import Callable, NamedTuple

import jax
import jax.numpy as jnp
from jax import lax
from jax.experimental import pallas as pl
from jax.experimental.pallas import tpu as pltpu

F32 = jnp.float32
BF16 = jnp.bfloat16

LN_EPS = 1e-5
RMS_EPS = 1e-6
DEPTH = 2
DEEPNORM_ALPHA = (2.0 * DEPTH) ** 0.25
QK_NOPE_DIM = 128
QK_ROPE_DIM = 64
V_HEAD_DIM = 128
HEAD_PAD = 256
ROPE_BASE = 10000.0
ADAM_LR = 0.001
ADAM_B1 = 0.9
ADAM_B2 = 0.999
ADAM_EPS = 1e-08
ADAM_WD = 0.01
ADAM_STEP = 10

N_DEV = 8
N_CHIP = 4
LANE = 128
SUBLANE = 8
VMEM_LIMIT = 48 * 1024 * 1024
CONV_HALO = 32
MESH_ID = pl.DeviceIdType.MESH
NT = (((1,), (1,)), ((), ()))


def _cp(*sem):
    return pltpu.CompilerParams(dimension_semantics=sem, vmem_limit_bytes=VMEM_LIMIT)


def _tile(n, pref, align):
    if n <= pref:
        return n
    t = (pref // align) * align
    while t > align and n % t:
        t -= align
    assert n % t == 0, (n, pref, align)
    return t


def _silu(v):
    return v * jax.nn.sigmoid(v)


def _dsilu(v):
    s = jax.nn.sigmoid(v)
    return s * (1.0 + v * (1.0 - s))


class _Carry(NamedTuple):
    bufs: tuple
    out_shapes: tuple
    aliases: dict
    sem_shape: tuple
    start: Callable
    finish: Callable


def _pcall(body, *, name, grid, in_specs, out_specs, out_shape, scratch_shapes, sem, args, carry=None):
    if carry is None:
        return pl.pallas_call(body, name=name, grid=grid, in_specs=in_specs, out_specs=out_specs, out_shape=out_shape,
                              scratch_shapes=scratch_shapes, compiler_params=_cp(*sem))(*args)
    n_in, n_out, n_sc = len(in_specs), len(out_specs), len(scratch_shapes)
    nc_in, nc_out = len(carry.bufs), len(carry.out_shapes)

    def wrapped(*refs):
        core_in, c_in = refs[:n_in], refs[n_in:n_in + nc_in]
        core_out = refs[n_in + nc_in:n_in + nc_in + n_out]
        c_out = refs[n_in + nc_in + n_out:n_in + nc_in + n_out + nc_out]
        core_sc = refs[n_in + nc_in + n_out + nc_out:n_in + nc_in + n_out + nc_out + n_sc]
        send_sems, recv_sems = refs[-2:]
        first = pl.program_id(0) == 0
        last = pl.program_id(0) == grid[0] - 1
        for d in range(1, len(grid)):
            first = jnp.logical_and(first, pl.program_id(d) == 0)
            last = jnp.logical_and(last, pl.program_id(d) == grid[d] - 1)

        @pl.when(first)
        def _():
            carry.start(c_in, c_out, send_sems, recv_sems)

        body(*core_in, *core_out, *core_sc)

        @pl.when(last)
        def _():
            carry.finish(c_in, c_out, send_sems, recv_sems)

    any_spec = pl.BlockSpec(memory_space=pl.ANY)
    res = pl.pallas_call(
        wrapped, name=name, grid=grid, in_specs=list(in_specs) + [any_spec] * nc_in,
        out_specs=list(out_specs) + [any_spec] * nc_out, out_shape=list(out_shape) + list(carry.out_shapes),
        input_output_aliases={n_in + i: n_out + o for i, o in carry.aliases.items()},
        scratch_shapes=list(scratch_shapes) + [pltpu.SemaphoreType.DMA(carry.sem_shape), pltpu.SemaphoreType.DMA(carry.sem_shape)],
        compiler_params=_cp(*(("arbitrary",) * len(grid))),
    )(*args, *carry.bufs)
    return res[:n_out], res[n_out:]


def _mm(a, b, *, name, ta=False, tb=False, b_sh=False, o_sh=False, out_dtype=F32, tm=1024, tn=1024, tk=2048, carry=None):
    M, K = (a.shape[1], a.shape[0]) if ta else a.shape
    if b_sh:
        assert b.shape[0] == N_CHIP
        nq = b.shape[2]
        Kb, N = (nq * N_CHIP, b.shape[1]) if tb else (b.shape[1], nq * N_CHIP)
    else:
        Kb, N = (b.shape[1], b.shape[0]) if tb else b.shape
        nq = N // N_CHIP
    assert K == Kb, (a.shape, b.shape)
    tm = _tile(M, tm, LANE)
    tk = _tile(nq if (b_sh and tb) else K, tk, LANE)
    tn = _tile(nq if ((b_sh and not tb) or o_sh) else N, tn, LANE)
    nk = K // tk

    def body(a_ref, b_ref, o_ref, acc_ref):
        k = pl.program_id(2)

        @pl.when(k == 0)
        def _():
            acc_ref[...] = jnp.zeros_like(acc_ref)

        dn = (((0 if ta else 1,), (1 if tb else 0,)), ((), ()))
        acc_ref[...] += lax.dot_general(a_ref[...].astype(BF16), b_ref[...].astype(BF16), dn,
                                        preferred_element_type=F32)

        @pl.when(k == nk - 1)
        def _():
            o_ref[...] = acc_ref[...].astype(o_ref.dtype)

    a_spec = pl.BlockSpec((tk, tm), lambda i, j, k: (k, i)) if ta else pl.BlockSpec((tm, tk), lambda i, j, k: (i, k))
    if b_sh and not tb:
        per = nq // tn
        b_spec = pl.BlockSpec((None, tk, tn), lambda i, j, k: (j // per, k, j % per))
    elif b_sh and tb:
        per = nq // tk
        b_spec = pl.BlockSpec((None, tn, tk), lambda i, j, k: (k // per, j, k % per))
    elif tb:
        b_spec = pl.BlockSpec((tn, tk), lambda i, j, k: (j, k))
    else:
        b_spec = pl.BlockSpec((tk, tn), lambda i, j, k: (k, j))
    if o_sh:
        per_o = nq // tn
        o_spec = pl.BlockSpec((None, tm, tn), lambda i, j, k: (j // per_o, i, j % per_o))
        o_shape = jax.ShapeDtypeStruct((N_CHIP, M, nq), out_dtype)
    else:
        o_spec = pl.BlockSpec((tm, tn), lambda i, j, k: (i, j))
        o_shape = jax.ShapeDtypeStruct((M, N), out_dtype)
    res = _pcall(body, name=name, grid=(M // tm, N // tn, nk), in_specs=[a_spec, b_spec], out_specs=[o_spec],
                 out_shape=[o_shape], scratch_shapes=[pltpu.VMEM((tm, tn), F32)], sem=("parallel", "parallel", "arbitrary"),
                 args=(a, b), carry=carry)
    return res[0] if carry is None else (res[0][0], res[1])


def _mod(c_all, w_ada, b_sh, name):
    L, D, nq = w_ada.shape
    B = c_all.shape[0]
    tn = _tile(nq, 512, LANE)

    def body(c_ref, w_ref, b_ref, o_ref):
        sc = _silu(c_ref[...]).astype(BF16)
        o_ref[...] = jnp.dot(sc, w_ref[...].astype(BF16), preferred_element_type=F32) + b_ref[...]

    return pl.pallas_call(
        body, name=name, grid=(L, nq // tn),
        in_specs=[pl.BlockSpec((B, D), lambda l, j: (0, 0)), pl.BlockSpec((None, D, tn), lambda l, j: (l, 0, j)),
                  pl.BlockSpec((None, 1, tn), lambda l, j: (l, 0, j))],
        out_specs=pl.BlockSpec((None, B, tn), lambda l, j: (l, 0, j)),
        out_shape=jax.ShapeDtypeStruct((L, B, nq), F32), compiler_params=_cp("parallel", "parallel"),
    )(c_all, w_ada, b_sh)


def _wada_grad(c_all_t, dmod, name):
    D, B = c_all_t.shape
    L, _, nq = dmod.shape
    tm = _tile(D, 512, SUBLANE)
    tn = _tile(nq, 1024, LANE)

    def body(c_ref, d_ref, o_ref):
        sc = _silu(c_ref[...])
        dm = d_ref[...]
        acc = sc[:, 0:1] * dm[0:1, :]
        for b in range(1, B):
            acc = acc + sc[:, b:b + 1] * dm[b:b + 1, :]
        o_ref[...] = acc

    return pl.pallas_call(
        body, name=name, grid=(L, D // tm, nq // tn),
        in_specs=[pl.BlockSpec((tm, B), lambda l, i, j: (i, 0)), pl.BlockSpec((None, B, tn), lambda l, i, j: (l, 0, j))],
        out_specs=pl.BlockSpec((None, tm, tn), lambda l, i, j: (l, i, j)),
        out_shape=jax.ShapeDtypeStruct((L, D, nq), F32), compiler_params=_cp("parallel", "parallel", "parallel"),
    )(c_all_t, dmod)


ROW_TILE = 128


def _ln_stats(v):
    mu = jnp.mean(v, axis=-1, keepdims=True)
    vc = v - mu
    var = jnp.mean(vc * vc, axis=-1, keepdims=True)
    rstd = lax.rsqrt(var + LN_EPS)
    return vc * rstd, rstd


def _ln_bwd(dxhat, xhat, rstd):
    m1 = jnp.mean(dxhat, axis=-1, keepdims=True)
    m2 = jnp.mean(dxhat * xhat, axis=-1, keepdims=True)
    return rstd * (dxhat - m1 - xhat * m2)


def _row_spec(ts, D):
    return pl.BlockSpec((ts, D), lambda i: (i, 0))


def _vec_spec(D):
    return pl.BlockSpec((1, D), lambda i: (0, 0))


def _lnmod_fwd(x, scale, shift, name):
    S, D = x.shape
    ts = _tile(S, ROW_TILE, SUBLANE)

    def body(x_ref, sc_ref, sh_ref, h_ref):
        xn, _ = _ln_stats(x_ref[...])
        h_ref[...] = (xn * (1.0 + sc_ref[...]) + sh_ref[...]).astype(h_ref.dtype)

    return pl.pallas_call(
        body, name=name, grid=(S // ts,), in_specs=[_row_spec(ts, D), _vec_spec(D), _vec_spec(D)],
        out_specs=_row_spec(ts, D), out_shape=jax.ShapeDtypeStruct((S, D), BF16), compiler_params=_cp("parallel"),
    )(x, scale, shift)


def _lnmod_bwd(x, dh, scale, adds, name):
    S, D = x.shape
    ts = _tile(S, ROW_TILE, SUBLANE)
    na = len(adds)

    def body(*refs):
        x_ref, dh_ref, sc_ref = refs[:3]
        add_refs = refs[3:3 + na]
        dx_ref, dsc_ref, dsh_ref = refs[3 + na:]
        i = pl.program_id(0)

        @pl.when(i == 0)
        def _():
            dsc_ref[...] = jnp.zeros_like(dsc_ref)
            dsh_ref[...] = jnp.zeros_like(dsh_ref)

        xn, rstd = _ln_stats(x_ref[...])
        dh = dh_ref[...].astype(F32)
        dx = _ln_bwd(dh * (1.0 + sc_ref[...]), xn, rstd)
        for r in add_refs:
            dx = dx + r[...]
        dx_ref[...] = dx
        dsc_ref[...] += jnp.sum(dh * xn, axis=0, keepdims=True)
        dsh_ref[...] += jnp.sum(dh, axis=0, keepdims=True)

    return pl.pallas_call(
        body, name=name, grid=(S // ts,),
        in_specs=[_row_spec(ts, D), _row_spec(ts, D), _vec_spec(D)] + [_row_spec(ts, D)] * na,
        out_specs=[_row_spec(ts, D), _vec_spec(D), _vec_spec(D)],
        out_shape=[jax.ShapeDtypeStruct((S, D), F32), jax.ShapeDtypeStruct((1, D), F32), jax.ShapeDtypeStruct((1, D), F32)],
        compiler_params=_cp("arbitrary"),
    )(x, dh, scale, *adds)


def _resln_fwd(x, out, gate, g, b, name):
    S, D = x.shape
    ts = _tile(S, ROW_TILE, SUBLANE)

    def body(x_ref, o_ref, gt_ref, g_ref, b_ref, y_ref):
        r = DEEPNORM_ALPHA * x_ref[...] + (1.0 + gt_ref[...]) * o_ref[...]
        xhat, _ = _ln_stats(r)
        y_ref[...] = xhat * g_ref[...] + b_ref[...]

    return pl.pallas_call(
        body, name=name, grid=(S // ts,),
        in_specs=[_row_spec(ts, D), _row_spec(ts, D), _vec_spec(D), _vec_spec(D), _vec_spec(D)],
        out_specs=_row_spec(ts, D), out_shape=jax.ShapeDtypeStruct((S, D), F32), compiler_params=_cp("parallel"),
    )(x, out, gate, g, b)


def _resln_bwd(x, out, gate, g, b, dy_or_target, from_target, name):
    S, D = x.shape
    ts = _tile(S, ROW_TILE, SUBLANE)

    def body(x_ref, o_ref, gt_ref, g_ref, b_ref, t_ref, dxa_ref, dout_ref, dgt_ref, dg_ref, db_ref, *maybe_loss):
        i = pl.program_id(0)

        @pl.when(i == 0)
        def _():
            dgt_ref[...] = jnp.zeros_like(dgt_ref)
            dg_ref[...] = jnp.zeros_like(dg_ref)
            db_ref[...] = jnp.zeros_like(db_ref)
            if from_target:
                maybe_loss[0][...] = jnp.zeros_like(maybe_loss[0])

        ov = o_ref[...]
        g1 = 1.0 + gt_ref[...]
        r = DEEPNORM_ALPHA * x_ref[...] + g1 * ov
        xhat, rstd = _ln_stats(r)
        if from_target:
            err = xhat * g_ref[...] + b_ref[...] - t_ref[...]
            dy = err * (1.0 / D)
            maybe_loss[0][...] += jnp.sum(jnp.sum(err * err, axis=-1, keepdims=True), axis=0, keepdims=True) * (0.5 / D)
        else:
            dy = t_ref[...]
        dr = _ln_bwd(dy * g_ref[...], xhat, rstd)
        dxa_ref[...] = DEEPNORM_ALPHA * dr
        dout_ref[...] = (dr * g1).astype(dout_ref.dtype)
        dgt_ref[...] += jnp.sum(dr * ov, axis=0, keepdims=True)
        dg_ref[...] += jnp.sum(dy * xhat, axis=0, keepdims=True)
        db_ref[...] += jnp.sum(dy, axis=0, keepdims=True)

    vec = jax.ShapeDtypeStruct((1, D), F32)
    out_specs = [_row_spec(ts, D), _row_spec(ts, D), _vec_spec(D), _vec_spec(D), _vec_spec(D)]
    out_shape = [jax.ShapeDtypeStruct((S, D), F32), jax.ShapeDtypeStruct((S, D), BF16), vec, vec, vec]
    if from_target:
        out_specs.append(pl.BlockSpec((1, 1), lambda i: (0, 0)))
        out_shape.append(jax.ShapeDtypeStruct((1, 1), F32))
    return pl.pallas_call(
        body, name=name, grid=(S // ts,),
        in_specs=[_row_spec(ts, D), _row_spec(ts, D), _vec_spec(D), _vec_spec(D), _vec_spec(D), _row_spec(ts, D)],
        out_specs=out_specs, out_shape=out_shape, compiler_params=_cp("arbitrary"),
    )(x, out, gate, g, b, dy_or_target)


def _conv_tiles(S, C):
    tt = _tile(S, 256, CONV_HALO)
    tc = _tile(C, 512, LANE)
    return tt, tc


def _shift_scratch(tt, tc):
    return pltpu.VMEM((SUBLANE - 1, tt + CONV_HALO - SUBLANE, tc), F32)


def _fill_shifts(src_ref, sh_ref, tt):
    rows = tt + CONV_HALO - SUBLANE
    for b in range(1, SUBLANE):
        sh_ref[b - 1] = src_ref[b:b + rows, :]


def _shifted(src_ref, sh_ref, q, tt):
    a8, b8 = divmod(q, SUBLANE)
    if b8 == 0:
        return src_ref[q:q + tt, :]
    return sh_ref[b8 - 1, a8 * SUBLANE:a8 * SUBLANE + tt, :]


def _conv1_fwd(u, w_dw, b_dw, name, carry=None):
    S, C3 = u.shape
    C = C3 // 3
    KS = w_dw.shape[0]
    tt, tc = _conv_tiles(S, C)
    ncb = C // tc
    hb = tt // CONV_HALO
    lead = CONV_HALO - (KS - 1)

    def body(a_ref, g_ref, ah_ref, gh_ref, w_ref, b_ref, o_ref, pad_ref, sh_ref):
        i = pl.program_id(0)
        halo = ah_ref[...] * jax.nn.sigmoid(gh_ref[...])
        pad_ref[0:CONV_HALO, :] = jnp.where(i > 0, halo, 0.0)
        pad_ref[CONV_HALO:, :] = a_ref[...] * jax.nn.sigmoid(g_ref[...])
        _fill_shifts(pad_ref, sh_ref, tt)
        acc = jnp.broadcast_to(b_ref[...], (tt, tc))
        for k in range(KS):
            acc = acc + w_ref[k:k + 1, :] * _shifted(pad_ref, sh_ref, lead + k, tt)
        o_ref[...] = acc

    main = lambda off: pl.BlockSpec((tt, tc), lambda i, j: (i, off + j))
    halo = lambda off: pl.BlockSpec((CONV_HALO, tc), lambda i, j: (jnp.maximum(i * hb - 1, 0), off + j))
    res = _pcall(
        body, name=name, grid=(S // tt, ncb),
        in_specs=[main(0), main(ncb), halo(0), halo(ncb), pl.BlockSpec((KS, tc), lambda i, j: (0, j)),
                  pl.BlockSpec((1, tc), lambda i, j: (0, j))],
        out_specs=[pl.BlockSpec((tt, tc), lambda i, j: (i, j))], out_shape=[jax.ShapeDtypeStruct((S, C), F32)],
        scratch_shapes=[pltpu.VMEM((CONV_HALO + tt, tc), F32), _shift_scratch(tt, tc)], sem=("parallel", "parallel"),
        args=(u, u, u, u, w_dw, b_dw), carry=carry)
    return res[0] if carry is None else (res[0][0], res[1])


def _conv2_fwd(v2, u, g_cn, b_cn, name):
    S, C = v2.shape
    ts = _tile(S, ROW_TILE, SUBLANE)

    def body(v_ref, z_ref, g_ref, b_ref, p_ref):
        xhat, _ = _ln_stats(v_ref[...])
        v3 = xhat * g_ref[...] + b_ref[...]
        p_ref[...] = (_silu(v3) * _silu(z_ref[...])).astype(p_ref.dtype)

    return pl.pallas_call(
        body, name=name, grid=(S // ts,),
        in_specs=[_row_spec(ts, C), pl.BlockSpec((ts, C), lambda i: (i, 2)), _vec_spec(C), _vec_spec(C)],
        out_specs=_row_spec(ts, C), out_shape=jax.ShapeDtypeStruct((S, C), BF16), compiler_params=_cp("parallel"),
    )(v2, u, g_cn, b_cn)


def _conv2_bwd(dp, v2, u, g_cn, b_cn, name):
    S, C = v2.shape
    ts = _tile(S, ROW_TILE, SUBLANE)

    def body(dp_ref, v_ref, z_ref, g_ref, b_ref, dv_ref, dz_ref, dg_ref, db_ref):
        i = pl.program_id(0)

        @pl.when(i == 0)
        def _():
            dg_ref[...] = jnp.zeros_like(dg_ref)
            db_ref[...] = jnp.zeros_like(db_ref)

        dp = dp_ref[...].astype(F32)
        z = z_ref[...]
        xhat, rstd = _ln_stats(v_ref[...])
        v3 = xhat * g_ref[...] + b_ref[...]
        dz_ref[...] = (dp * _silu(v3) * _dsilu(z)).astype(dz_ref.dtype)
        dv3 = dp * _silu(z) * _dsilu(v3)
        dv_ref[...] = _ln_bwd(dv3 * g_ref[...], xhat, rstd)
        dg_ref[...] += jnp.sum(dv3 * xhat, axis=0, keepdims=True)
        db_ref[...] += jnp.sum(dv3, axis=0, keepdims=True)

    vec = jax.ShapeDtypeStruct((1, C), F32)
    return pl.pallas_call(
        body, name=name, grid=(S // ts,),
        in_specs=[_row_spec(ts, C), _row_spec(ts, C), pl.BlockSpec((ts, C), lambda i: (i, 2)), _vec_spec(C), _vec_spec(C)],
        out_specs=[_row_spec(ts, C), _row_spec(ts, C), _vec_spec(C), _vec_spec(C)],
        out_shape=[jax.ShapeDtypeStruct((S, C), F32), jax.ShapeDtypeStruct((S, C), BF16), vec, vec],
        compiler_params=_cp("arbitrary"),
    )(dp, v2, u, g_cn, b_cn)


def _conv1_bwd(dv2, u, w_dw, name, carry=None):
    S, C = dv2.shape
    KS = w_dw.shape[0]
    tt, tc = _conv_tiles(S, C)
    ncb = C // tc
    nt = S // tt
    hb = tt // CONV_HALO
    lead = CONV_HALO - (KS - 1)

    def body(dv_ref, dvh_ref, a_ref, g_ref, ah_ref, gh_ref, w_ref, da_ref, dg_ref, dw_ref, db_ref, pad_ref, fpad_ref,
             sh_ref, fsh_ref):
        i = pl.program_id(1)

        @pl.when(i == 0)
        def _():
            dw_ref[...] = jnp.zeros_like(dw_ref)
            db_ref[...] = jnp.zeros_like(db_ref)

        dv = dv_ref[...]
        a = a_ref[...]
        sg = jax.nn.sigmoid(g_ref[...])
        halo = ah_ref[...] * jax.nn.sigmoid(gh_ref[...])
        pad_ref[0:CONV_HALO, :] = jnp.where(i > 0, halo, 0.0)
        pad_ref[CONV_HALO:, :] = a * sg
        fpad_ref[0:tt, :] = dv
        fpad_ref[tt:, :] = jnp.where(i < nt - 1, dvh_ref[...], 0.0)
        _fill_shifts(pad_ref, sh_ref, tt)
        _fill_shifts(fpad_ref, fsh_ref, tt)
        dv1 = jnp.zeros((tt, tc), F32)
        for k in range(KS):
            dv1 = dv1 + w_ref[k:k + 1, :] * _shifted(fpad_ref, fsh_ref, KS - 1 - k, tt)
            dw_ref[k:k + 1, :] += jnp.sum(dv * _shifted(pad_ref, sh_ref, lead + k, tt), axis=0, keepdims=True)
        db_ref[...] += jnp.sum(dv, axis=0, keepdims=True)
        da_ref[...] = (dv1 * sg).astype(da_ref.dtype)
        dg_ref[...] = (dv1 * a * sg * (1.0 - sg)).astype(dg_ref.dtype)

    main = lambda off: pl.BlockSpec((tt, tc), lambda j, i: (i, off + j))
    halo = lambda off: pl.BlockSpec((CONV_HALO, tc), lambda j, i: (jnp.maximum(i * hb - 1, 0), off + j))
    fhalo = pl.BlockSpec((CONV_HALO, tc), lambda j, i: (jnp.minimum((i + 1) * hb, nt * hb - 1), j))
    res = _pcall(
        body, name=name, grid=(ncb, nt),
        in_specs=[main(0), fhalo, main(0), main(ncb), halo(0), halo(ncb), pl.BlockSpec((KS, tc), lambda j, i: (0, j))],
        out_specs=[main(0), main(0), pl.BlockSpec((KS, tc), lambda j, i: (0, j)), pl.BlockSpec((1, tc), lambda j, i: (0, j))],
        out_shape=[jax.ShapeDtypeStruct((S, C), BF16), jax.ShapeDtypeStruct((S, C), BF16),
                   jax.ShapeDtypeStruct((KS, C), F32), jax.ShapeDtypeStruct((1, C), F32)],
        scratch_shapes=[pltpu.VMEM((CONV_HALO + tt, tc), F32), pltpu.VMEM((tt + CONV_HALO, tc), F32),
                        _shift_scratch(tt, tc), _shift_scratch(tt, tc)],
        sem=("parallel", "arbitrary"), args=(dv2, dv2, u, u, u, u, w_dw), carry=carry)
    return res


def _rope_tables(S):
    half = QK_ROPE_DIM // 2
    inv_freq = ROPE_BASE ** (-jnp.arange(half, dtype=F32) / half)
    ang = jnp.arange(S, dtype=jnp.int32).astype(F32)[:, None] * inv_freq[None, :]
    cos, sin, z = jnp.cos(ang), jnp.sin(ang), jnp.zeros((S, half), F32)
    tc = jnp.concatenate([cos, cos, z, z], axis=1)
    t1 = jnp.concatenate([-sin, z, z, z], axis=1)
    t2 = jnp.concatenate([z, sin, z, z], axis=1)
    return tc, t1, t2


def _rope128(r, tc, t1, t2, sign):
    return r * tc + sign * (pltpu.roll(r, LANE - QK_ROPE_DIM // 2, 1) * t1 + pltpu.roll(r, QK_ROPE_DIM // 2, 1) * t2)


def _rms_fwd(x, width, g, name):
    S = x.shape[0]
    ts = _tile(S, 256, SUBLANE)

    def body(x_ref, g_ref, o_ref):
        xv = x_ref[...]
        rr = lax.rsqrt(jnp.mean(xv * xv, axis=-1, keepdims=True) + RMS_EPS)
        o_ref[...] = (xv * rr * g_ref[...]).astype(o_ref.dtype)

    return pl.pallas_call(
        body, name=name, grid=(S // ts,), in_specs=[_row_spec(ts, width), _vec_spec(width)],
        out_specs=_row_spec(ts, width), out_shape=jax.ShapeDtypeStruct((S, width), BF16), compiler_params=_cp("parallel"),
    )(x, g)


def _rms_bwd_math(xv, dy, g):
    n = xv.shape[-1]
    rr = lax.rsqrt(jnp.mean(xv * xv, axis=-1, keepdims=True) + RMS_EPS)
    dyg = dy * g
    dx = rr * dyg - xv * (rr * rr * rr) * (jnp.sum(dyg * xv, axis=-1, keepdims=True) * (1.0 / n))
    dg = jnp.sum(dy * xv * rr, axis=0, keepdims=True)
    return dx, dg


def _rms_bwd(x, width, dy, g, du, name):
    S = x.shape[0]
    ts = _tile(S, 256, SUBLANE)

    def body(x_ref, dy_ref, g_ref, du_in, dx_ref, dg_ref):
        del du_in
        i = pl.program_id(0)

        @pl.when(i == 0)
        def _():
            dg_ref[...] = jnp.zeros_like(dg_ref)

        dx, dg = _rms_bwd_math(x_ref[...], dy_ref[...].astype(F32), g_ref[...])
        dx_ref[...] = dx.astype(dx_ref.dtype)
        dg_ref[...] += dg

    return pl.pallas_call(
        body, name=name, grid=(S // ts,),
        in_specs=[_row_spec(ts, width), _row_spec(ts, width), _vec_spec(width), pl.BlockSpec(memory_space=pl.ANY)],
        out_specs=[_row_spec(ts, width), _vec_spec(width)],
        out_shape=[jax.ShapeDtypeStruct(du.shape, du.dtype), jax.ShapeDtypeStruct((1, width), F32)],
        input_output_aliases={3: 0}, compiler_params=_cp("arbitrary"),
    )(x, dy, g, du)


def _kvprep_fwd(kva, g_kv, tabs, name):
    S, W = kva.shape
    KV = W - LANE
    ts = _tile(S, 256, SUBLANE)

    def body(x_ref, g_ref, tc_ref, t1_ref, t2_ref, c_ref, r_ref):
        xv = x_ref[:, 0:KV]
        rr = lax.rsqrt(jnp.mean(xv * xv, axis=-1, keepdims=True) + RMS_EPS)
        c_ref[...] = (xv * rr * g_ref[...]).astype(c_ref.dtype)
        r_ref[...] = _rope128(x_ref[:, KV:], tc_ref[...], t1_ref[...], t2_ref[...], 1.0).astype(r_ref.dtype)

    tab = _row_spec(ts, LANE)
    return pl.pallas_call(
        body, name=name, grid=(S // ts,), in_specs=[_row_spec(ts, W), _vec_spec(KV), tab, tab, tab],
        out_specs=[_row_spec(ts, KV), _row_spec(ts, LANE)],
        out_shape=[jax.ShapeDtypeStruct((S, KV), BF16), jax.ShapeDtypeStruct((S, LANE), BF16)], compiler_params=_cp("parallel"),
    )(kva, g_kv, *tabs)


def _kvprep_bwd(kva, dckv, dkr_h, g_kv, tabs, name):
    S, W = kva.shape
    KV = W - LANE
    H = dkr_h.shape[1] // LANE
    ts = _tile(S, 256, SUBLANE)

    def body(x_ref, dc_ref, dr_ref, g_ref, tc_ref, t1_ref, t2_ref, o_ref, dg_ref):
        i = pl.program_id(0)

        @pl.when(i == 0)
        def _():
            dg_ref[...] = jnp.zeros_like(dg_ref)

        dx, dg = _rms_bwd_math(x_ref[:, 0:KV], dc_ref[...].astype(F32), g_ref[...])
        o_ref[:, 0:KV] = dx.astype(o_ref.dtype)
        dg_ref[...] += dg
        dr = dr_ref[:, 0:LANE]
        for h in range(1, H):
            dr = dr + dr_ref[:, h * LANE:(h + 1) * LANE]
        o_ref[:, KV:] = _rope128(dr, tc_ref[...], t1_ref[...], t2_ref[...], -1.0).astype(o_ref.dtype)

    tab = _row_spec(ts, LANE)
    return pl.pallas_call(
        body, name=name, grid=(S // ts,),
        in_specs=[_row_spec(ts, W), _row_spec(ts, KV), _row_spec(ts, H * LANE), _vec_spec(KV), tab, tab, tab],
        out_specs=[_row_spec(ts, W), _vec_spec(KV)],
        out_shape=[jax.ShapeDtypeStruct((S, W), BF16), jax.ShapeDtypeStruct((1, KV), F32)], compiler_params=_cp("arbitrary"),
    )(kva, dckv, dkr_h, g_kv, *tabs)


ROPE_GROUP = 8


def _qrope_bwd(dq, tabs, name):
    S, W = dq.shape
    H = W // HEAD_PAD
    G = math.gcd(H, ROPE_GROUP)
    ts = _tile(S, 512, 2 * SUBLANE)

    def body(q_ref, tc_ref, t1_ref, t2_ref, o_ref):
        for g in range(G):
            lo = g * HEAD_PAD
            o_ref[:, lo:lo + LANE] = q_ref[:, lo:lo + LANE].astype(o_ref.dtype)
            o_ref[:, lo + LANE:lo + HEAD_PAD] = _rope128(
                q_ref[:, lo + LANE:lo + HEAD_PAD], tc_ref[...], t1_ref[...], t2_ref[...], -1.0).astype(o_ref.dtype)

    tab = pl.BlockSpec((ts, LANE), lambda i, h: (i, 0))
    blk = pl.BlockSpec((ts, G * HEAD_PAD), lambda i, h: (i, h))
    return pl.pallas_call(
        body, name=name, grid=(S // ts, H // G), in_specs=[blk, tab, tab, tab], out_specs=blk,
        out_shape=jax.ShapeDtypeStruct((S, W), BF16), compiler_params=_cp("parallel", "parallel"),
    )(dq, *tabs)


ATT_TILE = 512
LOG2E = math.log2(math.e)
LN2 = math.log(2.0)
ATT_FWD_HEADS = 4
ATT_BWD_KEY_TILES = 2


def _causal_mask(nk, nq, q0):
    r = lax.broadcasted_iota(jnp.int32, (nk, nq), 0)
    c = lax.broadcasted_iota(jnp.int32, (nk, nq), 1)
    return c + q0 >= r


def _attn_fwd(qraw, kvh, krp, u, zoff, tabs, scale, name, carry=None):
    S, W = qraw.shape
    H = W // HEAD_PAD
    G = math.gcd(H, ATT_FWD_HEADS)
    t = _tile(S, ATT_TILE, LANE)
    nq = S // t
    assert zoff % (G * V_HEAD_DIM) == 0
    zb = zoff // (G * V_HEAD_DIM)
    qc = scale * LOG2E
    t2 = t // 2

    def body(q_ref, tc_ref, t1_ref, t2_ref, kv_ref, kr_ref, z_ref, o_ref, p_ref, qr_ref, lse_ref, vt_sc, m_sc, l_sc, acc_sc):
        i = pl.program_id(1)

        @pl.when(i == 0)
        def _():
            for g in range(G):
                lo = g * HEAD_PAD + QK_NOPE_DIM
                for jj in range(nq):
                    vt_sc[g, jj] = kv_ref[jj * t:(jj + 1) * t, lo:lo + V_HEAD_DIM].astype(F32).T.astype(BF16)

        qs = []
        for g in range(G):
            lo = g * HEAD_PAD
            qrot = _rope128(q_ref[:, lo + LANE:lo + HEAD_PAD], tc_ref[...], t1_ref[...], t2_ref[...], 1.0)
            q = jnp.concatenate([(q_ref[:, lo:lo + LANE] * qc).astype(BF16), (qrot * qc).astype(BF16)], axis=-1)
            qr_ref[:, lo:lo + HEAD_PAD] = q
            qs.append(q)
        m_sc[...] = jnp.full_like(m_sc, -jnp.inf)
        l_sc[...] = jnp.zeros_like(l_sc)
        acc_sc[...] = jnp.zeros_like(acc_sc)

        def update(g, cols, st, vt):
            m_old = m_sc[g, :, cols]
            m_new = jnp.maximum(m_old, jnp.max(st, axis=0, keepdims=True))
            a = jnp.exp2(m_old - m_new)
            pt = jnp.exp2(st - m_new)
            l_sc[g, :, cols] = a * l_sc[g, :, cols] + jnp.sum(pt, axis=0, keepdims=True)
            acc_sc[g, :, cols] = a * acc_sc[g, :, cols] + jnp.dot(vt, pt.astype(BF16), preferred_element_type=F32)
            m_sc[g, :, cols] = m_new

        def keys(g, off):
            return jnp.concatenate([kv_ref[pl.ds(off, t), g * HEAD_PAD:g * HEAD_PAD + QK_NOPE_DIM],
                                    kr_ref[pl.ds(off, t), :]], axis=-1)

        def loop_body(j, carry):
            off = pl.multiple_of(j * t, t)
            for g in range(G):
                st = lax.dot_general(keys(g, off), qs[g], NT, preferred_element_type=F32)
                update(g, slice(0, t), st, vt_sc[g, j])
            return carry

        lax.fori_loop(0, i, loop_body, 0)
        off = pl.multiple_of(i * t, t)
        for g in range(G):
            k = keys(g, off)
            vt = vt_sc[g, i]
            st0 = lax.dot_general(k[0:t2], qs[g][0:t2], NT, preferred_element_type=F32)
            update(g, slice(0, t2), jnp.where(_causal_mask(t2, t2, 0), st0, -jnp.inf), vt[:, 0:t2])
            st1 = lax.dot_general(k, qs[g][t2:], NT, preferred_element_type=F32)
            update(g, slice(t2, t), jnp.where(_causal_mask(t, t2, t2), st1, -jnp.inf), vt)
        for g in range(G):
            ov = (acc_sc[g] / l_sc[g]).T
            cols = slice(g * V_HEAD_DIM, (g + 1) * V_HEAD_DIM)
            o_ref[:, cols] = ov
            p_ref[:, cols] = (ov * _silu(z_ref[:, cols])).astype(p_ref.dtype)
            lse_ref[g] = m_sc[g] + jnp.log2(l_sc[g])

    tab = pl.BlockSpec((t, LANE), lambda h, i: (i, 0))
    head = pl.BlockSpec((t, G * V_HEAD_DIM), lambda h, i: (i, h))
    return _pcall(
        body, name=name, grid=(H // G, nq),
        in_specs=[pl.BlockSpec((t, G * HEAD_PAD), lambda h, i: (i, h)), tab, tab, tab,
                  pl.BlockSpec((S, G * HEAD_PAD), lambda h, i: (0, h)), pl.BlockSpec((S, LANE), lambda h, i: (0, 0)),
                  pl.BlockSpec((t, G * V_HEAD_DIM), lambda h, i: (i, zb + h))],
        out_specs=[head, head, pl.BlockSpec((t, G * HEAD_PAD), lambda h, i: (i, h)),
                   pl.BlockSpec((G, None, 1, t), lambda h, i: (h, i, 0, 0))],
        out_shape=[jax.ShapeDtypeStruct((S, H * V_HEAD_DIM), F32), jax.ShapeDtypeStruct((S, H * V_HEAD_DIM), BF16),
                   jax.ShapeDtypeStruct((S, W), BF16), jax.ShapeDtypeStruct((H, nq, 1, t), F32)],
        scratch_shapes=[pltpu.VMEM((G, nq, V_HEAD_DIM, t), BF16), pltpu.VMEM((G, 1, t), F32), pltpu.VMEM((G, 1, t), F32),
                        pltpu.VMEM((G, V_HEAD_DIM, t), F32)],
        sem=("parallel", "arbitrary"), args=(qraw, *tabs, kvh, krp, u), carry=carry)


def _attn_bwd(qr, kvh, krp, dp, o, u, zoff, lse, scale, name):
    S, W = qr.shape
    H = W // HEAD_PAD
    U = u.shape[1]
    t = _tile(S, ATT_TILE, LANE)
    nq = S // t
    R = math.gcd(nq, ATT_BWD_KEY_TILES)
    tk = R * t
    nk = nq // R
    zb = zoff // V_HEAD_DIM

    def body(q_ref, kv_ref, kr_ref, dp_ref, o_ref, z_ref, lse_ref, dq_ref, dkv_ref, dkr_ref, dz_ref,
             do_sc, dl_sc, dqt_sc, dk_sc, dv_sc):
        j = pl.program_id(1)

        @pl.when(j == 0)
        def _():
            for ii in range(nq):
                rows = slice(ii * t, (ii + 1) * t)
                dpv, ov, z = dp_ref[rows, :], o_ref[rows, :], z_ref[rows, :]
                dov = dpv * _silu(z)
                do_sc[ii] = dov.astype(BF16)
                dz_ref[rows, :] = (dpv * ov * _dsilu(z)).astype(dz_ref.dtype)
                dl_sc[ii] = jnp.sum((dov * ov).T, axis=0, keepdims=True)
                dqt_sc[ii] = jnp.zeros((HEAD_PAD, t), F32)

        kvb = kv_ref[...]
        k = jnp.concatenate([kvb[:, 0:QK_NOPE_DIM], kr_ref[...]], axis=-1)
        v = kvb[:, QK_NOPE_DIM:]
        kt = k.astype(F32).T.astype(BF16)
        dk_sc[...] = jnp.zeros_like(dk_sc)
        dv_sc[...] = jnp.zeros_like(dv_sc)

        def part(i, nkr, mask):
            q = q_ref[pl.ds(pl.multiple_of(i * t, t), t), :]
            dov = do_sc[i]
            st = lax.dot_general(k[0:nkr], q, NT, preferred_element_type=F32)
            pt = jnp.exp2(st - lse_ref[i])
            if mask is not None:
                pt = jnp.where(mask, pt, 0.0)
            dv_sc[0:nkr, :] += jnp.dot(pt.astype(BF16), dov, preferred_element_type=F32)
            dpt = lax.dot_general(v[0:nkr], dov, NT, preferred_element_type=F32)
            dst = (pt * (dpt - dl_sc[i])).astype(BF16)
            dk_sc[0:nkr, :] += jnp.dot(dst, q, preferred_element_type=F32)
            dqt_sc[i] += jnp.dot(kt[:, 0:nkr], dst, preferred_element_type=F32)

        def loop_body(i, carry):
            part(i, tk, None)
            return carry

        for r in range(R):
            part(R * j + r, (r + 1) * t, _causal_mask((r + 1) * t, t, r * t))
        lax.fori_loop(R * (j + 1), nq, loop_body, 0)
        dkv_ref[:, 0:QK_NOPE_DIM] = (dk_sc[:, 0:QK_NOPE_DIM] * LN2).astype(dkv_ref.dtype)
        dkv_ref[:, QK_NOPE_DIM:] = dv_sc[...].astype(dkv_ref.dtype)
        dkr_ref[...] = dk_sc[:, QK_NOPE_DIM:] * LN2

        @pl.when(j == nk - 1)
        def _():
            for ii in range(nq):
                dq_ref[ii * t:(ii + 1) * t, :] = dqt_sc[ii].T * scale

    whole = lambda w, off: pl.BlockSpec((S, w), lambda h, j: (0, off + h))
    return pl.pallas_call(
        body, name=name, grid=(H, nk),
        in_specs=[whole(HEAD_PAD, 0), pl.BlockSpec((tk, HEAD_PAD), lambda h, j: (j, h)),
                  pl.BlockSpec((tk, LANE), lambda h, j: (j, 0)), whole(V_HEAD_DIM, 0), whole(V_HEAD_DIM, 0),
                  whole(V_HEAD_DIM, zb), pl.BlockSpec((None, nq, 1, t), lambda h, j: (h, 0, 0, 0))],
        out_specs=[whole(HEAD_PAD, 0), pl.BlockSpec((tk, HEAD_PAD), lambda h, j: (j, h)),
                   pl.BlockSpec((tk, LANE), lambda h, j: (j, h)), whole(V_HEAD_DIM, zb)],
        out_shape=[jax.ShapeDtypeStruct((S, W), F32), jax.ShapeDtypeStruct((S, W), BF16),
                   jax.ShapeDtypeStruct((S, H * LANE), F32), jax.ShapeDtypeStruct((S, U), BF16)],
        scratch_shapes=[pltpu.VMEM((nq, t, V_HEAD_DIM), BF16), pltpu.VMEM((nq, 1, t), F32),
                        pltpu.VMEM((nq, HEAD_PAD, t), F32), pltpu.VMEM((tk, HEAD_PAD), F32), pltpu.VMEM((tk, V_HEAD_DIM), F32)],
        compiler_params=_cp("parallel", "arbitrary"),
    )(qr, kvh, krp, dp, o, u, lse)


def _adamw_math(w, g, m, v):
    m = ADAM_B1 * m + (1.0 - ADAM_B1) * g
    v = ADAM_B2 * v + (1.0 - ADAM_B2) * (g * g)
    m_hat = m / (1.0 - ADAM_B1 ** ADAM_STEP)
    v_hat = v / (1.0 - ADAM_B2 ** ADAM_STEP)
    delta = -ADAM_LR * (m_hat / (jnp.sqrt(v_hat) + ADAM_EPS) + ADAM_WD * w)
    return delta, m, v


def _adamw(w, g, m, v, name):
    R, C = w.shape
    tr = _tile(R, 256, SUBLANE)
    tc = _tile(C, 1024, LANE)

    def body(w_ref, g_ref, m_ref, v_ref, d_ref, nm_ref, nv_ref):
        d, nm, nv = _adamw_math(w_ref[...], g_ref[...], m_ref[...], v_ref[...])
        d_ref[...] = d
        nm_ref[...] = nm
        nv_ref[...] = nv

    blk = pl.BlockSpec((tr, tc), lambda i, j: (i, j))
    sh = jax.ShapeDtypeStruct((R, C), F32)
    return pl.pallas_call(
        body, name=name, grid=(R // tr, C // tc), in_specs=[blk] * 4, out_specs=[blk] * 3, out_shape=[sh] * 3,
        compiler_params=_cp("parallel", "parallel"),
    )(w, g, m, v)


def _sum_leading(x, name):
    n, R, C = x.shape
    tr = _tile(R, 512, SUBLANE)

    def body(x_ref, o_ref):
        acc = x_ref[0]
        for k in range(1, n):
            acc = acc + x_ref[k]
        o_ref[...] = acc

    return pl.pallas_call(
        body, name=name, grid=(R // tr,), in_specs=[pl.BlockSpec((n, tr, C), lambda i: (0, i, 0))],
        out_specs=pl.BlockSpec((tr, C), lambda i: (i, 0)), out_shape=jax.ShapeDtypeStruct((R, C), F32),
        compiler_params=_cp("parallel"),
    )(x)


def _pair_add(full, recv, c_idx, name):
    n, R, C = full.shape
    h = R // 2
    tr = _tile(h, 256, 2 * SUBLANE)
    tc = _tile(C, 1024, LANE)
    nb = h // tr

    def body(c_ref, a_ref, b_ref, o_ref):
        del c_ref
        o_ref[...] = (a_ref[...].astype(F32) + b_ref[...].astype(F32)).astype(o_ref.dtype)

    return pl.pallas_call(
        body, name=name,
        grid_spec=pltpu.PrefetchScalarGridSpec(
            num_scalar_prefetch=1, grid=(n, nb, C // tc),
            in_specs=[pl.BlockSpec((None, tr, tc), lambda k, i, j, c: (k, c[0] * nb + i, j)),
                      pl.BlockSpec((None, tr, tc), lambda k, i, j, c: (k, i, j))],
            out_specs=pl.BlockSpec((None, tr, tc), lambda k, i, j, c: (k, i, j))),
        out_shape=jax.ShapeDtypeStruct((n, h, C), BF16), compiler_params=_cp("parallel", "parallel", "parallel"),
    )(c_idx, full, recv)


def _chip_sum(pair, recv, idx, name):
    n, h, C = pair.shape
    tr = _tile(h, 256, 2 * SUBLANE)
    tc = _tile(C, 1024, LANE)
    nb = h // tr

    def body(c_ref, chip_ref, a_ref, b_ref, o_ref):
        del c_ref, chip_ref
        acc = a_ref[...].astype(F32)
        for k in range(N_CHIP - 1):
            acc = acc + b_ref[k].astype(F32)
        o_ref[...] = acc

    return pl.pallas_call(
        body, name=name,
        grid_spec=pltpu.PrefetchScalarGridSpec(
            num_scalar_prefetch=2, grid=(nb, C // tc),
            in_specs=[pl.BlockSpec((None, tr, tc), lambda i, j, c, chip: (chip[0], i, j)),
                      pl.BlockSpec((N_CHIP - 1, tr, tc), lambda i, j, c, chip: (0, i, j))],
            out_specs=pl.BlockSpec((tr, tc), lambda i, j, c, chip: (c[0] * nb + i, j))),
        out_shape=jax.ShapeDtypeStruct((2 * h, C), F32), compiler_params=_cp("parallel", "parallel"),
    )(idx[0], idx[1], pair, recv)


def _cast_place(w, idx, name):
    R, C = w.shape
    h = R // 2
    tr = _tile(h, 256, 2 * SUBLANE)
    tc = _tile(C, 1024, LANE)
    nb = h // tr

    def body(c_ref, chip_ref, w_ref, o_ref):
        del c_ref, chip_ref
        o_ref[...] = w_ref[...].astype(o_ref.dtype)

    return pl.pallas_call(
        body, name=name,
        grid_spec=pltpu.PrefetchScalarGridSpec(
            num_scalar_prefetch=2, grid=(nb, C // tc),
            in_specs=[pl.BlockSpec((tr, tc), lambda i, j, c, chip: (c[0] * nb + i, j))],
            out_specs=pl.BlockSpec((None, tr, tc), lambda i, j, c, chip: (chip[0], c[0] * nb + i, j))),
        out_shape=jax.ShapeDtypeStruct((N_CHIP, R, C), BF16), compiler_params=_cp("parallel", "parallel"),
    )(idx[0], idx[1], w)


def _coords():
    return lax.axis_index("x"), lax.axis_index("y"), lax.axis_index("c")


def _allgather_small(x_shard, name):
    m_per, n = x_shard.shape

    def body(x_ref, out_ref, send_sems, recv_sems, local_sem):
        x, y, c = _coords()
        me, sibling = (x, y, c), (x, y, 1 - c)
        chips = [(1 - x, y), (x, 1 - y), (1 - x, 1 - y)]

        def rows(px, py, pc):
            return out_ref.at[pl.ds((4 * px + 2 * py + pc) * m_per, m_per), :]

        def copy(k, block, to, src=None):
            return pltpu.make_async_remote_copy(
                src_ref=rows(*block) if src is None else src, dst_ref=rows(*block), send_sem=send_sems.at[k],
                recv_sem=recv_sems.at[k], device_id=to, device_id_type=MESH_ID)

        mine = pltpu.make_async_copy(x_ref, rows(*me), local_sem)
        mine.start()
        first = [copy(0, me, sibling, src=x_ref)]
        first += [copy(1 + j, me, (*chip, c), src=x_ref) for j, chip in enumerate(chips)]
        for cp in first:
            cp.start()
        passed = [copy(4 + j, (*chip, c), sibling) for j, chip in enumerate(chips)]
        for j, chip in enumerate(chips):
            copy(1 + j, (*chip, c), me).wait_recv()
            passed[j].start()
        copy(0, sibling, me).wait_recv()
        for j, chip in enumerate(chips):
            copy(4 + j, (*chip, 1 - c), me).wait_recv()
        for cp in first + passed:
            cp.wait_send()
        mine.wait()

    return pl.pallas_call(
        body, name=name, out_shape=jax.ShapeDtypeStruct((N_DEV * m_per, n), x_shard.dtype),
        in_specs=[pl.BlockSpec(memory_space=pltpu.VMEM)], out_specs=pl.BlockSpec(memory_space=pltpu.VMEM),
        scratch_shapes=[pltpu.SemaphoreType.DMA((7,)), pltpu.SemaphoreType.DMA((7,)), pltpu.SemaphoreType.DMA],
        compiler_params=pltpu.CompilerParams(vmem_limit_bytes=VMEM_LIMIT),
    )(x_shard)


def _allgather_carry(bufs):
    n = len(bufs)

    def plan(outs, send_sems, recv_sems):
        x, y, c = _coords()
        me, sibling = (x, y, c), (x, y, 1 - c)
        chips = [(1 - x, y), (x, 1 - y), (1 - x, 1 - y)]

        def win(a, px, py, pc):
            h = bufs[a].shape[1] // 2
            return outs[a].at[2 * px + py, pl.ds(pc * h, h), :]

        def copy(a, k, block, to):
            return pltpu.make_async_remote_copy(
                src_ref=win(a, *block), dst_ref=win(a, *block), send_sem=send_sems.at[a, k],
                recv_sem=recv_sems.at[a, k], device_id=to, device_id_type=MESH_ID)

        return c, me, sibling, chips, copy

    def start(_, outs, send_sems, recv_sems):
        c, me, sibling, chips, copy = plan(outs, send_sems, recv_sems)
        for a in range(n):
            copy(a, 0, me, sibling).start()
            for j, chip in enumerate(chips):
                copy(a, 1 + j, me, (*chip, c)).start()

    def finish(_, outs, send_sems, recv_sems):
        c, me, sibling, chips, copy = plan(outs, send_sems, recv_sems)
        for a in range(n):
            for j, chip in enumerate(chips):
                copy(a, 1 + j, (*chip, c), me).wait_recv()
                copy(a, 4 + j, (*chip, c), sibling).start()
        for a in range(n):
            copy(a, 0, sibling, me).wait_recv()
            for j, chip in enumerate(chips):
                copy(a, 4 + j, (*chip, 1 - c), me).wait_recv()
        for a in range(n):
            copy(a, 0, me, sibling).wait_send()
            for j, chip in enumerate(chips):
                copy(a, 1 + j, me, (*chip, c)).wait_send()
                copy(a, 4 + j, (*chip, c), sibling).wait_send()

    return _Carry(tuple(bufs), tuple(jax.ShapeDtypeStruct(b.shape, b.dtype) for b in bufs), {a: a for a in range(n)},
                  (n, 7), start, finish)


def _exchange_alone(carry, name):
    n_in, n_out = len(carry.bufs), len(carry.out_shapes)

    def body(*refs):
        ins, outs = refs[:n_in], refs[n_in:n_in + n_out]
        send_sems, recv_sems = refs[n_in + n_out:]
        carry.start(ins, outs, send_sems, recv_sems)
        carry.finish(ins, outs, send_sems, recv_sems)

    any_spec = pl.BlockSpec(memory_space=pl.ANY)
    return pl.pallas_call(
        body, name=name, out_shape=list(carry.out_shapes), in_specs=[any_spec] * n_in, out_specs=[any_spec] * n_out,
        input_output_aliases=dict(carry.aliases),
        scratch_shapes=[pltpu.SemaphoreType.DMA(carry.sem_shape), pltpu.SemaphoreType.DMA(carry.sem_shape)],
    )(*carry.bufs)


def _pair_exchange_carry(grads):
    n = len(grads)

    def copies(ins, outs, send_sems, recv_sems):
        x, y, c = _coords()
        res = []
        for a in range(n):
            h = grads[a].shape[1] // 2
            res.append(pltpu.make_async_remote_copy(
                src_ref=ins[a].at[:, pl.ds((1 - c) * h, h), :], dst_ref=outs[a], send_sem=send_sems.at[a],
                recv_sem=recv_sems.at[a], device_id=(x, y, 1 - c), device_id_type=MESH_ID))
        return res

    def start(ins, outs, send_sems, recv_sems):
        for cp in copies(ins, outs, send_sems, recv_sems):
            cp.start()

    def finish(ins, outs, send_sems, recv_sems):
        for cp in copies(ins, outs, send_sems, recv_sems):
            cp.wait()

    return _Carry(tuple(grads),
                  tuple(jax.ShapeDtypeStruct((g.shape[0], g.shape[1] // 2, g.shape[2]), g.dtype) for g in grads), {},
                  (n,), start, finish)


def _chip_exchange_carry(pairs):
    n = len(pairs)

    def copies(ins, outs, send_sems, recv_sems):
        x, y, c = _coords()
        chips = [(1 - x, y), (x, 1 - y), (1 - x, 1 - y)]
        return [pltpu.make_async_remote_copy(
            src_ref=ins[a].at[2 * px + py], dst_ref=outs[a].at[k], send_sem=send_sems.at[a, k],
            recv_sem=recv_sems.at[a, k], device_id=(px, py, c), device_id_type=MESH_ID)
            for a in range(n) for k, (px, py) in enumerate(chips)]

    def start(ins, outs, send_sems, recv_sems):
        for cp in copies(ins, outs, send_sems, recv_sems):
            cp.start()

    def finish(ins, outs, send_sems, recv_sems):
        for cp in copies(ins, outs, send_sems, recv_sems):
            cp.wait()

    return _Carry(tuple(pairs), tuple(jax.ShapeDtypeStruct((N_CHIP - 1,) + p.shape[1:], p.dtype) for p in pairs), {},
                  (n, N_CHIP - 1), start, finish)


def _half_share(bufs, name):
    n = len(bufs)

    def body(*refs):
        outs = refs[n:2 * n]
        send_sems, recv_sems = refs[2 * n:]
        x, y, c = _coords()
        sibling = (x, y, 1 - c)

        def copy(a, pc):
            h = bufs[a].shape[0] // 2
            rows = outs[a].at[pl.ds(pc * h, h), :]
            return pltpu.make_async_remote_copy(
                src_ref=rows, dst_ref=rows, send_sem=send_sems.at[a], recv_sem=recv_sems.at[a], device_id=sibling,
                device_id_type=MESH_ID)

        for a in range(n):
            copy(a, c).start()
        for a in range(n):
            copy(a, c).wait_send()
            copy(a, 1 - c).wait_recv()

    any_spec = pl.BlockSpec(memory_space=pl.ANY)
    return pl.pallas_call(
        body, name=name, out_shape=[jax.ShapeDtypeStruct(b.shape, b.dtype) for b in bufs],
        in_specs=[any_spec] * n, out_specs=[any_spec] * n, input_output_aliases={a: a for a in range(n)},
        scratch_shapes=[pltpu.SemaphoreType.DMA((n,)), pltpu.SemaphoreType.DMA((n,))],
    )(*bufs)


def _pair_adds(grads, recv, idx):
    return {k: _pair_add(f, r, idx[0], "rs_pair_add_" + k) for (k, f), r in zip(grads.items(), recv)}


PACK_ALIGN = SUBLANE * LANE
PACK_ROWS_ALIGN = 256 * LANE


def _pack(parts):
    flat, offs, off = [], [], 0
    for p in parts:
        v = p.reshape(-1).astype(F32)
        n = v.shape[0]
        padded = -(-n // PACK_ALIGN) * PACK_ALIGN
        flat.append(jnp.pad(v, (0, padded - n)))
        offs.append((off, n))
        off += padded
    tail = -off % PACK_ROWS_ALIGN
    if tail:
        flat.append(jnp.zeros((tail,), F32))
    return jnp.concatenate(flat).reshape(-1, LANE), offs


def _unpack(flat, offs, shapes):
    return [flat[o:o + n].reshape(s) for (o, n), s in zip(offs, shapes)]


def _chipcat(g, per_dev_len, offs, shape, axis):
    o, n = offs
    parts = [g[2 * j, o:o + n].reshape(shape) for j in range(N_CHIP)]
    return jnp.concatenate(parts, axis=axis)


def kernel(x, c, w_ada, b_ada, ln_g, ln_b, a_w_in, a_w_dw, a_b_dw, a_norm_g, a_norm_b, a_w_out, b_w_in, b_q_norm_g, b_w_qb, b_w_out, kv_w_a, kv_norm_g, kv_w_b, loss_target, m_w_ada, m_b_ada, m_ln_g, m_ln_b, m_a_w_in, m_a_w_dw, m_a_b_dw, m_a_norm_g, m_a_norm_b, m_a_w_out, m_b_w_in, m_b_q_norm_g, m_b_w_qb, m_b_w_out, m_kv_w_a, m_kv_norm_g, m_kv_w_b, v_w_ada, v_b_ada, v_ln_g, v_ln_b, v_a_w_in, v_a_w_dw, v_a_b_dw, v_a_norm_g, v_a_norm_b, v_a_w_out, v_b_w_in, v_b_q_norm_g, v_b_w_qb, v_b_w_out, v_kv_w_a, v_kv_norm_g, v_kv_w_b):
    xi, yi, ci = _coords()
    chip = 2 * xi + yi
    dev = 4 * xi + 2 * yi + ci
    idx = (jnp.reshape(ci, (1,)).astype(jnp.int32), jnp.reshape(chip, (1,)).astype(jnp.int32))

    x2 = x[0]
    tgt = loss_target[0]
    S, D = x2.shape
    C = a_w_out.shape[1] * N_CHIP
    Cq = C // N_CHIP
    KS = a_w_dw.shape[1]
    Q = b_q_norm_g.shape[1]
    KV = kv_norm_g.shape[0]
    Hq = kv_w_b.shape[1] // HEAD_PAD
    H = Hq * N_CHIP
    W = H * V_HEAD_DIM
    Nq = w_ada.shape[2]
    head_q = QK_NOPE_DIM + QK_ROPE_DIM
    scale = head_q ** -0.5
    tabs = _rope_tables(S)

    qb_pad = jnp.pad(b_w_qb[0].reshape(Q, Hq, head_q), ((0, 0), (0, 0), (0, HEAD_PAD - head_q))).reshape(Q, Hq * HEAD_PAD)
    kva_pad = jnp.pad(kv_w_a, ((0, 0), (0, LANE - QK_ROPE_DIM)))
    shards = {"a_w_in": a_w_in[0], "a_w_out": a_w_out[0], "b_w_in": b_w_in[0], "b_w_qb": qb_pad, "b_w_out": b_w_out[0],
              "kv_w_a": kva_pad, "kv_w_b": kv_w_b}
    placed = {k: _cast_place(w, idx, "cast_" + k) for k, w in shards.items()}
    (W_ain,) = _exchange_alone(_allgather_carry([placed["a_w_in"]]), "allgather_a_w_in")

    pack1, offs1 = _pack([c[0], a_w_dw[0], a_b_dw[0], a_norm_g[0], a_norm_b[0]])
    L1 = pack1.shape[0] * LANE
    g1 = _allgather_small(pack1, "allgather_small_in").reshape(N_DEV, L1)
    c_all = g1[:, :D]
    w_dw = _chipcat(g1, L1, offs1[1], (KS, Cq), 1)
    b_dw = _chipcat(g1, L1, offs1[2], (1, Cq), 1)
    g_cn = _chipcat(g1, L1, offs1[3], (1, Cq), 1)
    b_cn = _chipcat(g1, L1, offs1[4], (1, Cq), 1)

    b_ada_sh = lax.dynamic_slice_in_dim(b_ada, chip * Nq, Nq, axis=1)[:, None, :]
    mod_sh = _mod(c_all, w_ada, b_ada_sh, "adaln_mod")
    gm = _allgather_small(mod_sh.reshape(DEPTH * N_DEV, Nq), "allgather_small_mod").reshape(N_CHIP, 2, DEPTH, N_DEV, Nq)
    mod_rows = lax.dynamic_index_in_dim(gm[:, 0], dev, axis=2, keepdims=False)
    mod_me = jnp.transpose(mod_rows, (1, 0, 2)).reshape(DEPTH, N_CHIP * Nq)
    shift = [mod_me[l:l + 1, 0:D] for l in range(DEPTH)]
    scl = [mod_me[l:l + 1, D:2 * D] for l in range(DEPTH)]
    gate = [mod_me[l:l + 1, 2 * D:3 * D] for l in range(DEPTH)]

    h0 = _lnmod_fwd(x2, scl[0], shift[0], "a_lnmod_fwd")
    u0, (W_bin,) = _mm(h0, W_ain, b_sh=True, name="a_in_fwd", carry=_allgather_carry([placed["b_w_in"]]))
    v2, (W_aout, W_kva, W_kvb) = _conv1_fwd(
        u0, w_dw, b_dw, "a_conv1_fwd",
        carry=_allgather_carry([placed["a_w_out"], placed["kv_w_a"], placed["kv_w_b"]]))
    W_aout = W_aout.reshape(C, D)
    W_kva = W_kva.reshape(D, KV + LANE)
    p0 = _conv2_fwd(v2, u0, g_cn, b_cn, "a_conv2_fwd")
    out0 = _mm(p0, W_aout, name="a_out_fwd")
    x1 = _resln_fwd(x2, out0, gate[0], ln_g[0:1], ln_b[0:1], "a_resln_fwd")

    kva = _mm(x1, W_kva, name="kv_a_fwd")
    ckv, krp = _kvprep_fwd(kva, kv_norm_g[None, :], tabs, "kv_prep_fwd")
    kvh = _mm(ckv, W_kvb, b_sh=True, out_dtype=BF16, name="kv_b_fwd")

    h1 = _lnmod_fwd(x1, scl[1], shift[1], "b_lnmod_fwd")
    u1, (W_qb,) = _mm(h1, W_bin, b_sh=True, name="b_in_fwd", carry=_allgather_carry([placed["b_w_qb"]]))
    qn = _rms_fwd(u1, Q, b_q_norm_g, "b_qnorm_fwd")
    qraw = _mm(qn, W_qb, b_sh=True, name="b_qb_fwd")
    (o, p1, qr, lse), (W_bout,) = _attn_fwd(qraw, kvh, krp, u1, Q, tabs, scale, "b_attn_fwd",
                                            carry=_allgather_carry([placed["b_w_out"]]))
    W_bout = W_bout.reshape(W, D)
    out1 = _mm(p1, W_bout, name="b_out_fwd")

    dxa1, dout1, dgate1, dlng1, dlnb1, loss_part = _resln_bwd(
        x1, out1, gate[1], ln_g[1:2], ln_b[1:2], tgt, True, "b_resln_bwd")
    dW_bout = _mm(p1, dout1, ta=True, out_dtype=BF16, name="b_out_dw").reshape(N_CHIP, W // N_CHIP, D)
    dp1, recv = _mm(dout1, W_bout, tb=True, name="b_out_dx", carry=_pair_exchange_carry([dW_bout]))
    pairs = _pair_adds({"b_w_out": dW_bout}, recv, idx)
    got = {}
    dqr, dkvh, dkr_h, du1 = _attn_bwd(qr, kvh, krp, dp1, o, u1, Q, lse, scale, "b_attn_bwd")
    dqraw = _qrope_bwd(dqr, tabs, "b_qrope_bwd")
    dW_qb = _mm(qn, dqraw, ta=True, o_sh=True, out_dtype=BF16, name="b_qb_dw")
    dqn = _mm(dqraw, W_qb, tb=True, b_sh=True, name="b_qb_dx")
    du1, dgq = _rms_bwd(u1, Q, dqn, b_q_norm_g, du1, "b_qnorm_bwd")
    dW_bin = _mm(h1, du1, ta=True, o_sh=True, out_dtype=BF16, name="b_in_dw")
    dh1, (got["b_w_out"],) = _mm(du1, W_bin, tb=True, b_sh=True, name="b_in_dx",
                                 carry=_chip_exchange_carry([pairs["b_w_out"]]))

    dW_kvb = _mm(ckv, dkvh, ta=True, o_sh=True, out_dtype=BF16, name="kv_b_dw")
    dckv = _mm(dkvh, W_kvb, tb=True, b_sh=True, name="kv_b_dx")
    dkva, dgkv = _kvprep_bwd(kva, dckv, dkr_h, kv_norm_g[None, :], tabs, "kv_prep_bwd")
    dW_kva = _mm(x1, dkva, ta=True, out_dtype=BF16, name="kv_a_dw").reshape(N_CHIP, D // N_CHIP, KV + LANE)
    dx1_kv = _mm(dkva, W_kva, tb=True, name="kv_a_dx")
    dx1, dsc1, dsh1 = _lnmod_bwd(x1, dh1, scl[1], [dxa1, dx1_kv], "b_lnmod_bwd")

    dxa0, dout0, dgate0, dlng0, dlnb0 = _resln_bwd(x2, out0, gate[0], ln_g[0:1], ln_b[0:1], dx1, False, "a_resln_bwd")
    mla = {"b_w_qb": dW_qb, "b_w_in": dW_bin, "kv_w_b": dW_kvb, "kv_w_a": dW_kva}
    dW_aout, recv = _mm(p0, dout0, ta=True, out_dtype=BF16, name="a_out_dw", carry=_pair_exchange_carry(list(mla.values())))
    pairs.update(_pair_adds(mla, recv, idx))
    dW_aout = dW_aout.reshape(N_CHIP, Cq, D)
    dp0, recv = _mm(dout0, W_aout, tb=True, name="a_out_dx", carry=_pair_exchange_carry([dW_aout]))
    pairs.update(_pair_adds({"a_w_out": dW_aout}, recv, idx))
    dv2, dz0, dgcn, dbcn = _conv2_bwd(dp0, v2, u0, g_cn, b_cn, "a_conv2_bwd")
    mid = ["b_w_in", "kv_w_b", "kv_w_a"]
    (da0, dg0, dwdw, dbdw), got_mid = _conv1_bwd(dv2, u0, w_dw, "a_conv1_bwd",
                                                 carry=_chip_exchange_carry([pairs[k] for k in mid]))
    got.update(zip(mid, got_mid))
    du0 = jnp.concatenate([da0, dg0, dz0], axis=1)
    dW_ain, (got["a_w_out"], got["b_w_qb"]) = _mm(
        h0, du0, ta=True, o_sh=True, out_dtype=BF16, name="a_in_dw",
        carry=_chip_exchange_carry([pairs["a_w_out"], pairs["b_w_qb"]]))
    pairs.update(_pair_adds({"a_w_in": dW_ain},
                            _exchange_alone(_pair_exchange_carry([dW_ain]), "rs_pair_exchange_a_w_in"), idx))
    dh0, (got["a_w_in"],) = _mm(du0, W_ain, tb=True, b_sh=True, name="a_in_dx",
                                carry=_chip_exchange_carry([pairs["a_w_in"]]))
    dx, dsc0, dsh0 = _lnmod_bwd(x2, dh0, scl[0], [dxa0], "a_lnmod_bwd")
    grad_x = dx[None]

    dmod = jnp.concatenate([dsh0, dsc0, dgate0, dsh1, dsc1, dgate1], axis=1).reshape(DEPTH, 3 * D)
    small = [loss_part, dmod, jnp.concatenate([dlng0, dlng1], 0), jnp.concatenate([dlnb0, dlnb1], 0),
             dwdw, dbdw, dgcn, dbcn, dgq, dgkv]
    small_shapes = [p.shape for p in small]
    pack2, offs2 = _pack(small)
    R2 = pack2.shape[0]
    g2 = _allgather_small(pack2, "allgather_small_grads").reshape(N_DEV, R2, LANE)
    tot = _sum_leading(g2, "small_grad_sum").reshape(-1)
    (loss_t, g_b_ada, g_ln_g, g_ln_b, g_wdw_full, g_bdw_full, g_gcn_full, g_bcn_full, g_gq, g_gkv) = _unpack(
        tot, offs2, small_shapes)
    loss = loss_t.reshape(())
    colsl = lambda a: lax.dynamic_slice_in_dim(a, chip * Cq, Cq, axis=1)
    g_wdw, g_bdw, g_gcn, g_bcn = colsl(g_wdw_full), colsl(g_bdw_full), colsl(g_gcn_full), colsl(g_bcn_full)

    dmod_all = jnp.stack([g2[d].reshape(-1)[offs2[1][0]:offs2[1][0] + offs2[1][1]].reshape(DEPTH, 3 * D)
                          for d in range(N_DEV)], axis=1)
    dmod_sh = lax.dynamic_slice_in_dim(dmod_all, chip * Nq, Nq, axis=2)
    g_w_ada = _wada_grad(jnp.transpose(c_all), dmod_sh, "w_ada_grad")

    mats = ["a_w_in", "a_w_out", "b_w_in", "b_w_qb", "b_w_out", "kv_w_a", "kv_w_b"]
    halves = [_chip_sum(pairs[k], got[k], idx, "rs_chip_sum_" + k) for k in mats]
    red = dict(zip(mats, _half_share(halves, "rs_half_share")))
    g_a_w_in = red["a_w_in"]
    g_a_w_out = red["a_w_out"]
    g_b_w_in = red["b_w_in"]
    g_b_w_qb = red["b_w_qb"].reshape(Q, Hq, HEAD_PAD)[:, :, :head_q].reshape(Q, Hq * head_q)
    g_b_w_out = red["b_w_out"]
    g_kv_w_a = red["kv_w_a"][:, :KV + QK_ROPE_DIM]
    g_kv_w_b = red["kv_w_b"]

    grads = {
        "w_ada": g_w_ada, "b_ada": g_b_ada, "ln_g": g_ln_g, "ln_b": g_ln_b, "a_w_in": g_a_w_in[None],
        "a_w_dw": g_wdw[None], "a_b_dw": g_bdw, "a_norm_g": g_gcn, "a_norm_b": g_bcn, "a_w_out": g_a_w_out[None],
        "b_w_in": g_b_w_in[None], "b_q_norm_g": g_gq, "b_w_qb": g_b_w_qb[None], "b_w_out": g_b_w_out[None],
        "kv_w_a": g_kv_w_a, "kv_norm_g": g_gkv.reshape(KV), "kv_w_b": g_kv_w_b,
    }
    weights = {
        "w_ada": (w_ada, m_w_ada, v_w_ada), "b_ada": (b_ada, m_b_ada, v_b_ada), "ln_g": (ln_g, m_ln_g, v_ln_g),
        "ln_b": (ln_b, m_ln_b, v_ln_b), "a_w_in": (a_w_in, m_a_w_in, v_a_w_in), "a_w_dw": (a_w_dw, m_a_w_dw, v_a_w_dw),
        "a_b_dw": (a_b_dw, m_a_b_dw, v_a_b_dw), "a_norm_g": (a_norm_g, m_a_norm_g, v_a_norm_g),
        "a_norm_b": (a_norm_b, m_a_norm_b, v_a_norm_b), "a_w_out": (a_w_out, m_a_w_out, v_a_w_out),
        "b_w_in": (b_w_in, m_b_w_in, v_b_w_in), "b_q_norm_g": (b_q_norm_g, m_b_q_norm_g, v_b_q_norm_g),
        "b_w_qb": (b_w_qb, m_b_w_qb, v_b_w_qb), "b_w_out": (b_w_out, m_b_w_out, v_b_w_out),
        "kv_w_a": (kv_w_a, m_kv_w_a, v_kv_w_a), "kv_norm_g": (kv_norm_g, m_kv_norm_g, v_kv_norm_g),
        "kv_w_b": (kv_w_b, m_kv_w_b, v_kv_w_b),
    }
    order = list(weights)
    big = [k for k in order if weights[k][0].size >= (1 << 16) and weights[k][0].shape[-1] % LANE == 0]
    small_names = [k for k in order if k not in big]
    upd = {}
    for k in big:
        w, m, v = weights[k]
        shp = w.shape
        two = (-1, shp[-1])
        d_, m_, v_ = _adamw(w.reshape(two), grads[k].reshape(two), m.reshape(two), v.reshape(two), "adamw_" + k)
        upd[k] = (grads[k].reshape(shp), d_.reshape(shp), m_.reshape(shp), v_.reshape(shp))
    sw, offs3 = _pack([weights[k][0] for k in small_names])
    sg, _ = _pack([grads[k] for k in small_names])
    sm, _ = _pack([weights[k][1] for k in small_names])
    sv, _ = _pack([weights[k][2] for k in small_names])
    sd, snm, snv = _adamw(sw, sg, sm, sv, "adamw_small")
    shapes3 = [weights[k][0].shape for k in small_names]
    for k, d_, m_, v_ in zip(small_names, _unpack(sd.reshape(-1), offs3, shapes3), _unpack(snm.reshape(-1), offs3, shapes3),
                             _unpack(snv.reshape(-1), offs3, shapes3)):
        upd[k] = (grads[k].reshape(weights[k][0].shape), d_, m_, v_)

    return (loss, grad_x, *[upd[k][0] for k in order], *[upd[k][1] for k in order], *[upd[k][2] for k in order],
            *[upd[k][3] for k in order])
```

```python
import math
from typing import Callable, NamedTuple

import jax
import jax.numpy as jnp
from jax import lax
from jax.experimental import pallas as pl
from jax.experimental.pallas import tpu as pltpu

F32 = jnp.float32
BF16 = jnp.bfloat16

LN_EPS = 1e-5
RMS_EPS = 1e-6
DEPTH = 2
DEEPNORM_ALPHA = (2.0 * DEPTH) ** 0.25
QK_NOPE_DIM = 128
QK_ROPE_DIM = 64
V_HEAD_DIM = 128
HEAD_PAD = 256
ROPE_BASE = 10000.0
ADAM_LR = 0.001
ADAM_B1 = 0.9
ADAM_B2 = 0.999
ADAM_EPS = 1e-08
ADAM_WD = 0.01
ADAM_STEP = 10

N_DEV = 8
N_CHIP = 4
LANE = 128
SUBLANE = 8
VMEM_LIMIT = 48 * 1024 * 1024
CONV_HALO = 32
MESH_ID = pl.DeviceIdType.MESH
NT = (((1,), (1,)), ((), ()))


def _cp(*sem):
    return pltpu.CompilerParams(dimension_semantics=sem, vmem_limit_bytes=VMEM_LIMIT)


def _tile(n, pref, align):
    if n <= pref:
        return n
    t = (pref // align) * align
    while t > align and n % t:
        t -= align
    assert n % t == 0, (n, pref, align)
    return t


def _silu(v):
    return v * jax.nn.sigmoid(v)


def _dsilu(v):
    s = jax.nn.sigmoid(v)
    return s * (1.0 + v * (1.0 - s))


class _Carry(NamedTuple):
    bufs: tuple
    out_shapes: tuple
    aliases: dict
    sem_shape: tuple
    start: Callable
    finish: Callable


def _pcall(body, *, name, grid, in_specs, out_specs, out_shape, scratch_shapes, sem, args, carry=None):
    if carry is None:
        return pl.pallas_call(body, name=name, grid=grid, in_specs=in_specs, out_specs=out_specs, out_shape=out_shape,
                              scratch_shapes=scratch_shapes, compiler_params=_cp(*sem))(*args)
    n_in, n_out, n_sc = len(in_specs), len(out_specs), len(scratch_shapes)
    nc_in, nc_out = len(carry.bufs), len(carry.out_shapes)

    def wrapped(*refs):
        core_in, c_in = refs[:n_in], refs[n_in:n_in + nc_in]
        core_out = refs[n_in + nc_in:n_in + nc_in + n_out]
        c_out = refs[n_in + nc_in + n_out:n_in + nc_in + n_out + nc_out]
        core_sc = refs[n_in + nc_in + n_out + nc_out:n_in + nc_in + n_out + nc_out + n_sc]
        send_sems, recv_sems = refs[-2:]
        first = pl.program_id(0) == 0
        last = pl.program_id(0) == grid[0] - 1
        for d in range(1, len(grid)):
            first = jnp.logical_and(first, pl.program_id(d) == 0)
            last = jnp.logical_and(last, pl.program_id(d) == grid[d] - 1)

        @pl.when(first)
        def _():
            carry.start(c_in, c_out, send_sems, recv_sems)

        body(*core_in, *core_out, *core_sc)

        @pl.when(last)
        def _():
            carry.finish(c_in, c_out, send_sems, recv_sems)

    any_spec = pl.BlockSpec(memory_space=pl.ANY)
    res = pl.pallas_call(
        wrapped, name=name, grid=grid, in_specs=list(in_specs) + [any_spec] * nc_in,
        out_specs=list(out_specs) + [any_spec] * nc_out, out_shape=list(out_shape) + list(carry.out_shapes),
        input_output_aliases={n_in + i: n_out + o for i, o in carry.aliases.items()},
        scratch_shapes=list(scratch_shapes) + [pltpu.SemaphoreType.DMA(carry.sem_shape), pltpu.SemaphoreType.DMA(carry.sem_shape)],
        compiler_params=_cp(*(("arbitrary",) * len(grid))),
    )(*args, *carry.bufs)
    return res[:n_out], res[n_out:]


def _mm(a, b, *, name, ta=False, tb=False, b_sh=False, o_sh=False, out_dtype=F32, tm=1024, tn=1024, tk=2048, carry=None):
    M, K = (a.shape[1], a.shape[0]) if ta else a.shape
    if b_sh:
        assert b.shape[0] == N_CHIP
        nq = b.shape[2]
        Kb, N = (nq * N_CHIP, b.shape[1]) if tb else (b.shape[1], nq * N_CHIP)
    else:
        Kb, N = (b.shape[1], b.shape[0]) if tb else b.shape
        nq = N // N_CHIP
    assert K == Kb, (a.shape, b.shape)
    tm = _tile(M, tm, LANE)
    tk = _tile(nq if (b_sh and tb) else K, tk, LANE)
    tn = _tile(nq if ((b_sh and not tb) or o_sh) else N, tn, LANE)
    nk = K // tk

    def body(a_ref, b_ref, o_ref, acc_ref):
        k = pl.program_id(2)

        @pl.when(k == 0)
        def _():
            acc_ref[...] = jnp.zeros_like(acc_ref)

        dn = (((0 if ta else 1,), (1 if tb else 0,)), ((), ()))
        acc_ref[...] += lax.dot_general(a_ref[...].astype(BF16), b_ref[...].astype(BF16), dn,
                                        preferred_element_type=F32)

        @pl.when(k == nk - 1)
        def _():
            o_ref[...] = acc_ref[...].astype(o_ref.dtype)

    a_spec = pl.BlockSpec((tk, tm), lambda i, j, k: (k, i)) if ta else pl.BlockSpec((tm, tk), lambda i, j, k: (i, k))
    if b_sh and not tb:
        per = nq // tn
        b_spec = pl.BlockSpec((None, tk, tn), lambda i, j, k: (j // per, k, j % per))
    elif b_sh and tb:
        per = nq // tk
        b_spec = pl.BlockSpec((None, tn, tk), lambda i, j, k: (k // per, j, k % per))
    elif tb:
        b_spec = pl.BlockSpec((tn, tk), lambda i, j, k: (j, k))
    else:
        b_spec = pl.BlockSpec((tk, tn), lambda i, j, k: (k, j))
    if o_sh:
        per_o = nq // tn
        o_spec = pl.BlockSpec((None, tm, tn), lambda i, j, k: (j // per_o, i, j % per_o))
        o_shape = jax.ShapeDtypeStruct((N_CHIP, M, nq), out_dtype)
    else:
        o_spec = pl.BlockSpec((tm, tn), lambda i, j, k: (i, j))
        o_shape = jax.ShapeDtypeStruct((M, N), out_dtype)
    res = _pcall(body, name=name, grid=(M // tm, N // tn, nk), in_specs=[a_spec, b_spec], out_specs=[o_spec],
                 out_shape=[o_shape], scratch_shapes=[pltpu.VMEM((tm, tn), F32)], sem=("parallel", "parallel", "arbitrary"),
                 args=(a, b), carry=carry)
    return res[0] if carry is None else (res[0][0], res[1])


def _mod(c_all, w_ada, b_sh, name):
    L, D, nq = w_ada.shape
    B = c_all.shape[0]
    tn = _tile(nq, 512, LANE)

    def body(c_ref, w_ref, b_ref, o_ref):
        sc = _silu(c_ref[...]).astype(BF16)
        o_ref[...] = jnp.dot(sc, w_ref[...].astype(BF16), preferred_element_type=F32) + b_ref[...]

    return pl.pallas_call(
        body, name=name, grid=(L, nq // tn),
        in_specs=[pl.BlockSpec((B, D), lambda l, j: (0, 0)), pl.BlockSpec((None, D, tn), lambda l, j: (l, 0, j)),
                  pl.BlockSpec((None, 1, tn), lambda l, j: (l, 0, j))],
        out_specs=pl.BlockSpec((None, B, tn), lambda l, j: (l, 0, j)),
        out_shape=jax.ShapeDtypeStruct((L, B, nq), F32), compiler_params=_cp("parallel", "parallel"),
    )(c_all, w_ada, b_sh)


def _wada_grad(c_all_t, dmod, name):
    D, B = c_all_t.shape
    L, _, nq = dmod.shape
    tm = _tile(D, 512, SUBLANE)
    tn = _tile(nq, 1024, LANE)

    def body(c_ref, d_ref, o_ref):
        sc = _silu(c_ref[...])
        dm = d_ref[...]
        acc = sc[:, 0:1] * dm[0:1, :]
        for b in range(1, B):
            acc = acc + sc[:, b:b + 1] * dm[b:b + 1, :]
        o_ref[...] = acc

    return pl.pallas_call(
        body, name=name, grid=(L, D // tm, nq // tn),
        in_specs=[pl.BlockSpec((tm, B), lambda l, i, j: (i, 0)), pl.BlockSpec((None, B, tn), lambda l, i, j: (l, 0, j))],
        out_specs=pl.BlockSpec((None, tm, tn), lambda l, i, j: (l, i, j)),
        out_shape=jax.ShapeDtypeStruct((L, D, nq), F32), compiler_params=_cp("parallel", "parallel", "parallel"),
    )(c_all_t, dmod)


ROW_TILE = 128


def _ln_stats(v):
    mu = jnp.mean(v, axis=-1, keepdims=True)
    vc = v - mu
    var = jnp.mean(vc * vc, axis=-1, keepdims=True)
    rstd = lax.rsqrt(var + LN_EPS)
    return vc * rstd, rstd


def _ln_bwd(dxhat, xhat, rstd):
    m1 = jnp.mean(dxhat, axis=-1, keepdims=True)
    m2 = jnp.mean(dxhat * xhat, axis=-1, keepdims=True)
    return rstd * (dxhat - m1 - xhat * m2)


def _row_spec(ts, D):
    return pl.BlockSpec((ts, D), lambda i: (i, 0))


def _vec_spec(D):
    return pl.BlockSpec((1, D), lambda i: (0, 0))


def _lnmod_fwd(x, scale, shift, name):
    S, D = x.shape
    ts = _tile(S, ROW_TILE, SUBLANE)

    def body(x_ref, sc_ref, sh_ref, h_ref):
        xn, _ = _ln_stats(x_ref[...])
        h_ref[...] = (xn * (1.0 + sc_ref[...]) + sh_ref[...]).astype(h_ref.dtype)

    return pl.pallas_call(
        body, name=name, grid=(S // ts,), in_specs=[_row_spec(ts, D), _vec_spec(D), _vec_spec(D)],
        out_specs=_row_spec(ts, D), out_shape=jax.ShapeDtypeStruct((S, D), BF16), compiler_params=_cp("parallel"),
    )(x, scale, shift)


def _lnmod_bwd(x, dh, scale, adds, name):
    S, D = x.shape
    ts = _tile(S, ROW_TILE, SUBLANE)
    na = len(adds)

    def body(*refs):
        x_ref, dh_ref, sc_ref = refs[:3]
        add_refs = refs[3:3 + na]
        dx_ref, dsc_ref, dsh_ref = refs[3 + na:]
        i = pl.program_id(0)

        @pl.when(i == 0)
        def _():
            dsc_ref[...] = jnp.zeros_like(dsc_ref)
            dsh_ref[...] = jnp.zeros_like(dsh_ref)

        xn, rstd = _ln_stats(x_ref[...])
        dh = dh_ref[...].astype(F32)
        dx = _ln_bwd(dh * (1.0 + sc_ref[...]), xn, rstd)
        for r in add_refs:
            dx = dx + r[...]
        dx_ref[...] = dx
        dsc_ref[...] += jnp.sum(dh * xn, axis=0, keepdims=True)
        dsh_ref[...] += jnp.sum(dh, axis=0, keepdims=True)

    return pl.pallas_call(
        body, name=name, grid=(S // ts,),
        in_specs=[_row_spec(ts, D), _row_spec(ts, D), _vec_spec(D)] + [_row_spec(ts, D)] * na,
        out_specs=[_row_spec(ts, D), _vec_spec(D), _vec_spec(D)],
        out_shape=[jax.ShapeDtypeStruct((S, D), F32), jax.ShapeDtypeStruct((1, D), F32), jax.ShapeDtypeStruct((1, D), F32)],
        compiler_params=_cp("arbitrary"),
    )(x, dh, scale, *adds)


def _resln_fwd(x, out, gate, g, b, name):
    S, D = x.shape
    ts = _tile(S, ROW_TILE, SUBLANE)

    def body(x_ref, o_ref, gt_ref, g_ref, b_ref, y_ref):
        r = DEEPNORM_ALPHA * x_ref[...] + (1.0 + gt_ref[...]) * o_ref[...]
        xhat, _ = _ln_stats(r)
        y_ref[...] = xhat * g_ref[...] + b_ref[...]

    return pl.pallas_call(
        body, name=name, grid=(S // ts,),
        in_specs=[_row_spec(ts, D), _row_spec(ts, D), _vec_spec(D), _vec_spec(D), _vec_spec(D)],
        out_specs=_row_spec(ts, D), out_shape=jax.ShapeDtypeStruct((S, D), F32), compiler_params=_cp("parallel"),
    )(x, out, gate, g, b)


def _resln_bwd(x, out, gate, g, b, dy_or_target, from_target, name):
    S, D = x.shape
    ts = _tile(S, ROW_TILE, SUBLANE)

    def body(x_ref, o_ref, gt_ref, g_ref, b_ref, t_ref, dxa_ref, dout_ref, dgt_ref, dg_ref, db_ref, *maybe_loss):
        i = pl.program_id(0)

        @pl.when(i == 0)
        def _():
            dgt_ref[...] = jnp.zeros_like(dgt_ref)
            dg_ref[...] = jnp.zeros_like(dg_ref)
            db_ref[...] = jnp.zeros_like(db_ref)
            if from_target:
                maybe_loss[0][...] = jnp.zeros_like(maybe_loss[0])

        ov = o_ref[...]
        g1 = 1.0 + gt_ref[...]
        r = DEEPNORM_ALPHA * x_ref[...] + g1 * ov
        xhat, rstd = _ln_stats(r)
        if from_target:
            err = xhat * g_ref[...] + b_ref[...] - t_ref[...]
            dy = err * (1.0 / D)
            maybe_loss[0][...] += jnp.sum(jnp.sum(err * err, axis=-1, keepdims=True), axis=0, keepdims=True) * (0.5 / D)
        else:
            dy = t_ref[...]
        dr = _ln_bwd(dy * g_ref[...], xhat, rstd)
        dxa_ref[...] = DEEPNORM_ALPHA * dr
        dout_ref[...] = (dr * g1).astype(dout_ref.dtype)
        dgt_ref[...] += jnp.sum(dr * ov, axis=0, keepdims=True)
        dg_ref[...] += jnp.sum(dy * xhat, axis=0, keepdims=True)
        db_ref[...] += jnp.sum(dy, axis=0, keepdims=True)

    vec = jax.ShapeDtypeStruct((1, D), F32)
    out_specs = [_row_spec(ts, D), _row_spec(ts, D), _vec_spec(D), _vec_spec(D), _vec_spec(D)]
    out_shape = [jax.ShapeDtypeStruct((S, D), F32), jax.ShapeDtypeStruct((S, D), BF16), vec, vec, vec]
    if from_target:
        out_specs.append(pl.BlockSpec((1, 1), lambda i: (0, 0)))
        out_shape.append(jax.ShapeDtypeStruct((1, 1), F32))
    return pl.pallas_call(
        body, name=name, grid=(S // ts,),
        in_specs=[_row_spec(ts, D), _row_spec(ts, D), _vec_spec(D), _vec_spec(D), _vec_spec(D), _row_spec(ts, D)],
        out_specs=out_specs, out_shape=out_shape, compiler_params=_cp("arbitrary"),
    )(x, out, gate, g, b, dy_or_target)


def _conv_tiles(S, C):
    tt = _tile(S, 256, CONV_HALO)
    tc = _tile(C, 512, LANE)
    return tt, tc


def _shift_scratch(tt, tc):
    return pltpu.VMEM((SUBLANE - 1, tt + CONV_HALO - SUBLANE, tc), F32)


def _fill_shifts(src_ref, sh_ref, tt):
    rows = tt + CONV_HALO - SUBLANE
    for b in range(1, SUBLANE):
        sh_ref[b - 1] = src_ref[b:b + rows, :]


def _shifted(src_ref, sh_ref, q, tt):
    a8, b8 = divmod(q, SUBLANE)
    if b8 == 0:
        return src_ref[q:q + tt, :]
    return sh_ref[b8 - 1, a8 * SUBLANE:a8 * SUBLANE + tt, :]


def _conv1_fwd(u, w_dw, b_dw, name, carry=None):
    S, C3 = u.shape
    C = C3 // 3
    KS = w_dw.shape[0]
    tt, tc = _conv_tiles(S, C)
    ncb = C // tc
    hb = tt // CONV_HALO
    lead = CONV_HALO - (KS - 1)

    def body(a_ref, g_ref, ah_ref, gh_ref, w_ref, b_ref, o_ref, pad_ref, sh_ref):
        i = pl.program_id(0)
        halo = ah_ref[...] * jax.nn.sigmoid(gh_ref[...])
        pad_ref[0:CONV_HALO, :] = jnp.where(i > 0, halo, 0.0)
        pad_ref[CONV_HALO:, :] = a_ref[...] * jax.nn.sigmoid(g_ref[...])
        _fill_shifts(pad_ref, sh_ref, tt)
        acc = jnp.broadcast_to(b_ref[...], (tt, tc))
        for k in range(KS):
            acc = acc + w_ref[k:k + 1, :] * _shifted(pad_ref, sh_ref, lead + k, tt)
        o_ref[...] = acc

    main = lambda off: pl.BlockSpec((tt, tc), lambda i, j: (i, off + j))
    halo = lambda off: pl.BlockSpec((CONV_HALO, tc), lambda i, j: (jnp.maximum(i * hb - 1, 0), off + j))
    res = _pcall(
        body, name=name, grid=(S // tt, ncb),
        in_specs=[main(0), main(ncb), halo(0), halo(ncb), pl.BlockSpec((KS, tc), lambda i, j: (0, j)),
                  pl.BlockSpec((1, tc), lambda i, j: (0, j))],
        out_specs=[pl.BlockSpec((tt, tc), lambda i, j: (i, j))], out_shape=[jax.ShapeDtypeStruct((S, C), F32)],
        scratch_shapes=[pltpu.VMEM((CONV_HALO + tt, tc), F32), _shift_scratch(tt, tc)], sem=("parallel", "parallel"),
        args=(u, u, u, u, w_dw, b_dw), carry=carry)
    return res[0] if carry is None else (res[0][0], res[1])


def _conv2_fwd(v2, u, g_cn, b_cn, name):
    S, C = v2.shape
    ts = _tile(S, ROW_TILE, SUBLANE)

    def body(v_ref, z_ref, g_ref, b_ref, p_ref):
        xhat, _ = _ln_stats(v_ref[...])
        v3 = xhat * g_ref[...] + b_ref[...]
        p_ref[...] = (_silu(v3) * _silu(z_ref[...])).astype(p_ref.dtype)

    return pl.pallas_call(
        body, name=name, grid=(S // ts,),
        in_specs=[_row_spec(ts, C), pl.BlockSpec((ts, C), lambda i: (i, 2)), _vec_spec(C), _vec_spec(C)],
        out_specs=_row_spec(ts, C), out_shape=jax.ShapeDtypeStruct((S, C), BF16), compiler_params=_cp("parallel"),
    )(v2, u, g_cn, b_cn)


def _conv2_bwd(dp, v2, u, g_cn, b_cn, name):
    S, C = v2.shape
    ts = _tile(S, ROW_TILE, SUBLANE)

    def body(dp_ref, v_ref, z_ref, g_ref, b_ref, dv_ref, dz_ref, dg_ref, db_ref):
        i = pl.program_id(0)

        @pl.when(i == 0)
        def _():
            dg_ref[...] = jnp.zeros_like(dg_ref)
            db_ref[...] = jnp.zeros_like(db_ref)

        dp = dp_ref[...].astype(F32)
        z = z_ref[...]
        xhat, rstd = _ln_stats(v_ref[...])
        v3 = xhat * g_ref[...] + b_ref[...]
        dz_ref[...] = (dp * _silu(v3) * _dsilu(z)).astype(dz_ref.dtype)
        dv3 = dp * _silu(z) * _dsilu(v3)
        dv_ref[...] = _ln_bwd(dv3 * g_ref[...], xhat, rstd)
        dg_ref[...] += jnp.sum(dv3 * xhat, axis=0, keepdims=True)
        db_ref[...] += jnp.sum(dv3, axis=0, keepdims=True)

    vec = jax.ShapeDtypeStruct((1, C), F32)
    return pl.pallas_call(
        body, name=name, grid=(S // ts,),
        in_specs=[_row_spec(ts, C), _row_spec(ts, C), pl.BlockSpec((ts, C), lambda i: (i, 2)), _vec_spec(C), _vec_spec(C)],
        out_specs=[_row_spec(ts, C), _row_spec(ts, C), _vec_spec(C), _vec_spec(C)],
        out_shape=[jax.ShapeDtypeStruct((S, C), F32), jax.ShapeDtypeStruct((S, C), BF16), vec, vec],
        compiler_params=_cp("arbitrary"),
    )(dp, v2, u, g_cn, b_cn)


def _conv1_bwd(dv2, u, w_dw, name, carry=None):
    S, C = dv2.shape
    KS = w_dw.shape[0]
    tt, tc = _conv_tiles(S, C)
    ncb = C // tc
    nt = S // tt
    hb = tt // CONV_HALO
    lead = CONV_HALO - (KS - 1)

    def body(dv_ref, dvh_ref, a_ref, g_ref, ah_ref, gh_ref, w_ref, da_ref, dg_ref, dw_ref, db_ref, pad_ref, fpad_ref,
             sh_ref, fsh_ref):
        i = pl.program_id(1)

        @pl.when(i == 0)
        def _():
            dw_ref[...] = jnp.zeros_like(dw_ref)
            db_ref[...] = jnp.zeros_like(db_ref)

        dv = dv_ref[...]
        a = a_ref[...]
        sg = jax.nn.sigmoid(g_ref[...])
        halo = ah_ref[...] * jax.nn.sigmoid(gh_ref[...])
        pad_ref[0:CONV_HALO, :] = jnp.where(i > 0, halo, 0.0)
        pad_ref[CONV_HALO:, :] = a * sg
        fpad_ref[0:tt, :] = dv
        fpad_ref[tt:, :] = jnp.where(i < nt - 1, dvh_ref[...], 0.0)
        _fill_shifts(pad_ref, sh_ref, tt)
        _fill_shifts(fpad_ref, fsh_ref, tt)
        dv1 = jnp.zeros((tt, tc), F32)
        for k in range(KS):
            dv1 = dv1 + w_ref[k:k + 1, :] * _shifted(fpad_ref, fsh_ref, KS - 1 - k, tt)
            dw_ref[k:k + 1, :] += jnp.sum(dv * _shifted(pad_ref, sh_ref, lead + k, tt), axis=0, keepdims=True)
        db_ref[...] += jnp.sum(dv, axis=0, keepdims=True)
        da_ref[...] = (dv1 * sg).astype(da_ref.dtype)
        dg_ref[...] = (dv1 * a * sg * (1.0 - sg)).astype(dg_ref.dtype)

    main = lambda off: pl.BlockSpec((tt, tc), lambda j, i: (i, off + j))
    halo = lambda off: pl.BlockSpec((CONV_HALO, tc), lambda j, i: (jnp.maximum(i * hb - 1, 0), off + j))
    fhalo = pl.BlockSpec((CONV_HALO, tc), lambda j, i: (jnp.minimum((i + 1) * hb, nt * hb - 1), j))
    res = _pcall(
        body, name=name, grid=(ncb, nt),
        in_specs=[main(0), fhalo, main(0), main(ncb), halo(0), halo(ncb), pl.BlockSpec((KS, tc), lambda j, i: (0, j))],
        out_specs=[main(0), main(0), pl.BlockSpec((KS, tc), lambda j, i: (0, j)), pl.BlockSpec((1, tc), lambda j, i: (0, j))],
        out_shape=[jax.ShapeDtypeStruct((S, C), BF16), jax.ShapeDtypeStruct((S, C), BF16),
                   jax.ShapeDtypeStruct((KS, C), F32), jax.ShapeDtypeStruct((1, C), F32)],
        scratch_shapes=[pltpu.VMEM((CONV_HALO + tt, tc), F32), pltpu.VMEM((tt + CONV_HALO, tc), F32),
                        _shift_scratch(tt, tc), _shift_scratch(tt, tc)],
        sem=("parallel", "arbitrary"), args=(dv2, dv2, u, u, u, u, w_dw), carry=carry)
    return res


def _rope_tables(S):
    half = QK_ROPE_DIM // 2
    inv_freq = ROPE_BASE ** (-jnp.arange(half, dtype=F32) / half)
    ang = jnp.arange(S, dtype=jnp.int32).astype(F32)[:, None] * inv_freq[None, :]
    cos, sin, z = jnp.cos(ang), jnp.sin(ang), jnp.zeros((S, half), F32)
    tc = jnp.concatenate([cos, cos, z, z], axis=1)
    t1 = jnp.concatenate([-sin, z, z, z], axis=1)
    t2 = jnp.concatenate([z, sin, z, z], axis=1)
    return tc, t1, t2


def _rope128(r, tc, t1, t2, sign):
    return r * tc + sign * (pltpu.roll(r, LANE - QK_ROPE_DIM // 2, 1) * t1 + pltpu.roll(r, QK_ROPE_DIM // 2, 1) * t2)


def _rms_fwd(x, width, g, name):
    S = x.shape[0]
    ts = _tile(S, 256, SUBLANE)

    def body(x_ref, g_ref, o_ref):
        xv = x_ref[...]
        rr = lax.rsqrt(jnp.mean(xv * xv, axis=-1, keepdims=True) + RMS_EPS)
        o_ref[...] = (xv * rr * g_ref[...]).astype(o_ref.dtype)

    return pl.pallas_call(
        body, name=name, grid=(S // ts,), in_specs=[_row_spec(ts, width), _vec_spec(width)],
        out_specs=_row_spec(ts, width), out_shape=jax.ShapeDtypeStruct((S, width), BF16), compiler_params=_cp("parallel"),
    )(x, g)


def _rms_bwd_math(xv, dy, g):
    n = xv.shape[-1]
    rr = lax.rsqrt(jnp.mean(xv * xv, axis=-1, keepdims=True) + RMS_EPS)
    dyg = dy * g
    dx = rr * dyg - xv * (rr * rr * rr) * (jnp.sum(dyg * xv, axis=-1, keepdims=True) * (1.0 / n))
    dg = jnp.sum(dy * xv * rr, axis=0, keepdims=True)
    return dx, dg


def _rms_bwd(x, width, dy, g, du, name):
    S = x.shape[0]
    ts = _tile(S, 256, SUBLANE)

    def body(x_ref, dy_ref, g_ref, du_in, dx_ref, dg_ref):
        del du_in
        i = pl.program_id(0)

        @pl.when(i == 0)
        def _():
            dg_ref[...] = jnp.zeros_like(dg_ref)

        dx, dg = _rms_bwd_math(x_ref[...], dy_ref[...].astype(F32), g_ref[...])
        dx_ref[...] = dx.astype(dx_ref.dtype)
        dg_ref[...] += dg

    return pl.pallas_call(
        body, name=name, grid=(S // ts,),
        in_specs=[_row_spec(ts, width), _row_spec(ts, width), _vec_spec(width), pl.BlockSpec(memory_space=pl.ANY)],
        out_specs=[_row_spec(ts, width), _vec_spec(width)],
        out_shape=[jax.ShapeDtypeStruct(du.shape, du.dtype), jax.ShapeDtypeStruct((1, width), F32)],
        input_output_aliases={3: 0}, compiler_params=_cp("arbitrary"),
    )(x, dy, g, du)


def _kvprep_fwd(kva, g_kv, tabs, name):
    S, W = kva.shape
    KV = W - LANE
    ts = _tile(S, 256, SUBLANE)

    def body(x_ref, g_ref, tc_ref, t1_ref, t2_ref, c_ref, r_ref):
        xv = x_ref[:, 0:KV]
        rr = lax.rsqrt(jnp.mean(xv * xv, axis=-1, keepdims=True) + RMS_EPS)
        c_ref[...] = (xv * rr * g_ref[...]).astype(c_ref.dtype)
        r_ref[...] = _rope128(x_ref[:, KV:], tc_ref[...], t1_ref[...], t2_ref[...], 1.0).astype(r_ref.dtype)

    tab = _row_spec(ts, LANE)
    return pl.pallas_call(
        body, name=name, grid=(S // ts,), in_specs=[_row_spec(ts, W), _vec_spec(KV), tab, tab, tab],
        out_specs=[_row_spec(ts, KV), _row_spec(ts, LANE)],
        out_shape=[jax.ShapeDtypeStruct((S, KV), BF16), jax.ShapeDtypeStruct((S, LANE), BF16)], compiler_params=_cp("parallel"),
    )(kva, g_kv, *tabs)


def _kvprep_bwd(kva, dckv, dkr_h, g_kv, tabs, name):
    S, W = kva.shape
    KV = W - LANE
    H = dkr_h.shape[1] // LANE
    ts = _tile(S, 256, SUBLANE)

    def body(x_ref, dc_ref, dr_ref, g_ref, tc_ref, t1_ref, t2_ref, o_ref, dg_ref):
        i = pl.program_id(0)

        @pl.when(i == 0)
        def _():
            dg_ref[...] = jnp.zeros_like(dg_ref)

        dx, dg = _rms_bwd_math(x_ref[:, 0:KV], dc_ref[...].astype(F32), g_ref[...])
        o_ref[:, 0:KV] = dx.astype(o_ref.dtype)
        dg_ref[...] += dg
        dr = dr_ref[:, 0:LANE]
        for h in range(1, H):
            dr = dr + dr_ref[:, h * LANE:(h + 1) * LANE]
        o_ref[:, KV:] = _rope128(dr, tc_ref[...], t1_ref[...], t2_ref[...], -1.0).astype(o_ref.dtype)

    tab = _row_spec(ts, LANE)
    return pl.pallas_call(
        body, name=name, grid=(S // ts,),
        in_specs=[_row_spec(ts, W), _row_spec(ts, KV), _row_spec(ts, H * LANE), _vec_spec(KV), tab, tab, tab],
        out_specs=[_row_spec(ts, W), _vec_spec(KV)],
        out_shape=[jax.ShapeDtypeStruct((S, W), BF16), jax.ShapeDtypeStruct((1, KV), F32)], compiler_params=_cp("arbitrary"),
    )(kva, dckv, dkr_h, g_kv, *tabs)


ROPE_GROUP = 8


def _qrope_bwd(dq, tabs, name):
    S, W = dq.shape
    H = W // HEAD_PAD
    G = math.gcd(H, ROPE_GROUP)
    ts = _tile(S, 512, 2 * SUBLANE)

    def body(q_ref, tc_ref, t1_ref, t2_ref, o_ref):
        for g in range(G):
            lo = g * HEAD_PAD
            o_ref[:, lo:lo + LANE] = q_ref[:, lo:lo + LANE].astype(o_ref.dtype)
            o_ref[:, lo + LANE:lo + HEAD_PAD] = _rope128(
                q_ref[:, lo + LANE:lo + HEAD_PAD], tc_ref[...], t1_ref[...], t2_ref[...], -1.0).astype(o_ref.dtype)

    tab = pl.BlockSpec((ts, LANE), lambda i, h: (i, 0))
    blk = pl.BlockSpec((ts, G * HEAD_PAD), lambda i, h: (i, h))
    return pl.pallas_call(
        body, name=name, grid=(S // ts, H // G), in_specs=[blk, tab, tab, tab], out_specs=blk,
        out_shape=jax.ShapeDtypeStruct((S, W), BF16), compiler_params=_cp("parallel", "parallel"),
    )(dq, *tabs)


ATT_TILE = 512
LOG2E = math.log2(math.e)
LN2 = math.log(2.0)
ATT_FWD_HEADS = 4
ATT_BWD_KEY_TILES = 2


def _causal_mask(nk, nq, q0):
    r = lax.broadcasted_iota(jnp.int32, (nk, nq), 0)
    c = lax.broadcasted_iota(jnp.int32, (nk, nq), 1)
    return c + q0 >= r


def _attn_fwd(qraw, kvh, krp, u, zoff, tabs, scale, name, carry=None):
    S, W = qraw.shape
    H = W // HEAD_PAD
    G = math.gcd(H, ATT_FWD_HEADS)
    t = _tile(S, ATT_TILE, LANE)
    nq = S // t
    assert zoff % (G * V_HEAD_DIM) == 0
    zb = zoff // (G * V_HEAD_DIM)
    qc = scale * LOG2E
    t2 = t // 2

    def body(q_ref, tc_ref, t1_ref, t2_ref, kv_ref, kr_ref, z_ref, o_ref, p_ref, qr_ref, lse_ref, vt_sc, m_sc, l_sc, acc_sc):
        i = pl.program_id(1)

        @pl.when(i == 0)
        def _():
            for g in range(G):
                lo = g * HEAD_PAD + QK_NOPE_DIM
                for jj in range(nq):
                    vt_sc[g, jj] = kv_ref[jj * t:(jj + 1) * t, lo:lo + V_HEAD_DIM].astype(F32).T.astype(BF16)

        qs = []
        for g in range(G):
            lo = g * HEAD_PAD
            qrot = _rope128(q_ref[:, lo + LANE:lo + HEAD_PAD], tc_ref[...], t1_ref[...], t2_ref[...], 1.0)
            q = jnp.concatenate([(q_ref[:, lo:lo + LANE] * qc).astype(BF16), (qrot * qc).astype(BF16)], axis=-1)
            qr_ref[:, lo:lo + HEAD_PAD] = q
            qs.append(q)
        m_sc[...] = jnp.full_like(m_sc, -jnp.inf)
        l_sc[...] = jnp.zeros_like(l_sc)
        acc_sc[...] = jnp.zeros_like(acc_sc)

        def update(g, cols, st, vt):
            m_old = m_sc[g, :, cols]
            m_new = jnp.maximum(m_old, jnp.max(st, axis=0, keepdims=True))
            a = jnp.exp2(m_old - m_new)
            pt = jnp.exp2(st - m_new)
            l_sc[g, :, cols] = a * l_sc[g, :, cols] + jnp.sum(pt, axis=0, keepdims=True)
            acc_sc[g, :, cols] = a * acc_sc[g, :, cols] + jnp.dot(vt, pt.astype(BF16), preferred_element_type=F32)
            m_sc[g, :, cols] = m_new

        def keys(g, off):
            return jnp.concatenate([kv_ref[pl.ds(off, t), g * HEAD_PAD:g * HEAD_PAD + QK_NOPE_DIM],
                                    kr_ref[pl.ds(off, t), :]], axis=-1)

        def loop_body(j, carry):
            off = pl.multiple_of(j * t, t)
            for g in range(G):
                k = keys(g, off)
                vt = vt_sc[g, j]
                for cols in (slice(0, t2), slice(t2, t)):
                    st = lax.dot_general(k, qs[g][cols], NT, preferred_element_type=F32)
                    update(g, cols, st, vt)
            return carry

        lax.fori_loop(0, i, loop_body, 0)
        off = pl.multiple_of(i * t, t)
        for g in range(G):
            k = keys(g, off)
            vt = vt_sc[g, i]
            st0 = lax.dot_general(k[0:t2], qs[g][0:t2], NT, preferred_element_type=F32)
            update(g, slice(0, t2), jnp.where(_causal_mask(t2, t2, 0), st0, -jnp.inf), vt[:, 0:t2])
            st1 = lax.dot_general(k, qs[g][t2:], NT, preferred_element_type=F32)
            update(g, slice(t2, t), jnp.where(_causal_mask(t, t2, t2), st1, -jnp.inf), vt)
        for g in range(G):
            ov = (acc_sc[g] / l_sc[g]).T
            cols = slice(g * V_HEAD_DIM, (g + 1) * V_HEAD_DIM)
            o_ref[:, cols] = ov
            p_ref[:, cols] = (ov * _silu(z_ref[:, cols])).astype(p_ref.dtype)
            lse_ref[g] = m_sc[g] + jnp.log2(l_sc[g])

    tab = pl.BlockSpec((t, LANE), lambda h, i: (i, 0))
    head = pl.BlockSpec((t, G * V_HEAD_DIM), lambda h, i: (i, h))
    return _pcall(
        body, name=name, grid=(H // G, nq),
        in_specs=[pl.BlockSpec((t, G * HEAD_PAD), lambda h, i: (i, h)), tab, tab, tab,
                  pl.BlockSpec((S, G * HEAD_PAD), lambda h, i: (0, h)), pl.BlockSpec((S, LANE), lambda h, i: (0, 0)),
                  pl.BlockSpec((t, G * V_HEAD_DIM), lambda h, i: (i, zb + h))],
        out_specs=[head, head, pl.BlockSpec((t, G * HEAD_PAD), lambda h, i: (i, h)),
                   pl.BlockSpec((G, None, 1, t), lambda h, i: (h, i, 0, 0))],
        out_shape=[jax.ShapeDtypeStruct((S, H * V_HEAD_DIM), F32), jax.ShapeDtypeStruct((S, H * V_HEAD_DIM), BF16),
                   jax.ShapeDtypeStruct((S, W), BF16), jax.ShapeDtypeStruct((H, nq, 1, t), F32)],
        scratch_shapes=[pltpu.VMEM((G, nq, V_HEAD_DIM, t), BF16), pltpu.VMEM((G, 1, t), F32), pltpu.VMEM((G, 1, t), F32),
                        pltpu.VMEM((G, V_HEAD_DIM, t), F32)],
        sem=("parallel", "arbitrary"), args=(qraw, *tabs, kvh, krp, u), carry=carry)


def _attn_bwd(qr, kvh, krp, dp, o, u, zoff, lse, scale, name):
    S, W = qr.shape
    H = W // HEAD_PAD
    U = u.shape[1]
    t = _tile(S, ATT_TILE, LANE)
    nq = S // t
    R = math.gcd(nq, ATT_BWD_KEY_TILES)
    tk = R * t
    nk = nq // R
    zb = zoff // V_HEAD_DIM

    def body(q_ref, kv_ref, kr_ref, dp_ref, o_ref, z_ref, lse_ref, dq_ref, dkv_ref, dkr_ref, dz_ref,
             do_sc, dl_sc, dqt_sc, dk_sc, dv_sc):
        j = pl.program_id(1)

        @pl.when(j == 0)
        def _():
            for ii in range(nq):
                rows = slice(ii * t, (ii + 1) * t)
                dpv, ov, z = dp_ref[rows, :], o_ref[rows, :], z_ref[rows, :]
                dov = dpv * _silu(z)
                do_sc[ii] = dov.astype(BF16)
                dz_ref[rows, :] = (dpv * ov * _dsilu(z)).astype(dz_ref.dtype)
                dl_sc[ii] = jnp.sum((dov * ov).T, axis=0, keepdims=True)
                dqt_sc[ii] = jnp.zeros((HEAD_PAD, t), F32)

        kvb = kv_ref[...]
        k = jnp.concatenate([kvb[:, 0:QK_NOPE_DIM], kr_ref[...]], axis=-1)
        v = kvb[:, QK_NOPE_DIM:]
        kt = k.astype(F32).T.astype(BF16)
        dk_sc[...] = jnp.zeros_like(dk_sc)
        dv_sc[...] = jnp.zeros_like(dv_sc)

        def part(i, nkr, mask):
            q = q_ref[pl.ds(pl.multiple_of(i * t, t), t), :]
            dov = do_sc[i]
            st = lax.dot_general(k[0:nkr], q, NT, preferred_element_type=F32)
            pt = jnp.exp2(st - lse_ref[i])
            if mask is not None:
                pt = jnp.where(mask, pt, 0.0)
            dv_sc[0:nkr, :] += jnp.dot(pt.astype(BF16), dov, preferred_element_type=F32)
            dpt = lax.dot_general(v[0:nkr], dov, NT, preferred_element_type=F32)
            dst = (pt * (dpt - dl_sc[i])).astype(BF16)
            dk_sc[0:nkr, :] += jnp.dot(dst, q, preferred_element_type=F32)
            dqt_sc[i] += jnp.dot(kt[:, 0:nkr], dst, preferred_element_type=F32)

        def loop_body(i, carry):
            part(i, tk, None)
            return carry

        for r in range(R):
            part(R * j + r, (r + 1) * t, _causal_mask((r + 1) * t, t, r * t))
        lax.fori_loop(R * (j + 1), nq, loop_body, 0)
        dkv_ref[:, 0:QK_NOPE_DIM] = (dk_sc[:, 0:QK_NOPE_DIM] * LN2).astype(dkv_ref.dtype)
        dkv_ref[:, QK_NOPE_DIM:] = dv_sc[...].astype(dkv_ref.dtype)
        dkr_ref[...] = dk_sc[:, QK_NOPE_DIM:] * LN2

        @pl.when(j == nk - 1)
        def _():
            for ii in range(nq):
                dq_ref[ii * t:(ii + 1) * t, :] = dqt_sc[ii].T * scale

    whole = lambda w, off: pl.BlockSpec((S, w), lambda h, j: (0, off + h))
    return pl.pallas_call(
        body, name=name, grid=(H, nk),
        in_specs=[whole(HEAD_PAD, 0), pl.BlockSpec((tk, HEAD_PAD), lambda h, j: (j, h)),
                  pl.BlockSpec((tk, LANE), lambda h, j: (j, 0)), whole(V_HEAD_DIM, 0), whole(V_HEAD_DIM, 0),
                  whole(V_HEAD_DIM, zb), pl.BlockSpec((None, nq, 1, t), lambda h, j: (h, 0, 0, 0))],
        out_specs=[whole(HEAD_PAD, 0), pl.BlockSpec((tk, HEAD_PAD), lambda h, j: (j, h)),
                   pl.BlockSpec((tk, LANE), lambda h, j: (j, h)), whole(V_HEAD_DIM, zb)],
        out_shape=[jax.ShapeDtypeStruct((S, W), F32), jax.ShapeDtypeStruct((S, W), BF16),
                   jax.ShapeDtypeStruct((S, H * LANE), F32), jax.ShapeDtypeStruct((S, U), BF16)],
        scratch_shapes=[pltpu.VMEM((nq, t, V_HEAD_DIM), BF16), pltpu.VMEM((nq, 1, t), F32),
                        pltpu.VMEM((nq, HEAD_PAD, t), F32), pltpu.VMEM((tk, HEAD_PAD), F32), pltpu.VMEM((tk, V_HEAD_DIM), F32)],
        compiler_params=_cp("parallel", "arbitrary"),
    )(qr, kvh, krp, dp, o, u, lse)


def _adamw_math(w, g, m, v):
    m = ADAM_B1 * m + (1.0 - ADAM_B1) * g
    v = ADAM_B2 * v + (1.0 - ADAM_B2) * (g * g)
    m_hat = m / (1.0 - ADAM_B1 ** ADAM_STEP)
    v_hat = v / (1.0 - ADAM_B2 ** ADAM_STEP)
    delta = -ADAM_LR * (m_hat / (jnp.sqrt(v_hat) + ADAM_EPS) + ADAM_WD * w)
    return delta, m, v


def _adamw(w, g, m, v, name):
    R, C = w.shape
    tr = _tile(R, 256, SUBLANE)
    tc = _tile(C, 1024, LANE)

    def body(w_ref, g_ref, m_ref, v_ref, d_ref, nm_ref, nv_ref):
        d, nm, nv = _adamw_math(w_ref[...], g_ref[...], m_ref[...], v_ref[...])
        d_ref[...] = d
        nm_ref[...] = nm
        nv_ref[...] = nv

    blk = pl.BlockSpec((tr, tc), lambda i, j: (i, j))
    sh = jax.ShapeDtypeStruct((R, C), F32)
    return pl.pallas_call(
        body, name=name, grid=(R // tr, C // tc), in_specs=[blk] * 4, out_specs=[blk] * 3, out_shape=[sh] * 3,
        compiler_params=_cp("parallel", "parallel"),
    )(w, g, m, v)


def _sum_leading(x, name):
    n, R, C = x.shape
    tr = _tile(R, 512, SUBLANE)

    def body(x_ref, o_ref):
        acc = x_ref[0]
        for k in range(1, n):
            acc = acc + x_ref[k]
        o_ref[...] = acc

    return pl.pallas_call(
        body, name=name, grid=(R // tr,), in_specs=[pl.BlockSpec((n, tr, C), lambda i: (0, i, 0))],
        out_specs=pl.BlockSpec((tr, C), lambda i: (i, 0)), out_shape=jax.ShapeDtypeStruct((R, C), F32),
        compiler_params=_cp("parallel"),
    )(x)


def _pair_add(full, recv, c_idx, name):
    n, R, C = full.shape
    h = R // 2
    tr = _tile(h, 256, 2 * SUBLANE)
    tc = _tile(C, 1024, LANE)
    nb = h // tr

    def body(c_ref, a_ref, b_ref, o_ref):
        del c_ref
        o_ref[...] = (a_ref[...].astype(F32) + b_ref[...].astype(F32)).astype(o_ref.dtype)

    return pl.pallas_call(
        body, name=name,
        grid_spec=pltpu.PrefetchScalarGridSpec(
            num_scalar_prefetch=1, grid=(n, nb, C // tc),
            in_specs=[pl.BlockSpec((None, tr, tc), lambda k, i, j, c: (k, c[0] * nb + i, j)),
                      pl.BlockSpec((None, tr, tc), lambda k, i, j, c: (k, i, j))],
            out_specs=pl.BlockSpec((None, tr, tc), lambda k, i, j, c: (k, i, j))),
        out_shape=jax.ShapeDtypeStruct((n, h, C), BF16), compiler_params=_cp("parallel", "parallel", "parallel"),
    )(c_idx, full, recv)


def _chip_sum(pair, recv, idx, name):
    n, h, C = pair.shape
    tr = _tile(h, 256, 2 * SUBLANE)
    tc = _tile(C, 1024, LANE)
    nb = h // tr

    def body(c_ref, chip_ref, a_ref, b_ref, o_ref):
        del c_ref, chip_ref
        acc = a_ref[...].astype(F32)
        for k in range(N_CHIP - 1):
            acc = acc + b_ref[k].astype(F32)
        o_ref[...] = acc

    return pl.pallas_call(
        body, name=name,
        grid_spec=pltpu.PrefetchScalarGridSpec(
            num_scalar_prefetch=2, grid=(nb, C // tc),
            in_specs=[pl.BlockSpec((None, tr, tc), lambda i, j, c, chip: (chip[0], i, j)),
                      pl.BlockSpec((N_CHIP - 1, tr, tc), lambda i, j, c, chip: (0, i, j))],
            out_specs=pl.BlockSpec((tr, tc), lambda i, j, c, chip: (c[0] * nb + i, j))),
        out_shape=jax.ShapeDtypeStruct((2 * h, C), F32), compiler_params=_cp("parallel", "parallel"),
    )(idx[0], idx[1], pair, recv)


def _cast_place(w, idx, name):
    R, C = w.shape
    h = R // 2
    tr = _tile(h, 256, 2 * SUBLANE)
    tc = _tile(C, 1024, LANE)
    nb = h // tr

    def body(c_ref, chip_ref, w_ref, o_ref):
        del c_ref, chip_ref
        o_ref[...] = w_ref[...].astype(o_ref.dtype)

    return pl.pallas_call(
        body, name=name,
        grid_spec=pltpu.PrefetchScalarGridSpec(
            num_scalar_prefetch=2, grid=(nb, C // tc),
            in_specs=[pl.BlockSpec((tr, tc), lambda i, j, c, chip: (c[0] * nb + i, j))],
            out_specs=pl.BlockSpec((None, tr, tc), lambda i, j, c, chip: (chip[0], c[0] * nb + i, j))),
        out_shape=jax.ShapeDtypeStruct((N_CHIP, R, C), BF16), compiler_params=_cp("parallel", "parallel"),
    )(idx[0], idx[1], w)


def _coords():
    return lax.axis_index("x"), lax.axis_index("y"), lax.axis_index("c")


def _allgather_small(x_shard, name):
    m_per, n = x_shard.shape

    def body(x_ref, out_ref, send_sems, recv_sems, local_sem):
        x, y, c = _coords()
        me, sibling = (x, y, c), (x, y, 1 - c)
        chips = [(1 - x, y), (x, 1 - y), (1 - x, 1 - y)]

        def rows(px, py, pc):
            return out_ref.at[pl.ds((4 * px + 2 * py + pc) * m_per, m_per), :]

        def copy(k, block, to, src=None):
            return pltpu.make_async_remote_copy(
                src_ref=rows(*block) if src is None else src, dst_ref=rows(*block), send_sem=send_sems.at[k],
                recv_sem=recv_sems.at[k], device_id=to, device_id_type=MESH_ID)

        mine = pltpu.make_async_copy(x_ref, rows(*me), local_sem)
        mine.start()
        first = [copy(0, me, sibling, src=x_ref)]
        first += [copy(1 + j, me, (*chip, c), src=x_ref) for j, chip in enumerate(chips)]
        for cp in first:
            cp.start()
        passed = [copy(4 + j, (*chip, c), sibling) for j, chip in enumerate(chips)]
        for j, chip in enumerate(chips):
            copy(1 + j, (*chip, c), me).wait_recv()
            passed[j].start()
        copy(0, sibling, me).wait_recv()
        for j, chip in enumerate(chips):
            copy(4 + j, (*chip, 1 - c), me).wait_recv()
        for cp in first + passed:
            cp.wait_send()
        mine.wait()

    return pl.pallas_call(
        body, name=name, out_shape=jax.ShapeDtypeStruct((N_DEV * m_per, n), x_shard.dtype),
        in_specs=[pl.BlockSpec(memory_space=pltpu.VMEM)], out_specs=pl.BlockSpec(memory_space=pltpu.VMEM),
        scratch_shapes=[pltpu.SemaphoreType.DMA((7,)), pltpu.SemaphoreType.DMA((7,)), pltpu.SemaphoreType.DMA],
        compiler_params=pltpu.CompilerParams(vmem_limit_bytes=VMEM_LIMIT),
    )(x_shard)


def _allgather_carry(bufs):
    n = len(bufs)

    def plan(outs, send_sems, recv_sems):
        x, y, c = _coords()
        me, sibling = (x, y, c), (x, y, 1 - c)
        chips = [(1 - x, y), (x, 1 - y), (1 - x, 1 - y)]

        def win(a, px, py, pc):
            h = bufs[a].shape[1] // 2
            return outs[a].at[2 * px + py, pl.ds(pc * h, h), :]

        def copy(a, k, block, to):
            return pltpu.make_async_remote_copy(
                src_ref=win(a, *block), dst_ref=win(a, *block), send_sem=send_sems.at[a, k],
                recv_sem=recv_sems.at[a, k], device_id=to, device_id_type=MESH_ID)

        return c, me, sibling, chips, copy

    def start(_, outs, send_sems, recv_sems):
        c, me, sibling, chips, copy = plan(outs, send_sems, recv_sems)
        for a in range(n):
            copy(a, 0, me, sibling).start()
            for j, chip in enumerate(chips):
                copy(a, 1 + j, me, (*chip, c)).start()

    def finish(_, outs, send_sems, recv_sems):
        c, me, sibling, chips, copy = plan(outs, send_sems, recv_sems)
        for a in range(n):
            for j, chip in enumerate(chips):
                copy(a, 1 + j, (*chip, c), me).wait_recv()
                copy(a, 4 + j, (*chip, c), sibling).start()
        for a in range(n):
            copy(a, 0, sibling, me).wait_recv()
            for j, chip in enumerate(chips):
                copy(a, 4 + j, (*chip, 1 - c), me).wait_recv()
        for a in range(n):
            copy(a, 0, me, sibling).wait_send()
            for j, chip in enumerate(chips):
                copy(a, 1 + j, me, (*chip, c)).wait_send()
                copy(a, 4 + j, (*chip, c), sibling).wait_send()

    return _Carry(tuple(bufs), tuple(jax.ShapeDtypeStruct(b.shape, b.dtype) for b in bufs), {a: a for a in range(n)},
                  (n, 7), start, finish)


def _exchange_alone(carry, name):
    n_in, n_out = len(carry.bufs), len(carry.out_shapes)

    def body(*refs):
        ins, outs = refs[:n_in], refs[n_in:n_in + n_out]
        send_sems, recv_sems = refs[n_in + n_out:]
        carry.start(ins, outs, send_sems, recv_sems)
        carry.finish(ins, outs, send_sems, recv_sems)

    any_spec = pl.BlockSpec(memory_space=pl.ANY)
    return pl.pallas_call(
        body, name=name, out_shape=list(carry.out_shapes), in_specs=[any_spec] * n_in, out_specs=[any_spec] * n_out,
        input_output_aliases=dict(carry.aliases),
        scratch_shapes=[pltpu.SemaphoreType.DMA(carry.sem_shape), pltpu.SemaphoreType.DMA(carry.sem_shape)],
    )(*carry.bufs)


def _pair_exchange_carry(grads):
    n = len(grads)

    def copies(ins, outs, send_sems, recv_sems):
        x, y, c = _coords()
        res = []
        for a in range(n):
            h = grads[a].shape[1] // 2
            res.append(pltpu.make_async_remote_copy(
                src_ref=ins[a].at[:, pl.ds((1 - c) * h, h), :], dst_ref=outs[a], send_sem=send_sems.at[a],
                recv_sem=recv_sems.at[a], device_id=(x, y, 1 - c), device_id_type=MESH_ID))
        return res

    def start(ins, outs, send_sems, recv_sems):
        for cp in copies(ins, outs, send_sems, recv_sems):
            cp.start()

    def finish(ins, outs, send_sems, recv_sems):
        for cp in copies(ins, outs, send_sems, recv_sems):
            cp.wait()

    return _Carry(tuple(grads),
                  tuple(jax.ShapeDtypeStruct((g.shape[0], g.shape[1] // 2, g.shape[2]), g.dtype) for g in grads), {},
                  (n,), start, finish)


def _chip_exchange_carry(pairs):
    n = len(pairs)

    def copies(ins, outs, send_sems, recv_sems):
        x, y, c = _coords()
        chips = [(1 - x, y), (x, 1 - y), (1 - x, 1 - y)]
        return [pltpu.make_async_remote_copy(
            src_ref=ins[a].at[2 * px + py], dst_ref=outs[a].at[k], send_sem=send_sems.at[a, k],
            recv_sem=recv_sems.at[a, k], device_id=(px, py, c), device_id_type=MESH_ID)
            for a in range(n) for k, (px, py) in enumerate(chips)]

    def start(ins, outs, send_sems, recv_sems):
        for cp in copies(ins, outs, send_sems, recv_sems):
            cp.start()

    def finish(ins, outs, send_sems, recv_sems):
        for cp in copies(ins, outs, send_sems, recv_sems):
            cp.wait()

    return _Carry(tuple(pairs), tuple(jax.ShapeDtypeStruct((N_CHIP - 1,) + p.shape[1:], p.dtype) for p in pairs), {},
                  (n, N_CHIP - 1), start, finish)


def _half_share(bufs, name):
    n = len(bufs)

    def body(*refs):
        outs = refs[n:2 * n]
        send_sems, recv_sems = refs[2 * n:]
        x, y, c = _coords()
        sibling = (x, y, 1 - c)

        def copy(a, pc):
            h = bufs[a].shape[0] // 2
            rows = outs[a].at[pl.ds(pc * h, h), :]
            return pltpu.make_async_remote_copy(
                src_ref=rows, dst_ref=rows, send_sem=send_sems.at[a], recv_sem=recv_sems.at[a], device_id=sibling,
                device_id_type=MESH_ID)

        for a in range(n):
            copy(a, c).start()
        for a in range(n):
            copy(a, c).wait_send()
            copy(a, 1 - c).wait_recv()

    any_spec = pl.BlockSpec(memory_space=pl.ANY)
    return pl.pallas_call(
        body, name=name, out_shape=[jax.ShapeDtypeStruct(b.shape, b.dtype) for b in bufs],
        in_specs=[any_spec] * n, out_specs=[any_spec] * n, input_output_aliases={a: a for a in range(n)},
        scratch_shapes=[pltpu.SemaphoreType.DMA((n,)), pltpu.SemaphoreType.DMA((n,))],
    )(*bufs)


def _pair_adds(grads, recv, idx):
    return {k: _pair_add(f, r, idx[0], "rs_pair_add_" + k) for (k, f), r in zip(grads.items(), recv)}


PACK_ALIGN = SUBLANE * LANE
PACK_ROWS_ALIGN = 256 * LANE


def _pack(parts):
    flat, offs, off = [], [], 0
    for p in parts:
        v = p.reshape(-1).astype(F32)
        n = v.shape[0]
        padded = -(-n // PACK_ALIGN) * PACK_ALIGN
        flat.append(jnp.pad(v, (0, padded - n)))
        offs.append((off, n))
        off += padded
    tail = -off % PACK_ROWS_ALIGN
    if tail:
        flat.append(jnp.zeros((tail,), F32))
    return jnp.concatenate(flat).reshape(-1, LANE), offs


def _unpack(flat, offs, shapes):
    return [flat[o:o + n].reshape(s) for (o, n), s in zip(offs, shapes)]


def _chipcat(g, per_dev_len, offs, shape, axis):
    o, n = offs
    parts = [g[2 * j, o:o + n].reshape(shape) for j in range(N_CHIP)]
    return jnp.concatenate(parts, axis=axis)


def kernel(x, c, w_ada, b_ada, ln_g, ln_b, a_w_in, a_w_dw, a_b_dw, a_norm_g, a_norm_b, a_w_out, b_w_in, b_q_norm_g, b_w_qb, b_w_out, kv_w_a, kv_norm_g, kv_w_b, loss_target, m_w_ada, m_b_ada, m_ln_g, m_ln_b, m_a_w_in, m_a_w_dw, m_a_b_dw, m_a_norm_g, m_a_norm_b, m_a_w_out, m_b_w_in, m_b_q_norm_g, m_b_w_qb, m_b_w_out, m_kv_w_a, m_kv_norm_g, m_kv_w_b, v_w_ada, v_b_ada, v_ln_g, v_ln_b, v_a_w_in, v_a_w_dw, v_a_b_dw, v_a_norm_g, v_a_norm_b, v_a_w_out, v_b_w_in, v_b_q_norm_g, v_b_w_qb, v_b_w_out, v_kv_w_a, v_kv_norm_g, v_kv_w_b):
    xi, yi, ci = _coords()
    chip = 2 * xi + yi
    dev = 4 * xi + 2 * yi + ci
    idx = (jnp.reshape(ci, (1,)).astype(jnp.int32), jnp.reshape(chip, (1,)).astype(jnp.int32))

    x2 = x[0]
    tgt = loss_target[0]
    S, D = x2.shape
    C = a_w_out.shape[1] * N_CHIP
    Cq = C // N_CHIP
    KS = a_w_dw.shape[1]
    Q = b_q_norm_g.shape[1]
    KV = kv_norm_g.shape[0]
    Hq = kv_w_b.shape[1] // HEAD_PAD
    H = Hq * N_CHIP
    W = H * V_HEAD_DIM
    Nq = w_ada.shape[2]
    head_q = QK_NOPE_DIM + QK_ROPE_DIM
    scale = head_q ** -0.5
    tabs = _rope_tables(S)

    qb_pad = jnp.pad(b_w_qb[0].reshape(Q, Hq, head_q), ((0, 0), (0, 0), (0, HEAD_PAD - head_q))).reshape(Q, Hq * HEAD_PAD)
    kva_pad = jnp.pad(kv_w_a, ((0, 0), (0, LANE - QK_ROPE_DIM)))
    shards = {"a_w_in": a_w_in[0], "a_w_out": a_w_out[0], "b_w_in": b_w_in[0], "b_w_qb": qb_pad, "b_w_out": b_w_out[0],
              "kv_w_a": kva_pad, "kv_w_b": kv_w_b}
    placed = {k: _cast_place(w, idx, "cast_" + k) for k, w in shards.items()}
    (W_ain,) = _exchange_alone(_allgather_carry([placed["a_w_in"]]), "allgather_a_w_in")

    pack1, offs1 = _pack([c[0], a_w_dw[0], a_b_dw[0], a_norm_g[0], a_norm_b[0]])
    L1 = pack1.shape[0] * LANE
    g1 = _allgather_small(pack1, "allgather_small_in").reshape(N_DEV, L1)
    c_all = g1[:, :D]
    w_dw = _chipcat(g1, L1, offs1[1], (KS, Cq), 1)
    b_dw = _chipcat(g1, L1, offs1[2], (1, Cq), 1)
    g_cn = _chipcat(g1, L1, offs1[3], (1, Cq), 1)
    b_cn = _chipcat(g1, L1, offs1[4], (1, Cq), 1)

    b_ada_sh = lax.dynamic_slice_in_dim(b_ada, chip * Nq, Nq, axis=1)[:, None, :]
    mod_sh = _mod(c_all, w_ada, b_ada_sh, "adaln_mod")
    gm = _allgather_small(mod_sh.reshape(DEPTH * N_DEV, Nq), "allgather_small_mod").reshape(N_CHIP, 2, DEPTH, N_DEV, Nq)
    mod_rows = lax.dynamic_index_in_dim(gm[:, 0], dev, axis=2, keepdims=False)
    mod_me = jnp.transpose(mod_rows, (1, 0, 2)).reshape(DEPTH, N_CHIP * Nq)
    shift = [mod_me[l:l + 1, 0:D] for l in range(DEPTH)]
    scl = [mod_me[l:l + 1, D:2 * D] for l in range(DEPTH)]
    gate = [mod_me[l:l + 1, 2 * D:3 * D] for l in range(DEPTH)]

    h0 = _lnmod_fwd(x2, scl[0], shift[0], "a_lnmod_fwd")
    u0, (W_bin,) = _mm(h0, W_ain, b_sh=True, name="a_in_fwd", carry=_allgather_carry([placed["b_w_in"]]))
    v2, (W_aout, W_kva, W_kvb) = _conv1_fwd(
        u0, w_dw, b_dw, "a_conv1_fwd",
        carry=_allgather_carry([placed["a_w_out"], placed["kv_w_a"], placed["kv_w_b"]]))
    W_aout = W_aout.reshape(C, D)
    W_kva = W_kva.reshape(D, KV + LANE)
    p0 = _conv2_fwd(v2, u0, g_cn, b_cn, "a_conv2_fwd")
    out0 = _mm(p0, W_aout, name="a_out_fwd")
    x1 = _resln_fwd(x2, out0, gate[0], ln_g[0:1], ln_b[0:1], "a_resln_fwd")

    kva = _mm(x1, W_kva, name="kv_a_fwd")
    ckv, krp = _kvprep_fwd(kva, kv_norm_g[None, :], tabs, "kv_prep_fwd")
    kvh = _mm(ckv, W_kvb, b_sh=True, out_dtype=BF16, name="kv_b_fwd")

    h1 = _lnmod_fwd(x1, scl[1], shift[1], "b_lnmod_fwd")
    u1, (W_qb,) = _mm(h1, W_bin, b_sh=True, name="b_in_fwd", carry=_allgather_carry([placed["b_w_qb"]]))
    qn = _rms_fwd(u1, Q, b_q_norm_g, "b_qnorm_fwd")
    qraw = _mm(qn, W_qb, b_sh=True, name="b_qb_fwd")
    (o, p1, qr, lse), (W_bout,) = _attn_fwd(qraw, kvh, krp, u1, Q, tabs, scale, "b_attn_fwd",
                                            carry=_allgather_carry([placed["b_w_out"]]))
    W_bout = W_bout.reshape(W, D)
    out1 = _mm(p1, W_bout, name="b_out_fwd")

    dxa1, dout1, dgate1, dlng1, dlnb1, loss_part = _resln_bwd(
        x1, out1, gate[1], ln_g[1:2], ln_b[1:2], tgt, True, "b_resln_bwd")
    dW_bout = _mm(p1, dout1, ta=True, out_dtype=BF16, name="b_out_dw").reshape(N_CHIP, W // N_CHIP, D)
    dp1, recv = _mm(dout1, W_bout, tb=True, name="b_out_dx", carry=_pair_exchange_carry([dW_bout]))
    pairs = _pair_adds({"b_w_out": dW_bout}, recv, idx)
    got = {}
    dqr, dkvh, dkr_h, du1 = _attn_bwd(qr, kvh, krp, dp1, o, u1, Q, lse, scale, "b_attn_bwd")
    dqraw = _qrope_bwd(dqr, tabs, "b_qrope_bwd")
    dW_qb = _mm(qn, dqraw, ta=True, o_sh=True, out_dtype=BF16, name="b_qb_dw")
    dqn = _mm(dqraw, W_qb, tb=True, b_sh=True, name="b_qb_dx")
    du1, dgq = _rms_bwd(u1, Q, dqn, b_q_norm_g, du1, "b_qnorm_bwd")
    dW_bin = _mm(h1, du1, ta=True, o_sh=True, out_dtype=BF16, name="b_in_dw")
    dh1, (got["b_w_out"],) = _mm(du1, W_bin, tb=True, b_sh=True, name="b_in_dx",
                                 carry=_chip_exchange_carry([pairs["b_w_out"]]))

    dW_kvb = _mm(ckv, dkvh, ta=True, o_sh=True, out_dtype=BF16, name="kv_b_dw")
    dckv = _mm(dkvh, W_kvb, tb=True, b_sh=True, name="kv_b_dx")
    dkva, dgkv = _kvprep_bwd(kva, dckv, dkr_h, kv_norm_g[None, :], tabs, "kv_prep_bwd")
    dW_kva = _mm(x1, dkva, ta=True, out_dtype=BF16, name="kv_a_dw").reshape(N_CHIP, D // N_CHIP, KV + LANE)
    dx1_kv = _mm(dkva, W_kva, tb=True, name="kv_a_dx")
    dx1, dsc1, dsh1 = _lnmod_bwd(x1, dh1, scl[1], [dxa1, dx1_kv], "b_lnmod_bwd")

    dxa0, dout0, dgate0, dlng0, dlnb0 = _resln_bwd(x2, out0, gate[0], ln_g[0:1], ln_b[0:1], dx1, False, "a_resln_bwd")
    mla = {"b_w_qb": dW_qb, "b_w_in": dW_bin, "kv_w_b": dW_kvb, "kv_w_a": dW_kva}
    dW_aout, recv = _mm(p0, dout0, ta=True, out_dtype=BF16, name="a_out_dw", carry=_pair_exchange_carry(list(mla.values())))
    pairs.update(_pair_adds(mla, recv, idx))
    dW_aout = dW_aout.reshape(N_CHIP, Cq, D)
    dp0, recv = _mm(dout0, W_aout, tb=True, name="a_out_dx", carry=_pair_exchange_carry([dW_aout]))
    pairs.update(_pair_adds({"a_w_out": dW_aout}, recv, idx))
    dv2, dz0, dgcn, dbcn = _conv2_bwd(dp0, v2, u0, g_cn, b_cn, "a_conv2_bwd")
    mid = ["b_w_in", "kv_w_b", "kv_w_a"]
    (da0, dg0, dwdw, dbdw), got_mid = _conv1_bwd(dv2, u0, w_dw, "a_conv1_bwd",
                                                 carry=_chip_exchange_carry([pairs[k] for k in mid]))
    got.update(zip(mid, got_mid))
    du0 = jnp.concatenate([da0, dg0, dz0], axis=1)
    dW_ain, (got["a_w_out"], got["b_w_qb"]) = _mm(
        h0, du0, ta=True, o_sh=True, out_dtype=BF16, name="a_in_dw",
        carry=_chip_exchange_carry([pairs["a_w_out"], pairs["b_w_qb"]]))
    pairs.update(_pair_adds({"a_w_in": dW_ain},
                            _exchange_alone(_pair_exchange_carry([dW_ain]), "rs_pair_exchange_a_w_in"), idx))
    dh0, (got["a_w_in"],) = _mm(du0, W_ain, tb=True, b_sh=True, name="a_in_dx",
                                carry=_chip_exchange_carry([pairs["a_w_in"]]))
    dx, dsc0, dsh0 = _lnmod_bwd(x2, dh0, scl[0], [dxa0], "a_lnmod_bwd")
    grad_x = dx[None]

    dmod = jnp.concatenate([dsh0, dsc0, dgate0, dsh1, dsc1, dgate1], axis=1).reshape(DEPTH, 3 * D)
    small = [loss_part, dmod, jnp.concatenate([dlng0, dlng1], 0), jnp.concatenate([dlnb0, dlnb1], 0),
             dwdw, dbdw, dgcn, dbcn, dgq, dgkv]
    small_shapes = [p.shape for p in small]
    pack2, offs2 = _pack(small)
    R2 = pack2.shape[0]
    g2 = _allgather_small(pack2, "allgather_small_grads").reshape(N_DEV, R2, LANE)
    tot = _sum_leading(g2, "small_grad_sum").reshape(-1)
    (loss_t, g_b_ada, g_ln_g, g_ln_b, g_wdw_full, g_bdw_full, g_gcn_full, g_bcn_full, g_gq, g_gkv) = _unpack(
        tot, offs2, small_shapes)
    loss = loss_t.reshape(())
    colsl = lambda a: lax.dynamic_slice_in_dim(a, chip * Cq, Cq, axis=1)
    g_wdw, g_bdw, g_gcn, g_bcn = colsl(g_wdw_full), colsl(g_bdw_full), colsl(g_gcn_full), colsl(g_bcn_full)

    dmod_all = jnp.stack([g2[d].reshape(-1)[offs2[1][0]:offs2[1][0] + offs2[1][1]].reshape(DEPTH, 3 * D)
                          for d in range(N_DEV)], axis=1)
    dmod_sh = lax.dynamic_slice_in_dim(dmod_all, chip * Nq, Nq, axis=2)
    g_w_ada = _wada_grad(jnp.transpose(c_all), dmod_sh, "w_ada_grad")

    mats = ["a_w_in", "a_w_out", "b_w_in", "b_w_qb", "b_w_out", "kv_w_a", "kv_w_b"]
    halves = [_chip_sum(pairs[k], got[k], idx, "rs_chip_sum_" + k) for k in mats]
    red = dict(zip(mats, _half_share(halves, "rs_half_share")))
    g_a_w_in = red["a_w_in"]
    g_a_w_out = red["a_w_out"]
    g_b_w_in = red["b_w_in"]
    g_b_w_qb = red["b_w_qb"].reshape(Q, Hq, HEAD_PAD)[:, :, :head_q].reshape(Q, Hq * head_q)
    g_b_w_out = red["b_w_out"]
    g_kv_w_a = red["kv_w_a"][:, :KV + QK_ROPE_DIM]
    g_kv_w_b = red["kv_w_b"]

    grads = {
        "w_ada": g_w_ada, "b_ada": g_b_ada, "ln_g": g_ln_g, "ln_b": g_ln_b, "a_w_in": g_a_w_in[None],
        "a_w_dw": g_wdw[None], "a_b_dw": g_bdw, "a_norm_g": g_gcn, "a_norm_b": g_bcn, "a_w_out": g_a_w_out[None],
        "b_w_in": g_b_w_in[None], "b_q_norm_g": g_gq, "b_w_qb": g_b_w_qb[None], "b_w_out": g_b_w_out[None],
        "kv_w_a": g_kv_w_a, "kv_norm_g": g_gkv.reshape(KV), "kv_w_b": g_kv_w_b,
    }
    weights = {
        "w_ada": (w_ada, m_w_ada, v_w_ada), "b_ada": (b_ada, m_b_ada, v_b_ada), "ln_g": (ln_g, m_ln_g, v_ln_g),
        "ln_b": (ln_b, m_ln_b, v_ln_b), "a_w_in": (a_w_in, m_a_w_in, v_a_w_in), "a_w_dw": (a_w_dw, m_a_w_dw, v_a_w_dw),
        "a_b_dw": (a_b_dw, m_a_b_dw, v_a_b_dw), "a_norm_g": (a_norm_g, m_a_norm_g, v_a_norm_g),
        "a_norm_b": (a_norm_b, m_a_norm_b, v_a_norm_b), "a_w_out": (a_w_out, m_a_w_out, v_a_w_out),
        "b_w_in": (b_w_in, m_b_w_in, v_b_w_in), "b_q_norm_g": (b_q_norm_g, m_b_q_norm_g, v_b_q_norm_g),
        "b_w_qb": (b_w_qb, m_b_w_qb, v_b_w_qb), "b_w_out": (b_w_out, m_b_w_out, v_b_w_out),
        "kv_w_a": (kv_w_a, m_kv_w_a, v_kv_w_a), "kv_norm_g": (kv_norm_g, m_kv_norm_g, v_kv_norm_g),
        "kv_w_b": (kv_w_b, m_kv_w_b, v_kv_w_b),
    }
    order = list(weights)
    big = [k for k in order if weights[k][0].size >= (1 << 16) and weights[k][0].shape[-1] % LANE == 0]
    small_names = [k for k in order if k not in big]
    upd = {}
    for k in big:
        w, m, v = weights[k]
        shp = w.shape
        two = (-1, shp[-1])
        d_, m_, v_ = _adamw(w.reshape(two), grads[k].reshape(two), m.reshape(two), v.reshape(two), "adamw_" + k)
        upd[k] = (grads[k].reshape(shp), d_.reshape(shp), m_.reshape(shp), v_.reshape(shp))
    sw, offs3 = _pack([weights[k][0] for k in small_names])
    sg, _ = _pack([grads[k] for k in small_names])
    sm, _ = _pack([weights[k][1] for k in small_names])
    sv, _ = _pack([weights[k][2] for k in small_names])
    sd, snm, snv = _adamw(sw, sg, sm, sv, "adamw_small")
    shapes3 = [weights[k][0].shape for k in small_names]
    for k, d_, m_, v_ in zip(small_names, _unpack(sd.reshape(-1), offs3, shapes3), _unpack(snm.reshape(-1), offs3, shapes3),
                             _unpack(snv.reshape(-1), offs3, shapes3)):
        upd[k] = (grads[k].reshape(weights[k][0].shape), d_, m_, v_)

    return (loss, grad_x, *[upd[k][0] for k in order], *[upd[k][1] for k in order], *[upd[k][2] for k in order],
            *[upd[k][3] for k in order])
```
